```python
import jax, jax.numpy as jnp
from jax import lax
import numpy as np

D_MODEL = 2048
BATCH = 2
SEQ = 4096
DEPTH = 1

ATT_HEADS = 8
ATT_HEAD_DIM = 128
ATT_WIDTH = ATT_HEADS * ATT_HEAD_DIM
RWKV_WIDTH = D_MODEL - ATT_WIDTH
RWKV_HEAD_DIM = 64
RWKV_HEADS = RWKV_WIDTH // RWKV_HEAD_DIM
IDX_HEADS = 16
IDX_HEAD_DIM = 64
TOPK_MAX = 256
ROPE_THETA = 500000.0
ROPE_FRACTION = 4
DECAY_LORA = 96
AAA_LORA = 96
GATE_LORA = 256
D_FF = ((8 * D_MODEL // 3 + 255) // 256) * 256
Q_BLOCK = 64
NORM_EPS = 1e-6
LNX_EPS = 64e-5

ATT_SPLITS = (ATT_WIDTH, ATT_WIDTH, ATT_WIDTH, IDX_HEADS * IDX_HEAD_DIM, IDX_HEAD_DIM, IDX_HEADS)
RWKV_SPLITS = (RWKV_WIDTH, RWKV_WIDTH, RWKV_WIDTH, DECAY_LORA, AAA_LORA, GATE_LORA)
ATT_COLS = sum(ATT_SPLITS)
RWKV_COLS = sum(RWKV_SPLITS)
IN_COLS = ATT_COLS + RWKV_COLS

kernel_name = "hymba_dsa_rwkv7_adaln_layer"


def split_cols(y, sizes):
    offs = [int(o) for o in np.cumsum(sizes)[:-1]]
    return jnp.split(y, offs, axis=-1)


def rms_norm(x, gain):
    xf = x.astype(jnp.float32)
    y = xf * lax.rsqrt(jnp.mean(xf * xf, axis=-1, keepdims=True) + NORM_EPS)
    return (y * gain.astype(jnp.float32)).astype(x.dtype)


def partial_rope(x, positions):
    d = x.shape[-1]
    rot = d // ROPE_FRACTION
    half = rot // 2
    inv_freq = ROPE_THETA ** (-jnp.arange(half, dtype=jnp.float32) / half)
    ang = positions.astype(jnp.float32)[..., None] * inv_freq
    cos = jnp.cos(ang)[:, :, None, :]
    sin = jnp.sin(ang)[:, :, None, :]
    xr = x[..., :rot].astype(jnp.float32)
    x1, x2 = xr[..., :half], xr[..., half:]
    rotated = jnp.concatenate([x1 * cos - x2 * sin, x2 * cos + x1 * sin], axis=-1)
    return jnp.concatenate([rotated.astype(x.dtype), x[..., rot:]], axis=-1)


def dsa_sparse_attention(q, k, v, iq, ik, iw):
    B, S, H, D = q.shape
    topk = min(TOPK_MAX, S // 4)
    nblk = S // Q_BLOCK
    key_pos = jnp.arange(S)
    bidx = jnp.arange(B)[:, None, None]

    def blocks(t):
        return jnp.moveaxis(t.reshape(B, nblk, Q_BLOCK, *t.shape[2:]), 1, 0)

    def one_block(args):
        bi, qb, iqb, iwb = args
        qpos = bi * Q_BLOCK + jnp.arange(Q_BLOCK)
        causal = key_pos[None, None, :] <= qpos[None, :, None]
        dots = jnp.einsum('bqhd,bsd->bqhs', iqb, ik, preferred_element_type=jnp.float32)
        score = jnp.einsum('bqhs,bqh->bqs', jax.nn.relu(dots), iwb.astype(jnp.float32))
        score = jnp.where(causal, score, -jnp.inf)
        _, idx = lax.top_k(score, topk)
        valid = idx <= qpos[None, :, None]
        k_sel = k[bidx, idx]
        v_sel = v[bidx, idx]
        logits = jnp.einsum('bqhd,bqkhd->bhqk', qb, k_sel,
                            preferred_element_type=jnp.float32) * (D ** -0.5)
        logits = jnp.where(valid[:, None], logits, -jnp.inf)
        p = jax.nn.softmax(logits, axis=-1).astype(v.dtype)
        return jnp.einsum('bhqk,bqkhd->bqhd', p, v_sel)

    out = lax.map(one_block, (jnp.arange(nblk), blocks(q), blocks(iq), blocks(iw)))
    return jnp.moveaxis(out, 0, 1).reshape(B, S, H * D)


def rwkv7_time_mix(y, mu, w0, w_up, a0, a_up, g_up, k_k, k_a, r_k, lnx_g, lnx_b):
    B, S, _ = y.shape
    H, N = RWKV_HEADS, RWKV_HEAD_DIM
    f32 = jnp.float32
    y_prev = jnp.pad(y, ((0, 0), (1, 0), (0, 0)))[:, :-1]
    y = y + (y_prev - y) * mu
    r, k, v, xw, xa, xg = split_cols(y, RWKV_SPLITS)
    w_raw = (w0 + jnp.tanh(xw) @ w_up).astype(f32)
    decay = jnp.exp(-jnp.exp(-jax.nn.softplus(-w_raw) - 0.5))
    a = jax.nn.sigmoid(a0 + xa @ a_up)
    g = jax.nn.sigmoid(xg) @ g_up

    def heads(t):
        return t.reshape(B, S, H, N)

    kk = heads(k * k_k).astype(f32)
    kk = kk / jnp.maximum(jnp.sqrt(jnp.sum(kk * kk, axis=-1, keepdims=True)), 1e-12)
    k = k * (1 + (a - 1) * k_a)
    r_h, k_h, v_h, a_h, w_h = heads(r), heads(k), heads(v), heads(a), heads(decay)

    def step(state, inp):
        r_t, w_t, k_t, v_t, kk_t, a_t = inp
        sa = jnp.einsum('bhvk,bhk->bhv', state, -kk_t)
        state = (state * w_t[:, :, None, :]
                 + sa[..., None] * (kk_t * a_t)[:, :, None, :]
                 + v_t[..., None] * k_t[:, :, None, :])
        out = jnp.einsum('bhvk,bhk->bhv', state, r_t)
        return state, out

    def seq_major(t):
        return jnp.moveaxis(t.astype(f32), 1, 0)

    state0 = jnp.zeros((B, H, N, N), f32)
    _, o = lax.scan(step, state0, (seq_major(r_h), seq_major(w_h), seq_major(k_h),
                                   seq_major(v_h), seq_major(kk), seq_major(a_h)))
    o = jnp.moveaxis(o, 0, 1)
    mean = jnp.mean(o, axis=-1, keepdims=True)
    var = jnp.mean(jnp.square(o - mean), axis=-1, keepdims=True)
    o = ((o - mean) * lax.rsqrt(var + LNX_EPS)).reshape(B, S, H * N) * lnx_g + lnx_b
    bonus = jnp.sum(r_h * k_h * r_k, axis=-1, keepdims=True) * v_h
    o = o + bonus.reshape(B, S, H * N)
    return (o * g).astype(y.dtype)


def setup_inputs(seed: int = 0) -> dict:
    key = jax.random.key(seed)
    ks = jax.random.split(key, 26)
    f32 = jnp.float32
    L = DEPTH

    def nrm(k, shape, scale):
        return jax.random.normal(k, shape, f32) * scale

    offset = jax.random.randint(ks[2], (BATCH, 1), 0, 1024, dtype=jnp.int32)
    return {
        "x": nrm(ks[0], (BATCH, SEQ, D_MODEL), 1.0),
        "c": nrm(ks[1], (BATCH, D_MODEL), 1.0),
        "positions": offset + jnp.arange(SEQ, dtype=jnp.int32)[None, :],
        "w_ada": nrm(ks[3], (L, D_MODEL, 6 * D_MODEL), D_MODEL ** -0.5),
        "b_ada": nrm(ks[4], (L, 6 * D_MODEL), 0.01),
        "norm1_g": 1.0 + nrm(ks[5], (L, D_MODEL), 0.02),
        "w_in": nrm(ks[6], (L, D_MODEL, IN_COLS), D_MODEL ** -0.5),
        "q_norm_g": 1.0 + nrm(ks[7], (L, ATT_HEAD_DIM), 0.02),
        "k_norm_g": 1.0 + nrm(ks[8], (L, ATT_HEAD_DIM), 0.02),
        "rwkv_mu": jax.random.uniform(ks[9], (L, RWKV_COLS), f32),
        "rwkv_w0": nrm(ks[10], (L, RWKV_WIDTH), 0.5),
        "rwkv_w_up": nrm(ks[11], (L, DECAY_LORA, RWKV_WIDTH), DECAY_LORA ** -0.5),
        "rwkv_a0": nrm(ks[12], (L, RWKV_WIDTH), 0.1),
        "rwkv_a_up": nrm(ks[13], (L, AAA_LORA, RWKV_WIDTH), AAA_LORA ** -0.5),
        "rwkv_g_up": nrm(ks[14], (L, GATE_LORA, RWKV_WIDTH), GATE_LORA ** -0.5),
        "rwkv_k_k": 0.85 + nrm(ks[15], (L, RWKV_WIDTH), 0.02),
        "rwkv_k_a": 1.0 + nrm(ks[16], (L, RWKV_WIDTH), 0.02),
        "rwkv_r_k": nrm(ks[17], (L, RWKV_HEADS, RWKV_HEAD_DIM), 0.1),
        "rwkv_lnx_g": 1.0 + nrm(ks[18], (L, RWKV_WIDTH), 0.02),
        "rwkv_lnx_b": nrm(ks[19], (L, RWKV_WIDTH), 0.01),
        "w_out": nrm(ks[20], (L, ATT_WIDTH + RWKV_WIDTH, D_MODEL), D_MODEL ** -0.5),
        "norm2_g": 1.0 + nrm(ks[21], (L, D_MODEL), 0.02),
        "w_ffn_gate": nrm(ks[22], (L, D_MODEL, D_FF), D_MODEL ** -0.5),
        "w_ffn_up": nrm(ks[23], (L, D_MODEL, D_FF), D_MODEL ** -0.5),
        "w_ffn_down": nrm(ks[24], (L, D_FF, D_MODEL), D_FF ** -0.5),
    }


def reference(x, c, positions, w_ada, b_ada, norm1_g, w_in, q_norm_g, k_norm_g,
              rwkv_mu, rwkv_w0, rwkv_w_up, rwkv_a0, rwkv_a_up, rwkv_g_up,
              rwkv_k_k, rwkv_k_a, rwkv_r_k, rwkv_lnx_g, rwkv_lnx_b, w_out,
              norm2_g, w_ffn_gate, w_ffn_up, w_ffn_down):
    B, S, _ = x.shape
    c_act = jax.nn.silu(c)
    for l in range(DEPTH):
        mod = c_act @ w_ada[l] + b_ada[l]
        sh1, sc1, gt1, sh2, sc2, gt2 = jnp.split(mod[:, None, :], 6, axis=-1)

        h = rms_norm(x, norm1_g[l]) * (1 + sc1) + sh1
        proj = h @ w_in[l]
        att_proj, rwkv_proj = proj[..., :ATT_COLS], proj[..., ATT_COLS:]
        q, k, v, iq, ik, iw = split_cols(att_proj, ATT_SPLITS)
        q = partial_rope(rms_norm(q.reshape(B, S, ATT_HEADS, ATT_HEAD_DIM), q_norm_g[l]), positions)
        k = partial_rope(rms_norm(k.reshape(B, S, ATT_HEADS, ATT_HEAD_DIM), k_norm_g[l]), positions)
        v = v.reshape(B, S, ATT_HEADS, ATT_HEAD_DIM)
        iq = partial_rope(iq.reshape(B, S, IDX_HEADS, IDX_HEAD_DIM), positions) * (IDX_HEAD_DIM ** -0.5)
        ik = partial_rope(ik[:, :, None, :], positions)[:, :, 0, :]
        iw = iw * (IDX_HEADS ** -0.5)
        att_out = dsa_sparse_attention(q, k, v, iq, ik, iw)
        rwkv_out = rwkv7_time_mix(rwkv_proj, rwkv_mu[l], rwkv_w0[l], rwkv_w_up[l],
                                  rwkv_a0[l], rwkv_a_up[l], rwkv_g_up[l], rwkv_k_k[l],
                                  rwkv_k_a[l], rwkv_r_k[l], rwkv_lnx_g[l], rwkv_lnx_b[l])
        mixed = jnp.concatenate([att_out, rwkv_out], axis=-1) @ w_out[l]
        x = x + gt1 * mixed

        h2 = rms_norm(x, norm2_g[l]) * (1 + sc2) + sh2
        ffn = (jax.nn.silu(h2 @ w_ffn_gate[l]) * (h2 @ w_ffn_up[l])) @ w_ffn_down[l]
        x = x + gt2 * ffn
    return x
```

```python
import functools

import jax
import jax.numpy as jnp
import numpy as np
from jax import lax
from jax.experimental import pallas as pl
from jax.experimental.pallas import tpu as pltpu

F32 = jnp.float32
BF16 = jnp.bfloat16
I32 = jnp.int32

D_MODEL = 2048
ATT_HEADS = 8
ATT_HEAD_DIM = 128
ATT_WIDTH = ATT_HEADS * ATT_HEAD_DIM
RWKV_WIDTH = D_MODEL - ATT_WIDTH
RWKV_HEAD_DIM = 64
RWKV_HEADS = RWKV_WIDTH // RWKV_HEAD_DIM
IDX_HEADS = 16
IDX_HEAD_DIM = 64
TOPK_MAX = 256
ROPE_THETA = 500000.0
ROPE_FRACTION = 4
DECAY_LORA = 96
AAA_LORA = 96
GATE_LORA = 256
NORM_EPS = 1e-6
LNX_EPS = 64e-5

LANES = 128
LORA_PAD = 128
IN_COLS_PAD = 8192
OFF_Q, OFF_K, OFF_V, OFF_IQ = 0, 1024, 2048, 3072
OFF_RR, OFF_RK, OFF_RV = 4096, 5120, 6144
OFF_IK, OFF_IW, OFF_XW, OFF_XA, OFF_XG = 7168, 7296, 7424, 7552, 7680

CHUNK = 64
VMEM_LIMIT = 56 * 1024 * 1024


def _cparams(sem):
    return pltpu.CompilerParams(dimension_semantics=sem, vmem_limit_bytes=VMEM_LIMIT)


def _dot(a, b):
    return jnp.dot(a, b, preferred_element_type=F32)


def _dot_nt(a, b):
    return lax.dot_general(a, b, (((1,), (1,)), ((), ())), preferred_element_type=F32)


def _dot_tn(a, b):
    return lax.dot_general(a, b, (((0,), (0,)), ((), ())), preferred_element_type=F32)


def _split2(x):
    hi = x.astype(BF16)
    lo = (x - hi.astype(F32)).astype(BF16)
    return hi, lo


def _dot_hi(x, w):
    hi, lo = _split2(x)
    return _dot(hi, w) + _dot(lo, w)


def _sigmoid(x):
    return 1.0 / (1.0 + jnp.exp(-x))


def _adaln_kernel(c_ref, w_ref, b_ref, o_ref):
    c = c_ref[...]
    ca = c * _sigmoid(c)
    o_ref[...] = _dot(ca.astype(BF16), w_ref[...].astype(BF16)) + b_ref[...]


def _adaln(c, w, b):
    bsz, d = c.shape
    n = w.shape[1]
    rows = 8
    cp = jnp.zeros((rows, d), F32).at[:bsz].set(c)
    tn = 1024
    out = pl.pallas_call(
        _adaln_kernel,
        grid=(n // tn,),
        in_specs=[pl.BlockSpec((rows, d), lambda j: (0, 0)),
                  pl.BlockSpec((d, tn), lambda j: (0, j)),
                  pl.BlockSpec((1, tn), lambda j: (0, j))],
        out_specs=pl.BlockSpec((rows, tn), lambda j: (0, j)),
        out_shape=jax.ShapeDtypeStruct((rows, n), F32),
        compiler_params=_cparams(("arbitrary",)),
        name="adaln",
    )(cp, w, b.reshape(1, n))
    return out[:bsz]


def _norm_mod(x, g, sc, sh):
    ms = jnp.mean(x * x, axis=-1, keepdims=True)
    y = x * lax.rsqrt(ms + NORM_EPS)
    y = y * g
    return y * (1.0 + sc) + sh


def _in_proj_kernel(x_ref, g_ref, sc_ref, sh_ref, w_ref, o_ref, h_ref):
    @pl.when(pl.program_id(1) == 0)
    def _():
        h_ref[...] = _norm_mod(x_ref[...], g_ref[...], sc_ref[...], sh_ref[...]).astype(BF16)

    o_ref[...] = _dot(h_ref[...], w_ref[...])


def _in_proj(x2, g, mod3, w, seq, tm=512, tn=1024):
    m, d = x2.shape
    n = w.shape[1]
    per_b = seq // tm
    return pl.pallas_call(
        _in_proj_kernel,
        grid=(m // tm, n // tn),
        in_specs=[pl.BlockSpec((tm, d), lambda i, j: (i, 0)),
                  pl.BlockSpec((1, d), lambda i, j: (0, 0)),
                  pl.BlockSpec((None, 1, d), lambda i, j: ((i // per_b) * 6 + 1, 0, 0)),
                  pl.BlockSpec((None, 1, d), lambda i, j: ((i // per_b) * 6 + 0, 0, 0)),
                  pl.BlockSpec((d, tn), lambda i, j: (0, j))],
        out_specs=pl.BlockSpec((tm, tn), lambda i, j: (i, j)),
        out_shape=jax.ShapeDtypeStruct((m, n), F32),
        scratch_shapes=[pltpu.VMEM((tm, d), BF16)],
        compiler_params=_cparams(("arbitrary", "arbitrary")),
        name="in_proj",
    )(x2, g, mod3, mod3, w)


def _ffn_glu_kernel(x_ref, g_ref, sc_ref, sh_ref, wg_ref, wu_ref, o_ref, h_ref):
    @pl.when(pl.program_id(1) == 0)
    def _():
        h_ref[...] = _norm_mod(x_ref[...], g_ref[...], sc_ref[...], sh_ref[...]).astype(BF16)

    h = h_ref[...]
    a = _dot(h, wg_ref[...])
    u = _dot(h, wu_ref[...])
    o_ref[...] = (a * _sigmoid(a) * u).astype(o_ref.dtype)


def _ffn_glu(x2, g, mod3, wg, wu, seq, tm=512, tn=512):
    m, d = x2.shape
    n = wg.shape[1]
    per_b = seq // tm
    return pl.pallas_call(
        _ffn_glu_kernel,
        grid=(m // tm, n // tn),
        in_specs=[pl.BlockSpec((tm, d), lambda i, j: (i, 0)),
                  pl.BlockSpec((1, d), lambda i, j: (0, 0)),
                  pl.BlockSpec((None, 1, d), lambda i, j: ((i // per_b) * 6 + 4, 0, 0)),
                  pl.BlockSpec((None, 1, d), lambda i, j: ((i // per_b) * 6 + 3, 0, 0)),
                  pl.BlockSpec((d, tn), lambda i, j: (0, j)),
                  pl.BlockSpec((d, tn), lambda i, j: (0, j))],
        out_specs=pl.BlockSpec((tm, tn), lambda i, j: (i, j)),
        out_shape=jax.ShapeDtypeStruct((m, n), BF16),
        scratch_shapes=[pltpu.VMEM((tm, d), BF16)],
        compiler_params=_cparams(("arbitrary", "arbitrary")),
        name="ffn_glu",
    )(x2, g, mod3, mod3, wg, wu)


def _out_proj_kernel(a_ref, r_ref, wa_ref, wr_ref, x_ref, gt_ref, o_ref):
    mixed = _dot(a_ref[...], wa_ref[...]) + _dot(r_ref[...], wr_ref[...])
    o_ref[...] = x_ref[...] + gt_ref[...] * mixed


def _out_proj(att, rwkv, wa, wr, x2, mod3, seq, tm=512, tn=1024):
    m, ka = att.shape
    kr = rwkv.shape[1]
    n = wa.shape[1]
    per_b = seq // tm
    return pl.pallas_call(
        _out_proj_kernel,
        grid=(m // tm, n // tn),
        in_specs=[pl.BlockSpec((tm, ka), lambda i, j: (i, 0)),
                  pl.BlockSpec((tm, kr), lambda i, j: (i, 0)),
                  pl.BlockSpec((ka, tn), lambda i, j: (0, j)),
                  pl.BlockSpec((kr, tn), lambda i, j: (0, j)),
                  pl.BlockSpec((tm, tn), lambda i, j: (i, j)),
                  pl.BlockSpec((None, 1, tn), lambda i, j: ((i // per_b) * 6 + 2, 0, j))],
        out_specs=pl.BlockSpec((tm, tn), lambda i, j: (i, j)),
        out_shape=jax.ShapeDtypeStruct((m, n), F32),
        compiler_params=_cparams(("arbitrary", "arbitrary")),
        name="out_proj",
    )(att, rwkv, wa, wr, x2, mod3)


def _ffn_down_kernel(h_ref, w_ref, x_ref, gt_ref, o_ref, acc_ref):
    kk = pl.program_id(2)

    @pl.when(kk == 0)
    def _():
        acc_ref[...] = jnp.zeros_like(acc_ref)

    acc_ref[...] += _dot(h_ref[...], w_ref[...])

    @pl.when(kk == pl.num_programs(2) - 1)
    def _():
        o_ref[...] = x_ref[...] + gt_ref[...] * acc_ref[...]


def _ffn_down(h, w, x2, mod3, seq, tm=512, tn=1024, tk=512):
    m, kdim = h.shape
    n = w.shape[1]
    per_b = seq // tm
    return pl.pallas_call(
        _ffn_down_kernel,
        grid=(m // tm, n // tn, kdim // tk),
        in_specs=[pl.BlockSpec((tm, tk), lambda i, j, k: (i, k)),
                  pl.BlockSpec((tk, tn), lambda i, j, k: (k, j)),
                  pl.BlockSpec((tm, tn), lambda i, j, k: (i, j)),
                  pl.BlockSpec((None, 1, tn), lambda i, j, k: ((i // per_b) * 6 + 5, 0, j))],
        out_specs=pl.BlockSpec((tm, tn), lambda i, j, k: (i, j)),
        out_shape=jax.ShapeDtypeStruct((m, n), F32),
        scratch_shapes=[pltpu.VMEM((tm, tn), F32)],
        compiler_params=_cparams(("arbitrary", "arbitrary", "arbitrary")),
        name="ffn_down",
    )(h, w, x2, mod3)


def _rope(x, cos, sin_lo, sin_hi, half):
    return (x * cos + pltpu.roll(x, LANES - half, 1) * sin_lo
            + pltpu.roll(x, half, 1) * sin_hi)


def _att_prep_kernel(pos_ref, q_ref, k_ref, v_ref, iq_ref, ik_ref, iw_ref,
                     qg_ref, kg_ref, fa_ref, fi_ref,
                     qo_ref, ko_ref, vo_ref, iqo_ref, iko_ref, iwo_ref):
    pos = pos_ref[...].astype(F32)
    lane = lax.broadcasted_iota(I32, (1, LANES), 1)

    half_a = ATT_HEAD_DIM // ROPE_FRACTION // 2
    ang = pos * fa_ref[...]
    cos_a, sin_a = jnp.cos(ang), jnp.sin(ang)
    lo_a = jnp.where(lane < half_a, -sin_a, 0.0)
    hi_a = jnp.where((lane >= half_a) & (lane < 2 * half_a), sin_a, 0.0)

    half_i = IDX_HEAD_DIM // ROPE_FRACTION // 2
    lane_i = lane & (IDX_HEAD_DIM - 1)
    ang = pos * fi_ref[...]
    cos_i, sin_i = jnp.cos(ang), jnp.sin(ang)
    lo_i = jnp.where(lane_i < half_i, -sin_i, 0.0)
    hi_i = jnp.where((lane_i >= half_i) & (lane_i < 2 * half_i), sin_i, 0.0)

    def head_norm(x, g):
        ms = jnp.mean(x * x, axis=-1, keepdims=True)
        return x * lax.rsqrt(ms + NORM_EPS) * g

    qg, kg = qg_ref[...], kg_ref[...]
    for h in range(ATT_HEADS):
        sl = slice(h * LANES, (h + 1) * LANES)
        qh = _rope(head_norm(q_ref[:, sl], qg), cos_a, lo_a, hi_a, half_a)
        qo_ref[:, sl] = (qh * (ATT_HEAD_DIM ** -0.5)).astype(BF16)
        kh = _rope(head_norm(k_ref[:, sl], kg), cos_a, lo_a, hi_a, half_a)
        ko_ref[:, sl] = kh.astype(BF16)
    vo_ref[...] = v_ref[...].astype(BF16)
    for h in range(IDX_HEADS * IDX_HEAD_DIM // LANES):
        sl = slice(h * LANES, (h + 1) * LANES)
        ih = _rope(iq_ref[:, sl], cos_i, lo_i, hi_i, half_i)
        iqo_ref[:, sl] = (ih * (IDX_HEAD_DIM ** -0.5)).astype(BF16)
    iko_ref[...] = _rope(ik_ref[...], cos_i, lo_i, hi_i, half_i).astype(BF16)
    iwo_ref[...] = iw_ref[...] * (IDX_HEADS ** -0.5)


def _att_prep(proj, pos2, qg, kg, fa, fi, tm=512):
    m = proj.shape[0]
    w = ATT_WIDTH

    def wide(off):
        return pl.BlockSpec((tm, w), lambda i: (i, off // w))

    def narrow(off):
        return pl.BlockSpec((tm, LANES), lambda i: (i, off // LANES))

    const = pl.BlockSpec((1, LANES), lambda i: (0, 0))
    return pl.pallas_call(
        _att_prep_kernel,
        grid=(m // tm,),
        in_specs=[pl.BlockSpec((tm, 1), lambda i: (i, 0)),
                  wide(OFF_Q), wide(OFF_K), wide(OFF_V), wide(OFF_IQ),
                  narrow(OFF_IK), narrow(OFF_IW), const, const, const, const],
        out_specs=[pl.BlockSpec((tm, w), lambda i: (i, 0))] * 4
        + [pl.BlockSpec((tm, LANES), lambda i: (i, 0))] * 2,
        out_shape=[jax.ShapeDtypeStruct((m, w), BF16)] * 4
        + [jax.ShapeDtypeStruct((m, LANES), BF16), jax.ShapeDtypeStruct((m, LANES), F32)],
        compiler_params=_cparams(("arbitrary",)),
        name="att_prep",
    )(pos2, proj, proj, proj, proj, proj, proj, qg, kg, fa, fi)


NEG_BIG = -1e30


def _dsa_kernel(q_ref, iq_ref, iw_ref, k_ref, v_ref, ik_ref, o_ref,
                keys_ref, iqm_ref, m_ref, l_ref, acc_ref, *, tq, tk, topk):
    qi = pl.program_id(1)
    n_kb = (qi * tq + tq - 1) // tk + 1
    row = qi * tq + lax.broadcasted_iota(I32, (tq, tk), 0)
    col0 = lax.broadcasted_iota(I32, (tq, tk), 1)
    lane = lax.broadcasted_iota(I32, (1, LANES), 1)
    low_half = lane < IDX_HEAD_DIM

    for h in range(IDX_HEADS):
        p = h // 2
        x = iq_ref[:, p * LANES:(p + 1) * LANES]
        keep = low_half if h % 2 == 0 else jnp.logical_not(low_half)
        iqm_ref[h] = jnp.where(keep, x, jnp.zeros_like(x))
    iw = iw_ref[...]

    def score_body(kb, carry):
        ikb = ik_ref[pl.ds(pl.multiple_of(kb * tk, tk), tk), :]
        s = jnp.zeros((tq, tk), F32)
        for h in range(IDX_HEADS):
            d = _dot_nt(iqm_ref[h], ikb)
            s = s + jnp.maximum(d, 0.0) * iw[:, h:h + 1]
        s = jnp.where(kb * tk + col0 <= row, s, -jnp.inf)
        bits = lax.bitcast_convert_type(s, I32)
        keys_ref[kb] = bits ^ ((bits >> 31) & jnp.int32(0x7FFFFFFF))
        return carry

    lax.fori_loop(0, n_kb, score_body, 0)

    def bit_body(i, cand):
        trial = cand ^ lax.shift_left(jnp.int32(1), 31 - i)

        def cnt_body(kb, acc):
            return acc + (keys_ref[kb] >= trial).astype(I32)

        acc = lax.fori_loop(0, n_kb, cnt_body, jnp.zeros((tq, tk), I32))
        cnt = jnp.sum(acc, axis=1, keepdims=True)
        return jnp.where(cnt >= topk, trial, cand)

    tau = lax.fori_loop(0, 32, bit_body, jnp.full((tq, 1), jnp.iinfo(jnp.int32).min, I32))

    m_ref[...] = jnp.full_like(m_ref, NEG_BIG)
    l_ref[...] = jnp.zeros_like(l_ref)
    acc_ref[...] = jnp.zeros_like(acc_ref)

    def att_body(kb, carry):
        start = pl.multiple_of(kb * tk, tk)
        sel = (keys_ref[kb] >= tau) & (kb * tk + col0 <= row)
        for h in range(ATT_HEADS):
            sl = slice(h * LANES, (h + 1) * LANES)
            kh = k_ref[pl.ds(start, tk), sl]
            vh = v_ref[pl.ds(start, tk), sl]
            s = _dot_nt(q_ref[:, sl], kh)
            s = jnp.where(sel, s, NEG_BIG)
            m_prev = m_ref[h]
            m_next = jnp.maximum(m_prev, jnp.max(s, axis=1, keepdims=True))
            p = jnp.where(sel, jnp.exp(s - m_next[:, :1]), 0.0)
            alpha = jnp.exp(m_prev - m_next)
            l_ref[h] = alpha * l_ref[h] + jnp.sum(p, axis=1, keepdims=True)
            acc_ref[:, sl] = alpha * acc_ref[:, sl] + _dot(p.astype(BF16), vh)
            m_ref[h] = m_next
        return carry

    lax.fori_loop(0, n_kb, att_body, 0)
    for h in range(ATT_HEADS):
        sl = slice(h * LANES, (h + 1) * LANES)
        o_ref[:, sl] = (acc_ref[:, sl] / l_ref[h]).astype(o_ref.dtype)


def _dsa(q, iq, iw, k, v, ik, bsz, seq, tq=128):
    tk = min(512, seq)
    topk = min(TOPK_MAX, seq // 4)
    w = ATT_WIDTH
    nq = seq // tq
    kern = functools.partial(_dsa_kernel, tq=tq, tk=tk, topk=topk)
    blk_q = lambda b, i: (b * nq + i, 0)
    return pl.pallas_call(
        kern,
        grid=(bsz, nq),
        in_specs=[pl.BlockSpec((tq, w), blk_q),
                  pl.BlockSpec((tq, w), blk_q),
                  pl.BlockSpec((tq, LANES), blk_q),
                  pl.BlockSpec((seq, w), lambda b, i: (b, 0)),
                  pl.BlockSpec((seq, w), lambda b, i: (b, 0)),
                  pl.BlockSpec((seq, LANES), lambda b, i: (b, 0))],
        out_specs=pl.BlockSpec((tq, w), blk_q),
        out_shape=jax.ShapeDtypeStruct((bsz * seq, w), BF16),
        scratch_shapes=[pltpu.VMEM((seq // tk, tq, tk), I32),
                        pltpu.VMEM((IDX_HEADS, tq, LANES), BF16),
                        pltpu.VMEM((ATT_HEADS, tq, LANES), F32),
                        pltpu.VMEM((ATT_HEADS, tq, LANES), F32),
                        pltpu.VMEM((tq, w), F32)],
        compiler_params=_cparams(("arbitrary", "arbitrary")),
        name="dsa",
    )(q, iq, iw, k, v, ik)


def _rwkv_prep_kernel(rr_ref, rk_ref, rv_ref, xw_ref, xa_ref, xg_ref,
                      mr_ref, mk_ref, mv_ref, mw_ref, ma_ref, mg_ref,
                      w0_ref, a0_ref, kk_ref, ka_ref, rkp_ref,
                      wup_ref, aup_ref, gup_ref, e_ref,
                      r_o, lw_o, k_o, v_o, kkn_o, bb_o, g_o, bon_o,
                      c_r, c_k, c_v, c_w, c_a, c_g, *, tm):
    first = pl.program_id(1) == 0

    def shift(y_ref, carry_ref, mu_ref):
        y = y_ref[...]

        @pl.when(first)
        def _():
            carry_ref[...] = jnp.zeros_like(carry_ref)

        prev_last = carry_ref[7:8, :]
        rolled = pltpu.roll(y, 1, 0)
        rows = lax.broadcasted_iota(I32, y.shape, 0)
        yprev = jnp.where(rows == 0, prev_last, rolled)
        carry_ref[...] = y[tm - 8:tm, :]
        return y + (yprev - y) * mu_ref[...]

    r = shift(rr_ref, c_r, mr_ref)
    k = shift(rk_ref, c_k, mk_ref)
    v = shift(rv_ref, c_v, mv_ref)
    xw = shift(xw_ref, c_w, mw_ref)
    xa = shift(xa_ref, c_a, ma_ref)
    xg = shift(xg_ref, c_g, mg_ref)

    w_raw = w0_ref[...] + _dot(jnp.tanh(xw).astype(BF16), wup_ref[...])
    z = -w_raw
    softplus = jnp.maximum(z, 0.0) + jnp.log(1.0 + jnp.exp(-jnp.abs(z)))
    lw_o[...] = -jnp.exp(-softplus - 0.5)
    a = _sigmoid(a0_ref[...] + _dot(xa.astype(BF16), aup_ref[...]))
    g_o[...] = _dot(_sigmoid(xg).astype(BF16), gup_ref[...])

    e = e_ref[...]
    kk = k * kk_ref[...]
    ss = _dot_hi(kk * kk, e)
    kk = kk / jnp.maximum(jnp.sqrt(ss), 1e-12)
    kmod = k * (1.0 + (a - 1.0) * ka_ref[...])
    r_o[...] = r
    k_o[...] = kmod
    v_o[...] = v
    kkn_o[...] = kk
    bb_o[...] = kk * a
    bon_o[...] = _dot_hi(r * kmod * rkp_ref[...], e) * v


def _rwkv_prep(proj, mus, vecs, wup, aup, gup, e, bsz, seq, tm=256):
    w = RWKV_WIDTH
    per_b = seq // tm

    def wide(off):
        return pl.BlockSpec((tm, w), lambda b, i: (b * per_b + i, off // w))

    def narrow(off, width):
        return pl.BlockSpec((tm, width), lambda b, i: (b * per_b + i, off // width))

    def const(shape):
        return pl.BlockSpec(shape, lambda b, i: (0, 0))

    out_blk = pl.BlockSpec((tm, w), lambda b, i: (b * per_b + i, 0))
    kern = functools.partial(_rwkv_prep_kernel, tm=tm)
    return pl.pallas_call(
        kern,
        grid=(bsz, per_b),
        in_specs=[wide(OFF_RR), wide(OFF_RK), wide(OFF_RV),
                  narrow(OFF_XW, LORA_PAD), narrow(OFF_XA, LORA_PAD), narrow(OFF_XG, GATE_LORA),
                  const((1, w)), const((1, w)), const((1, w)),
                  const((1, LORA_PAD)), const((1, LORA_PAD)), const((1, GATE_LORA)),
                  const((1, w)), const((1, w)), const((1, w)), const((1, w)), const((1, w)),
                  const((LORA_PAD, w)), const((LORA_PAD, w)), const((GATE_LORA, w)), const((w, w))],
        out_specs=[out_blk] * 8,
        out_shape=[jax.ShapeDtypeStruct((bsz * seq, w), F32)] * 8,
        scratch_shapes=[pltpu.VMEM((8, w), F32)] * 3
        + [pltpu.VMEM((8, LORA_PAD), F32)] * 2 + [pltpu.VMEM((8, GATE_LORA), F32)],
        compiler_params=_cparams(("arbitrary", "arbitrary")),
        name="rwkv_prep",
    )(proj, proj, proj, proj, proj, proj, *mus, *vecs, wup, aup, gup, e)


def _rwkv_core_kernel(r_ref, lw_ref, k_ref, v_ref, kk_ref, bb_ref, o_ref, z_ref):
    c = CHUNK

    @pl.when(pl.program_id(1) == 0)
    def _():
        z_ref[...] = jnp.zeros_like(z_ref)

    lw = lw_ref[...]
    tri = jnp.where(lax.broadcasted_iota(I32, (c, c), 1) <= lax.broadcasted_iota(I32, (c, c), 0),
                    1.0, 0.0).astype(BF16)
    hi = lw.astype(BF16)
    rem = lw - hi.astype(F32)
    mid = rem.astype(BF16)
    lo = (rem - mid.astype(F32)).astype(BF16)
    cum = _dot(tri, hi) + _dot(tri, mid) + _dot(tri, lo)
    p_in = jnp.exp(cum)
    p_ex = jnp.exp(cum - lw)
    p_inv = jnp.exp(-cum)
    p_end = p_in[c - 1:c, :]
    a_t = -kk_ref[...] * p_ex
    r_t = r_ref[...] * p_in
    b_h = bb_ref[...] * p_inv
    k_h = k_ref[...] * p_inv
    b_e = b_h * p_end
    k_e = k_h * p_end
    v = v_ref[...]

    n2 = 2 * c
    lane = lax.broadcasted_iota(I32, (1, LANES), 1)
    head0 = lane < RWKV_HEAD_DIM
    ri = lax.broadcasted_iota(I32, (n2, n2), 0)
    ci = lax.broadcasted_iota(I32, (n2, n2), 1)
    same = (ri >= c) == (ci >= c)
    strict = same & (ci < ri)
    incl = same & (ci <= ri)
    eye = ri == ci

    def stack(y):
        return jnp.concatenate([jnp.where(head0, y, 0.0), jnp.where(head0, 0.0, y)], axis=0)

    for p in range(RWKV_WIDTH // LANES):
        sl = slice(p * LANES, (p + 1) * LANES)
        a_s = stack(a_t[:, sl])
        r_s = stack(r_t[:, sl])
        v_s = stack(v[:, sl]).astype(BF16)
        x = jnp.concatenate([a_s, r_s], axis=0).astype(BF16)
        y = jnp.concatenate([stack(b_h[:, sl]), stack(k_h[:, sl])], axis=0).astype(BF16)
        g1 = _dot_nt(x, y)
        nmat = jnp.where(strict, g1[:n2, :n2], 0.0)
        a_ak = jnp.where(strict, g1[:n2, n2:], 0.0)
        a_rb = jnp.where(incl, g1[n2:, :n2], 0.0)
        a_rk = jnp.where(incl, g1[n2:, n2:], 0.0)
        xc = jnp.concatenate([a_s, _dot(a_ak.astype(BF16), v_s)], axis=1)
        pw = nmat.astype(BF16)
        steps = int(np.log2(c))
        for i in range(steps):
            if i + 1 < steps:
                res = _dot(pw, jnp.concatenate([pw, xc.astype(BF16)], axis=1))
                xc = xc + res[:, n2:]
                pw = res[:, :n2].astype(BF16)
            else:
                xc = xc + _dot(pw, xc.astype(BF16))
        xcb = xc.astype(BF16)
        r2 = _dot(a_rb.astype(BF16), xcb)
        q_s = r_s + r2[:, :LANES]
        ov_s = r2[:, LANES:] + _dot(a_rk.astype(BF16), v_s)
        mg = _dot_tn(stack(b_e[:, sl]).astype(BF16), xcb)
        mmat = mg[:, :LANES] + jnp.where(eye, p_end[:, sl], 0.0)
        gmat = mg[:, LANES:] + _dot_tn(stack(k_e[:, sl]).astype(BF16), v_s)
        z = z_ref[p]
        z_hi, z_lo = _split2(z)
        qm = jnp.concatenate([q_s, mmat], axis=0).astype(BF16)
        res = _dot(qm, z_hi) + _dot(qm, z_lo)
        o_s = res[:n2] + ov_s
        o_ref[:, sl] = o_s[:c] + o_s[c:]
        z_ref[p] = res[n2:] + gmat


def _rwkv_core(r, lw, k, v, kk, bb, bsz, seq):
    c = CHUNK
    w = RWKV_WIDTH
    per_b = seq // c
    blk = pl.BlockSpec((c, w), lambda b, i: (b * per_b + i, 0))
    return pl.pallas_call(
        _rwkv_core_kernel,
        grid=(bsz, per_b),
        in_specs=[blk] * 6,
        out_specs=blk,
        out_shape=jax.ShapeDtypeStruct((bsz * seq, w), F32),
        scratch_shapes=[pltpu.VMEM((w // LANES, LANES, LANES), F32)],
        compiler_params=_cparams(("arbitrary", "arbitrary")),
        name="rwkv_core",
    )(r, lw, k, v, kk, bb)


def _rwkv_post_kernel(o_ref, bon_ref, g_ref, lg_ref, lb_ref, e_ref, out_ref):
    o = o_ref[...]
    e = e_ref[...]
    inv_n = 1.0 / RWKV_HEAD_DIM
    mean = _dot_hi(o, e) * inv_n
    d = o - mean
    var = _dot_hi(d * d, e) * inv_n
    y = d * lax.rsqrt(var + LNX_EPS) * lg_ref[...] + lb_ref[...]
    out_ref[...] = ((y + bon_ref[...]) * g_ref[...]).astype(out_ref.dtype)


def _rwkv_post(o, bon, g, lg, lb, e, tm=512):
    m, w = o.shape
    blk = pl.BlockSpec((tm, w), lambda i: (i, 0))
    vec = pl.BlockSpec((1, w), lambda i: (0, 0))
    return pl.pallas_call(
        _rwkv_post_kernel,
        grid=(m // tm,),
        in_specs=[blk, blk, blk, vec, vec, pl.BlockSpec((w, w), lambda i: (0, 0))],
        out_specs=blk,
        out_shape=jax.ShapeDtypeStruct((m, w), BF16),
        compiler_params=_cparams(("arbitrary",)),
        name="rwkv_post",
    )(o, bon, g, lg, lb, e)


def _regroup_in_proj(w_in):
    d = w_in.shape[0]
    a = ATT_WIDTH
    o_q, o_k, o_v, o_iq = 0, a, 2 * a, 3 * a
    o_ik = 4 * a
    o_iw = o_ik + IDX_HEAD_DIM
    base = o_iw + IDX_HEADS
    o_rr, o_rk, o_rv = base, base + RWKV_WIDTH, base + 2 * RWKV_WIDTH
    o_xw = base + 3 * RWKV_WIDTH
    o_xa = o_xw + DECAY_LORA
    o_xg = o_xa + AAA_LORA

    def z(n):
        return jnp.zeros((d, n), w_in.dtype)

    ik = w_in[:, o_ik:o_ik + IDX_HEAD_DIM]
    parts = [w_in[:, o_q:o_q + 4 * a],
             w_in[:, o_rr:o_rr + 3 * RWKV_WIDTH],
             ik, ik,
             w_in[:, o_iw:o_iw + IDX_HEADS], z(LANES - IDX_HEADS),
             w_in[:, o_xw:o_xw + DECAY_LORA], z(LORA_PAD - DECAY_LORA),
             w_in[:, o_xa:o_xa + AAA_LORA], z(LORA_PAD - AAA_LORA),
             w_in[:, o_xg:o_xg + GATE_LORA],
             z(IN_COLS_PAD - (OFF_XG + GATE_LORA))]
    return jnp.concatenate(parts, axis=1).astype(BF16)


def _rope_freqs(head_dim):
    rot = head_dim // ROPE_FRACTION
    half = rot // 2
    inv = ROPE_THETA ** (-jnp.arange(half, dtype=F32) / half)
    per_head = jnp.concatenate([inv, inv, jnp.zeros((head_dim - rot,), F32)])
    return jnp.tile(per_head, LANES // head_dim).reshape(1, LANES)


def kernel(x, c, positions, w_ada, b_ada, norm1_g, w_in, q_norm_g, k_norm_g, rwkv_mu, rwkv_w0,
           rwkv_w_up, rwkv_a0, rwkv_a_up, rwkv_g_up, rwkv_k_k, rwkv_k_a, rwkv_r_k, rwkv_lnx_g,
           rwkv_lnx_b, w_out, norm2_g, w_ffn_gate, w_ffn_up, w_ffn_down):
    bsz, seq, d = x.shape
    depth = w_ada.shape[0]
    m = bsz * seq
    pos2 = positions.reshape(m, 1)
    fa = _rope_freqs(ATT_HEAD_DIM)
    fi = _rope_freqs(IDX_HEAD_DIM)
    hd = RWKV_HEAD_DIM
    e = jnp.kron(jnp.eye(RWKV_HEADS, dtype=F32), jnp.ones((hd, hd), F32)).astype(BF16)
    x2 = x.reshape(m, d)

    for l in range(depth):
        mod = _adaln(c, w_ada[l], b_ada[l])
        mod3 = mod.reshape(bsz * 6, 1, d)

        proj = _in_proj(x2, norm1_g[l].reshape(1, d), mod3, _regroup_in_proj(w_in[l]), seq)

        q, k, v, iq, ik, iw = _att_prep(proj, pos2, q_norm_g[l].reshape(1, -1),
                                        k_norm_g[l].reshape(1, -1), fa, fi)
        att = _dsa(q, iq, iw, k, v, ik, bsz, seq)

        mu = rwkv_mu[l]
        w3 = 3 * RWKV_WIDTH

        def padded(vec, width):
            return jnp.zeros((1, width), F32).at[0, :vec.shape[0]].set(vec)

        mus = [mu[0:RWKV_WIDTH].reshape(1, -1), mu[RWKV_WIDTH:2 * RWKV_WIDTH].reshape(1, -1),
               mu[2 * RWKV_WIDTH:w3].reshape(1, -1),
               padded(mu[w3:w3 + DECAY_LORA], LORA_PAD),
               padded(mu[w3 + DECAY_LORA:w3 + DECAY_LORA + AAA_LORA], LORA_PAD),
               mu[w3 + DECAY_LORA + AAA_LORA:].reshape(1, -1)]
        vecs = [rwkv_w0[l].reshape(1, -1), rwkv_a0[l].reshape(1, -1), rwkv_k_k[l].reshape(1, -1),
                rwkv_k_a[l].reshape(1, -1), rwkv_r_k[l].reshape(1, -1)]

        def pad_rows(wm):
            return jnp.zeros((LORA_PAD, wm.shape[1]), F32).at[:wm.shape[0]].set(wm).astype(BF16)

        r, lw, km, vv, kk, bb, g, bon = _rwkv_prep(
            proj, mus, vecs, pad_rows(rwkv_w_up[l]), pad_rows(rwkv_a_up[l]),
            rwkv_g_up[l].astype(BF16), e, bsz, seq)
        o = _rwkv_core(r, lw, km, vv, kk, bb, bsz, seq)
        rw = _rwkv_post(o, bon, g, rwkv_lnx_g[l].reshape(1, -1), rwkv_lnx_b[l].reshape(1, -1), e)

        wo = w_out[l].astype(BF16)
        x2 = _out_proj(att, rw, wo[:ATT_WIDTH], wo[ATT_WIDTH:], x2, mod3, seq)

        hglu = _ffn_glu(x2, norm2_g[l].reshape(1, d), mod3, w_ffn_gate[l].astype(BF16),
                        w_ffn_up[l].astype(BF16), seq)
        x2 = _ffn_down(hglu, w_ffn_down[l].astype(BF16), x2, mod3, seq)
    return x2.reshape(bsz, seq, d)
```

```python
import functools

import jax
import jax.numpy as jnp
import numpy as np
from jax import lax
from jax.experimental import pallas as pl
from jax.experimental.pallas import tpu as pltpu

F32 = jnp.float32
BF16 = jnp.bfloat16
I32 = jnp.int32

D_MODEL = 2048
ATT_HEADS = 8
ATT_HEAD_DIM = 128
ATT_WIDTH = ATT_HEADS * ATT_HEAD_DIM
RWKV_WIDTH = D_MODEL - ATT_WIDTH
RWKV_HEAD_DIM = 64
RWKV_HEADS = RWKV_WIDTH // RWKV_HEAD_DIM
IDX_HEADS = 16
IDX_HEAD_DIM = 64
TOPK_MAX = 256
ROPE_THETA = 500000.0
ROPE_FRACTION = 4
DECAY_LORA = 96
AAA_LORA = 96
GATE_LORA = 256
NORM_EPS = 1e-6
LNX_EPS = 64e-5

LANES = 128
LORA_PAD = 128
IN_COLS_PAD = 8192
OFF_Q, OFF_K, OFF_V, OFF_IQ = 0, 1024, 2048, 3072
OFF_RR, OFF_RK, OFF_RV = 4096, 5120, 6144
OFF_IK, OFF_IW, OFF_XW, OFF_XA, OFF_XG = 7168, 7296, 7424, 7552, 7680

LOG2E = 1.4426950408889634
V_ROWS = 144
KEY_TILE = 512
CHUNK = 64
VMEM_LIMIT = 56 * 1024 * 1024


def _cparams(sem):
    return pltpu.CompilerParams(dimension_semantics=sem, vmem_limit_bytes=VMEM_LIMIT)


def _dot(a, b):
    return jnp.dot(a, b, preferred_element_type=F32)


def _dot_nt(a, b):
    return lax.dot_general(a, b, (((1,), (1,)), ((), ())), preferred_element_type=F32)


def _dot_tn(a, b):
    return lax.dot_general(a, b, (((0,), (0,)), ((), ())), preferred_element_type=F32)


def _split2(x):
    hi = x.astype(BF16)
    lo = (x - hi.astype(F32)).astype(BF16)
    return hi, lo


def _dot_hi(x, w):
    hi, lo = _split2(x)
    return _dot(hi, w) + _dot(lo, w)


def _sigmoid(x):
    return 1.0 / (1.0 + jnp.exp(-x))


def _adaln_kernel(c_ref, w_ref, b_ref, o_ref):
    c = c_ref[...]
    ca = c * _sigmoid(c)
    o_ref[...] = _dot(ca.astype(BF16), w_ref[...].astype(BF16)) + b_ref[...]


def _adaln(c, w, b):
    bsz, d = c.shape
    n = w.shape[1]
    rows = 8
    cp = jnp.zeros((rows, d), F32).at[:bsz].set(c)
    tn = 1024
    out = pl.pallas_call(
        _adaln_kernel,
        grid=(n // tn,),
        in_specs=[pl.BlockSpec((rows, d), lambda j: (0, 0)),
                  pl.BlockSpec((d, tn), lambda j: (0, j)),
                  pl.BlockSpec((1, tn), lambda j: (0, j))],
        out_specs=pl.BlockSpec((rows, tn), lambda j: (0, j)),
        out_shape=jax.ShapeDtypeStruct((rows, n), F32),
        compiler_params=_cparams(("arbitrary",)),
        name="adaln",
    )(cp, w, b.reshape(1, n))
    return out[:bsz]


def _norm_mod(x, g, sc, sh):
    ms = jnp.mean(x * x, axis=-1, keepdims=True)
    y = x * lax.rsqrt(ms + NORM_EPS)
    y = y * g
    return y * (1.0 + sc) + sh


def _in_proj_kernel(x_ref, g_ref, sc_ref, sh_ref, w_ref, o_ref, h_ref):
    @pl.when(pl.program_id(1) == 0)
    def _():
        h_ref[...] = _norm_mod(x_ref[...], g_ref[...], sc_ref[...], sh_ref[...]).astype(BF16)

    o_ref[...] = _dot(h_ref[...], w_ref[...])


def _in_proj(x2, g, mod3, w, seq, tm=512, tn=1024):
    m, d = x2.shape
    n = w.shape[1]
    per_b = seq // tm
    return pl.pallas_call(
        _in_proj_kernel,
        grid=(m // tm, n // tn),
        in_specs=[pl.BlockSpec((tm, d), lambda i, j: (i, 0)),
                  pl.BlockSpec((1, d), lambda i, j: (0, 0)),
                  pl.BlockSpec((None, 1, d), lambda i, j: ((i // per_b) * 6 + 1, 0, 0)),
                  pl.BlockSpec((None, 1, d), lambda i, j: ((i // per_b) * 6 + 0, 0, 0)),
                  pl.BlockSpec((d, tn), lambda i, j: (0, j))],
        out_specs=pl.BlockSpec((tm, tn), lambda i, j: (i, j)),
        out_shape=jax.ShapeDtypeStruct((m, n), F32),
        scratch_shapes=[pltpu.VMEM((tm, d), BF16)],
        compiler_params=_cparams(("arbitrary", "arbitrary")),
        name="in_proj",
    )(x2, g, mod3, mod3, w)


def _ffn_glu_kernel(x_ref, g_ref, sc_ref, sh_ref, wg_ref, wu_ref, o_ref, h_ref):
    @pl.when(pl.program_id(1) == 0)
    def _():
        h_ref[...] = _norm_mod(x_ref[...], g_ref[...], sc_ref[...], sh_ref[...]).astype(BF16)

    h = h_ref[...]
    a = _dot(h, wg_ref[...])
    u = _dot(h, wu_ref[...])
    o_ref[...] = (a * _sigmoid(a) * u).astype(o_ref.dtype)


def _ffn_glu(x2, g, mod3, wg, wu, seq, tm=512, tn=512):
    m, d = x2.shape
    n = wg.shape[1]
    per_b = seq // tm
    return pl.pallas_call(
        _ffn_glu_kernel,
        grid=(m // tm, n // tn),
        in_specs=[pl.BlockSpec((tm, d), lambda i, j: (i, 0)),
                  pl.BlockSpec((1, d), lambda i, j: (0, 0)),
                  pl.BlockSpec((None, 1, d), lambda i, j: ((i // per_b) * 6 + 4, 0, 0)),
                  pl.BlockSpec((None, 1, d), lambda i, j: ((i // per_b) * 6 + 3, 0, 0)),
                  pl.BlockSpec((d, tn), lambda i, j: (0, j)),
                  pl.BlockSpec((d, tn), lambda i, j: (0, j))],
        out_specs=pl.BlockSpec((tm, tn), lambda i, j: (i, j)),
        out_shape=jax.ShapeDtypeStruct((m, n), BF16),
        scratch_shapes=[pltpu.VMEM((tm, d), BF16)],
        compiler_params=_cparams(("arbitrary", "arbitrary")),
        name="ffn_glu",
    )(x2, g, mod3, mod3, wg, wu)


def _out_proj_kernel(a_ref, r_ref, wa_ref, wr_ref, x_ref, gt_ref, o_ref):
    mixed = _dot(a_ref[...], wa_ref[...]) + _dot(r_ref[...], wr_ref[...])
    o_ref[...] = x_ref[...] + gt_ref[...] * mixed


def _out_proj(att, rwkv, wa, wr, x2, mod3, seq, tm=512, tn=1024):
    m, ka = att.shape
    kr = rwkv.shape[1]
    n = wa.shape[1]
    per_b = seq // tm
    return pl.pallas_call(
        _out_proj_kernel,
        grid=(m // tm, n // tn),
        in_specs=[pl.BlockSpec((tm, ka), lambda i, j: (i, 0)),
                  pl.BlockSpec((tm, kr), lambda i, j: (i, 0)),
                  pl.BlockSpec((ka, tn), lambda i, j: (0, j)),
                  pl.BlockSpec((kr, tn), lambda i, j: (0, j)),
                  pl.BlockSpec((tm, tn), lambda i, j: (i, j)),
                  pl.BlockSpec((None, 1, tn), lambda i, j: ((i // per_b) * 6 + 2, 0, j))],
        out_specs=pl.BlockSpec((tm, tn), lambda i, j: (i, j)),
        out_shape=jax.ShapeDtypeStruct((m, n), F32),
        compiler_params=_cparams(("arbitrary", "arbitrary")),
        name="out_proj",
    )(att, rwkv, wa, wr, x2, mod3)


def _ffn_down_kernel(h_ref, w_ref, x_ref, gt_ref, o_ref, acc_ref):
    kk = pl.program_id(2)

    @pl.when(kk == 0)
    def _():
        acc_ref[...] = jnp.zeros_like(acc_ref)

    acc_ref[...] += _dot(h_ref[...], w_ref[...])

    @pl.when(kk == pl.num_programs(2) - 1)
    def _():
        o_ref[...] = x_ref[...] + gt_ref[...] * acc_ref[...]


def _ffn_down(h, w, x2, mod3, seq, tm=512, tn=1024, tk=512):
    m, kdim = h.shape
    n = w.shape[1]
    per_b = seq // tm
    return pl.pallas_call(
        _ffn_down_kernel,
        grid=(m // tm, n // tn, kdim // tk),
        in_specs=[pl.BlockSpec((tm, tk), lambda i, j, k: (i, k)),
                  pl.BlockSpec((tk, tn), lambda i, j, k: (k, j)),
                  pl.BlockSpec((tm, tn), lambda i, j, k: (i, j)),
                  pl.BlockSpec((None, 1, tn), lambda i, j, k: ((i // per_b) * 6 + 5, 0, j))],
        out_specs=pl.BlockSpec((tm, tn), lambda i, j, k: (i, j)),
        out_shape=jax.ShapeDtypeStruct((m, n), F32),
        scratch_shapes=[pltpu.VMEM((tm, tn), F32)],
        compiler_params=_cparams(("arbitrary", "arbitrary", "arbitrary")),
        name="ffn_down",
    )(h, w, x2, mod3)


def _rope(x, cos, sin_lo, sin_hi, half):
    return (x * cos + pltpu.roll(x, LANES - half, 1) * sin_lo
            + pltpu.roll(x, half, 1) * sin_hi)


def _att_prep_kernel(pos_ref, q_ref, k_ref, v_ref, iq_ref, ik_ref, iw_ref,
                     qg_ref, kg_ref, fa_ref, fi_ref,
                     qo_ref, ko_ref, vo_ref, iqo_ref, iko_ref, iwo_ref):
    pos = pos_ref[...].astype(F32)
    lane = lax.broadcasted_iota(I32, (1, LANES), 1)

    half_a = ATT_HEAD_DIM // ROPE_FRACTION // 2
    ang = pos * fa_ref[...]
    cos_a, sin_a = jnp.cos(ang), jnp.sin(ang)
    lo_a = jnp.where(lane < half_a, -sin_a, 0.0)
    hi_a = jnp.where((lane >= half_a) & (lane < 2 * half_a), sin_a, 0.0)

    half_i = IDX_HEAD_DIM // ROPE_FRACTION // 2
    lane_i = lane & (IDX_HEAD_DIM - 1)
    ang = pos * fi_ref[...]
    cos_i, sin_i = jnp.cos(ang), jnp.sin(ang)
    lo_i = jnp.where(lane_i < half_i, -sin_i, 0.0)
    hi_i = jnp.where((lane_i >= half_i) & (lane_i < 2 * half_i), sin_i, 0.0)

    def head_norm(x, g):
        ms = jnp.mean(x * x, axis=-1, keepdims=True)
        return x * lax.rsqrt(ms + NORM_EPS) * g

    qg, kg = qg_ref[...], kg_ref[...]
    tm = q_ref.shape[0]
    pad_row = lax.broadcasted_iota(I32, (V_ROWS - LANES, tm), 0)
    ones_rows = jnp.where(pad_row == 0, 1.0, 0.0).astype(BF16)
    q_scale = (ATT_HEAD_DIM ** -0.5) * LOG2E
    for h in range(ATT_HEADS):
        sl = slice(h * LANES, (h + 1) * LANES)
        qh = _rope(head_norm(q_ref[:, sl], qg), cos_a, lo_a, hi_a, half_a)
        qo_ref[h] = (qh * q_scale).T.astype(BF16)
        kh = _rope(head_norm(k_ref[:, sl], kg), cos_a, lo_a, hi_a, half_a)
        ko_ref[:, sl] = kh.astype(BF16)
        vo_ref[h, 0:LANES, :] = v_ref[:, sl].T.astype(BF16)
        vo_ref[h, LANES:V_ROWS, :] = ones_rows
    for h in range(IDX_HEADS * IDX_HEAD_DIM // LANES):
        sl = slice(h * LANES, (h + 1) * LANES)
        ih = _rope(iq_ref[:, sl], cos_i, lo_i, hi_i, half_i)
        iqo_ref[h] = (ih * (IDX_HEAD_DIM ** -0.5)).T.astype(BF16)
    iko_ref[...] = _rope(ik_ref[...], cos_i, lo_i, hi_i, half_i).astype(BF16)
    iwo_ref[...] = (iw_ref[...] * (IDX_HEADS ** -0.5)).T[:IDX_HEADS, :]


def _att_prep(proj, pos2, qg, kg, fa, fi, tm):
    m = proj.shape[0]
    w = ATT_WIDTH
    npair = IDX_HEADS * IDX_HEAD_DIM // LANES

    def wide(off):
        return pl.BlockSpec((tm, w), lambda i: (i, off // w))

    def narrow(off):
        return pl.BlockSpec((tm, LANES), lambda i: (i, off // LANES))

    const = pl.BlockSpec((1, LANES), lambda i: (0, 0))
    return pl.pallas_call(
        _att_prep_kernel,
        grid=(m // tm,),
        in_specs=[pl.BlockSpec((tm, 1), lambda i: (i, 0)),
                  wide(OFF_Q), wide(OFF_K), wide(OFF_V), wide(OFF_IQ),
                  narrow(OFF_IK), narrow(OFF_IW), const, const, const, const],
        out_specs=[pl.BlockSpec((ATT_HEADS, LANES, tm), lambda i: (0, 0, i)),
                   pl.BlockSpec((tm, w), lambda i: (i, 0)),
                   pl.BlockSpec((ATT_HEADS, None, V_ROWS, tm), lambda i: (0, i, 0, 0)),
                   pl.BlockSpec((npair, LANES, tm), lambda i: (0, 0, i)),
                   pl.BlockSpec((tm, LANES), lambda i: (i, 0)),
                   pl.BlockSpec((IDX_HEADS, tm), lambda i: (0, i))],
        out_shape=[jax.ShapeDtypeStruct((ATT_HEADS, LANES, m), BF16),
                   jax.ShapeDtypeStruct((m, w), BF16),
                   jax.ShapeDtypeStruct((ATT_HEADS, m // tm, V_ROWS, tm), BF16),
                   jax.ShapeDtypeStruct((npair, LANES, m), BF16),
                   jax.ShapeDtypeStruct((m, LANES), BF16),
                   jax.ShapeDtypeStruct((IDX_HEADS, m), F32)],
        compiler_params=_cparams(("arbitrary",)),
        name="att_prep",
    )(pos2, proj, proj, proj, proj, proj, proj, qg, kg, fa, fi)


NEG_BIG = -1e30


INT_MIN = -2 ** 31
MAGNITUDE_BITS = 0x7FFFFFFF
KEY_NEG_INF = -2139095041
COUNT_ROWS = 32


def _key_to_float(key):
    bits = key ^ ((key >> 31) & jnp.int32(MAGNITUDE_BITS))
    return lax.bitcast_convert_type(bits, F32)


def _dsa_kernel(qt_ref, iqt_ref, iwt_ref, k_ref, vt_ref, ik_ref, o_ref,
                sc_ref, m_ref, acc_ref, *, tq, tk, topk):
    qi = pl.program_id(1)
    n_kb = (qi * tq + tq - 1) // tk + 1
    key0 = lax.broadcasted_iota(I32, (tk, tq), 0)
    qidx = qi * tq + lax.broadcasted_iota(I32, (tk, tq), 1)
    iw = iwt_ref[...]

    def score_body(kb, carry):
        start = pl.multiple_of(kb * tk, tk)
        ikb = ik_ref[pl.ds(start, tk), 0:IDX_HEAD_DIM]
        s = jnp.zeros((tk, tq), F32)
        for h in range(IDX_HEADS):
            off = (h % 2) * IDX_HEAD_DIM
            d = _dot(ikb, iqt_ref[h // 2, off:off + IDX_HEAD_DIM, :])
            s = s + jnp.maximum(d, 0.0) * iw[h:h + 1, :]
        sc_ref[kb] = jnp.where(kb * tk + key0 <= qidx, s, -jnp.inf)
        return carry

    lax.fori_loop(0, n_kb, score_body, 0)

    def bit_body(i, cand):
        trial = cand ^ lax.shift_left(jnp.int32(1), 31 - i)
        trial_f = _key_to_float(trial)

        def cnt_body(kb, acc):
            ge = jnp.where(sc_ref[kb] >= trial_f, 1.0, 0.0)
            return acc + jnp.sum(ge.reshape(tk // COUNT_ROWS, COUNT_ROWS, tq), axis=0)

        acc = lax.fori_loop(0, n_kb, cnt_body, jnp.zeros((COUNT_ROWS, tq), F32))
        cnt = jnp.sum(acc, axis=0, keepdims=True)
        return jnp.where(cnt >= topk, trial, cand)

    cand = lax.fori_loop(0, 32, bit_body, jnp.full((1, tq), INT_MIN, I32))
    tau = _key_to_float(jnp.maximum(cand, jnp.int32(KEY_NEG_INF)))

    m_ref[...] = jnp.full_like(m_ref, NEG_BIG)
    acc_ref[...] = jnp.zeros_like(acc_ref)

    def att_body(kb, carry):
        start = pl.multiple_of(kb * tk, tk)
        sel = (sc_ref[kb] >= tau) & (kb * tk + key0 <= qidx)
        bias = jnp.where(sel, 0.0, -jnp.inf)
        for h in range(ATT_HEADS):
            kh = k_ref[pl.ds(start, tk), h * LANES:(h + 1) * LANES]
            s = _dot(kh, qt_ref[h]) + bias
            m_prev = m_ref[h]
            m_next = jnp.maximum(m_prev, jnp.max(s, axis=0, keepdims=True))
            p = jnp.exp2(s - m_next)
            alpha = jnp.exp2(m_prev - m_next)
            acc_ref[h] = alpha * acc_ref[h] + _dot(vt_ref[h, kb], p.astype(BF16))
            m_ref[h] = m_next
        return carry

    lax.fori_loop(0, n_kb, att_body, 0)
    for h in range(ATT_HEADS):
        out = acc_ref[h, 0:LANES, :] / acc_ref[h, LANES:LANES + 1, :]
        o_ref[:, h * LANES:(h + 1) * LANES] = out.T.astype(o_ref.dtype)


def _dsa(qt, iqt, iwt, k, vt, ik, bsz, seq, tk, tq=256):
    tq = min(tq, seq)
    topk = min(TOPK_MAX, seq // 4)
    w = ATT_WIDTH
    nq = seq // tq
    nkb = seq // tk
    npair = iqt.shape[0]
    kern = functools.partial(_dsa_kernel, tq=tq, tk=tk, topk=topk)
    return pl.pallas_call(
        kern,
        grid=(bsz, nq),
        in_specs=[pl.BlockSpec((ATT_HEADS, LANES, tq), lambda b, i: (0, 0, b * nq + i)),
                  pl.BlockSpec((npair, LANES, tq), lambda b, i: (0, 0, b * nq + i)),
                  pl.BlockSpec((IDX_HEADS, tq), lambda b, i: (0, b * nq + i)),
                  pl.BlockSpec((seq, w), lambda b, i: (b, 0)),
                  pl.BlockSpec((ATT_HEADS, nkb, V_ROWS, tk), lambda b, i: (0, b, 0, 0)),
                  pl.BlockSpec((seq, LANES), lambda b, i: (b, 0))],
        out_specs=pl.BlockSpec((tq, w), lambda b, i: (b * nq + i, 0)),
        out_shape=jax.ShapeDtypeStruct((bsz * seq, w), BF16),
        scratch_shapes=[pltpu.VMEM((nkb, tk, tq), F32),
                        pltpu.VMEM((ATT_HEADS, 1, tq), F32),
                        pltpu.VMEM((ATT_HEADS, V_ROWS, tq), F32)],
        compiler_params=_cparams(("arbitrary", "arbitrary")),
        name="dsa",
    )(qt, iqt, iwt, k, vt, ik)


def _rwkv_prep_kernel(rr_ref, rk_ref, rv_ref, xw_ref, xa_ref, xg_ref,
                      mr_ref, mk_ref, mv_ref, mw_ref, ma_ref, mg_ref,
                      w0_ref, a0_ref, kk_ref, ka_ref, rkp_ref,
                      wup_ref, aup_ref, gup_ref, e_ref,
                      r_o, lw_o, k_o, v_o, kkn_o, bb_o, g_o, bon_o,
                      c_r, c_k, c_v, c_w, c_a, c_g, *, tm):
    first = pl.program_id(1) == 0

    def shift(y_ref, carry_ref, mu_ref):
        y = y_ref[...]

        @pl.when(first)
        def _():
            carry_ref[...] = jnp.zeros_like(carry_ref)

        prev_last = carry_ref[7:8, :]
        rolled = pltpu.roll(y, 1, 0)
        rows = lax.broadcasted_iota(I32, y.shape, 0)
        yprev = jnp.where(rows == 0, prev_last, rolled)
        carry_ref[...] = y[tm - 8:tm, :]
        return y + (yprev - y) * mu_ref[...]

    r = shift(rr_ref, c_r, mr_ref)
    k = shift(rk_ref, c_k, mk_ref)
    v = shift(rv_ref, c_v, mv_ref)
    xw = shift(xw_ref, c_w, mw_ref)
    xa = shift(xa_ref, c_a, ma_ref)
    xg = shift(xg_ref, c_g, mg_ref)

    w_raw = w0_ref[...] + _dot(jnp.tanh(xw).astype(BF16), wup_ref[...])
    z = -w_raw
    softplus = jnp.maximum(z, 0.0) + jnp.log(1.0 + jnp.exp(-jnp.abs(z)))
    lw_o[...] = -jnp.exp(-softplus - 0.5)
    a = _sigmoid(a0_ref[...] + _dot(xa.astype(BF16), aup_ref[...]))
    g_o[...] = _dot(_sigmoid(xg).astype(BF16), gup_ref[...])

    e = e_ref[...]
    kk = k * kk_ref[...]
    ss = _dot_hi(kk * kk, e)
    kk = kk / jnp.maximum(jnp.sqrt(ss), 1e-12)
    kmod = k * (1.0 + (a - 1.0) * ka_ref[...])
    r_o[...] = r
    k_o[...] = kmod
    v_o[...] = v
    kkn_o[...] = kk
    bb_o[...] = kk * a
    bon_o[...] = _dot_hi(r * kmod * rkp_ref[...], e) * v


def _rwkv_prep(proj, mus, vecs, wup, aup, gup, e, bsz, seq, tm=256):
    w = RWKV_WIDTH
    per_b = seq // tm

    def wide(off):
        return pl.BlockSpec((tm, w), lambda b, i: (b * per_b + i, off // w))

    def narrow(off, width):
        return pl.BlockSpec((tm, width), lambda b, i: (b * per_b + i, off // width))

    def const(shape):
        return pl.BlockSpec(shape, lambda b, i: (0, 0))

    out_blk = pl.BlockSpec((tm, w), lambda b, i: (b * per_b + i, 0))
    kern = functools.partial(_rwkv_prep_kernel, tm=tm)
    return pl.pallas_call(
        kern,
        grid=(bsz, per_b),
        in_specs=[wide(OFF_RR), wide(OFF_RK), wide(OFF_RV),
                  narrow(OFF_XW, LORA_PAD), narrow(OFF_XA, LORA_PAD), narrow(OFF_XG, GATE_LORA),
                  const((1, w)), const((1, w)), const((1, w)),
                  const((1, LORA_PAD)), const((1, LORA_PAD)), const((1, GATE_LORA)),
                  const((1, w)), const((1, w)), const((1, w)), const((1, w)), const((1, w)),
                  const((LORA_PAD, w)), const((LORA_PAD, w)), const((GATE_LORA, w)), const((w, w))],
        out_specs=[out_blk] * 8,
        out_shape=[jax.ShapeDtypeStruct((bsz * seq, w), F32)] * 8,
        scratch_shapes=[pltpu.VMEM((8, w), F32)] * 3
        + [pltpu.VMEM((8, LORA_PAD), F32)] * 2 + [pltpu.VMEM((8, GATE_LORA), F32)],
        compiler_params=_cparams(("arbitrary", "arbitrary")),
        name="rwkv_prep",
    )(proj, proj, proj, proj, proj, proj, *mus, *vecs, wup, aup, gup, e)


def _rwkv_core_kernel(r_ref, lw_ref, k_ref, v_ref, kk_ref, bb_ref, o_ref, z_ref):
    c = CHUNK

    @pl.when(pl.program_id(1) == 0)
    def _():
        z_ref[...] = jnp.zeros_like(z_ref)

    lw = lw_ref[...]
    tri = jnp.where(lax.broadcasted_iota(I32, (c, c), 1) <= lax.broadcasted_iota(I32, (c, c), 0),
                    1.0, 0.0).astype(BF16)
    hi = lw.astype(BF16)
    rem = lw - hi.astype(F32)
    mid = rem.astype(BF16)
    lo = (rem - mid.astype(F32)).astype(BF16)
    cum = _dot(tri, hi) + _dot(tri, mid) + _dot(tri, lo)
    p_in = jnp.exp(cum)
    p_ex = jnp.exp(cum - lw)
    p_inv = jnp.exp(-cum)
    p_end = p_in[c - 1:c, :]
    a_t = -kk_ref[...] * p_ex
    r_t = r_ref[...] * p_in
    b_h = bb_ref[...] * p_inv
    k_h = k_ref[...] * p_inv
    b_e = b_h * p_end
    k_e = k_h * p_end
    v = v_ref[...]

    n2 = 2 * c
    lane = lax.broadcasted_iota(I32, (1, LANES), 1)
    head0 = lane < RWKV_HEAD_DIM
    ri = lax.broadcasted_iota(I32, (n2, n2), 0)
    ci = lax.broadcasted_iota(I32, (n2, n2), 1)
    same = (ri >= c) == (ci >= c)
    strict = same & (ci < ri)
    incl = same & (ci <= ri)
    eye = ri == ci

    def stack(y):
        return jnp.concatenate([jnp.where(head0, y, 0.0), jnp.where(head0, 0.0, y)], axis=0)

    for p in range(RWKV_WIDTH // LANES):
        sl = slice(p * LANES, (p + 1) * LANES)
        a_s = stack(a_t[:, sl])
        r_s = stack(r_t[:, sl])
        v_s = stack(v[:, sl]).astype(BF16)
        x = jnp.concatenate([a_s, r_s], axis=0).astype(BF16)
        y = jnp.concatenate([stack(b_h[:, sl]), stack(k_h[:, sl])], axis=0).astype(BF16)
        g1 = _dot_nt(x, y)
        nmat = jnp.where(strict, g1[:n2, :n2], 0.0)
        a_ak = jnp.where(strict, g1[:n2, n2:], 0.0)
        a_rb = jnp.where(incl, g1[n2:, :n2], 0.0)
        a_rk = jnp.where(incl, g1[n2:, n2:], 0.0)
        xc = jnp.concatenate([a_s, _dot(a_ak.astype(BF16), v_s)], axis=1)
        pw = nmat.astype(BF16)
        steps = int(np.log2(c))
        for i in range(steps):
            if i + 1 < steps:
                res = _dot(pw, jnp.concatenate([pw, xc.astype(BF16)], axis=1))
                xc = xc + res[:, n2:]
                pw = res[:, :n2].astype(BF16)
            else:
                xc = xc + _dot(pw, xc.astype(BF16))
        xcb = xc.astype(BF16)
        r2 = _dot(a_rb.astype(BF16), xcb)
        q_s = r_s + r2[:, :LANES]
        ov_s = r2[:, LANES:] + _dot(a_rk.astype(BF16), v_s)
        mg = _dot_tn(stack(b_e[:, sl]).astype(BF16), xcb)
        mmat = mg[:, :LANES] + jnp.where(eye, p_end[:, sl], 0.0)
        gmat = mg[:, LANES:] + _dot_tn(stack(k_e[:, sl]).astype(BF16), v_s)
        z = z_ref[p]
        z_hi, z_lo = _split2(z)
        qm = jnp.concatenate([q_s, mmat], axis=0).astype(BF16)
        res = _dot(qm, z_hi) + _dot(qm, z_lo)
        o_s = res[:n2] + ov_s
        o_ref[:, sl] = o_s[:c] + o_s[c:]
        z_ref[p] = res[n2:] + gmat


def _rwkv_core(r, lw, k, v, kk, bb, bsz, seq):
    c = CHUNK
    w = RWKV_WIDTH
    per_b = seq // c
    blk = pl.BlockSpec((c, w), lambda b, i: (b * per_b + i, 0))
    return pl.pallas_call(
        _rwkv_core_kernel,
        grid=(bsz, per_b),
        in_specs=[blk] * 6,
        out_specs=blk,
        out_shape=jax.ShapeDtypeStruct((bsz * seq, w), F32),
        scratch_shapes=[pltpu.VMEM((w // LANES, LANES, LANES), F32)],
        compiler_params=_cparams(("arbitrary", "arbitrary")),
        name="rwkv_core",
    )(r, lw, k, v, kk, bb)


def _rwkv_post_kernel(o_ref, bon_ref, g_ref, lg_ref, lb_ref, e_ref, out_ref):
    o = o_ref[...]
    e = e_ref[...]
    inv_n = 1.0 / RWKV_HEAD_DIM
    mean = _dot_hi(o, e) * inv_n
    d = o - mean
    var = _dot_hi(d * d, e) * inv_n
    y = d * lax.rsqrt(var + LNX_EPS) * lg_ref[...] + lb_ref[...]
    out_ref[...] = ((y + bon_ref[...]) * g_ref[...]).astype(out_ref.dtype)


def _rwkv_post(o, bon, g, lg, lb, e, tm=512):
    m, w = o.shape
    blk = pl.BlockSpec((tm, w), lambda i: (i, 0))
    vec = pl.BlockSpec((1, w), lambda i: (0, 0))
    return pl.pallas_call(
        _rwkv_post_kernel,
        grid=(m // tm,),
        in_specs=[blk, blk, blk, vec, vec, pl.BlockSpec((w, w), lambda i: (0, 0))],
        out_specs=blk,
        out_shape=jax.ShapeDtypeStruct((m, w), BF16),
        compiler_params=_cparams(("arbitrary",)),
        name="rwkv_post",
    )(o, bon, g, lg, lb, e)


def _regroup_in_proj(w_in):
    d = w_in.shape[0]
    a = ATT_WIDTH
    o_q, o_k, o_v, o_iq = 0, a, 2 * a, 3 * a
    o_ik = 4 * a
    o_iw = o_ik + IDX_HEAD_DIM
    base = o_iw + IDX_HEADS
    o_rr, o_rk, o_rv = base, base + RWKV_WIDTH, base + 2 * RWKV_WIDTH
    o_xw = base + 3 * RWKV_WIDTH
    o_xa = o_xw + DECAY_LORA
    o_xg = o_xa + AAA_LORA

    def z(n):
        return jnp.zeros((d, n), w_in.dtype)

    ik = w_in[:, o_ik:o_ik + IDX_HEAD_DIM]
    parts = [w_in[:, o_q:o_q + 4 * a],
             w_in[:, o_rr:o_rr + 3 * RWKV_WIDTH],
             ik, ik,
             w_in[:, o_iw:o_iw + IDX_HEADS], z(LANES - IDX_HEADS),
             w_in[:, o_xw:o_xw + DECAY_LORA], z(LORA_PAD - DECAY_LORA),
             w_in[:, o_xa:o_xa + AAA_LORA], z(LORA_PAD - AAA_LORA),
             w_in[:, o_xg:o_xg + GATE_LORA],
             z(IN_COLS_PAD - (OFF_XG + GATE_LORA))]
    return jnp.concatenate(parts, axis=1).astype(BF16)


def _rope_freqs(head_dim):
    rot = head_dim // ROPE_FRACTION
    half = rot // 2
    inv = ROPE_THETA ** (-jnp.arange(half, dtype=F32) / half)
    per_head = jnp.concatenate([inv, inv, jnp.zeros((head_dim - rot,), F32)])
    return jnp.tile(per_head, LANES // head_dim).reshape(1, LANES)


def kernel(x, c, positions, w_ada, b_ada, norm1_g, w_in, q_norm_g, k_norm_g, rwkv_mu, rwkv_w0,
           rwkv_w_up, rwkv_a0, rwkv_a_up, rwkv_g_up, rwkv_k_k, rwkv_k_a, rwkv_r_k, rwkv_lnx_g,
           rwkv_lnx_b, w_out, norm2_g, w_ffn_gate, w_ffn_up, w_ffn_down):
    bsz, seq, d = x.shape
    depth = w_ada.shape[0]
    m = bsz * seq
    pos2 = positions.reshape(m, 1)
    fa = _rope_freqs(ATT_HEAD_DIM)
    fi = _rope_freqs(IDX_HEAD_DIM)
    hd = RWKV_HEAD_DIM
    e = jnp.kron(jnp.eye(RWKV_HEADS, dtype=F32), jnp.ones((hd, hd), F32)).astype(BF16)
    x2 = x.reshape(m, d)

    for l in range(depth):
        mod = _adaln(c, w_ada[l], b_ada[l])
        mod3 = mod.reshape(bsz * 6, 1, d)

        proj = _in_proj(x2, norm1_g[l].reshape(1, d), mod3, _regroup_in_proj(w_in[l]), seq)

        qt, kn, vt, iqt, ik, iwt = _att_prep(proj, pos2, q_norm_g[l].reshape(1, -1),
                                             k_norm_g[l].reshape(1, -1), fa, fi, KEY_TILE)
        att = _dsa(qt, iqt, iwt, kn, vt, ik, bsz, seq, KEY_TILE)

        mu = rwkv_mu[l]
        w3 = 3 * RWKV_WIDTH

        def padded(vec, width):
            return jnp.zeros((1, width), F32).at[0, :vec.shape[0]].set(vec)

        mus = [mu[0:RWKV_WIDTH].reshape(1, -1), mu[RWKV_WIDTH:2 * RWKV_WIDTH].reshape(1, -1),
               mu[2 * RWKV_WIDTH:w3].reshape(1, -1),
               padded(mu[w3:w3 + DECAY_LORA], LORA_PAD),
               padded(mu[w3 + DECAY_LORA:w3 + DECAY_LORA + AAA_LORA], LORA_PAD),
               mu[w3 + DECAY_LORA + AAA_LORA:].reshape(1, -1)]
        vecs = [rwkv_w0[l].reshape(1, -1), rwkv_a0[l].reshape(1, -1), rwkv_k_k[l].reshape(1, -1),
                rwkv_k_a[l].reshape(1, -1), rwkv_r_k[l].reshape(1, -1)]

        def pad_rows(wm):
            return jnp.zeros((LORA_PAD, wm.shape[1]), F32).at[:wm.shape[0]].set(wm).astype(BF16)

        r, lw, km, vv, kk, bb, g, bon = _rwkv_prep(
            proj, mus, vecs, pad_rows(rwkv_w_up[l]), pad_rows(rwkv_a_up[l]),
            rwkv_g_up[l].astype(BF16), e, bsz, seq)
        o = _rwkv_core(r, lw, km, vv, kk, bb, bsz, seq)
        rw = _rwkv_post(o, bon, g, rwkv_lnx_g[l].reshape(1, -1), rwkv_lnx_b[l].reshape(1, -1), e)

        wo = w_out[l].astype(BF16)
        x2 = _out_proj(att, rw, wo[:ATT_WIDTH], wo[ATT_WIDTH:], x2, mod3, seq)

        hglu = _ffn_glu(x2, norm2_g[l].reshape(1, d), mod3, w_ffn_gate[l].astype(BF16),
                        w_ffn_up[l].astype(BF16), seq)
        x2 = _ffn_down(hglu, w_ffn_down[l].astype(BF16), x2, mod3, seq)
    return x2.reshape(bsz, seq, d)
```

```python
import functools

import jax
import jax.numpy as jnp
import numpy as np
from jax import lax
from jax.experimental import pallas as pl
from jax.experimental.pallas import tpu as pltpu

F32 = jnp.float32
BF16 = jnp.bfloat16
I32 = jnp.int32

D_MODEL = 2048
ATT_HEADS = 8
ATT_HEAD_DIM = 128
ATT_WIDTH = ATT_HEADS * ATT_HEAD_DIM
RWKV_WIDTH = D_MODEL - ATT_WIDTH
RWKV_HEAD_DIM = 64
RWKV_HEADS = RWKV_WIDTH // RWKV_HEAD_DIM
IDX_HEADS = 16
IDX_HEAD_DIM = 64
TOPK_MAX = 256
ROPE_THETA = 500000.0
ROPE_FRACTION = 4
DECAY_LORA = 96
AAA_LORA = 96
GATE_LORA = 256
NORM_EPS = 1e-6
LNX_EPS = 64e-5

LANES = 128
LORA_PAD = 128
IN_COLS_PAD = 8192
OFF_Q, OFF_K, OFF_V, OFF_IQ = 0, 1024, 2048, 3072
OFF_RR, OFF_RK, OFF_RV = 4096, 5120, 6144
OFF_IK, OFF_IW, OFF_XW, OFF_XA, OFF_XG = 7168, 7296, 7424, 7552, 7680

LOG2E = 1.4426950408889634
V_ROWS = 144
KEY_TILE = 512
CHUNK = 64
VMEM_LIMIT = 56 * 1024 * 1024


def _cparams(sem):
    return pltpu.CompilerParams(dimension_semantics=sem, vmem_limit_bytes=VMEM_LIMIT)


def _dot(a, b):
    return jnp.dot(a, b, preferred_element_type=F32)


def _dot_nt(a, b):
    return lax.dot_general(a, b, (((1,), (1,)), ((), ())), preferred_element_type=F32)


def _dot_tn(a, b):
    return lax.dot_general(a, b, (((0,), (0,)), ((), ())), preferred_element_type=F32)


def _split2(x):
    hi = x.astype(BF16)
    lo = (x - hi.astype(F32)).astype(BF16)
    return hi, lo


def _dot_hi(x, w):
    hi, lo = _split2(x)
    return _dot(hi, w) + _dot(lo, w)


def _sigmoid(x):
    return 1.0 / (1.0 + jnp.exp(-x))


def _adaln_kernel(c_ref, w_ref, b_ref, o_ref):
    c = c_ref[...]
    ca = c * _sigmoid(c)
    o_ref[...] = _dot(ca.astype(BF16), w_ref[...].astype(BF16)) + b_ref[...]


def _adaln(c, w, b):
    bsz, d = c.shape
    n = w.shape[1]
    rows = 8
    cp = jnp.zeros((rows, d), F32).at[:bsz].set(c)
    tn = 1024
    out = pl.pallas_call(
        _adaln_kernel,
        grid=(n // tn,),
        in_specs=[pl.BlockSpec((rows, d), lambda j: (0, 0)),
                  pl.BlockSpec((d, tn), lambda j: (0, j)),
                  pl.BlockSpec((1, tn), lambda j: (0, j))],
        out_specs=pl.BlockSpec((rows, tn), lambda j: (0, j)),
        out_shape=jax.ShapeDtypeStruct((rows, n), F32),
        compiler_params=_cparams(("arbitrary",)),
        name="adaln",
    )(cp, w, b.reshape(1, n))
    return out[:bsz]


def _norm_mod(x, g, sc, sh):
    ms = jnp.mean(x * x, axis=-1, keepdims=True)
    y = x * lax.rsqrt(ms + NORM_EPS)
    y = y * g
    return y * (1.0 + sc) + sh


def _in_proj_kernel(x_ref, g_ref, sc_ref, sh_ref, w_ref, o_ref, h_ref):
    @pl.when(pl.program_id(1) == 0)
    def _():
        h_ref[...] = _norm_mod(x_ref[...], g_ref[...], sc_ref[...], sh_ref[...]).astype(BF16)

    o_ref[...] = _dot(h_ref[...], w_ref[...])


def _in_proj(x2, g, mod3, w, seq, tm=512, tn=1024):
    m, d = x2.shape
    n = w.shape[1]
    per_b = seq // tm
    return pl.pallas_call(
        _in_proj_kernel,
        grid=(m // tm, n // tn),
        in_specs=[pl.BlockSpec((tm, d), lambda i, j: (i, 0)),
                  pl.BlockSpec((1, d), lambda i, j: (0, 0)),
                  pl.BlockSpec((None, 1, d), lambda i, j: ((i // per_b) * 6 + 1, 0, 0)),
                  pl.BlockSpec((None, 1, d), lambda i, j: ((i // per_b) * 6 + 0, 0, 0)),
                  pl.BlockSpec((d, tn), lambda i, j: (0, j))],
        out_specs=pl.BlockSpec((tm, tn), lambda i, j: (i, j)),
        out_shape=jax.ShapeDtypeStruct((m, n), F32),
        scratch_shapes=[pltpu.VMEM((tm, d), BF16)],
        compiler_params=_cparams(("arbitrary", "arbitrary")),
        name="in_proj",
    )(x2, g, mod3, mod3, w)


def _ffn_glu_kernel(x_ref, g_ref, sc_ref, sh_ref, wg_ref, wu_ref, o_ref, h_ref):
    @pl.when(pl.program_id(1) == 0)
    def _():
        h_ref[...] = _norm_mod(x_ref[...], g_ref[...], sc_ref[...], sh_ref[...]).astype(BF16)

    h = h_ref[...]
    a = _dot(h, wg_ref[...])
    u = _dot(h, wu_ref[...])
    o_ref[...] = (a * _sigmoid(a) * u).astype(o_ref.dtype)


def _ffn_glu(x2, g, mod3, wg, wu, seq, tm=512, tn=512):
    m, d = x2.shape
    n = wg.shape[1]
    per_b = seq // tm
    return pl.pallas_call(
        _ffn_glu_kernel,
        grid=(m // tm, n // tn),
        in_specs=[pl.BlockSpec((tm, d), lambda i, j: (i, 0)),
                  pl.BlockSpec((1, d), lambda i, j: (0, 0)),
                  pl.BlockSpec((None, 1, d), lambda i, j: ((i // per_b) * 6 + 4, 0, 0)),
                  pl.BlockSpec((None, 1, d), lambda i, j: ((i // per_b) * 6 + 3, 0, 0)),
                  pl.BlockSpec((d, tn), lambda i, j: (0, j)),
                  pl.BlockSpec((d, tn), lambda i, j: (0, j))],
        out_specs=pl.BlockSpec((tm, tn), lambda i, j: (i, j)),
        out_shape=jax.ShapeDtypeStruct((m, n), BF16),
        scratch_shapes=[pltpu.VMEM((tm, d), BF16)],
        compiler_params=_cparams(("arbitrary", "arbitrary")),
        name="ffn_glu",
    )(x2, g, mod3, mod3, wg, wu)


def _out_proj_kernel(a_ref, r_ref, wa_ref, wr_ref, x_ref, gt_ref, o_ref):
    mixed = _dot(a_ref[...], wa_ref[...]) + _dot(r_ref[...], wr_ref[...])
    o_ref[...] = x_ref[...] + gt_ref[...] * mixed


def _out_proj(att, rwkv, wa, wr, x2, mod3, seq, tm=512, tn=1024):
    m, ka = att.shape
    kr = rwkv.shape[1]
    n = wa.shape[1]
    per_b = seq // tm
    return pl.pallas_call(
        _out_proj_kernel,
        grid=(m // tm, n // tn),
        in_specs=[pl.BlockSpec((tm, ka), lambda i, j: (i, 0)),
                  pl.BlockSpec((tm, kr), lambda i, j: (i, 0)),
                  pl.BlockSpec((ka, tn), lambda i, j: (0, j)),
                  pl.BlockSpec((kr, tn), lambda i, j: (0, j)),
                  pl.BlockSpec((tm, tn), lambda i, j: (i, j)),
                  pl.BlockSpec((None, 1, tn), lambda i, j: ((i // per_b) * 6 + 2, 0, j))],
        out_specs=pl.BlockSpec((tm, tn), lambda i, j: (i, j)),
        out_shape=jax.ShapeDtypeStruct((m, n), F32),
        compiler_params=_cparams(("arbitrary", "arbitrary")),
        name="out_proj",
    )(att, rwkv, wa, wr, x2, mod3)


def _ffn_down_kernel(h_ref, w_ref, x_ref, gt_ref, o_ref, acc_ref):
    kk = pl.program_id(2)

    @pl.when(kk == 0)
    def _():
        acc_ref[...] = jnp.zeros_like(acc_ref)

    acc_ref[...] += _dot(h_ref[...], w_ref[...])

    @pl.when(kk == pl.num_programs(2) - 1)
    def _():
        o_ref[...] = x_ref[...] + gt_ref[...] * acc_ref[...]


def _ffn_down(h, w, x2, mod3, seq, tm=512, tn=1024, tk=512):
    m, kdim = h.shape
    n = w.shape[1]
    per_b = seq // tm
    return pl.pallas_call(
        _ffn_down_kernel,
        grid=(m // tm, n // tn, kdim // tk),
        in_specs=[pl.BlockSpec((tm, tk), lambda i, j, k: (i, k)),
                  pl.BlockSpec((tk, tn), lambda i, j, k: (k, j)),
                  pl.BlockSpec((tm, tn), lambda i, j, k: (i, j)),
                  pl.BlockSpec((None, 1, tn), lambda i, j, k: ((i // per_b) * 6 + 5, 0, j))],
        out_specs=pl.BlockSpec((tm, tn), lambda i, j, k: (i, j)),
        out_shape=jax.ShapeDtypeStruct((m, n), F32),
        scratch_shapes=[pltpu.VMEM((tm, tn), F32)],
        compiler_params=_cparams(("arbitrary", "arbitrary", "arbitrary")),
        name="ffn_down",
    )(h, w, x2, mod3)


def _rope(x, cos, sin_lo, sin_hi, half):
    return (x * cos + pltpu.roll(x, LANES - half, 1) * sin_lo
            + pltpu.roll(x, half, 1) * sin_hi)


def _att_prep_kernel(pos_ref, q_ref, k_ref, v_ref, iq_ref, ik_ref, iw_ref,
                     qg_ref, kg_ref, fa_ref, fi_ref,
                     qo_ref, ko_ref, vo_ref, iqo_ref, iko_ref, iwo_ref):
    pos = pos_ref[...].astype(F32)
    lane = lax.broadcasted_iota(I32, (1, LANES), 1)

    half_a = ATT_HEAD_DIM // ROPE_FRACTION // 2
    ang = pos * fa_ref[...]
    cos_a, sin_a = jnp.cos(ang), jnp.sin(ang)
    lo_a = jnp.where(lane < half_a, -sin_a, 0.0)
    hi_a = jnp.where((lane >= half_a) & (lane < 2 * half_a), sin_a, 0.0)

    half_i = IDX_HEAD_DIM // ROPE_FRACTION // 2
    lane_i = lane & (IDX_HEAD_DIM - 1)
    ang = pos * fi_ref[...]
    cos_i, sin_i = jnp.cos(ang), jnp.sin(ang)
    lo_i = jnp.where(lane_i < half_i, -sin_i, 0.0)
    hi_i = jnp.where((lane_i >= half_i) & (lane_i < 2 * half_i), sin_i, 0.0)

    def head_norm(x, g):
        ms = jnp.mean(x * x, axis=-1, keepdims=True)
        return x * lax.rsqrt(ms + NORM_EPS) * g

    qg, kg = qg_ref[...], kg_ref[...]
    tm = q_ref.shape[0]
    pad_row = lax.broadcasted_iota(I32, (V_ROWS - LANES, tm), 0)
    ones_rows = jnp.where(pad_row == 0, 1.0, 0.0).astype(BF16)
    q_scale = (ATT_HEAD_DIM ** -0.5) * LOG2E
    for h in range(ATT_HEADS):
        sl = slice(h * LANES, (h + 1) * LANES)
        qh = _rope(head_norm(q_ref[:, sl], qg), cos_a, lo_a, hi_a, half_a)
        qo_ref[h] = (qh * q_scale).T.astype(BF16)
        kh = _rope(head_norm(k_ref[:, sl], kg), cos_a, lo_a, hi_a, half_a)
        ko_ref[:, sl] = kh.astype(BF16)
        vo_ref[h, 0:LANES, :] = v_ref[:, sl].T.astype(BF16)
        vo_ref[h, LANES:V_ROWS, :] = ones_rows
    for h in range(IDX_HEADS * IDX_HEAD_DIM // LANES):
        sl = slice(h * LANES, (h + 1) * LANES)
        ih = _rope(iq_ref[:, sl], cos_i, lo_i, hi_i, half_i)
        iqo_ref[h] = (ih * (IDX_HEAD_DIM ** -0.5)).T.astype(BF16)
    iko_ref[...] = _rope(ik_ref[...], cos_i, lo_i, hi_i, half_i).astype(BF16)
    iwo_ref[...] = (iw_ref[...] * (IDX_HEADS ** -0.5)).T[:IDX_HEADS, :]


def _att_prep(proj, pos2, qg, kg, fa, fi, tm):
    m = proj.shape[0]
    w = ATT_WIDTH
    npair = IDX_HEADS * IDX_HEAD_DIM // LANES

    def wide(off):
        return pl.BlockSpec((tm, w), lambda i: (i, off // w))

    def narrow(off):
        return pl.BlockSpec((tm, LANES), lambda i: (i, off // LANES))

    const = pl.BlockSpec((1, LANES), lambda i: (0, 0))
    return pl.pallas_call(
        _att_prep_kernel,
        grid=(m // tm,),
        in_specs=[pl.BlockSpec((tm, 1), lambda i: (i, 0)),
                  wide(OFF_Q), wide(OFF_K), wide(OFF_V), wide(OFF_IQ),
                  narrow(OFF_IK), narrow(OFF_IW), const, const, const, const],
        out_specs=[pl.BlockSpec((ATT_HEADS, LANES, tm), lambda i: (0, 0, i)),
                   pl.BlockSpec((tm, w), lambda i: (i, 0)),
                   pl.BlockSpec((ATT_HEADS, None, V_ROWS, tm), lambda i: (0, i, 0, 0)),
                   pl.BlockSpec((npair, LANES, tm), lambda i: (0, 0, i)),
                   pl.BlockSpec((tm, LANES), lambda i: (i, 0)),
                   pl.BlockSpec((IDX_HEADS, tm), lambda i: (0, i))],
        out_shape=[jax.ShapeDtypeStruct((ATT_HEADS, LANES, m), BF16),
                   jax.ShapeDtypeStruct((m, w), BF16),
                   jax.ShapeDtypeStruct((ATT_HEADS, m // tm, V_ROWS, tm), BF16),
                   jax.ShapeDtypeStruct((npair, LANES, m), BF16),
                   jax.ShapeDtypeStruct((m, LANES), BF16),
                   jax.ShapeDtypeStruct((IDX_HEADS, m), F32)],
        compiler_params=_cparams(("arbitrary",)),
        name="att_prep",
    )(pos2, proj, proj, proj, proj, proj, proj, qg, kg, fa, fi)


NEG_BIG = -1e30


INT_MIN = -2 ** 31
MAGNITUDE_BITS = 0x7FFFFFFF
KEY_NEG_INF = -2139095041
BOUND_SLACK = 1.02
MAX_STATIC_SHIFT = 60.0
COUNT_ROWS = 32


def _key_to_float(key):
    bits = key ^ ((key >> 31) & jnp.int32(MAGNITUDE_BITS))
    return lax.bitcast_convert_type(bits, F32)


def _dsa_kernel(bound_ref, qt_ref, iqt_ref, iwt_ref, k_ref, vt_ref, ik_ref, o_ref,
                sc_ref, m_ref, acc_ref, *, tq, tk, topk):
    qi = pl.program_id(1)
    n_kb = (qi * tq + tq - 1) // tk + 1
    key0 = lax.broadcasted_iota(I32, (tk, tq), 0)
    qidx = qi * tq + lax.broadcasted_iota(I32, (tk, tq), 1)
    iw = iwt_ref[...]

    def score_body(kb, carry):
        start = pl.multiple_of(kb * tk, tk)
        ikb = ik_ref[pl.ds(start, tk), 0:IDX_HEAD_DIM]
        s = jnp.zeros((tk, tq), F32)
        for h in range(IDX_HEADS):
            off = (h % 2) * IDX_HEAD_DIM
            d = _dot(ikb, iqt_ref[h // 2, off:off + IDX_HEAD_DIM, :])
            s = s + jnp.maximum(d, 0.0) * iw[h:h + 1, :]
        sc_ref[kb] = jnp.where(kb * tk + key0 <= qidx, s, -jnp.inf)
        return carry

    lax.fori_loop(0, n_kb, score_body, 0)

    def bit_body(i, cand):
        trial = cand ^ lax.shift_left(jnp.int32(1), 31 - i)
        trial_f = _key_to_float(trial)

        def cnt_body(kb, acc):
            ge = jnp.where(sc_ref[kb] >= trial_f, 1.0, 0.0)
            return acc + jnp.sum(ge.reshape(tk // COUNT_ROWS, COUNT_ROWS, tq), axis=0)

        acc = lax.fori_loop(0, n_kb, cnt_body, jnp.zeros((COUNT_ROWS, tq), F32))
        cnt = jnp.sum(acc, axis=0, keepdims=True)
        return jnp.where(cnt >= topk, trial, cand)

    cand = lax.fori_loop(0, 32, bit_body, jnp.full((1, tq), INT_MIN, I32))
    tau = _key_to_float(jnp.maximum(cand, jnp.int32(KEY_NEG_INF)))

    acc_ref[...] = jnp.zeros_like(acc_ref)
    bound = bound_ref[0]

    def logits(kb, h):
        start = pl.multiple_of(kb * tk, tk)
        return _dot(k_ref[pl.ds(start, tk), h * LANES:(h + 1) * LANES], qt_ref[h])

    def selected(kb):
        return (sc_ref[kb] >= tau) & (kb * tk + key0 <= qidx)

    @pl.when(bound <= MAX_STATIC_SHIFT)
    def _():
        def att_body(kb, carry):
            bias = jnp.where(selected(kb), -bound, -jnp.inf)
            s_next = logits(kb, 0)
            for h in range(ATT_HEADS):
                s = s_next
                if h + 1 < ATT_HEADS:
                    s_next = logits(kb, h + 1)
                p = jnp.exp2(s + bias)
                acc_ref[h] += _dot(vt_ref[h, kb], p.astype(BF16))
            return carry

        lax.fori_loop(0, n_kb, att_body, 0)

    @pl.when(bound > MAX_STATIC_SHIFT)
    def _():
        m_ref[...] = jnp.full_like(m_ref, NEG_BIG)

        def att_body(kb, carry):
            bias = jnp.where(selected(kb), 0.0, -jnp.inf)
            for h in range(ATT_HEADS):
                s = logits(kb, h) + bias
                m_prev = m_ref[h]
                m_next = jnp.maximum(m_prev, jnp.max(s, axis=0, keepdims=True))
                p = jnp.exp2(s - m_next)
                alpha = jnp.exp2(m_prev - m_next)
                acc_ref[h] = alpha * acc_ref[h] + _dot(vt_ref[h, kb], p.astype(BF16))
                m_ref[h] = m_next
            return carry

        lax.fori_loop(0, n_kb, att_body, 0)

    for h in range(ATT_HEADS):
        out = acc_ref[h, 0:LANES, :] / acc_ref[h, LANES:LANES + 1, :]
        o_ref[:, h * LANES:(h + 1) * LANES] = out.T.astype(o_ref.dtype)


def _dsa(bound, qt, iqt, iwt, k, vt, ik, bsz, seq, tk, tq=256):
    tq = min(tq, seq)
    topk = min(TOPK_MAX, seq // 4)
    w = ATT_WIDTH
    nq = seq // tq
    nkb = seq // tk
    npair = iqt.shape[0]
    kern = functools.partial(_dsa_kernel, tq=tq, tk=tk, topk=topk)
    return pl.pallas_call(
        kern,
        grid=(bsz, nq),
        in_specs=[pl.BlockSpec(memory_space=pltpu.SMEM),
                  pl.BlockSpec((ATT_HEADS, LANES, tq), lambda b, i: (0, 0, b * nq + i)),
                  pl.BlockSpec((npair, LANES, tq), lambda b, i: (0, 0, b * nq + i)),
                  pl.BlockSpec((IDX_HEADS, tq), lambda b, i: (0, b * nq + i)),
                  pl.BlockSpec((seq, w), lambda b, i: (b, 0)),
                  pl.BlockSpec((ATT_HEADS, nkb, V_ROWS, tk), lambda b, i: (0, b, 0, 0)),
                  pl.BlockSpec((seq, LANES), lambda b, i: (b, 0))],
        out_specs=pl.BlockSpec((tq, w), lambda b, i: (b * nq + i, 0)),
        out_shape=jax.ShapeDtypeStruct((bsz * seq, w), BF16),
        scratch_shapes=[pltpu.VMEM((nkb, tk, tq), F32),
                        pltpu.VMEM((ATT_HEADS, 1, tq), F32),
                        pltpu.VMEM((ATT_HEADS, V_ROWS, tq), F32)],
        compiler_params=_cparams(("arbitrary", "arbitrary")),
        name="dsa",
    )(bound, qt, iqt, iwt, k, vt, ik)


def _rwkv_prep_kernel(rr_ref, rk_ref, rv_ref, xw_ref, xa_ref, xg_ref,
                      mr_ref, mk_ref, mv_ref, mw_ref, ma_ref, mg_ref,
                      w0_ref, a0_ref, kk_ref, ka_ref, rkp_ref,
                      wup_ref, aup_ref, gup_ref, e_ref,
                      r_o, lw_o, k_o, v_o, kkn_o, bb_o, g_o, bon_o,
                      c_r, c_k, c_v, c_w, c_a, c_g, *, tm):
    first = pl.program_id(1) == 0

    def shift(y_ref, carry_ref, mu_ref):
        y = y_ref[...]

        @pl.when(first)
        def _():
            carry_ref[...] = jnp.zeros_like(carry_ref)

        prev_last = carry_ref[7:8, :]
        rolled = pltpu.roll(y, 1, 0)
        rows = lax.broadcasted_iota(I32, y.shape, 0)
        yprev = jnp.where(rows == 0, prev_last, rolled)
        carry_ref[...] = y[tm - 8:tm, :]
        return y + (yprev - y) * mu_ref[...]

    r = shift(rr_ref, c_r, mr_ref)
    k = shift(rk_ref, c_k, mk_ref)
    v = shift(rv_ref, c_v, mv_ref)
    xw = shift(xw_ref, c_w, mw_ref)
    xa = shift(xa_ref, c_a, ma_ref)
    xg = shift(xg_ref, c_g, mg_ref)

    w_raw = w0_ref[...] + _dot(jnp.tanh(xw).astype(BF16), wup_ref[...])
    z = -w_raw
    softplus = jnp.maximum(z, 0.0) + jnp.log(1.0 + jnp.exp(-jnp.abs(z)))
    lw_o[...] = -jnp.exp(-softplus - 0.5)
    a = _sigmoid(a0_ref[...] + _dot(xa.astype(BF16), aup_ref[...]))
    g_o[...] = _dot(_sigmoid(xg).astype(BF16), gup_ref[...])

    e = e_ref[...]
    kk = k * kk_ref[...]
    ss = _dot_hi(kk * kk, e)
    kk = kk / jnp.maximum(jnp.sqrt(ss), 1e-12)
    kmod = k * (1.0 + (a - 1.0) * ka_ref[...])
    r_o[...] = r
    k_o[...] = kmod
    v_o[...] = v
    kkn_o[...] = kk
    bb_o[...] = kk * a
    bon_o[...] = _dot_hi(r * kmod * rkp_ref[...], e) * v


def _rwkv_prep(proj, mus, vecs, wup, aup, gup, e, bsz, seq, tm=256):
    w = RWKV_WIDTH
    per_b = seq // tm

    def wide(off):
        return pl.BlockSpec((tm, w), lambda b, i: (b * per_b + i, off // w))

    def narrow(off, width):
        return pl.BlockSpec((tm, width), lambda b, i: (b * per_b + i, off // width))

    def const(shape):
        return pl.BlockSpec(shape, lambda b, i: (0, 0))

    out_blk = pl.BlockSpec((tm, w), lambda b, i: (b * per_b + i, 0))
    kern = functools.partial(_rwkv_prep_kernel, tm=tm)
    return pl.pallas_call(
        kern,
        grid=(bsz, per_b),
        in_specs=[wide(OFF_RR), wide(OFF_RK), wide(OFF_RV),
                  narrow(OFF_XW, LORA_PAD), narrow(OFF_XA, LORA_PAD), narrow(OFF_XG, GATE_LORA),
                  const((1, w)), const((1, w)), const((1, w)),
                  const((1, LORA_PAD)), const((1, LORA_PAD)), const((1, GATE_LORA)),
                  const((1, w)), const((1, w)), const((1, w)), const((1, w)), const((1, w)),
                  const((LORA_PAD, w)), const((LORA_PAD, w)), const((GATE_LORA, w)), const((w, w))],
        out_specs=[out_blk] * 8,
        out_shape=[jax.ShapeDtypeStruct((bsz * seq, w), F32)] * 8,
        scratch_shapes=[pltpu.VMEM((8, w), F32)] * 3
        + [pltpu.VMEM((8, LORA_PAD), F32)] * 2 + [pltpu.VMEM((8, GATE_LORA), F32)],
        compiler_params=_cparams(("arbitrary", "arbitrary")),
        name="rwkv_prep",
    )(proj, proj, proj, proj, proj, proj, *mus, *vecs, wup, aup, gup, e)


def _rwkv_core_kernel(r_ref, lw_ref, k_ref, v_ref, kk_ref, bb_ref, o_ref, z_ref):
    c = CHUNK

    @pl.when(pl.program_id(1) == 0)
    def _():
        z_ref[...] = jnp.zeros_like(z_ref)

    lw = lw_ref[...]
    tri = jnp.where(lax.broadcasted_iota(I32, (c, c), 1) <= lax.broadcasted_iota(I32, (c, c), 0),
                    1.0, 0.0).astype(BF16)
    hi = lw.astype(BF16)
    rem = lw - hi.astype(F32)
    mid = rem.astype(BF16)
    lo = (rem - mid.astype(F32)).astype(BF16)
    cum = _dot(tri, hi) + _dot(tri, mid) + _dot(tri, lo)
    p_in = jnp.exp(cum)
    p_ex = jnp.exp(cum - lw)
    p_inv = jnp.exp(-cum)
    p_end = p_in[c - 1:c, :]
    a_t = -kk_ref[...] * p_ex
    r_t = r_ref[...] * p_in
    b_h = bb_ref[...] * p_inv
    k_h = k_ref[...] * p_inv
    b_e = b_h * p_end
    k_e = k_h * p_end
    v = v_ref[...]

    n2 = 2 * c
    lane = lax.broadcasted_iota(I32, (1, LANES), 1)
    head0 = lane < RWKV_HEAD_DIM
    ri = lax.broadcasted_iota(I32, (n2, n2), 0)
    ci = lax.broadcasted_iota(I32, (n2, n2), 1)
    same = (ri >= c) == (ci >= c)
    strict = same & (ci < ri)
    incl = same & (ci <= ri)
    eye = ri == ci

    def stack(y):
        return jnp.concatenate([jnp.where(head0, y, 0.0), jnp.where(head0, 0.0, y)], axis=0)

    pairs = range(RWKV_WIDTH // LANES)
    sls = [slice(p * LANES, (p + 1) * LANES) for p in pairs]
    a_s = [stack(a_t[:, sl]) for sl in sls]
    r_s = [stack(r_t[:, sl]) for sl in sls]
    v_s = [stack(v[:, sl]).astype(BF16) for sl in sls]
    g1 = [_dot_nt(jnp.concatenate([a_s[p], r_s[p]], axis=0).astype(BF16),
                  jnp.concatenate([stack(b_h[:, sls[p]]), stack(k_h[:, sls[p]])], axis=0).astype(BF16))
          for p in pairs]
    pw = [jnp.where(strict, g[:n2, :n2], 0.0).astype(BF16) for g in g1]
    a_rb = [jnp.where(incl, g[n2:, :n2], 0.0).astype(BF16) for g in g1]
    a_rk = [jnp.where(incl, g[n2:, n2:], 0.0).astype(BF16) for g in g1]
    akv = [_dot(jnp.where(strict, g1[p][:n2, n2:], 0.0).astype(BF16), v_s[p]) for p in pairs]
    xc = [jnp.concatenate([a_s[p], akv[p]], axis=1) for p in pairs]
    steps = int(np.log2(c))
    for i in range(steps):
        if i + 1 < steps:
            res = [_dot(pw[p], jnp.concatenate([pw[p], xc[p].astype(BF16)], axis=1)) for p in pairs]
            xc = [xc[p] + res[p][:, n2:] for p in pairs]
            pw = [res[p][:, :n2].astype(BF16) for p in pairs]
        else:
            xc = [xc[p] + _dot(pw[p], xc[p].astype(BF16)) for p in pairs]
    xcb = [x.astype(BF16) for x in xc]
    r2 = [_dot(a_rb[p], xcb[p]) for p in pairs]
    ov = [r2[p][:, LANES:] + _dot(a_rk[p], v_s[p]) for p in pairs]
    mg = [_dot_tn(stack(b_e[:, sls[p]]).astype(BF16), xcb[p]) for p in pairs]
    kv = [_dot_tn(stack(k_e[:, sls[p]]).astype(BF16), v_s[p]) for p in pairs]
    for p in pairs:
        q_s = r_s[p] + r2[p][:, :LANES]
        mmat = mg[p][:, :LANES] + jnp.where(eye, p_end[:, sls[p]], 0.0)
        z_hi, z_lo = _split2(z_ref[p])
        qm = jnp.concatenate([q_s, mmat], axis=0).astype(BF16)
        res = _dot(qm, z_hi) + _dot(qm, z_lo)
        o_s = res[:n2] + ov[p]
        o_ref[:, sls[p]] = o_s[:c] + o_s[c:]
        z_ref[p] = res[n2:] + mg[p][:, LANES:] + kv[p]


def _rwkv_core(r, lw, k, v, kk, bb, bsz, seq):
    c = CHUNK
    w = RWKV_WIDTH
    per_b = seq // c
    blk = pl.BlockSpec((c, w), lambda b, i: (b * per_b + i, 0))
    return pl.pallas_call(
        _rwkv_core_kernel,
        grid=(bsz, per_b),
        in_specs=[blk] * 6,
        out_specs=blk,
        out_shape=jax.ShapeDtypeStruct((bsz * seq, w), F32),
        scratch_shapes=[pltpu.VMEM((w // LANES, LANES, LANES), F32)],
        compiler_params=_cparams(("arbitrary", "arbitrary")),
        name="rwkv_core",
    )(r, lw, k, v, kk, bb)


def _rwkv_post_kernel(o_ref, bon_ref, g_ref, lg_ref, lb_ref, e_ref, out_ref):
    o = o_ref[...]
    e = e_ref[...]
    inv_n = 1.0 / RWKV_HEAD_DIM
    mean = _dot_hi(o, e) * inv_n
    d = o - mean
    var = _dot_hi(d * d, e) * inv_n
    y = d * lax.rsqrt(var + LNX_EPS) * lg_ref[...] + lb_ref[...]
    out_ref[...] = ((y + bon_ref[...]) * g_ref[...]).astype(out_ref.dtype)


def _rwkv_post(o, bon, g, lg, lb, e, tm=512):
    m, w = o.shape
    blk = pl.BlockSpec((tm, w), lambda i: (i, 0))
    vec = pl.BlockSpec((1, w), lambda i: (0, 0))
    return pl.pallas_call(
        _rwkv_post_kernel,
        grid=(m // tm,),
        in_specs=[blk, blk, blk, vec, vec, pl.BlockSpec((w, w), lambda i: (0, 0))],
        out_specs=blk,
        out_shape=jax.ShapeDtypeStruct((m, w), BF16),
        compiler_params=_cparams(("arbitrary",)),
        name="rwkv_post",
    )(o, bon, g, lg, lb, e)


def _regroup_in_proj(w_in):
    d = w_in.shape[0]
    a = ATT_WIDTH
    o_q, o_k, o_v, o_iq = 0, a, 2 * a, 3 * a
    o_ik = 4 * a
    o_iw = o_ik + IDX_HEAD_DIM
    base = o_iw + IDX_HEADS
    o_rr, o_rk, o_rv = base, base + RWKV_WIDTH, base + 2 * RWKV_WIDTH
    o_xw = base + 3 * RWKV_WIDTH
    o_xa = o_xw + DECAY_LORA
    o_xg = o_xa + AAA_LORA

    def z(n):
        return jnp.zeros((d, n), w_in.dtype)

    ik = w_in[:, o_ik:o_ik + IDX_HEAD_DIM]
    parts = [w_in[:, o_q:o_q + 4 * a],
             w_in[:, o_rr:o_rr + 3 * RWKV_WIDTH],
             ik, ik,
             w_in[:, o_iw:o_iw + IDX_HEADS], z(LANES - IDX_HEADS),
             w_in[:, o_xw:o_xw + DECAY_LORA], z(LORA_PAD - DECAY_LORA),
             w_in[:, o_xa:o_xa + AAA_LORA], z(LORA_PAD - AAA_LORA),
             w_in[:, o_xg:o_xg + GATE_LORA],
             z(IN_COLS_PAD - (OFF_XG + GATE_LORA))]
    return jnp.concatenate(parts, axis=1).astype(BF16)


def _rope_freqs(head_dim):
    rot = head_dim // ROPE_FRACTION
    half = rot // 2
    inv = ROPE_THETA ** (-jnp.arange(half, dtype=F32) / half)
    per_head = jnp.concatenate([inv, inv, jnp.zeros((head_dim - rot,), F32)])
    return jnp.tile(per_head, LANES // head_dim).reshape(1, LANES)


def kernel(x, c, positions, w_ada, b_ada, norm1_g, w_in, q_norm_g, k_norm_g, rwkv_mu, rwkv_w0,
           rwkv_w_up, rwkv_a0, rwkv_a_up, rwkv_g_up, rwkv_k_k, rwkv_k_a, rwkv_r_k, rwkv_lnx_g,
           rwkv_lnx_b, w_out, norm2_g, w_ffn_gate, w_ffn_up, w_ffn_down):
    bsz, seq, d = x.shape
    depth = w_ada.shape[0]
    m = bsz * seq
    pos2 = positions.reshape(m, 1)
    fa = _rope_freqs(ATT_HEAD_DIM)
    fi = _rope_freqs(IDX_HEAD_DIM)
    hd = RWKV_HEAD_DIM
    e = jnp.kron(jnp.eye(RWKV_HEADS, dtype=F32), jnp.ones((hd, hd), F32)).astype(BF16)
    x2 = x.reshape(m, d)

    for l in range(depth):
        mod = _adaln(c, w_ada[l], b_ada[l])
        mod3 = mod.reshape(bsz * 6, 1, d)

        proj = _in_proj(x2, norm1_g[l].reshape(1, d), mod3, _regroup_in_proj(w_in[l]), seq)

        qt, kn, vt, iqt, ik, iwt = _att_prep(proj, pos2, q_norm_g[l].reshape(1, -1),
                                             k_norm_g[l].reshape(1, -1), fa, fi, KEY_TILE)
        bound = (ATT_HEAD_DIM ** 0.5 * LOG2E * BOUND_SLACK
                 * jnp.max(jnp.abs(q_norm_g[l])) * jnp.max(jnp.abs(k_norm_g[l])))
        att = _dsa(bound.reshape(1).astype(F32), qt, iqt, iwt, kn, vt, ik, bsz, seq, KEY_TILE)

        mu = rwkv_mu[l]
        w3 = 3 * RWKV_WIDTH

        def padded(vec, width):
            return jnp.zeros((1, width), F32).at[0, :vec.shape[0]].set(vec)

        mus = [mu[0:RWKV_WIDTH].reshape(1, -1), mu[RWKV_WIDTH:2 * RWKV_WIDTH].reshape(1, -1),
               mu[2 * RWKV_WIDTH:w3].reshape(1, -1),
               padded(mu[w3:w3 + DECAY_LORA], LORA_PAD),
               padded(mu[w3 + DECAY_LORA:w3 + DECAY_LORA + AAA_LORA], LORA_PAD),
               mu[w3 + DECAY_LORA + AAA_LORA:].reshape(1, -1)]
        vecs = [rwkv_w0[l].reshape(1, -1), rwkv_a0[l].reshape(1, -1), rwkv_k_k[l].reshape(1, -1),
                rwkv_k_a[l].reshape(1, -1), rwkv_r_k[l].reshape(1, -1)]

        def pad_rows(wm):
            return jnp.zeros((LORA_PAD, wm.shape[1]), F32).at[:wm.shape[0]].set(wm).astype(BF16)

        r, lw, km, vv, kk, bb, g, bon = _rwkv_prep(
            proj, mus, vecs, pad_rows(rwkv_w_up[l]), pad_rows(rwkv_a_up[l]),
            rwkv_g_up[l].astype(BF16), e, bsz, seq)
        o = _rwkv_core(r, lw, km, vv, kk, bb, bsz, seq)
        rw = _rwkv_post(o, bon, g, rwkv_lnx_g[l].reshape(1, -1), rwkv_lnx_b[l].reshape(1, -1), e)

        wo = w_out[l].astype(BF16)
        x2 = _out_proj(att, rw, wo[:ATT_WIDTH], wo[ATT_WIDTH:], x2, mod3, seq)

        hglu = _ffn_glu(x2, norm2_g[l].reshape(1, d), mod3, w_ffn_gate[l].astype(BF16),
                        w_ffn_up[l].astype(BF16), seq)
        x2 = _ffn_down(hglu, w_ffn_down[l].astype(BF16), x2, mod3, seq)
    return x2.reshape(bsz, seq, d)
```

```python
import functools

import jax
import jax.numpy as jnp
import numpy as np
from jax import lax
from jax.experimental import pallas as pl
from jax.experimental.pallas import tpu as pltpu

F32 = jnp.float32
BF16 = jnp.bfloat16
I32 = jnp.int32

D_MODEL = 2048
ATT_HEADS = 8
ATT_HEAD_DIM = 128
ATT_WIDTH = ATT_HEADS * ATT_HEAD_DIM
RWKV_WIDTH = D_MODEL - ATT_WIDTH
RWKV_HEAD_DIM = 64
RWKV_HEADS = RWKV_WIDTH // RWKV_HEAD_DIM
IDX_HEADS = 16
IDX_HEAD_DIM = 64
TOPK_MAX = 256
ROPE_THETA = 500000.0
ROPE_FRACTION = 4
DECAY_LORA = 96
AAA_LORA = 96
GATE_LORA = 256
NORM_EPS = 1e-6
LNX_EPS = 64e-5

LANES = 128
LORA_PAD = 128
OFF_Q, OFF_K, OFF_V, OFF_IQ = 0, 1024, 2048, 3072
ATT_MAIN_COLS = 4096
OFF_RR, OFF_RK, OFF_RV, OFF_XG = 0, 1024, 2048, 3072
OFF_IKW, OFF_XW, OFF_XA = 3328, 3456, 3584
REST_COLS = 4096

LOG2E = 1.4426950408889634
V_ROWS = 144
KEY_TILE = 512
CHUNK = 64
VMEM_LIMIT = 56 * 1024 * 1024


def _cparams(sem):
    return pltpu.CompilerParams(dimension_semantics=sem, vmem_limit_bytes=VMEM_LIMIT)


def _dot(a, b):
    return jnp.dot(a, b, preferred_element_type=F32)


def _dot_nt(a, b):
    return lax.dot_general(a, b, (((1,), (1,)), ((), ())), preferred_element_type=F32)


def _dot_tn(a, b):
    return lax.dot_general(a, b, (((0,), (0,)), ((), ())), preferred_element_type=F32)


def _split2(x):
    hi = x.astype(BF16)
    lo = (x - hi.astype(F32)).astype(BF16)
    return hi, lo


def _dot_hi(x, w):
    hi, lo = _split2(x)
    return _dot(hi, w) + _dot(lo, w)


def _sigmoid(x):
    return 1.0 / (1.0 + jnp.exp(-x))


def _adaln_kernel(c_ref, w_ref, b_ref, o_ref):
    c = c_ref[...]
    ca = c * _sigmoid(c)
    o_ref[...] = _dot(ca.astype(BF16), w_ref[...].astype(BF16)) + b_ref[...]


def _adaln(c, w, b):
    bsz, d = c.shape
    n = w.shape[1]
    rows = 8
    cp = jnp.zeros((rows, d), F32).at[:bsz].set(c)
    tn = 1024
    out = pl.pallas_call(
        _adaln_kernel,
        grid=(n // tn,),
        in_specs=[pl.BlockSpec((rows, d), lambda j: (0, 0)),
                  pl.BlockSpec((d, tn), lambda j: (0, j)),
                  pl.BlockSpec((1, tn), lambda j: (0, j))],
        out_specs=pl.BlockSpec((rows, tn), lambda j: (0, j)),
        out_shape=jax.ShapeDtypeStruct((rows, n), F32),
        compiler_params=_cparams(("arbitrary",)),
        name="adaln",
    )(cp, w, b.reshape(1, n))
    return out[:bsz]


def _norm_mod(x, g, sc, sh):
    ms = jnp.mean(x * x, axis=-1, keepdims=True)
    y = x * lax.rsqrt(ms + NORM_EPS)
    y = y * g
    return y * (1.0 + sc) + sh


def _norm_kernel(x_ref, g_ref, sc_ref, sh_ref, o_ref):
    o_ref[...] = _norm_mod(x_ref[...], g_ref[...], sc_ref[...], sh_ref[...]).astype(BF16)


def _norm(x2, g, mod3, seq, tm=512):
    m, d = x2.shape
    per_b = seq // tm
    return pl.pallas_call(
        _norm_kernel,
        grid=(m // tm,),
        in_specs=[pl.BlockSpec((tm, d), lambda i: (i, 0)),
                  pl.BlockSpec((1, d), lambda i: (0, 0)),
                  pl.BlockSpec((None, 1, d), lambda i: ((i // per_b) * 6 + 1, 0, 0)),
                  pl.BlockSpec((None, 1, d), lambda i: ((i // per_b) * 6 + 0, 0, 0))],
        out_specs=pl.BlockSpec((tm, d), lambda i: (i, 0)),
        out_shape=jax.ShapeDtypeStruct((m, d), BF16),
        compiler_params=_cparams(("arbitrary",)),
        name="norm1",
    )(x2, g, mod3, mod3)


def _mm_kernel(h_ref, w_ref, o_ref, wb_ref):
    @pl.when(pl.program_id(1) == 0)
    def _():
        wb_ref[...] = w_ref[...].astype(BF16)

    o_ref[...] = _dot(h_ref[...], wb_ref[...])


def _matmul(h, w, n_cols, name, tm=1024, tn=1024):
    m, d = h.shape
    return pl.pallas_call(
        _mm_kernel,
        grid=(n_cols // tn, m // tm),
        in_specs=[pl.BlockSpec((tm, d), lambda j, i: (i, 0)),
                  pl.BlockSpec((d, tn), lambda j, i: (0, j))],
        out_specs=pl.BlockSpec((tm, tn), lambda j, i: (i, j)),
        out_shape=jax.ShapeDtypeStruct((m, n_cols), F32),
        scratch_shapes=[pltpu.VMEM((d, tn), BF16)],
        compiler_params=_cparams(("arbitrary", "arbitrary")),
        name=name,
    )(h, w)


def _ffn_glu_kernel(h_ref, wg_ref, wu_ref, o_ref, wgb_ref, wub_ref):
    @pl.when(pl.program_id(1) == 0)
    def _():
        wgb_ref[...] = wg_ref[...].astype(BF16)
        wub_ref[...] = wu_ref[...].astype(BF16)

    h = h_ref[...]
    a = _dot(h, wgb_ref[...])
    u = _dot(h, wub_ref[...])
    o_ref[...] = (a * _sigmoid(a) * u).astype(o_ref.dtype)


def _ffn_glu(h, wg, wu, tm=1024, tn=512):
    m, d = h.shape
    n = wg.shape[1]
    return pl.pallas_call(
        _ffn_glu_kernel,
        grid=(n // tn, m // tm),
        in_specs=[pl.BlockSpec((tm, d), lambda j, i: (i, 0)),
                  pl.BlockSpec((d, tn), lambda j, i: (0, j)),
                  pl.BlockSpec((d, tn), lambda j, i: (0, j))],
        out_specs=pl.BlockSpec((tm, tn), lambda j, i: (i, j)),
        out_shape=jax.ShapeDtypeStruct((m, n), BF16),
        scratch_shapes=[pltpu.VMEM((d, tn), BF16)] * 2,
        compiler_params=_cparams(("arbitrary", "arbitrary")),
        name="ffn_glu",
    )(h, wg, wu)


def _out_proj_kernel(a_ref, r_ref, wa_ref, wr_ref, x_ref, gt_ref, g_ref, sc_ref, sh_ref,
                     x1_ref, h2_ref):
    mixed = _dot(a_ref[...], wa_ref[...]) + _dot(r_ref[...], wr_ref[...])
    x1 = x_ref[...] + gt_ref[...] * mixed
    x1_ref[...] = x1
    h2_ref[...] = _norm_mod(x1, g_ref[...], sc_ref[...], sh_ref[...]).astype(BF16)


def _out_proj(att, rwkv, wa, wr, x2, mod3, g2, seq, tm=512):
    m, ka = att.shape
    kr = rwkv.shape[1]
    n = wa.shape[1]
    per_b = seq // tm

    def mod_row(j):
        return pl.BlockSpec((None, 1, n), lambda i: ((i // per_b) * 6 + j, 0, 0))

    row = pl.BlockSpec((tm, n), lambda i: (i, 0))
    return pl.pallas_call(
        _out_proj_kernel,
        grid=(m // tm,),
        in_specs=[pl.BlockSpec((tm, ka), lambda i: (i, 0)),
                  pl.BlockSpec((tm, kr), lambda i: (i, 0)),
                  pl.BlockSpec((ka, n), lambda i: (0, 0)),
                  pl.BlockSpec((kr, n), lambda i: (0, 0)),
                  row, mod_row(2),
                  pl.BlockSpec((1, n), lambda i: (0, 0)), mod_row(4), mod_row(3)],
        out_specs=[row, row],
        out_shape=[jax.ShapeDtypeStruct((m, n), F32), jax.ShapeDtypeStruct((m, n), BF16)],
        compiler_params=_cparams(("arbitrary",)),
        name="out_proj",
    )(att, rwkv, wa, wr, x2, mod3, g2, mod3, mod3)


def _ffn_down_kernel(h_ref, w_ref, x_ref, gt_ref, o_ref, wb_ref):
    @pl.when(pl.program_id(1) == 0)
    def _():
        wb_ref[...] = w_ref[...].astype(BF16)

    o_ref[...] = x_ref[...] + gt_ref[...] * _dot(h_ref[...], wb_ref[...])


def _ffn_down(h, w, x2, mod3, seq, tm=512, tn=512):
    m, kdim = h.shape
    n = w.shape[1]
    per_b = seq // tm
    return pl.pallas_call(
        _ffn_down_kernel,
        grid=(n // tn, m // tm),
        in_specs=[pl.BlockSpec((tm, kdim), lambda j, i: (i, 0)),
                  pl.BlockSpec((kdim, tn), lambda j, i: (0, j)),
                  pl.BlockSpec((tm, tn), lambda j, i: (i, j)),
                  pl.BlockSpec((None, 1, tn), lambda j, i: ((i // per_b) * 6 + 5, 0, j))],
        out_specs=pl.BlockSpec((tm, tn), lambda j, i: (i, j)),
        out_shape=jax.ShapeDtypeStruct((m, n), F32),
        scratch_shapes=[pltpu.VMEM((kdim, tn), BF16)],
        compiler_params=_cparams(("arbitrary", "arbitrary")),
        name="ffn_down",
    )(h, w, x2, mod3)


def _rope(x, cos, sin_lo, sin_hi, half):
    return (x * cos + pltpu.roll(x, LANES - half, 1) * sin_lo
            + pltpu.roll(x, half, 1) * sin_hi)


def _att_prep_kernel(pos_ref, q_ref, k_ref, v_ref, iq_ref, ikw_ref,
                     qg_ref, kg_ref, fa_ref, fi_ref,
                     qo_ref, ko_ref, vo_ref, iqo_ref, iko_ref, iwo_ref):
    pos = pos_ref[...].astype(F32)
    lane = lax.broadcasted_iota(I32, (1, LANES), 1)

    half_a = ATT_HEAD_DIM // ROPE_FRACTION // 2
    ang = pos * fa_ref[...]
    cos_a, sin_a = jnp.cos(ang), jnp.sin(ang)
    lo_a = jnp.where(lane < half_a, -sin_a, 0.0)
    hi_a = jnp.where((lane >= half_a) & (lane < 2 * half_a), sin_a, 0.0)

    half_i = IDX_HEAD_DIM // ROPE_FRACTION // 2
    lane_i = lane & (IDX_HEAD_DIM - 1)
    ang = pos * fi_ref[...]
    cos_i, sin_i = jnp.cos(ang), jnp.sin(ang)
    lo_i = jnp.where(lane_i < half_i, -sin_i, 0.0)
    hi_i = jnp.where((lane_i >= half_i) & (lane_i < 2 * half_i), sin_i, 0.0)

    def head_norm(x, g):
        ms = jnp.mean(x * x, axis=-1, keepdims=True)
        return x * lax.rsqrt(ms + NORM_EPS) * g

    qg, kg = qg_ref[...], kg_ref[...]
    tm = q_ref.shape[0]
    pad_row = lax.broadcasted_iota(I32, (V_ROWS - LANES, tm), 0)
    ones_rows = jnp.where(pad_row == 0, 1.0, 0.0).astype(BF16)
    q_scale = (ATT_HEAD_DIM ** -0.5) * LOG2E
    for h in range(ATT_HEADS):
        sl = slice(h * LANES, (h + 1) * LANES)
        qh = _rope(head_norm(q_ref[:, sl], qg), cos_a, lo_a, hi_a, half_a)
        qo_ref[h] = (qh * q_scale).T.astype(BF16)
        kh = _rope(head_norm(k_ref[:, sl], kg), cos_a, lo_a, hi_a, half_a)
        ko_ref[:, sl] = kh.astype(BF16)
        vo_ref[h, 0:LANES, :] = v_ref[:, sl].T.astype(BF16)
        vo_ref[h, LANES:V_ROWS, :] = ones_rows
    for h in range(IDX_HEADS * IDX_HEAD_DIM // LANES):
        sl = slice(h * LANES, (h + 1) * LANES)
        ih = _rope(iq_ref[:, sl], cos_i, lo_i, hi_i, half_i)
        iqo_ref[h] = (ih * (IDX_HEAD_DIM ** -0.5)).T.astype(BF16)
    ikw = ikw_ref[...]
    iko_ref[...] = _rope(ikw, cos_i, lo_i, hi_i, half_i).astype(BF16)
    iwo_ref[...] = (ikw * (IDX_HEADS ** -0.5)).T[IDX_HEAD_DIM:IDX_HEAD_DIM + IDX_HEADS, :]


def _att_prep(proj_a, proj_r, pos2, qg, kg, fa, fi, tm):
    m = proj_a.shape[0]
    w = ATT_WIDTH
    npair = IDX_HEADS * IDX_HEAD_DIM // LANES

    def wide(off):
        return pl.BlockSpec((tm, w), lambda i: (i, off // w))

    const = pl.BlockSpec((1, LANES), lambda i: (0, 0))
    return pl.pallas_call(
        _att_prep_kernel,
        grid=(m // tm,),
        in_specs=[pl.BlockSpec((tm, 1), lambda i: (i, 0)),
                  wide(OFF_Q), wide(OFF_K), wide(OFF_V), wide(OFF_IQ),
                  pl.BlockSpec((tm, LANES), lambda i: (i, OFF_IKW // LANES)),
                  const, const, const, const],
        out_specs=[pl.BlockSpec((ATT_HEADS, LANES, tm), lambda i: (0, 0, i)),
                   pl.BlockSpec((tm, w), lambda i: (i, 0)),
                   pl.BlockSpec((ATT_HEADS, None, V_ROWS, tm), lambda i: (0, i, 0, 0)),
                   pl.BlockSpec((npair, LANES, tm), lambda i: (0, 0, i)),
                   pl.BlockSpec((tm, LANES), lambda i: (i, 0)),
                   pl.BlockSpec((IDX_HEADS, tm), lambda i: (0, i))],
        out_shape=[jax.ShapeDtypeStruct((ATT_HEADS, LANES, m), BF16),
                   jax.ShapeDtypeStruct((m, w), BF16),
                   jax.ShapeDtypeStruct((ATT_HEADS, m // tm, V_ROWS, tm), BF16),
                   jax.ShapeDtypeStruct((npair, LANES, m), BF16),
                   jax.ShapeDtypeStruct((m, LANES), BF16),
                   jax.ShapeDtypeStruct((IDX_HEADS, m), F32)],
        compiler_params=_cparams(("arbitrary",)),
        name="att_prep",
    )(pos2, proj_a, proj_a, proj_a, proj_a, proj_r, qg, kg, fa, fi)


NEG_BIG = -1e30


INT_MIN = -2 ** 31
MAGNITUDE_BITS = 0x7FFFFFFF
KEY_NEG_INF = -2139095041
BOUND_SLACK = 1.02
MAX_STATIC_SHIFT = 60.0
COUNT_ROWS = 32


def _key_to_float(key):
    bits = key ^ ((key >> 31) & jnp.int32(MAGNITUDE_BITS))
    return lax.bitcast_convert_type(bits, F32)


def _dsa_kernel(bound_ref, qt_ref, iqt_ref, iwt_ref, k_ref, vt_ref, ik_ref, o_ref,
                sc_ref, m_ref, acc_ref, *, tq, tk, topk):
    qi = pl.program_id(1)
    n_kb = (qi * tq + tq - 1) // tk + 1
    key0 = lax.broadcasted_iota(I32, (tk, tq), 0)
    qidx = qi * tq + lax.broadcasted_iota(I32, (tk, tq), 1)
    iw = iwt_ref[...]

    def score_body(kb, carry):
        start = pl.multiple_of(kb * tk, tk)
        ikb = ik_ref[pl.ds(start, tk), 0:IDX_HEAD_DIM]
        s = jnp.zeros((tk, tq), F32)
        for h in range(IDX_HEADS):
            off = (h % 2) * IDX_HEAD_DIM
            d = _dot(ikb, iqt_ref[h // 2, off:off + IDX_HEAD_DIM, :])
            s = s + jnp.maximum(d, 0.0) * iw[h:h + 1, :]
        sc_ref[kb] = jnp.where(kb * tk + key0 <= qidx, s, -jnp.inf)
        return carry

    lax.fori_loop(0, n_kb, score_body, 0)

    def bit_body(i, cand):
        trial = cand ^ lax.shift_left(jnp.int32(1), 31 - i)
        trial_f = _key_to_float(trial)

        def cnt_body(kb, acc):
            ge = jnp.where(sc_ref[kb] >= trial_f, 1.0, 0.0)
            return acc + jnp.sum(ge.reshape(tk // COUNT_ROWS, COUNT_ROWS, tq), axis=0)

        acc = lax.fori_loop(0, n_kb, cnt_body, jnp.zeros((COUNT_ROWS, tq), F32))
        cnt = jnp.sum(acc, axis=0, keepdims=True)
        return jnp.where(cnt >= topk, trial, cand)

    cand = lax.fori_loop(0, 32, bit_body, jnp.full((1, tq), INT_MIN, I32))
    tau = _key_to_float(jnp.maximum(cand, jnp.int32(KEY_NEG_INF)))

    acc_ref[...] = jnp.zeros_like(acc_ref)
    bound = bound_ref[0]

    def logits(kb, h):
        start = pl.multiple_of(kb * tk, tk)
        return _dot(k_ref[pl.ds(start, tk), h * LANES:(h + 1) * LANES], qt_ref[h])

    def selected(kb):
        return (sc_ref[kb] >= tau) & (kb * tk + key0 <= qidx)

    @pl.when(bound <= MAX_STATIC_SHIFT)
    def _():
        def att_body(kb, carry):
            bias = jnp.where(selected(kb), -bound, -jnp.inf)
            for h in range(ATT_HEADS):
                p = jnp.exp2(logits(kb, h) + bias)
                acc_ref[h] += _dot(vt_ref[h, kb], p.astype(BF16))
            return carry

        lax.fori_loop(0, n_kb, att_body, 0)

    @pl.when(bound > MAX_STATIC_SHIFT)
    def _():
        m_ref[...] = jnp.full_like(m_ref, NEG_BIG)

        def att_body(kb, carry):
            bias = jnp.where(selected(kb), 0.0, -jnp.inf)
            for h in range(ATT_HEADS):
                s = logits(kb, h) + bias
                m_prev = m_ref[h]
                m_next = jnp.maximum(m_prev, jnp.max(s, axis=0, keepdims=True))
                p = jnp.exp2(s - m_next)
                alpha = jnp.exp2(m_prev - m_next)
                acc_ref[h] = alpha * acc_ref[h] + _dot(vt_ref[h, kb], p.astype(BF16))
                m_ref[h] = m_next
            return carry

        lax.fori_loop(0, n_kb, att_body, 0)

    for h in range(ATT_HEADS):
        out = acc_ref[h, 0:LANES, :] / acc_ref[h, LANES:LANES + 1, :]
        o_ref[:, h * LANES:(h + 1) * LANES] = out.T.astype(o_ref.dtype)


def _dsa(bound, qt, iqt, iwt, k, vt, ik, bsz, seq, tk, tq=512):
    tq = min(tq, seq)
    topk = min(TOPK_MAX, seq // 4)
    w = ATT_WIDTH
    nq = seq // tq
    nkb = seq // tk
    npair = iqt.shape[0]
    kern = functools.partial(_dsa_kernel, tq=tq, tk=tk, topk=topk)
    return pl.pallas_call(
        kern,
        grid=(bsz, nq),
        in_specs=[pl.BlockSpec(memory_space=pltpu.SMEM),
                  pl.BlockSpec((ATT_HEADS, LANES, tq), lambda b, i: (0, 0, b * nq + i)),
                  pl.BlockSpec((npair, LANES, tq), lambda b, i: (0, 0, b * nq + i)),
                  pl.BlockSpec((IDX_HEADS, tq), lambda b, i: (0, b * nq + i)),
                  pl.BlockSpec((seq, w), lambda b, i: (b, 0)),
                  pl.BlockSpec((ATT_HEADS, nkb, V_ROWS, tk), lambda b, i: (0, b, 0, 0)),
                  pl.BlockSpec((seq, LANES), lambda b, i: (b, 0))],
        out_specs=pl.BlockSpec((tq, w), lambda b, i: (b * nq + i, 0)),
        out_shape=jax.ShapeDtypeStruct((bsz * seq, w), BF16),
        scratch_shapes=[pltpu.VMEM((nkb, tk, tq), F32),
                        pltpu.VMEM((ATT_HEADS, 1, tq), F32),
                        pltpu.VMEM((ATT_HEADS, V_ROWS, tq), F32)],
        compiler_params=_cparams(("arbitrary", "arbitrary")),
        name="dsa",
    )(bound, qt, iqt, iwt, k, vt, ik)


def _rwkv_prep_kernel(rr_ref, rk_ref, rv_ref, xw_ref, xa_ref, xg_ref,
                      mr_ref, mk_ref, mv_ref, mw_ref, ma_ref, mg_ref,
                      w0_ref, a0_ref, kk_ref, ka_ref, rkp_ref,
                      wup_ref, aup_ref, gup_ref, e_ref,
                      r_o, lw_o, k_o, v_o, kkn_o, bb_o, g_o, bon_o,
                      c_r, c_k, c_v, c_w, c_a, c_g, *, tm):
    first = pl.program_id(1) == 0

    def shift(y_ref, carry_ref, mu_ref):
        y = y_ref[...]

        @pl.when(first)
        def _():
            carry_ref[...] = jnp.zeros_like(carry_ref)

        prev_last = carry_ref[7:8, :]
        rolled = pltpu.roll(y, 1, 0)
        rows = lax.broadcasted_iota(I32, y.shape, 0)
        yprev = jnp.where(rows == 0, prev_last, rolled)
        carry_ref[...] = y[tm - 8:tm, :]
        return y + (yprev - y) * mu_ref[...]

    r = shift(rr_ref, c_r, mr_ref)
    k = shift(rk_ref, c_k, mk_ref)
    v = shift(rv_ref, c_v, mv_ref)
    xw = shift(xw_ref, c_w, mw_ref)
    xa = shift(xa_ref, c_a, ma_ref)
    xg = shift(xg_ref, c_g, mg_ref)

    w_raw = w0_ref[...] + _dot(jnp.tanh(xw).astype(BF16), wup_ref[...])
    z = -w_raw
    softplus = jnp.maximum(z, 0.0) + jnp.log(1.0 + jnp.exp(-jnp.abs(z)))
    lw_o[...] = -jnp.exp(-softplus - 0.5)
    a = _sigmoid(a0_ref[...] + _dot(xa.astype(BF16), aup_ref[...]))
    g_o[...] = _dot(_sigmoid(xg).astype(BF16), gup_ref[...])

    e = e_ref[...]
    kk = k * kk_ref[...]
    ss = _dot_hi(kk * kk, e)
    kk = kk / jnp.maximum(jnp.sqrt(ss), 1e-12)
    kmod = k * (1.0 + (a - 1.0) * ka_ref[...])
    r_o[...] = r
    k_o[...] = kmod
    v_o[...] = v
    kkn_o[...] = kk
    bb_o[...] = kk * a
    bon_o[...] = _dot_hi(r * kmod * rkp_ref[...], e) * v


def _rwkv_prep(proj, mus, vecs, wup, aup, gup, e, bsz, seq, tm=256):
    w = RWKV_WIDTH
    per_b = seq // tm

    def wide(off):
        return pl.BlockSpec((tm, w), lambda b, i: (b * per_b + i, off // w))

    def narrow(off, width):
        return pl.BlockSpec((tm, width), lambda b, i: (b * per_b + i, off // width))

    def const(shape):
        return pl.BlockSpec(shape, lambda b, i: (0, 0))

    out_blk = pl.BlockSpec((tm, w), lambda b, i: (b * per_b + i, 0))
    kern = functools.partial(_rwkv_prep_kernel, tm=tm)
    return pl.pallas_call(
        kern,
        grid=(bsz, per_b),
        in_specs=[wide(OFF_RR), wide(OFF_RK), wide(OFF_RV),
                  narrow(OFF_XW, LORA_PAD), narrow(OFF_XA, LORA_PAD), narrow(OFF_XG, GATE_LORA),
                  const((1, w)), const((1, w)), const((1, w)),
                  const((1, LORA_PAD)), const((1, LORA_PAD)), const((1, GATE_LORA)),
                  const((1, w)), const((1, w)), const((1, w)), const((1, w)), const((1, w)),
                  const((LORA_PAD, w)), const((LORA_PAD, w)), const((GATE_LORA, w)), const((w, w))],
        out_specs=[out_blk] * 8,
        out_shape=[jax.ShapeDtypeStruct((bsz * seq, w), F32)] * 8,
        scratch_shapes=[pltpu.VMEM((8, w), F32)] * 3
        + [pltpu.VMEM((8, LORA_PAD), F32)] * 2 + [pltpu.VMEM((8, GATE_LORA), F32)],
        compiler_params=_cparams(("arbitrary", "arbitrary")),
        name="rwkv_prep",
    )(proj, proj, proj, proj, proj, proj, *mus, *vecs, wup, aup, gup, e)


def _rwkv_core_kernel(r_ref, lw_ref, k_ref, v_ref, kk_ref, bb_ref, o_ref, z_ref):
    c = CHUNK

    @pl.when(pl.program_id(1) == 0)
    def _():
        z_ref[...] = jnp.zeros_like(z_ref)

    lw = lw_ref[...]
    tri = jnp.where(lax.broadcasted_iota(I32, (c, c), 1) <= lax.broadcasted_iota(I32, (c, c), 0),
                    1.0, 0.0).astype(BF16)
    hi = lw.astype(BF16)
    rem = lw - hi.astype(F32)
    mid = rem.astype(BF16)
    lo = (rem - mid.astype(F32)).astype(BF16)
    cum = _dot(tri, hi) + _dot(tri, mid) + _dot(tri, lo)
    p_in = jnp.exp(cum)
    p_ex = jnp.exp(cum - lw)
    p_inv = jnp.exp(-cum)
    p_end = p_in[c - 1:c, :]
    a_t = -kk_ref[...] * p_ex
    r_t = r_ref[...] * p_in
    b_h = bb_ref[...] * p_inv
    k_h = k_ref[...] * p_inv
    b_e = b_h * p_end
    k_e = k_h * p_end
    v = v_ref[...]

    n2 = 2 * c
    lane = lax.broadcasted_iota(I32, (1, LANES), 1)
    head0 = lane < RWKV_HEAD_DIM
    ri = lax.broadcasted_iota(I32, (n2, n2), 0)
    ci = lax.broadcasted_iota(I32, (n2, n2), 1)
    same = (ri >= c) == (ci >= c)
    strict = same & (ci < ri)
    incl = same & (ci <= ri)
    eye = ri == ci

    def stack(y):
        return jnp.concatenate([jnp.where(head0, y, 0.0), jnp.where(head0, 0.0, y)], axis=0)

    pairs = range(RWKV_WIDTH // LANES)
    sls = [slice(p * LANES, (p + 1) * LANES) for p in pairs]
    a_s = [stack(a_t[:, sl]) for sl in sls]
    r_s = [stack(r_t[:, sl]) for sl in sls]
    v_s = [stack(v[:, sl]).astype(BF16) for sl in sls]
    g1 = [_dot_nt(jnp.concatenate([a_s[p], r_s[p]], axis=0).astype(BF16),
                  jnp.concatenate([stack(b_h[:, sls[p]]), stack(k_h[:, sls[p]])], axis=0).astype(BF16))
          for p in pairs]
    pw = [jnp.where(strict, g[:n2, :n2], 0.0).astype(BF16) for g in g1]
    a_rb = [jnp.where(incl, g[n2:, :n2], 0.0).astype(BF16) for g in g1]
    a_rk = [jnp.where(incl, g[n2:, n2:], 0.0).astype(BF16) for g in g1]
    akv = [_dot(jnp.where(strict, g1[p][:n2, n2:], 0.0).astype(BF16), v_s[p]) for p in pairs]
    xc = [jnp.concatenate([a_s[p], akv[p]], axis=1) for p in pairs]
    steps = int(np.log2(c))
    for i in range(steps):
        if i + 1 < steps:
            res = [_dot(pw[p], jnp.concatenate([pw[p], xc[p].astype(BF16)], axis=1)) for p in pairs]
            xc = [xc[p] + res[p][:, n2:] for p in pairs]
            pw = [res[p][:, :n2].astype(BF16) for p in pairs]
        else:
            xc = [xc[p] + _dot(pw[p], xc[p].astype(BF16)) for p in pairs]
    xcb = [x.astype(BF16) for x in xc]
    r2 = [_dot(a_rb[p], xcb[p]) for p in pairs]
    ov = [r2[p][:, LANES:] + _dot(a_rk[p], v_s[p]) for p in pairs]
    mg = [_dot_tn(stack(b_e[:, sls[p]]).astype(BF16), xcb[p]) for p in pairs]
    kv = [_dot_tn(stack(k_e[:, sls[p]]).astype(BF16), v_s[p]) for p in pairs]
    for p in pairs:
        q_s = r_s[p] + r2[p][:, :LANES]
        mmat = mg[p][:, :LANES] + jnp.where(eye, p_end[:, sls[p]], 0.0)
        z_hi, z_lo = _split2(z_ref[p])
        qm = jnp.concatenate([q_s, mmat], axis=0).astype(BF16)
        res = _dot(qm, z_hi) + _dot(qm, z_lo)
        o_s = res[:n2] + ov[p]
        o_ref[:, sls[p]] = o_s[:c] + o_s[c:]
        z_ref[p] = res[n2:] + mg[p][:, LANES:] + kv[p]


def _rwkv_core(r, lw, k, v, kk, bb, bsz, seq):
    c = CHUNK
    w = RWKV_WIDTH
    per_b = seq // c
    blk = pl.BlockSpec((c, w), lambda b, i: (b * per_b + i, 0))
    return pl.pallas_call(
        _rwkv_core_kernel,
        grid=(bsz, per_b),
        in_specs=[blk] * 6,
        out_specs=blk,
        out_shape=jax.ShapeDtypeStruct((bsz * seq, w), F32),
        scratch_shapes=[pltpu.VMEM((w // LANES, LANES, LANES), F32)],
        compiler_params=_cparams(("arbitrary", "arbitrary")),
        name="rwkv_core",
    )(r, lw, k, v, kk, bb)


def _rwkv_post_kernel(o_ref, bon_ref, g_ref, lg_ref, lb_ref, e_ref, out_ref):
    o = o_ref[...]
    e = e_ref[...]
    inv_n = 1.0 / RWKV_HEAD_DIM
    mean = _dot_hi(o, e) * inv_n
    d = o - mean
    var = _dot_hi(d * d, e) * inv_n
    y = d * lax.rsqrt(var + LNX_EPS) * lg_ref[...] + lb_ref[...]
    out_ref[...] = ((y + bon_ref[...]) * g_ref[...]).astype(out_ref.dtype)


def _rwkv_post(o, bon, g, lg, lb, e, tm=512):
    m, w = o.shape
    blk = pl.BlockSpec((tm, w), lambda i: (i, 0))
    vec = pl.BlockSpec((1, w), lambda i: (0, 0))
    return pl.pallas_call(
        _rwkv_post_kernel,
        grid=(m // tm,),
        in_specs=[blk, blk, blk, vec, vec, pl.BlockSpec((w, w), lambda i: (0, 0))],
        out_specs=blk,
        out_shape=jax.ShapeDtypeStruct((m, w), BF16),
        compiler_params=_cparams(("arbitrary",)),
        name="rwkv_post",
    )(o, bon, g, lg, lb, e)


def _regroup_rest(w_in):
    d = w_in.shape[0]
    o_ikw = 4 * ATT_WIDTH
    n_ikw = IDX_HEAD_DIM + IDX_HEADS
    o_r = o_ikw + n_ikw
    o_xw = o_r + 3 * RWKV_WIDTH
    o_xa = o_xw + DECAY_LORA
    o_xg = o_xa + AAA_LORA

    def z(n):
        return jnp.zeros((d, n), w_in.dtype)

    parts = [w_in[:, o_r:o_xw],
             w_in[:, o_xg:o_xg + GATE_LORA],
             w_in[:, o_ikw:o_r], z(LANES - n_ikw),
             w_in[:, o_xw:o_xa], z(LORA_PAD - DECAY_LORA),
             w_in[:, o_xa:o_xg], z(LORA_PAD - AAA_LORA),
             z(REST_COLS - (OFF_XA + LORA_PAD))]
    return jnp.concatenate(parts, axis=1)


def _rope_freqs(head_dim):
    rot = head_dim // ROPE_FRACTION
    half = rot // 2
    inv = ROPE_THETA ** (-jnp.arange(half, dtype=F32) / half)
    per_head = jnp.concatenate([inv, inv, jnp.zeros((head_dim - rot,), F32)])
    return jnp.tile(per_head, LANES // head_dim).reshape(1, LANES)


def kernel(x, c, positions, w_ada, b_ada, norm1_g, w_in, q_norm_g, k_norm_g, rwkv_mu, rwkv_w0,
           rwkv_w_up, rwkv_a0, rwkv_a_up, rwkv_g_up, rwkv_k_k, rwkv_k_a, rwkv_r_k, rwkv_lnx_g,
           rwkv_lnx_b, w_out, norm2_g, w_ffn_gate, w_ffn_up, w_ffn_down):
    bsz, seq, d = x.shape
    depth = w_ada.shape[0]
    m = bsz * seq
    pos2 = positions.reshape(m, 1)
    fa = _rope_freqs(ATT_HEAD_DIM)
    fi = _rope_freqs(IDX_HEAD_DIM)
    hd = RWKV_HEAD_DIM
    e = jnp.kron(jnp.eye(RWKV_HEADS, dtype=F32), jnp.ones((hd, hd), F32)).astype(BF16)
    x2 = x.reshape(m, d)

    for l in range(depth):
        mod = _adaln(c, w_ada[l], b_ada[l])
        mod3 = mod.reshape(bsz * 6, 1, d)

        h1 = _norm(x2, norm1_g[l].reshape(1, d), mod3, seq)
        proj_a = _matmul(h1, w_in[l], ATT_MAIN_COLS, "in_proj_att")
        proj = _matmul(h1, _regroup_rest(w_in[l]), REST_COLS, "in_proj_rest")

        qt, kn, vt, iqt, ik, iwt = _att_prep(proj_a, proj, pos2, q_norm_g[l].reshape(1, -1),
                                             k_norm_g[l].reshape(1, -1), fa, fi, KEY_TILE)
        bound = (ATT_HEAD_DIM ** 0.5 * LOG2E * BOUND_SLACK
                 * jnp.max(jnp.abs(q_norm_g[l])) * jnp.max(jnp.abs(k_norm_g[l])))
        att = _dsa(bound.reshape(1).astype(F32), qt, iqt, iwt, kn, vt, ik, bsz, seq, KEY_TILE)

        mu = rwkv_mu[l]
        w3 = 3 * RWKV_WIDTH

        def padded(vec, width):
            return jnp.zeros((1, width), F32).at[0, :vec.shape[0]].set(vec)

        mus = [mu[0:RWKV_WIDTH].reshape(1, -1), mu[RWKV_WIDTH:2 * RWKV_WIDTH].reshape(1, -1),
               mu[2 * RWKV_WIDTH:w3].reshape(1, -1),
               padded(mu[w3:w3 + DECAY_LORA], LORA_PAD),
               padded(mu[w3 + DECAY_LORA:w3 + DECAY_LORA + AAA_LORA], LORA_PAD),
               mu[w3 + DECAY_LORA + AAA_LORA:].reshape(1, -1)]
        vecs = [rwkv_w0[l].reshape(1, -1), rwkv_a0[l].reshape(1, -1), rwkv_k_k[l].reshape(1, -1),
                rwkv_k_a[l].reshape(1, -1), rwkv_r_k[l].reshape(1, -1)]

        def pad_rows(wm):
            return jnp.zeros((LORA_PAD, wm.shape[1]), F32).at[:wm.shape[0]].set(wm).astype(BF16)

        r, lw, km, vv, kk, bb, g, bon = _rwkv_prep(
            proj, mus, vecs, pad_rows(rwkv_w_up[l]), pad_rows(rwkv_a_up[l]),
            rwkv_g_up[l].astype(BF16), e, bsz, seq)
        o = _rwkv_core(r, lw, km, vv, kk, bb, bsz, seq)
        rw = _rwkv_post(o, bon, g, rwkv_lnx_g[l].reshape(1, -1), rwkv_lnx_b[l].reshape(1, -1), e)

        wo = w_out[l].astype(BF16)
        x2, h2 = _out_proj(att, rw, wo[:ATT_WIDTH], wo[ATT_WIDTH:], x2, mod3,
                           norm2_g[l].reshape(1, d), seq)

        hglu = _ffn_glu(h2, w_ffn_gate[l], w_ffn_up[l])
        x2 = _ffn_down(hglu, w_ffn_down[l], x2, mod3, seq)
    return x2.reshape(bsz, seq, d)
```

```python
import functools

import jax
import jax.numpy as jnp
import numpy as np
from jax import lax
from jax.experimental import pallas as pl
from jax.experimental.pallas import tpu as pltpu

F32 = jnp.float32
BF16 = jnp.bfloat16
I32 = jnp.int32

D_MODEL = 2048
ATT_HEADS = 8
ATT_HEAD_DIM = 128
ATT_WIDTH = ATT_HEADS * ATT_HEAD_DIM
RWKV_WIDTH = D_MODEL - ATT_WIDTH
RWKV_HEAD_DIM = 64
RWKV_HEADS = RWKV_WIDTH // RWKV_HEAD_DIM
IDX_HEADS = 16
IDX_HEAD_DIM = 64
TOPK_MAX = 256
ROPE_THETA = 500000.0
ROPE_FRACTION = 4
DECAY_LORA = 96
AAA_LORA = 96
GATE_LORA = 256
NORM_EPS = 1e-6
LNX_EPS = 64e-5

LANES = 128
SUBLANES = 8
LORA_PAD = 128
ATT_COLS = 4 * ATT_WIDTH + IDX_HEAD_DIM + IDX_HEADS
IN_COLS = ATT_COLS + 3 * RWKV_WIDTH + DECAY_LORA + AAA_LORA + GATE_LORA
OFF_Q, OFF_K, OFF_V, OFF_IQ = 0, 1024, 2048, 3072
OFF_RR, OFF_RK, OFF_RV = 0, 1024, 2048
TAIL_TILE = 256
TAIL_COL0 = IN_COLS - 2 * TAIL_TILE
XW_LANE = IN_COLS - GATE_LORA - AAA_LORA - DECAY_LORA - TAIL_COL0
XA_LANE = XW_LANE + DECAY_LORA
OFF_WA, OFF_XG, OFF_IKW = 0, 256, 512

LOG2E = 1.4426950408889634
V_ROWS = 144
KEY_TILE = 512
CHUNK = 64
VMEM_LIMIT = 56 * 1024 * 1024


def _cparams(sem):
    return pltpu.CompilerParams(dimension_semantics=sem, vmem_limit_bytes=VMEM_LIMIT)


def _dot(a, b):
    return jnp.dot(a, b, preferred_element_type=F32)


def _dot_nt(a, b):
    return lax.dot_general(a, b, (((1,), (1,)), ((), ())), preferred_element_type=F32)


def _dot_tn(a, b):
    return lax.dot_general(a, b, (((0,), (0,)), ((), ())), preferred_element_type=F32)


def _split2(x):
    hi = x.astype(BF16)
    lo = (x - hi.astype(F32)).astype(BF16)
    return hi, lo


def _dot_hi(x, w):
    hi, lo = _split2(x)
    return _dot(hi, w) + _dot(lo, w)


def _sigmoid(x):
    return 1.0 / (1.0 + jnp.exp(-x))


def _adaln_kernel(c_ref, w_ref, b_ref, o_ref):
    c = c_ref[...]
    ca = c * _sigmoid(c)
    o_ref[...] = _dot(ca.astype(BF16), w_ref[...].astype(BF16)) + b_ref[...]


def _adaln(c, w, b):
    bsz, d = c.shape
    n = w.shape[1]
    rows = 8
    cp = jnp.zeros((rows, d), F32).at[:bsz].set(c)
    tn = 1024
    out = pl.pallas_call(
        _adaln_kernel,
        grid=(n // tn,),
        in_specs=[pl.BlockSpec((rows, d), lambda j: (0, 0)),
                  pl.BlockSpec((d, tn), lambda j: (0, j)),
                  pl.BlockSpec((1, tn), lambda j: (0, j))],
        out_specs=pl.BlockSpec((rows, tn), lambda j: (0, j)),
        out_shape=jax.ShapeDtypeStruct((rows, n), F32),
        compiler_params=_cparams(("arbitrary",)),
        name="adaln",
    )(cp, w, b.reshape(1, n))
    return out[:bsz]


def _norm_mod(x, g, sc, sh):
    ms = jnp.mean(x * x, axis=-1, keepdims=True)
    y = x * lax.rsqrt(ms + NORM_EPS)
    y = y * g
    return y * (1.0 + sc) + sh


def _norm_kernel(x_ref, g_ref, sc_ref, sh_ref, o_ref):
    o_ref[...] = _norm_mod(x_ref[...], g_ref[...], sc_ref[...], sh_ref[...]).astype(BF16)


def _norm(x2, g, mod3, seq, tm=512):
    m, d = x2.shape
    per_b = seq // tm
    return pl.pallas_call(
        _norm_kernel,
        grid=(m // tm,),
        in_specs=[pl.BlockSpec((tm, d), lambda i: (i, 0)),
                  pl.BlockSpec((1, d), lambda i: (0, 0)),
                  pl.BlockSpec((None, 1, d), lambda i: ((i // per_b) * 6 + 1, 0, 0)),
                  pl.BlockSpec((None, 1, d), lambda i: ((i // per_b) * 6 + 0, 0, 0))],
        out_specs=pl.BlockSpec((tm, d), lambda i: (i, 0)),
        out_shape=jax.ShapeDtypeStruct((m, d), BF16),
        compiler_params=_cparams(("arbitrary",)),
        name="norm1",
    )(x2, g, mod3, mod3)


def _mm_nt_kernel(h_ref, wt_ref, o_ref, wb_ref):
    @pl.when(pl.program_id(1) == 0)
    def _():
        wb_ref[...] = wt_ref[...].astype(BF16)

    o_ref[...] = _dot_nt(h_ref[...], wb_ref[...])


def _matmul_nt(h, wt, col_of_tile, n_tiles, tn, name, tm=1024):
    m, d = h.shape
    return pl.pallas_call(
        _mm_nt_kernel,
        grid=(n_tiles, m // tm),
        in_specs=[pl.BlockSpec((tm, d), lambda j, i: (i, 0)),
                  pl.BlockSpec((pl.Element(tn), pl.Element(d)),
                               lambda j, i: (pl.multiple_of(col_of_tile(j), SUBLANES), 0))],
        out_specs=pl.BlockSpec((tm, tn), lambda j, i: (i, j)),
        out_shape=jax.ShapeDtypeStruct((m, n_tiles * tn), F32),
        scratch_shapes=[pltpu.VMEM((tn, d), BF16)],
        compiler_params=_cparams(("arbitrary", "arbitrary")),
        name=name,
    )(h, wt)


def _ffn_glu_kernel(h_ref, wg_ref, wu_ref, o_ref, wgb_ref, wub_ref):
    @pl.when(pl.program_id(1) == 0)
    def _():
        wgb_ref[...] = wg_ref[...].astype(BF16)
        wub_ref[...] = wu_ref[...].astype(BF16)

    h = h_ref[...]
    a = _dot(h, wgb_ref[...])
    u = _dot(h, wub_ref[...])
    o_ref[...] = (a * _sigmoid(a) * u).astype(o_ref.dtype)


def _ffn_glu(h, wg, wu, tm=1024, tn=512):
    m, d = h.shape
    n = wg.shape[1]
    return pl.pallas_call(
        _ffn_glu_kernel,
        grid=(n // tn, m // tm),
        in_specs=[pl.BlockSpec((tm, d), lambda j, i: (i, 0)),
                  pl.BlockSpec((d, tn), lambda j, i: (0, j)),
                  pl.BlockSpec((d, tn), lambda j, i: (0, j))],
        out_specs=pl.BlockSpec((tm, tn), lambda j, i: (i, j)),
        out_shape=jax.ShapeDtypeStruct((m, n), BF16),
        scratch_shapes=[pltpu.VMEM((d, tn), BF16)] * 2,
        compiler_params=_cparams(("arbitrary", "arbitrary")),
        name="ffn_glu",
    )(h, wg, wu)


def _out_proj_kernel(a_ref, r_ref, wa_ref, wr_ref, x_ref, gt_ref, g_ref, sc_ref, sh_ref,
                     x1_ref, h2_ref):
    mixed = _dot(a_ref[...], wa_ref[...]) + _dot(r_ref[...], wr_ref[...])
    x1 = x_ref[...] + gt_ref[...] * mixed
    x1_ref[...] = x1
    h2_ref[...] = _norm_mod(x1, g_ref[...], sc_ref[...], sh_ref[...]).astype(BF16)


def _out_proj(att, rwkv, wa, wr, x2, mod3, g2, seq, tm=512):
    m, ka = att.shape
    kr = rwkv.shape[1]
    n = wa.shape[1]
    per_b = seq // tm

    def mod_row(j):
        return pl.BlockSpec((None, 1, n), lambda i: ((i // per_b) * 6 + j, 0, 0))

    row = pl.BlockSpec((tm, n), lambda i: (i, 0))
    return pl.pallas_call(
        _out_proj_kernel,
        grid=(m // tm,),
        in_specs=[pl.BlockSpec((tm, ka), lambda i: (i, 0)),
                  pl.BlockSpec((tm, kr), lambda i: (i, 0)),
                  pl.BlockSpec((ka, n), lambda i: (0, 0)),
                  pl.BlockSpec((kr, n), lambda i: (0, 0)),
                  row, mod_row(2),
                  pl.BlockSpec((1, n), lambda i: (0, 0)), mod_row(4), mod_row(3)],
        out_specs=[row, row],
        out_shape=[jax.ShapeDtypeStruct((m, n), F32), jax.ShapeDtypeStruct((m, n), BF16)],
        compiler_params=_cparams(("arbitrary",)),
        name="out_proj",
    )(att, rwkv, wa, wr, x2, mod3, g2, mod3, mod3)


def _ffn_down_kernel(h_ref, w_ref, x_ref, gt_ref, o_ref, wb_ref):
    @pl.when(pl.program_id(1) == 0)
    def _():
        wb_ref[...] = w_ref[...].astype(BF16)

    o_ref[...] = x_ref[...] + gt_ref[...] * _dot(h_ref[...], wb_ref[...])


def _ffn_down(h, w, x2, mod3, seq, tm=512, tn=512):
    m, kdim = h.shape
    n = w.shape[1]
    per_b = seq // tm
    return pl.pallas_call(
        _ffn_down_kernel,
        grid=(n // tn, m // tm),
        in_specs=[pl.BlockSpec((tm, kdim), lambda j, i: (i, 0)),
                  pl.BlockSpec((kdim, tn), lambda j, i: (0, j)),
                  pl.BlockSpec((tm, tn), lambda j, i: (i, j)),
                  pl.BlockSpec((None, 1, tn), lambda j, i: ((i // per_b) * 6 + 5, 0, j))],
        out_specs=pl.BlockSpec((tm, tn), lambda j, i: (i, j)),
        out_shape=jax.ShapeDtypeStruct((m, n), F32),
        scratch_shapes=[pltpu.VMEM((kdim, tn), BF16)],
        compiler_params=_cparams(("arbitrary", "arbitrary")),
        name="ffn_down",
    )(h, w, x2, mod3)


def _rope(x, cos, sin_lo, sin_hi, half):
    return (x * cos + pltpu.roll(x, LANES - half, 1) * sin_lo
            + pltpu.roll(x, half, 1) * sin_hi)


def _att_prep_kernel(pos_ref, q_ref, k_ref, v_ref, iq_ref, ikw_ref,
                     qg_ref, kg_ref, fa_ref, fi_ref,
                     qo_ref, ko_ref, vo_ref, iqo_ref, iko_ref, iwo_ref):
    pos = pos_ref[...].astype(F32)
    lane = lax.broadcasted_iota(I32, (1, LANES), 1)

    half_a = ATT_HEAD_DIM // ROPE_FRACTION // 2
    ang = pos * fa_ref[...]
    cos_a, sin_a = jnp.cos(ang), jnp.sin(ang)
    lo_a = jnp.where(lane < half_a, -sin_a, 0.0)
    hi_a = jnp.where((lane >= half_a) & (lane < 2 * half_a), sin_a, 0.0)

    half_i = IDX_HEAD_DIM // ROPE_FRACTION // 2
    lane_i = lane & (IDX_HEAD_DIM - 1)
    ang = pos * fi_ref[...]
    cos_i, sin_i = jnp.cos(ang), jnp.sin(ang)
    lo_i = jnp.where(lane_i < half_i, -sin_i, 0.0)
    hi_i = jnp.where((lane_i >= half_i) & (lane_i < 2 * half_i), sin_i, 0.0)

    def head_norm(x, g):
        ms = jnp.mean(x * x, axis=-1, keepdims=True)
        return x * lax.rsqrt(ms + NORM_EPS) * g

    qg, kg = qg_ref[...], kg_ref[...]
    tm = q_ref.shape[0]
    pad_row = lax.broadcasted_iota(I32, (V_ROWS - LANES, tm), 0)
    ones_rows = jnp.where(pad_row == 0, 1.0, 0.0).astype(BF16)
    q_scale = (ATT_HEAD_DIM ** -0.5) * LOG2E
    for h in range(ATT_HEADS):
        sl = slice(h * LANES, (h + 1) * LANES)
        qh = _rope(head_norm(q_ref[:, sl], qg), cos_a, lo_a, hi_a, half_a)
        qo_ref[h] = (qh * q_scale).T.astype(BF16)
        kh = _rope(head_norm(k_ref[:, sl], kg), cos_a, lo_a, hi_a, half_a)
        ko_ref[:, sl] = kh.astype(BF16)
        vo_ref[h, 0:LANES, :] = v_ref[:, sl].T.astype(BF16)
        vo_ref[h, LANES:V_ROWS, :] = ones_rows
    for h in range(IDX_HEADS * IDX_HEAD_DIM // LANES):
        sl = slice(h * LANES, (h + 1) * LANES)
        ih = _rope(iq_ref[:, sl], cos_i, lo_i, hi_i, half_i)
        iqo_ref[h] = (ih * (IDX_HEAD_DIM ** -0.5)).T.astype(BF16)
    ikw = ikw_ref[...]
    iko_ref[...] = _rope(ikw, cos_i, lo_i, hi_i, half_i).astype(BF16)
    iwo_ref[...] = (ikw * (IDX_HEADS ** -0.5)).T[IDX_HEAD_DIM:IDX_HEAD_DIM + IDX_HEADS, :]


def _att_prep(proj_a, proj_r, pos2, qg, kg, fa, fi, tm):
    m = proj_a.shape[0]
    w = ATT_WIDTH
    npair = IDX_HEADS * IDX_HEAD_DIM // LANES

    def wide(off):
        return pl.BlockSpec((tm, w), lambda i: (i, off // w))

    const = pl.BlockSpec((1, LANES), lambda i: (0, 0))
    return pl.pallas_call(
        _att_prep_kernel,
        grid=(m // tm,),
        in_specs=[pl.BlockSpec((tm, 1), lambda i: (i, 0)),
                  wide(OFF_Q), wide(OFF_K), wide(OFF_V), wide(OFF_IQ),
                  pl.BlockSpec((tm, LANES), lambda i: (i, OFF_IKW // LANES)),
                  const, const, const, const],
        out_specs=[pl.BlockSpec((ATT_HEADS, LANES, tm), lambda i: (0, 0, i)),
                   pl.BlockSpec((tm, w), lambda i: (i, 0)),
                   pl.BlockSpec((ATT_HEADS, None, V_ROWS, tm), lambda i: (0, i, 0, 0)),
                   pl.BlockSpec((npair, LANES, tm), lambda i: (0, 0, i)),
                   pl.BlockSpec((tm, LANES), lambda i: (i, 0)),
                   pl.BlockSpec((IDX_HEADS, tm), lambda i: (0, i))],
        out_shape=[jax.ShapeDtypeStruct((ATT_HEADS, LANES, m), BF16),
                   jax.ShapeDtypeStruct((m, w), BF16),
                   jax.ShapeDtypeStruct((ATT_HEADS, m // tm, V_ROWS, tm), BF16),
                   jax.ShapeDtypeStruct((npair, LANES, m), BF16),
                   jax.ShapeDtypeStruct((m, LANES), BF16),
                   jax.ShapeDtypeStruct((IDX_HEADS, m), F32)],
        compiler_params=_cparams(("arbitrary",)),
        name="att_prep",
    )(pos2, proj_a, proj_a, proj_a, proj_a, proj_r, qg, kg, fa, fi)


NEG_BIG = -1e30


INT_MIN = -2 ** 31
MAGNITUDE_BITS = 0x7FFFFFFF
KEY_NEG_INF = -2139095041
BOUND_SLACK = 1.02
MAX_STATIC_SHIFT = 60.0
COUNT_ROWS = 32


def _key_to_float(key):
    bits = key ^ ((key >> 31) & jnp.int32(MAGNITUDE_BITS))
    return lax.bitcast_convert_type(bits, F32)


def _dsa_kernel(bound_ref, qt_ref, iqt_ref, iwt_ref, k_ref, vt_ref, ik_ref, o_ref,
                sc_ref, m_ref, acc_ref, *, tq, tk, topk):
    qi = pl.program_id(1)
    n_kb = (qi * tq + tq - 1) // tk + 1
    key0 = lax.broadcasted_iota(I32, (tk, tq), 0)
    qidx = qi * tq + lax.broadcasted_iota(I32, (tk, tq), 1)
    iw = iwt_ref[...]

    def score_body(kb, carry):
        start = pl.multiple_of(kb * tk, tk)
        ikb = ik_ref[pl.ds(start, tk), 0:IDX_HEAD_DIM]
        s = jnp.zeros((tk, tq), F32)
        for h in range(IDX_HEADS):
            off = (h % 2) * IDX_HEAD_DIM
            d = _dot(ikb, iqt_ref[h // 2, off:off + IDX_HEAD_DIM, :])
            s = s + jnp.maximum(d, 0.0) * iw[h:h + 1, :]
        sc_ref[kb] = jnp.where(kb * tk + key0 <= qidx, s, -jnp.inf)
        return carry

    lax.fori_loop(0, n_kb, score_body, 0)

    def bit_body(i, cand):
        trial = cand ^ lax.shift_left(jnp.int32(1), 31 - i)
        trial_f = _key_to_float(trial)

        def cnt_body(kb, acc):
            ge = jnp.where(sc_ref[kb] >= trial_f, 1.0, 0.0)
            return acc + jnp.sum(ge.reshape(tk // COUNT_ROWS, COUNT_ROWS, tq), axis=0)

        acc = lax.fori_loop(0, n_kb, cnt_body, jnp.zeros((COUNT_ROWS, tq), F32))
        cnt = jnp.sum(acc, axis=0, keepdims=True)
        return jnp.where(cnt >= topk, trial, cand)

    cand = lax.fori_loop(0, 32, bit_body, jnp.full((1, tq), INT_MIN, I32))
    tau = _key_to_float(jnp.maximum(cand, jnp.int32(KEY_NEG_INF)))

    acc_ref[...] = jnp.zeros_like(acc_ref)
    bound = bound_ref[0]

    def logits(kb, h):
        start = pl.multiple_of(kb * tk, tk)
        return _dot(k_ref[pl.ds(start, tk), h * LANES:(h + 1) * LANES], qt_ref[h])

    def selected(kb):
        return (sc_ref[kb] >= tau) & (kb * tk + key0 <= qidx)

    @pl.when(bound <= MAX_STATIC_SHIFT)
    def _():
        def att_body(kb, carry):
            bias = jnp.where(selected(kb), -bound, -jnp.inf)
            for h in range(ATT_HEADS):
                p = jnp.exp2(logits(kb, h) + bias)
                acc_ref[h] += _dot(vt_ref[h, kb], p.astype(BF16))
            return carry

        lax.fori_loop(0, n_kb, att_body, 0)

    @pl.when(bound > MAX_STATIC_SHIFT)
    def _():
        m_ref[...] = jnp.full_like(m_ref, NEG_BIG)

        def att_body(kb, carry):
            bias = jnp.where(selected(kb), 0.0, -jnp.inf)
            for h in range(ATT_HEADS):
                s = logits(kb, h) + bias
                m_prev = m_ref[h]
                m_next = jnp.maximum(m_prev, jnp.max(s, axis=0, keepdims=True))
                p = jnp.exp2(s - m_next)
                alpha = jnp.exp2(m_prev - m_next)
                acc_ref[h] = alpha * acc_ref[h] + _dot(vt_ref[h, kb], p.astype(BF16))
                m_ref[h] = m_next
            return carry

        lax.fori_loop(0, n_kb, att_body, 0)

    for h in range(ATT_HEADS):
        out = acc_ref[h, 0:LANES, :] / acc_ref[h, LANES:LANES + 1, :]
        o_ref[:, h * LANES:(h + 1) * LANES] = out.T.astype(o_ref.dtype)


def _dsa(bound, qt, iqt, iwt, k, vt, ik, bsz, seq, tk, tq=512):
    tq = min(tq, seq)
    topk = min(TOPK_MAX, seq // 4)
    w = ATT_WIDTH
    nq = seq // tq
    nkb = seq // tk
    npair = iqt.shape[0]
    kern = functools.partial(_dsa_kernel, tq=tq, tk=tk, topk=topk)
    return pl.pallas_call(
        kern,
        grid=(bsz, nq),
        in_specs=[pl.BlockSpec(memory_space=pltpu.SMEM),
                  pl.BlockSpec((ATT_HEADS, LANES, tq), lambda b, i: (0, 0, b * nq + i)),
                  pl.BlockSpec((npair, LANES, tq), lambda b, i: (0, 0, b * nq + i)),
                  pl.BlockSpec((IDX_HEADS, tq), lambda b, i: (0, b * nq + i)),
                  pl.BlockSpec((seq, w), lambda b, i: (b, 0)),
                  pl.BlockSpec((ATT_HEADS, nkb, V_ROWS, tk), lambda b, i: (0, b, 0, 0)),
                  pl.BlockSpec((seq, LANES), lambda b, i: (b, 0))],
        out_specs=pl.BlockSpec((tq, w), lambda b, i: (b * nq + i, 0)),
        out_shape=jax.ShapeDtypeStruct((bsz * seq, w), BF16),
        scratch_shapes=[pltpu.VMEM((nkb, tk, tq), F32),
                        pltpu.VMEM((ATT_HEADS, 1, tq), F32),
                        pltpu.VMEM((ATT_HEADS, V_ROWS, tq), F32)],
        compiler_params=_cparams(("arbitrary", "arbitrary")),
        name="dsa",
    )(bound, qt, iqt, iwt, k, vt, ik)


def _rwkv_prep_kernel(rr_ref, rk_ref, rv_ref, wa_ref, xg_ref,
                      mr_ref, mk_ref, mv_ref, mwa_ref, mg_ref,
                      w0_ref, a0_ref, kk_ref, ka_ref, rkp_ref,
                      wup_ref, aup_ref, gup_ref, e_ref,
                      r_o, lw_o, k_o, v_o, kkn_o, bb_o, g_o, bon_o,
                      c_r, c_k, c_v, c_wa, c_g, *, tm):
    first = pl.program_id(1) == 0

    def shift(y_ref, carry_ref, mu_ref):
        y = y_ref[...]

        @pl.when(first)
        def _():
            carry_ref[...] = jnp.zeros_like(carry_ref)

        prev_last = carry_ref[7:8, :]
        rolled = pltpu.roll(y, 1, 0)
        rows = lax.broadcasted_iota(I32, y.shape, 0)
        yprev = jnp.where(rows == 0, prev_last, rolled)
        carry_ref[...] = y[tm - 8:tm, :]
        return y + (yprev - y) * mu_ref[...]

    r = shift(rr_ref, c_r, mr_ref)
    k = shift(rk_ref, c_k, mk_ref)
    v = shift(rv_ref, c_v, mv_ref)
    wa = shift(wa_ref, c_wa, mwa_ref)
    xw = pltpu.roll(wa, TAIL_TILE - XW_LANE, 1)[:, :LORA_PAD]
    xa = pltpu.roll(wa, TAIL_TILE - XA_LANE, 1)[:, :LORA_PAD]
    xg = shift(xg_ref, c_g, mg_ref)

    w_raw = w0_ref[...] + _dot(jnp.tanh(xw).astype(BF16), wup_ref[...])
    z = -w_raw
    softplus = jnp.maximum(z, 0.0) + jnp.log(1.0 + jnp.exp(-jnp.abs(z)))
    lw_o[...] = -jnp.exp(-softplus - 0.5)
    a = _sigmoid(a0_ref[...] + _dot(xa.astype(BF16), aup_ref[...]))
    g_o[...] = _dot(_sigmoid(xg).astype(BF16), gup_ref[...]).astype(BF16)

    e = e_ref[...]
    kk = k * kk_ref[...]
    ss = _dot_hi(kk * kk, e)
    kk = kk / jnp.maximum(jnp.sqrt(ss), 1e-12)
    kmod = k * (1.0 + (a - 1.0) * ka_ref[...])
    r_o[...] = r.astype(BF16)
    k_o[...] = kmod.astype(BF16)
    v_o[...] = v.astype(BF16)
    kkn_o[...] = kk.astype(BF16)
    bb_o[...] = (kk * a).astype(BF16)
    bon_o[...] = (_dot_hi(r * kmod * rkp_ref[...], e) * v).astype(BF16)


def _rwkv_prep(proj_r, proj_t, mus, vecs, wup, aup, gup, e, bsz, seq, tm=256):
    w = RWKV_WIDTH
    per_b = seq // tm

    def wide(off):
        return pl.BlockSpec((tm, w), lambda b, i: (b * per_b + i, off // w))

    def tail(off):
        return pl.BlockSpec((tm, TAIL_TILE), lambda b, i: (b * per_b + i, off // TAIL_TILE))

    def const(shape):
        return pl.BlockSpec(shape, lambda b, i: (0, 0))

    out_blk = pl.BlockSpec((tm, w), lambda b, i: (b * per_b + i, 0))
    kern = functools.partial(_rwkv_prep_kernel, tm=tm)
    return pl.pallas_call(
        kern,
        grid=(bsz, per_b),
        in_specs=[wide(OFF_RR), wide(OFF_RK), wide(OFF_RV), tail(OFF_WA), tail(OFF_XG),
                  const((1, w)), const((1, w)), const((1, w)),
                  const((1, TAIL_TILE)), const((1, GATE_LORA)),
                  const((1, w)), const((1, w)), const((1, w)), const((1, w)), const((1, w)),
                  const((LORA_PAD, w)), const((LORA_PAD, w)), const((GATE_LORA, w)), const((w, w))],
        out_specs=[out_blk] * 8,
        out_shape=[jax.ShapeDtypeStruct((bsz * seq, w), F32 if i == 1 else BF16) for i in range(8)],
        scratch_shapes=[pltpu.VMEM((8, w), F32)] * 3
        + [pltpu.VMEM((8, TAIL_TILE), F32), pltpu.VMEM((8, GATE_LORA), F32)],
        compiler_params=_cparams(("arbitrary", "arbitrary")),
        name="rwkv_prep",
    )(proj_r, proj_r, proj_r, proj_t, proj_t, *mus, *vecs, wup, aup, gup, e)


def _rwkv_core_kernel(r_ref, lw_ref, k_ref, v_ref, kk_ref, bb_ref, g_ref, bon_ref, lg_ref, lb_ref,
                      o_ref, z_ref):
    c = CHUNK

    @pl.when(pl.program_id(1) == 0)
    def _():
        z_ref[...] = jnp.zeros_like(z_ref)

    lw = lw_ref[...]
    tri = jnp.where(lax.broadcasted_iota(I32, (c, c), 1) <= lax.broadcasted_iota(I32, (c, c), 0),
                    1.0, 0.0).astype(BF16)
    hi = lw.astype(BF16)
    rem = lw - hi.astype(F32)
    mid = rem.astype(BF16)
    lo = (rem - mid.astype(F32)).astype(BF16)
    cum = _dot(tri, hi) + _dot(tri, mid) + _dot(tri, lo)
    p_in = jnp.exp(cum)
    p_ex = jnp.exp(cum - lw)
    p_inv = jnp.exp(-cum)
    p_end = p_in[c - 1:c, :]
    a_t = -kk_ref[...].astype(F32) * p_ex
    r_t = r_ref[...].astype(F32) * p_in
    b_h = bb_ref[...].astype(F32) * p_inv
    k_h = k_ref[...].astype(F32) * p_inv
    b_e = b_h * p_end
    k_e = k_h * p_end
    v = v_ref[...].astype(F32)

    n2 = 2 * c
    lane = lax.broadcasted_iota(I32, (1, LANES), 1)
    head0 = lane < RWKV_HEAD_DIM
    ri = lax.broadcasted_iota(I32, (n2, n2), 0)
    ci = lax.broadcasted_iota(I32, (n2, n2), 1)
    same = (ri >= c) == (ci >= c)
    strict = same & (ci < ri)
    incl = same & (ci <= ri)
    eye = ri == ci
    own = ((lax.broadcasted_iota(I32, (n2, LANES), 0) >= c)
           == (lax.broadcasted_iota(I32, (n2, LANES), 1) >= RWKV_HEAD_DIM))
    inv_n = 1.0 / RWKV_HEAD_DIM

    def stack(y):
        return jnp.concatenate([jnp.where(head0, y, 0.0), jnp.where(head0, 0.0, y)], axis=0)

    pairs = range(RWKV_WIDTH // LANES)
    sls = [slice(p * LANES, (p + 1) * LANES) for p in pairs]
    a_s = [stack(a_t[:, sl]) for sl in sls]
    r_s = [stack(r_t[:, sl]) for sl in sls]
    v_s = [stack(v[:, sl]).astype(BF16) for sl in sls]
    g1 = [_dot_nt(jnp.concatenate([a_s[p], r_s[p]], axis=0).astype(BF16),
                  jnp.concatenate([stack(b_h[:, sls[p]]), stack(k_h[:, sls[p]])], axis=0).astype(BF16))
          for p in pairs]
    pw = [jnp.where(strict, g[:n2, :n2], 0.0).astype(BF16) for g in g1]
    a_rb = [jnp.where(incl, g[n2:, :n2], 0.0).astype(BF16) for g in g1]
    a_rk = [jnp.where(incl, g[n2:, n2:], 0.0).astype(BF16) for g in g1]
    akv = [_dot(jnp.where(strict, g1[p][:n2, n2:], 0.0).astype(BF16), v_s[p]) for p in pairs]
    xc = [jnp.concatenate([a_s[p], akv[p]], axis=1) for p in pairs]
    steps = int(np.log2(c))
    for i in range(steps):
        if i + 1 < steps:
            res = [_dot(pw[p], jnp.concatenate([pw[p], xc[p].astype(BF16)], axis=1)) for p in pairs]
            xc = [xc[p] + res[p][:, n2:] for p in pairs]
            pw = [res[p][:, :n2].astype(BF16) for p in pairs]
        else:
            xc = [xc[p] + _dot(pw[p], xc[p].astype(BF16)) for p in pairs]
    xcb = [x.astype(BF16) for x in xc]
    r2 = [_dot(a_rb[p], xcb[p]) for p in pairs]
    ov = [r2[p][:, LANES:] + _dot(a_rk[p], v_s[p]) for p in pairs]
    mg = [_dot_tn(stack(b_e[:, sls[p]]).astype(BF16), xcb[p]) for p in pairs]
    kv = [_dot_tn(stack(k_e[:, sls[p]]).astype(BF16), v_s[p]) for p in pairs]
    for p in pairs:
        q_s = r_s[p] + r2[p][:, :LANES]
        mmat = mg[p][:, :LANES] + jnp.where(eye, p_end[:, sls[p]], 0.0)
        z_hi, z_lo = _split2(z_ref[p])
        qm = jnp.concatenate([q_s, mmat], axis=0).astype(BF16)
        res = _dot(qm, z_hi) + _dot(qm, z_lo)
        z_ref[p] = res[n2:] + mg[p][:, LANES:] + kv[p]
        o_s = res[:n2] + ov[p]
        mean = jnp.sum(o_s, axis=1, keepdims=True) * inv_n
        dev = jnp.where(own, o_s - mean, 0.0)
        var = jnp.sum(dev * dev, axis=1, keepdims=True) * inv_n
        y = dev * lax.rsqrt(var + LNX_EPS)
        y = (y[:c] + y[c:]) * lg_ref[:, sls[p]] + lb_ref[:, sls[p]]
        out = (y + bon_ref[:, sls[p]].astype(F32)) * g_ref[:, sls[p]].astype(F32)
        o_ref[:, sls[p]] = out.astype(o_ref.dtype)


def _rwkv_core(r, lw, k, v, kk, bb, g, bon, lg, lb, bsz, seq):
    c = CHUNK
    w = RWKV_WIDTH
    per_b = seq // c
    blk = pl.BlockSpec((c, w), lambda b, i: (b * per_b + i, 0))
    vec = pl.BlockSpec((1, w), lambda b, i: (0, 0))
    return pl.pallas_call(
        _rwkv_core_kernel,
        grid=(bsz, per_b),
        in_specs=[blk] * 8 + [vec, vec],
        out_specs=blk,
        out_shape=jax.ShapeDtypeStruct((bsz * seq, w), BF16),
        scratch_shapes=[pltpu.VMEM((w // LANES, LANES, LANES), F32)],
        compiler_params=_cparams(("arbitrary", "arbitrary")),
        name="rwkv_core",
    )(r, lw, k, v, kk, bb, g, bon, lg, lb)


def _rope_freqs(head_dim):
    rot = head_dim // ROPE_FRACTION
    half = rot // 2
    inv = ROPE_THETA ** (-jnp.arange(half, dtype=F32) / half)
    per_head = jnp.concatenate([inv, inv, jnp.zeros((head_dim - rot,), F32)])
    return jnp.tile(per_head, LANES // head_dim).reshape(1, LANES)


def kernel(x, c, positions, w_ada, b_ada, norm1_g, w_in, q_norm_g, k_norm_g, rwkv_mu, rwkv_w0,
           rwkv_w_up, rwkv_a0, rwkv_a_up, rwkv_g_up, rwkv_k_k, rwkv_k_a, rwkv_r_k, rwkv_lnx_g,
           rwkv_lnx_b, w_out, norm2_g, w_ffn_gate, w_ffn_up, w_ffn_down):
    bsz, seq, d = x.shape
    depth = w_ada.shape[0]
    m = bsz * seq
    pos2 = positions.reshape(m, 1)
    fa = _rope_freqs(ATT_HEAD_DIM)
    fi = _rope_freqs(IDX_HEAD_DIM)
    hd = RWKV_HEAD_DIM
    e = jnp.kron(jnp.eye(RWKV_HEADS, dtype=F32), jnp.ones((hd, hd), F32)).astype(BF16)
    x2 = x.reshape(m, d)

    for l in range(depth):
        mod = _adaln(c, w_ada[l], b_ada[l])
        mod3 = mod.reshape(bsz * 6, 1, d)

        h1 = _norm(x2, norm1_g[l].reshape(1, d), mod3, seq)
        w_in_t = w_in[l].T
        proj_a = _matmul_nt(h1, w_in_t, lambda j: j * ATT_WIDTH, 4, ATT_WIDTH, "in_proj_att")
        proj_r = _matmul_nt(h1, w_in_t, lambda j: ATT_COLS + j * RWKV_WIDTH, 3, RWKV_WIDTH,
                            "in_proj_rkv")
        proj_t = _matmul_nt(
            h1, w_in_t,
            lambda j: jnp.where(j < 2, TAIL_COL0 + j * TAIL_TILE, 4 * ATT_WIDTH),
            3, TAIL_TILE, "in_proj_tail")

        qt, kn, vt, iqt, ik, iwt = _att_prep(proj_a, proj_t, pos2, q_norm_g[l].reshape(1, -1),
                                             k_norm_g[l].reshape(1, -1), fa, fi, KEY_TILE)
        bound = (ATT_HEAD_DIM ** 0.5 * LOG2E * BOUND_SLACK
                 * jnp.max(jnp.abs(q_norm_g[l])) * jnp.max(jnp.abs(k_norm_g[l])))
        att = _dsa(bound.reshape(1).astype(F32), qt, iqt, iwt, kn, vt, ik, bsz, seq, KEY_TILE)

        mu = rwkv_mu[l]
        w3 = 3 * RWKV_WIDTH

        n_wa = DECAY_LORA + AAA_LORA
        mu_wa = jnp.zeros((1, TAIL_TILE), F32).at[0, XW_LANE:XW_LANE + n_wa].set(mu[w3:w3 + n_wa])
        mus = [mu[0:RWKV_WIDTH].reshape(1, -1), mu[RWKV_WIDTH:2 * RWKV_WIDTH].reshape(1, -1),
               mu[2 * RWKV_WIDTH:w3].reshape(1, -1), mu_wa, mu[w3 + n_wa:].reshape(1, -1)]
        vecs = [rwkv_w0[l].reshape(1, -1), rwkv_a0[l].reshape(1, -1), rwkv_k_k[l].reshape(1, -1),
                rwkv_k_a[l].reshape(1, -1), rwkv_r_k[l].reshape(1, -1)]

        def pad_rows(wm):
            return jnp.zeros((LORA_PAD, wm.shape[1]), F32).at[:wm.shape[0]].set(wm).astype(BF16)

        r, lw, km, vv, kk, bb, g, bon = _rwkv_prep(
            proj_r, proj_t, mus, vecs, pad_rows(rwkv_w_up[l]), pad_rows(rwkv_a_up[l]),
            rwkv_g_up[l].astype(BF16), e, bsz, seq)
        rw = _rwkv_core(r, lw, km, vv, kk, bb, g, bon, rwkv_lnx_g[l].reshape(1, -1),
                        rwkv_lnx_b[l].reshape(1, -1), bsz, seq)

        wo = w_out[l].astype(BF16)
        x2, h2 = _out_proj(att, rw, wo[:ATT_WIDTH], wo[ATT_WIDTH:], x2, mod3,
                           norm2_g[l].reshape(1, d), seq)

        hglu = _ffn_glu(h2, w_ffn_gate[l], w_ffn_up[l])
        x2 = _ffn_down(hglu, w_ffn_down[l], x2, mod3, seq)
    return x2.reshape(bsz, seq, d)
```

```python
import functools

import jax
import jax.numpy as jnp
import numpy as np
from jax import lax
from jax.experimental import pallas as pl
from jax.experimental.pallas import tpu as pltpu

F32 = jnp.float32
BF16 = jnp.bfloat16
I32 = jnp.int32

D_MODEL = 2048
ATT_HEADS = 8
ATT_HEAD_DIM = 128
ATT_WIDTH = ATT_HEADS * ATT_HEAD_DIM
RWKV_WIDTH = D_MODEL - ATT_WIDTH
RWKV_HEAD_DIM = 64
RWKV_HEADS = RWKV_WIDTH // RWKV_HEAD_DIM
IDX_HEADS = 16
IDX_HEAD_DIM = 64
TOPK_MAX = 256
ROPE_THETA = 500000.0
ROPE_FRACTION = 4
DECAY_LORA = 96
AAA_LORA = 96
GATE_LORA = 256
NORM_EPS = 1e-6
LNX_EPS = 64e-5

LANES = 128
SUBLANES = 8
LORA_PAD = 128
ATT_COLS = 4 * ATT_WIDTH + IDX_HEAD_DIM + IDX_HEADS
IN_COLS = ATT_COLS + 3 * RWKV_WIDTH + DECAY_LORA + AAA_LORA + GATE_LORA
OFF_Q, OFF_K, OFF_V, OFF_IQ = 0, 1024, 2048, 3072
OFF_RR, OFF_RK, OFF_RV = 0, 1024, 2048
TAIL_TILE = 256
TAIL_COL0 = IN_COLS - 2 * TAIL_TILE
XW_LANE = IN_COLS - GATE_LORA - AAA_LORA - DECAY_LORA - TAIL_COL0
XA_LANE = XW_LANE + DECAY_LORA
OFF_WA, OFF_XG, OFF_IKW = 0, 256, 512

LOG2E = 1.4426950408889634
V_ROWS = 144
KEY_TILE = 512
CHUNK = 64
VMEM_LIMIT = 56 * 1024 * 1024


def _cparams(sem):
    return pltpu.CompilerParams(dimension_semantics=sem, vmem_limit_bytes=VMEM_LIMIT)


def _dot(a, b):
    return jnp.dot(a, b, preferred_element_type=F32)


def _dot_nt(a, b):
    return lax.dot_general(a, b, (((1,), (1,)), ((), ())), preferred_element_type=F32)


def _dot_tn(a, b):
    return lax.dot_general(a, b, (((0,), (0,)), ((), ())), preferred_element_type=F32)


def _split2(x):
    hi = x.astype(BF16)
    lo = (x - hi.astype(F32)).astype(BF16)
    return hi, lo


def _dot_hi(x, w):
    hi, lo = _split2(x)
    return _dot(hi, w) + _dot(lo, w)


def _sigmoid(x):
    return 1.0 / (1.0 + jnp.exp(-x))


def _adaln_kernel(c_ref, w_ref, b_ref, o_ref):
    c = c_ref[...]
    ca = c * _sigmoid(c)
    o_ref[...] = _dot(ca.astype(BF16), w_ref[...].astype(BF16)) + b_ref[...]


def _adaln(c, w, b):
    bsz, d = c.shape
    n = w.shape[1]
    rows = 8
    cp = jnp.zeros((rows, d), F32).at[:bsz].set(c)
    tn = 1024
    out = pl.pallas_call(
        _adaln_kernel,
        grid=(n // tn,),
        in_specs=[pl.BlockSpec((rows, d), lambda j: (0, 0)),
                  pl.BlockSpec((d, tn), lambda j: (0, j)),
                  pl.BlockSpec((1, tn), lambda j: (0, j))],
        out_specs=pl.BlockSpec((rows, tn), lambda j: (0, j)),
        out_shape=jax.ShapeDtypeStruct((rows, n), F32),
        compiler_params=_cparams(("arbitrary",)),
        name="adaln",
    )(cp, w, b.reshape(1, n))
    return out[:bsz]


def _norm_mod(x, g, sc, sh):
    ms = jnp.mean(x * x, axis=-1, keepdims=True)
    y = x * lax.rsqrt(ms + NORM_EPS)
    y = y * g
    return y * (1.0 + sc) + sh


def _norm_kernel(x_ref, g_ref, sc_ref, sh_ref, o_ref):
    o_ref[...] = _norm_mod(x_ref[...], g_ref[...], sc_ref[...], sh_ref[...]).astype(BF16)


def _norm(x2, g, mod3, seq, tm=512):
    m, d = x2.shape
    per_b = seq // tm
    return pl.pallas_call(
        _norm_kernel,
        grid=(m // tm,),
        in_specs=[pl.BlockSpec((tm, d), lambda i: (i, 0)),
                  pl.BlockSpec((1, d), lambda i: (0, 0)),
                  pl.BlockSpec((None, 1, d), lambda i: ((i // per_b) * 6 + 1, 0, 0)),
                  pl.BlockSpec((None, 1, d), lambda i: ((i // per_b) * 6 + 0, 0, 0))],
        out_specs=pl.BlockSpec((tm, d), lambda i: (i, 0)),
        out_shape=jax.ShapeDtypeStruct((m, d), BF16),
        compiler_params=_cparams(("arbitrary",)),
        name="norm1",
    )(x2, g, mod3, mod3)


def _mm_nt_kernel(h_ref, wt_ref, o_ref, wb_ref):
    @pl.when(pl.program_id(1) == 0)
    def _():
        wb_ref[...] = wt_ref[...].astype(BF16)

    o_ref[...] = _dot_nt(h_ref[...], wb_ref[...])


def _matmul_nt(h, wt, col_of_tile, n_tiles, tn, name, tm=1024):
    m, d = h.shape
    return pl.pallas_call(
        _mm_nt_kernel,
        grid=(n_tiles, m // tm),
        in_specs=[pl.BlockSpec((tm, d), lambda j, i: (i, 0)),
                  pl.BlockSpec((pl.Element(tn), pl.Element(d)),
                               lambda j, i: (pl.multiple_of(col_of_tile(j), SUBLANES), 0))],
        out_specs=pl.BlockSpec((tm, tn), lambda j, i: (i, j)),
        out_shape=jax.ShapeDtypeStruct((m, n_tiles * tn), F32),
        scratch_shapes=[pltpu.VMEM((tn, d), BF16)],
        compiler_params=_cparams(("arbitrary", "arbitrary")),
        name=name,
    )(h, wt)


def _ffn_glu_kernel(h_ref, wg_ref, wu_ref, o_ref, wgb_ref, wub_ref):
    @pl.when(pl.program_id(1) == 0)
    def _():
        wgb_ref[...] = wg_ref[...].astype(BF16)
        wub_ref[...] = wu_ref[...].astype(BF16)

    h = h_ref[...]
    a = _dot(h, wgb_ref[...])
    u = _dot(h, wub_ref[...])
    o_ref[...] = (a * _sigmoid(a) * u).astype(o_ref.dtype)


def _ffn_glu(h, wg, wu, tm=1024, tn=512):
    m, d = h.shape
    n = wg.shape[1]
    return pl.pallas_call(
        _ffn_glu_kernel,
        grid=(n // tn, m // tm),
        in_specs=[pl.BlockSpec((tm, d), lambda j, i: (i, 0)),
                  pl.BlockSpec((d, tn), lambda j, i: (0, j)),
                  pl.BlockSpec((d, tn), lambda j, i: (0, j))],
        out_specs=pl.BlockSpec((tm, tn), lambda j, i: (i, j)),
        out_shape=jax.ShapeDtypeStruct((m, n), BF16),
        scratch_shapes=[pltpu.VMEM((d, tn), BF16)] * 2,
        compiler_params=_cparams(("arbitrary", "arbitrary")),
        name="ffn_glu",
    )(h, wg, wu)


def _out_proj_kernel(a_ref, r_ref, wa_ref, wr_ref, x_ref, gt_ref, g_ref, sc_ref, sh_ref,
                     x1_ref, h2_ref):
    mixed = _dot(a_ref[...], wa_ref[...]) + _dot(r_ref[...], wr_ref[...])
    x1 = x_ref[...] + gt_ref[...] * mixed
    x1_ref[...] = x1
    h2_ref[...] = _norm_mod(x1, g_ref[...], sc_ref[...], sh_ref[...]).astype(BF16)


def _out_proj(att, rwkv, wa, wr, x2, mod3, g2, seq, tm=512):
    m, ka = att.shape
    kr = rwkv.shape[1]
    n = wa.shape[1]
    per_b = seq // tm

    def mod_row(j):
        return pl.BlockSpec((None, 1, n), lambda i: ((i // per_b) * 6 + j, 0, 0))

    row = pl.BlockSpec((tm, n), lambda i: (i, 0))
    return pl.pallas_call(
        _out_proj_kernel,
        grid=(m // tm,),
        in_specs=[pl.BlockSpec((tm, ka), lambda i: (i, 0)),
                  pl.BlockSpec((tm, kr), lambda i: (i, 0)),
                  pl.BlockSpec((ka, n), lambda i: (0, 0)),
                  pl.BlockSpec((kr, n), lambda i: (0, 0)),
                  row, mod_row(2),
                  pl.BlockSpec((1, n), lambda i: (0, 0)), mod_row(4), mod_row(3)],
        out_specs=[row, row],
        out_shape=[jax.ShapeDtypeStruct((m, n), F32), jax.ShapeDtypeStruct((m, n), BF16)],
        compiler_params=_cparams(("arbitrary",)),
        name="out_proj",
    )(att, rwkv, wa, wr, x2, mod3, g2, mod3, mod3)


def _ffn_down_kernel(h_ref, w_ref, x_ref, gt_ref, o_ref, wb_ref):
    @pl.when(pl.program_id(1) == 0)
    def _():
        wb_ref[...] = w_ref[...].astype(BF16)

    o_ref[...] = x_ref[...] + gt_ref[...] * _dot(h_ref[...], wb_ref[...])


def _ffn_down(h, w, x2, mod3, seq, tm=512, tn=512):
    m, kdim = h.shape
    n = w.shape[1]
    per_b = seq // tm
    return pl.pallas_call(
        _ffn_down_kernel,
        grid=(n // tn, m // tm),
        in_specs=[pl.BlockSpec((tm, kdim), lambda j, i: (i, 0)),
                  pl.BlockSpec((kdim, tn), lambda j, i: (0, j)),
                  pl.BlockSpec((tm, tn), lambda j, i: (i, j)),
                  pl.BlockSpec((None, 1, tn), lambda j, i: ((i // per_b) * 6 + 5, 0, j))],
        out_specs=pl.BlockSpec((tm, tn), lambda j, i: (i, j)),
        out_shape=jax.ShapeDtypeStruct((m, n), F32),
        scratch_shapes=[pltpu.VMEM((kdim, tn), BF16)],
        compiler_params=_cparams(("arbitrary", "arbitrary")),
        name="ffn_down",
    )(h, w, x2, mod3)


def _rope(x, cos, sin_lo, sin_hi, half):
    return (x * cos + pltpu.roll(x, LANES - half, 1) * sin_lo
            + pltpu.roll(x, half, 1) * sin_hi)


def _att_prep_kernel(pos_ref, q_ref, k_ref, v_ref, iq_ref, ikw_ref,
                     qg_ref, kg_ref, fa_ref, fi_ref,
                     qo_ref, ko_ref, vo_ref, iqo_ref, iko_ref, iwo_ref):
    pos = pos_ref[...].astype(F32)
    lane = lax.broadcasted_iota(I32, (1, LANES), 1)

    half_a = ATT_HEAD_DIM // ROPE_FRACTION // 2
    ang = pos * fa_ref[...]
    cos_a, sin_a = jnp.cos(ang), jnp.sin(ang)
    lo_a = jnp.where(lane < half_a, -sin_a, 0.0)
    hi_a = jnp.where((lane >= half_a) & (lane < 2 * half_a), sin_a, 0.0)

    half_i = IDX_HEAD_DIM // ROPE_FRACTION // 2
    lane_i = lane & (IDX_HEAD_DIM - 1)
    ang = pos * fi_ref[...]
    cos_i, sin_i = jnp.cos(ang), jnp.sin(ang)
    lo_i = jnp.where(lane_i < half_i, -sin_i, 0.0)
    hi_i = jnp.where((lane_i >= half_i) & (lane_i < 2 * half_i), sin_i, 0.0)

    def head_norm(x, g):
        ms = jnp.mean(x * x, axis=-1, keepdims=True)
        return x * lax.rsqrt(ms + NORM_EPS) * g

    qg, kg = qg_ref[...], kg_ref[...]
    tm = q_ref.shape[0]
    pad_row = lax.broadcasted_iota(I32, (V_ROWS - LANES, tm), 0)
    ones_rows = jnp.where(pad_row == 0, 1.0, 0.0).astype(BF16)
    q_scale = (ATT_HEAD_DIM ** -0.5) * LOG2E
    for h in range(ATT_HEADS):
        sl = slice(h * LANES, (h + 1) * LANES)
        qh = _rope(head_norm(q_ref[:, sl], qg), cos_a, lo_a, hi_a, half_a)
        qo_ref[h] = (qh * q_scale).T.astype(BF16)
        kh = _rope(head_norm(k_ref[:, sl], kg), cos_a, lo_a, hi_a, half_a)
        ko_ref[:, sl] = kh.astype(BF16)
        vo_ref[h, 0:LANES, :] = v_ref[:, sl].T.astype(BF16)
        vo_ref[h, LANES:V_ROWS, :] = ones_rows
    for h in range(IDX_HEADS * IDX_HEAD_DIM // LANES):
        sl = slice(h * LANES, (h + 1) * LANES)
        ih = _rope(iq_ref[:, sl], cos_i, lo_i, hi_i, half_i)
        iqo_ref[h] = (ih * (IDX_HEAD_DIM ** -0.5)).T.astype(BF16)
    ikw = ikw_ref[...]
    iko_ref[...] = _rope(ikw, cos_i, lo_i, hi_i, half_i).astype(BF16)
    iwo_ref[...] = (ikw * (IDX_HEADS ** -0.5)).T[IDX_HEAD_DIM:IDX_HEAD_DIM + IDX_HEADS, :]


def _att_prep(proj_a, proj_r, pos2, qg, kg, fa, fi, tm):
    m = proj_a.shape[0]
    w = ATT_WIDTH
    npair = IDX_HEADS * IDX_HEAD_DIM // LANES

    def wide(off):
        return pl.BlockSpec((tm, w), lambda i: (i, off // w))

    const = pl.BlockSpec((1, LANES), lambda i: (0, 0))
    return pl.pallas_call(
        _att_prep_kernel,
        grid=(m // tm,),
        in_specs=[pl.BlockSpec((tm, 1), lambda i: (i, 0)),
                  wide(OFF_Q), wide(OFF_K), wide(OFF_V), wide(OFF_IQ),
                  pl.BlockSpec((tm, LANES), lambda i: (i, OFF_IKW // LANES)),
                  const, const, const, const],
        out_specs=[pl.BlockSpec((ATT_HEADS, LANES, tm), lambda i: (0, 0, i)),
                   pl.BlockSpec((tm, w), lambda i: (i, 0)),
                   pl.BlockSpec((ATT_HEADS, None, V_ROWS, tm), lambda i: (0, i, 0, 0)),
                   pl.BlockSpec((npair, LANES, tm), lambda i: (0, 0, i)),
                   pl.BlockSpec((tm, LANES), lambda i: (i, 0)),
                   pl.BlockSpec((IDX_HEADS, tm), lambda i: (0, i))],
        out_shape=[jax.ShapeDtypeStruct((ATT_HEADS, LANES, m), BF16),
                   jax.ShapeDtypeStruct((m, w), BF16),
                   jax.ShapeDtypeStruct((ATT_HEADS, m // tm, V_ROWS, tm), BF16),
                   jax.ShapeDtypeStruct((npair, LANES, m), BF16),
                   jax.ShapeDtypeStruct((m, LANES), BF16),
                   jax.ShapeDtypeStruct((IDX_HEADS, m), F32)],
        compiler_params=_cparams(("arbitrary",)),
        name="att_prep",
    )(pos2, proj_a, proj_a, proj_a, proj_a, proj_r, qg, kg, fa, fi)


NEG_BIG = -1e30


INT_MIN = -2 ** 31
INT_MAX = 2 ** 31 - 1
MAGNITUDE_BITS = 0x7FFFFFFF
KEY_NEG_INF = -2139095041
BOUND_SLACK = 1.02
MAX_STATIC_SHIFT = 60.0
COUNT_ROWS = 32


def _key_to_float(key):
    bits = key ^ ((key >> 31) & jnp.int32(MAGNITUDE_BITS))
    return lax.bitcast_convert_type(bits, F32)


def _dsa_kernel(bound_ref, qt_ref, iqt_ref, iwt_ref, k_ref, vt_ref, ik_ref, o_ref,
                sc_ref, m_ref, acc_ref, last_ref, *, tq, tk, topk, index_bits):
    qi = pl.program_id(1)
    n_kb = (qi * tq + tq - 1) // tk + 1
    key0 = lax.broadcasted_iota(I32, (tk, tq), 0)
    qidx = qi * tq + lax.broadcasted_iota(I32, (tk, tq), 1)
    iw = iwt_ref[...]

    def score_body(kb, carry):
        start = pl.multiple_of(kb * tk, tk)
        ikb = ik_ref[pl.ds(start, tk), 0:IDX_HEAD_DIM]
        s = jnp.zeros((tk, tq), F32)
        for h in range(IDX_HEADS):
            off = (h % 2) * IDX_HEAD_DIM
            d = _dot(ikb, iqt_ref[h // 2, off:off + IDX_HEAD_DIM, :])
            s = s + jnp.maximum(d, 0.0) * iw[h:h + 1, :]
        sc_ref[kb] = jnp.where(kb * tk + key0 <= qidx, s, -jnp.inf)
        return carry

    lax.fori_loop(0, n_kb, score_body, 0)

    def count(pred):
        def cnt_body(kb, acc):
            one = jnp.where(pred(kb), 1.0, 0.0)
            return acc + jnp.sum(one.reshape(tk // COUNT_ROWS, COUNT_ROWS, tq), axis=0)

        acc = lax.fori_loop(0, n_kb, cnt_body, jnp.zeros((COUNT_ROWS, tq), F32))
        return jnp.sum(acc, axis=0, keepdims=True)

    def bit_body(i, cand):
        trial = cand ^ lax.shift_left(jnp.int32(1), 31 - i)
        trial_f = _key_to_float(trial)
        cnt = count(lambda kb: sc_ref[kb] >= trial_f)
        return jnp.where(cnt >= topk, trial, cand)

    cand = lax.fori_loop(0, 32, bit_body, jnp.full((1, tq), INT_MIN, I32))
    tau = _key_to_float(jnp.maximum(cand, jnp.int32(KEY_NEG_INF)))

    last_ref[...] = jnp.full_like(last_ref, INT_MAX)
    n_ge = count(lambda kb: sc_ref[kb] >= tau)

    @pl.when(jnp.max(n_ge) > topk)
    def _():
        need = topk - count(lambda kb: sc_ref[kb] > tau)

        def idx_body(i, last):
            trial = last | lax.shift_left(jnp.int32(1), index_bits - 1 - i)
            below = count(lambda kb: (sc_ref[kb] == tau) & (kb * tk + key0 < trial))
            return jnp.where(below < need, trial, last)

        last_ref[...] = lax.fori_loop(0, index_bits, idx_body, jnp.zeros((1, tq), I32))

    acc_ref[...] = jnp.zeros_like(acc_ref)
    bound = bound_ref[0]

    def logits(kb, h):
        start = pl.multiple_of(kb * tk, tk)
        return _dot(k_ref[pl.ds(start, tk), h * LANES:(h + 1) * LANES], qt_ref[h])

    last = last_ref[...]

    def selected(kb):
        s = sc_ref[kb]
        kidx = kb * tk + key0
        return ((s > tau) | ((s == tau) & (kidx <= last))) & (kidx <= qidx)

    @pl.when(bound <= MAX_STATIC_SHIFT)
    def _():
        def att_body(kb, carry):
            bias = jnp.where(selected(kb), -bound, -jnp.inf)
            for h in range(ATT_HEADS):
                p = jnp.exp2(logits(kb, h) + bias)
                acc_ref[h] += _dot(vt_ref[h, kb], p.astype(BF16))
            return carry

        lax.fori_loop(0, n_kb, att_body, 0)

    @pl.when(bound > MAX_STATIC_SHIFT)
    def _():
        m_ref[...] = jnp.full_like(m_ref, NEG_BIG)

        def att_body(kb, carry):
            bias = jnp.where(selected(kb), 0.0, -jnp.inf)
            for h in range(ATT_HEADS):
                s = logits(kb, h) + bias
                m_prev = m_ref[h]
                m_next = jnp.maximum(m_prev, jnp.max(s, axis=0, keepdims=True))
                p = jnp.exp2(s - m_next)
                alpha = jnp.exp2(m_prev - m_next)
                acc_ref[h] = alpha * acc_ref[h] + _dot(vt_ref[h, kb], p.astype(BF16))
                m_ref[h] = m_next
            return carry

        lax.fori_loop(0, n_kb, att_body, 0)

    for h in range(ATT_HEADS):
        out = acc_ref[h, 0:LANES, :] / acc_ref[h, LANES:LANES + 1, :]
        o_ref[:, h * LANES:(h + 1) * LANES] = out.T.astype(o_ref.dtype)


def _dsa(bound, qt, iqt, iwt, k, vt, ik, bsz, seq, tk, tq=512):
    tq = min(tq, seq)
    topk = min(TOPK_MAX, seq // 4)
    w = ATT_WIDTH
    nq = seq // tq
    nkb = seq // tk
    npair = iqt.shape[0]
    kern = functools.partial(_dsa_kernel, tq=tq, tk=tk, topk=topk,
                             index_bits=max(1, (seq - 1).bit_length()))
    return pl.pallas_call(
        kern,
        grid=(bsz, nq),
        in_specs=[pl.BlockSpec(memory_space=pltpu.SMEM),
                  pl.BlockSpec((ATT_HEADS, LANES, tq), lambda b, i: (0, 0, b * nq + i)),
                  pl.BlockSpec((npair, LANES, tq), lambda b, i: (0, 0, b * nq + i)),
                  pl.BlockSpec((IDX_HEADS, tq), lambda b, i: (0, b * nq + i)),
                  pl.BlockSpec((seq, w), lambda b, i: (b, 0)),
                  pl.BlockSpec((ATT_HEADS, nkb, V_ROWS, tk), lambda b, i: (0, b, 0, 0)),
                  pl.BlockSpec((seq, LANES), lambda b, i: (b, 0))],
        out_specs=pl.BlockSpec((tq, w), lambda b, i: (b * nq + i, 0)),
        out_shape=jax.ShapeDtypeStruct((bsz * seq, w), BF16),
        scratch_shapes=[pltpu.VMEM((nkb, tk, tq), F32),
                        pltpu.VMEM((ATT_HEADS, 1, tq), F32),
                        pltpu.VMEM((ATT_HEADS, V_ROWS, tq), F32),
                        pltpu.VMEM((1, tq), I32)],
        compiler_params=_cparams(("arbitrary", "arbitrary")),
        name="dsa",
    )(bound, qt, iqt, iwt, k, vt, ik)


def _rwkv_prep_kernel(rr_ref, rk_ref, rv_ref, wa_ref, xg_ref,
                      mr_ref, mk_ref, mv_ref, mwa_ref, mg_ref,
                      w0_ref, a0_ref, kk_ref, ka_ref, rkp_ref,
                      wup_ref, aup_ref, gup_ref, e_ref, et_ref,
                      r_o, lw_o, k_o, v_o, kkn_o, bb_o, g_o, bon_o,
                      c_r, c_k, c_v, c_wa, c_g, *, tm):
    first = pl.program_id(1) == 0

    def shift(y_ref, carry_ref, mu_ref):
        y = y_ref[...]

        @pl.when(first)
        def _():
            carry_ref[...] = jnp.zeros_like(carry_ref)

        prev_last = carry_ref[7:8, :]
        rolled = pltpu.roll(y, 1, 0)
        rows = lax.broadcasted_iota(I32, y.shape, 0)
        yprev = jnp.where(rows == 0, prev_last, rolled)
        carry_ref[...] = y[tm - 8:tm, :]
        return y + (yprev - y) * mu_ref[...]

    r = shift(rr_ref, c_r, mr_ref)
    k = shift(rk_ref, c_k, mk_ref)
    v = shift(rv_ref, c_v, mv_ref)
    wa = shift(wa_ref, c_wa, mwa_ref)
    xw = pltpu.roll(wa, TAIL_TILE - XW_LANE, 1)[:, :LORA_PAD]
    xa = pltpu.roll(wa, TAIL_TILE - XA_LANE, 1)[:, :LORA_PAD]
    xg = shift(xg_ref, c_g, mg_ref)

    w_raw = w0_ref[...] + _dot(jnp.tanh(xw).astype(BF16), wup_ref[...])
    z = -w_raw
    softplus = jnp.maximum(z, 0.0) + jnp.log(1.0 + jnp.exp(-jnp.abs(z)))
    lw_o[...] = -jnp.exp(-softplus - 0.5)
    a = _sigmoid(a0_ref[...] + _dot(xa.astype(BF16), aup_ref[...]))
    g_o[...] = _dot(_sigmoid(xg).astype(BF16), gup_ref[...]).astype(BF16)

    def head_sum(x):
        return _dot_hi(_dot_hi(x, e_ref[...]), et_ref[...])

    kk = k * kk_ref[...]
    ss = head_sum(kk * kk)
    kk = kk / jnp.maximum(jnp.sqrt(ss), 1e-12)
    kmod = k * (1.0 + (a - 1.0) * ka_ref[...])
    r_o[...] = r.astype(BF16)
    k_o[...] = kmod.astype(BF16)
    v_o[...] = v.astype(BF16)
    kkn_o[...] = kk.astype(BF16)
    bb_o[...] = (kk * a).astype(BF16)
    bon_o[...] = (head_sum(r * kmod * rkp_ref[...]) * v).astype(BF16)


def _rwkv_prep(proj_r, proj_t, mus, vecs, wup, aup, gup, e, bsz, seq, tm=256):
    w = RWKV_WIDTH
    per_b = seq // tm

    def wide(off):
        return pl.BlockSpec((tm, w), lambda b, i: (b * per_b + i, off // w))

    def tail(off):
        return pl.BlockSpec((tm, TAIL_TILE), lambda b, i: (b * per_b + i, off // TAIL_TILE))

    def const(shape):
        return pl.BlockSpec(shape, lambda b, i: (0, 0))

    out_blk = pl.BlockSpec((tm, w), lambda b, i: (b * per_b + i, 0))
    kern = functools.partial(_rwkv_prep_kernel, tm=tm)
    return pl.pallas_call(
        kern,
        grid=(bsz, per_b),
        in_specs=[wide(OFF_RR), wide(OFF_RK), wide(OFF_RV), tail(OFF_WA), tail(OFF_XG),
                  const((1, w)), const((1, w)), const((1, w)),
                  const((1, TAIL_TILE)), const((1, GATE_LORA)),
                  const((1, w)), const((1, w)), const((1, w)), const((1, w)), const((1, w)),
                  const((LORA_PAD, w)), const((LORA_PAD, w)), const((GATE_LORA, w)),
                  const((w, LANES)), const((LANES, w))],
        out_specs=[out_blk] * 8,
        out_shape=[jax.ShapeDtypeStruct((bsz * seq, w), F32 if i == 1 else BF16) for i in range(8)],
        scratch_shapes=[pltpu.VMEM((8, w), F32)] * 3
        + [pltpu.VMEM((8, TAIL_TILE), F32), pltpu.VMEM((8, GATE_LORA), F32)],
        compiler_params=_cparams(("arbitrary", "arbitrary")),
        name="rwkv_prep",
    )(proj_r, proj_r, proj_r, proj_t, proj_t, *mus, *vecs, wup, aup, gup, e, e.T)


def _rwkv_core_kernel(r_ref, lw_ref, k_ref, v_ref, kk_ref, bb_ref, g_ref, bon_ref, lg_ref, lb_ref,
                      o_ref, z_ref):
    c = CHUNK

    @pl.when(pl.program_id(1) == 0)
    def _():
        z_ref[...] = jnp.zeros_like(z_ref)

    lw = lw_ref[...]
    tri = jnp.where(lax.broadcasted_iota(I32, (c, c), 1) <= lax.broadcasted_iota(I32, (c, c), 0),
                    1.0, 0.0).astype(BF16)
    hi = lw.astype(BF16)
    rem = lw - hi.astype(F32)
    mid = rem.astype(BF16)
    lo = (rem - mid.astype(F32)).astype(BF16)
    cum = _dot(tri, hi) + _dot(tri, mid) + _dot(tri, lo)
    p_in = jnp.exp(cum)
    p_ex = jnp.exp(cum - lw)
    p_inv = jnp.exp(-cum)
    p_end = p_in[c - 1:c, :]
    a_t = -kk_ref[...].astype(F32) * p_ex
    r_t = r_ref[...].astype(F32) * p_in
    b_h = bb_ref[...].astype(F32) * p_inv
    k_h = k_ref[...].astype(F32) * p_inv
    b_e = b_h * p_end
    k_e = k_h * p_end
    v = v_ref[...].astype(F32)

    n2 = 2 * c
    lane = lax.broadcasted_iota(I32, (1, LANES), 1)
    head0 = lane < RWKV_HEAD_DIM
    ri = lax.broadcasted_iota(I32, (n2, n2), 0)
    ci = lax.broadcasted_iota(I32, (n2, n2), 1)
    same = (ri >= c) == (ci >= c)
    strict = same & (ci < ri)
    incl = same & (ci <= ri)
    eye = ri == ci
    own = ((lax.broadcasted_iota(I32, (n2, LANES), 0) >= c)
           == (lax.broadcasted_iota(I32, (n2, LANES), 1) >= RWKV_HEAD_DIM))
    inv_n = 1.0 / RWKV_HEAD_DIM

    def stack(y):
        return jnp.concatenate([jnp.where(head0, y, 0.0), jnp.where(head0, 0.0, y)], axis=0)

    pairs = range(RWKV_WIDTH // LANES)
    sls = [slice(p * LANES, (p + 1) * LANES) for p in pairs]
    a_s = [stack(a_t[:, sl]) for sl in sls]
    r_s = [stack(r_t[:, sl]) for sl in sls]
    v_s = [stack(v[:, sl]).astype(BF16) for sl in sls]
    g1 = [_dot_nt(jnp.concatenate([a_s[p], r_s[p]], axis=0).astype(BF16),
                  jnp.concatenate([stack(b_h[:, sls[p]]), stack(k_h[:, sls[p]])], axis=0).astype(BF16))
          for p in pairs]
    pw = [jnp.where(strict, g[:n2, :n2], 0.0).astype(BF16) for g in g1]
    a_rb = [jnp.where(incl, g[n2:, :n2], 0.0).astype(BF16) for g in g1]
    a_rk = [jnp.where(incl, g[n2:, n2:], 0.0).astype(BF16) for g in g1]
    akv = [_dot(jnp.where(strict, g1[p][:n2, n2:], 0.0).astype(BF16), v_s[p]) for p in pairs]
    xc = [jnp.concatenate([a_s[p], akv[p]], axis=1) for p in pairs]
    steps = int(np.log2(c))
    for i in range(steps):
        if i + 1 < steps:
            res = [_dot(pw[p], jnp.concatenate([pw[p], xc[p].astype(BF16)], axis=1)) for p in pairs]
            xc = [xc[p] + res[p][:, n2:] for p in pairs]
            pw = [res[p][:, :n2].astype(BF16) for p in pairs]
        else:
            xc = [xc[p] + _dot(pw[p], xc[p].astype(BF16)) for p in pairs]
    xcb = [x.astype(BF16) for x in xc]
    r2 = [_dot(a_rb[p], xcb[p]) for p in pairs]
    ov = [r2[p][:, LANES:] + _dot(a_rk[p], v_s[p]) for p in pairs]
    mg = [_dot_tn(stack(b_e[:, sls[p]]).astype(BF16), xcb[p]) for p in pairs]
    kv = [_dot_tn(stack(k_e[:, sls[p]]).astype(BF16), v_s[p]) for p in pairs]
    for p in pairs:
        q_s = r_s[p] + r2[p][:, :LANES]
        mmat = mg[p][:, :LANES] + jnp.where(eye, p_end[:, sls[p]], 0.0)
        z_hi, z_lo = _split2(z_ref[p])
        qm = jnp.concatenate([q_s, mmat], axis=0).astype(BF16)
        res = _dot(qm, z_hi) + _dot(qm, z_lo)
        z_ref[p] = res[n2:] + mg[p][:, LANES:] + kv[p]
        o_s = res[:n2] + ov[p]
        mean = jnp.sum(o_s, axis=1, keepdims=True) * inv_n
        dev = jnp.where(own, o_s - mean, 0.0)
        var = jnp.sum(dev * dev, axis=1, keepdims=True) * inv_n
        y = dev * lax.rsqrt(var + LNX_EPS)
        y = (y[:c] + y[c:]) * lg_ref[:, sls[p]] + lb_ref[:, sls[p]]
        out = (y + bon_ref[:, sls[p]].astype(F32)) * g_ref[:, sls[p]].astype(F32)
        o_ref[:, sls[p]] = out.astype(o_ref.dtype)


def _rwkv_core(r, lw, k, v, kk, bb, g, bon, lg, lb, bsz, seq):
    c = CHUNK
    w = RWKV_WIDTH
    per_b = seq // c
    blk = pl.BlockSpec((c, w), lambda b, i: (b * per_b + i, 0))
    vec = pl.BlockSpec((1, w), lambda b, i: (0, 0))
    return pl.pallas_call(
        _rwkv_core_kernel,
        grid=(bsz, per_b),
        in_specs=[blk] * 8 + [vec, vec],
        out_specs=blk,
        out_shape=jax.ShapeDtypeStruct((bsz * seq, w), BF16),
        scratch_shapes=[pltpu.VMEM((w // LANES, LANES, LANES), F32)],
        compiler_params=_cparams(("arbitrary", "arbitrary")),
        name="rwkv_core",
    )(r, lw, k, v, kk, bb, g, bon, lg, lb)


def _rope_freqs(head_dim):
    rot = head_dim // ROPE_FRACTION
    half = rot // 2
    inv = ROPE_THETA ** (-jnp.arange(half, dtype=F32) / half)
    per_head = jnp.concatenate([inv, inv, jnp.zeros((head_dim - rot,), F32)])
    return jnp.tile(per_head, LANES // head_dim).reshape(1, LANES)


def kernel(x, c, positions, w_ada, b_ada, norm1_g, w_in, q_norm_g, k_norm_g, rwkv_mu, rwkv_w0,
           rwkv_w_up, rwkv_a0, rwkv_a_up, rwkv_g_up, rwkv_k_k, rwkv_k_a, rwkv_r_k, rwkv_lnx_g,
           rwkv_lnx_b, w_out, norm2_g, w_ffn_gate, w_ffn_up, w_ffn_down):
    bsz, seq, d = x.shape
    depth = w_ada.shape[0]
    m = bsz * seq
    pos2 = positions.reshape(m, 1)
    fa = _rope_freqs(ATT_HEAD_DIM)
    fi = _rope_freqs(IDX_HEAD_DIM)
    hd = RWKV_HEAD_DIM
    e = (jnp.arange(RWKV_WIDTH)[:, None] // hd == jnp.arange(LANES)[None, :]).astype(BF16)
    x2 = x.reshape(m, d)

    for l in range(depth):
        mod = _adaln(c, w_ada[l], b_ada[l])
        mod3 = mod.reshape(bsz * 6, 1, d)

        h1 = _norm(x2, norm1_g[l].reshape(1, d), mod3, seq)
        w_in_t = w_in[l].T
        proj_a = _matmul_nt(h1, w_in_t, lambda j: j * ATT_WIDTH, 4, ATT_WIDTH, "in_proj_att")
        proj_r = _matmul_nt(h1, w_in_t, lambda j: ATT_COLS + j * RWKV_WIDTH, 3, RWKV_WIDTH,
                            "in_proj_rkv")
        proj_t = _matmul_nt(
            h1, w_in_t,
            lambda j: jnp.where(j < 2, TAIL_COL0 + j * TAIL_TILE, 4 * ATT_WIDTH),
            3, TAIL_TILE, "in_proj_tail", tm=min(2048, m))

        qt, kn, vt, iqt, ik, iwt = _att_prep(proj_a, proj_t, pos2, q_norm_g[l].reshape(1, -1),
                                             k_norm_g[l].reshape(1, -1), fa, fi, KEY_TILE)
        bound = (ATT_HEAD_DIM ** 0.5 * LOG2E * BOUND_SLACK
                 * jnp.max(jnp.abs(q_norm_g[l])) * jnp.max(jnp.abs(k_norm_g[l])))
        att = _dsa(bound.reshape(1).astype(F32), qt, iqt, iwt, kn, vt, ik, bsz, seq, KEY_TILE)

        mu = rwkv_mu[l]
        w3 = 3 * RWKV_WIDTH

        n_wa = DECAY_LORA + AAA_LORA
        mu_wa = jnp.zeros((1, TAIL_TILE), F32).at[0, XW_LANE:XW_LANE + n_wa].set(mu[w3:w3 + n_wa])
        mus = [mu[0:RWKV_WIDTH].reshape(1, -1), mu[RWKV_WIDTH:2 * RWKV_WIDTH].reshape(1, -1),
               mu[2 * RWKV_WIDTH:w3].reshape(1, -1), mu_wa, mu[w3 + n_wa:].reshape(1, -1)]
        vecs = [rwkv_w0[l].reshape(1, -1), rwkv_a0[l].reshape(1, -1), rwkv_k_k[l].reshape(1, -1),
                rwkv_k_a[l].reshape(1, -1), rwkv_r_k[l].reshape(1, -1)]

        def pad_rows(wm):
            return jnp.zeros((LORA_PAD, wm.shape[1]), F32).at[:wm.shape[0]].set(wm).astype(BF16)

        r, lw, km, vv, kk, bb, g, bon = _rwkv_prep(
            proj_r, proj_t, mus, vecs, pad_rows(rwkv_w_up[l]), pad_rows(rwkv_a_up[l]),
            rwkv_g_up[l].astype(BF16), e, bsz, seq)
        rw = _rwkv_core(r, lw, km, vv, kk, bb, g, bon, rwkv_lnx_g[l].reshape(1, -1),
                        rwkv_lnx_b[l].reshape(1, -1), bsz, seq)

        wo = w_out[l].astype(BF16)
        x2, h2 = _out_proj(att, rw, wo[:ATT_WIDTH], wo[ATT_WIDTH:], x2, mod3,
                           norm2_g[l].reshape(1, d), seq)

        hglu = _ffn_glu(h2, w_ffn_gate[l], w_ffn_up[l])
        x2 = _ffn_down(hglu, w_ffn_down[l], x2, mod3, seq)
    return x2.reshape(bsz, seq, d)
```

```python
import functools

import jax
import jax.numpy as jnp
import numpy as np
from jax import lax
from jax.experimental import pallas as pl
from jax.experimental.pallas import tpu as pltpu

F32 = jnp.float32
BF16 = jnp.bfloat16
I32 = jnp.int32

D_MODEL = 2048
ATT_HEADS = 8
ATT_HEAD_DIM = 128
ATT_WIDTH = ATT_HEADS * ATT_HEAD_DIM
RWKV_WIDTH = D_MODEL - ATT_WIDTH
RWKV_HEAD_DIM = 64
RWKV_HEADS = RWKV_WIDTH // RWKV_HEAD_DIM
IDX_HEADS = 16
IDX_HEAD_DIM = 64
TOPK_MAX = 256
ROPE_THETA = 500000.0
ROPE_FRACTION = 4
DECAY_LORA = 96
AAA_LORA = 96
GATE_LORA = 256
NORM_EPS = 1e-6
LNX_EPS = 64e-5

LANES = 128
SUBLANES = 8
LORA_PAD = 128
ATT_COLS = 4 * ATT_WIDTH + IDX_HEAD_DIM + IDX_HEADS
IN_COLS = ATT_COLS + 3 * RWKV_WIDTH + DECAY_LORA + AAA_LORA + GATE_LORA
OFF_Q, OFF_K, OFF_V, OFF_IQ = 0, 1024, 2048, 3072
OFF_RR, OFF_RK, OFF_RV = 0, 1024, 2048
TAIL_TILE = 256
TAIL_COL0 = IN_COLS - 2 * TAIL_TILE
XW_LANE = IN_COLS - GATE_LORA - AAA_LORA - DECAY_LORA - TAIL_COL0
XA_LANE = XW_LANE + DECAY_LORA
OFF_WA, OFF_XG = 0, 256

LOG2E = 1.4426950408889634
V_ROWS = 144
KEY_TILE = 512
CHUNK = 64
VMEM_LIMIT = 56 * 1024 * 1024


def _cparams(sem):
    return pltpu.CompilerParams(dimension_semantics=sem, vmem_limit_bytes=VMEM_LIMIT)


def _dot(a, b):
    return jnp.dot(a, b, preferred_element_type=F32)


def _dot_nt(a, b):
    return lax.dot_general(a, b, (((1,), (1,)), ((), ())), preferred_element_type=F32)


def _dot_tn(a, b):
    return lax.dot_general(a, b, (((0,), (0,)), ((), ())), preferred_element_type=F32)


def _split2(x):
    hi = x.astype(BF16)
    lo = (x - hi.astype(F32)).astype(BF16)
    return hi, lo


def _dot_hi(x, w):
    hi, lo = _split2(x)
    return _dot(hi, w) + _dot(lo, w)


def _sigmoid(x):
    return 1.0 / (1.0 + jnp.exp(-x))


def _adaln_kernel(c_ref, w_ref, b_ref, o_ref):
    c = c_ref[...]
    ca = c * _sigmoid(c)
    o_ref[...] = _dot(ca.astype(BF16), w_ref[...].astype(BF16)) + b_ref[...]


def _adaln(c, w, b):
    bsz, d = c.shape
    n = w.shape[1]
    rows = 8
    cp = jnp.zeros((rows, d), F32).at[:bsz].set(c)
    tn = 1024
    out = pl.pallas_call(
        _adaln_kernel,
        grid=(n // tn,),
        in_specs=[pl.BlockSpec((rows, d), lambda j: (0, 0)),
                  pl.BlockSpec((d, tn), lambda j: (0, j)),
                  pl.BlockSpec((1, tn), lambda j: (0, j))],
        out_specs=pl.BlockSpec((rows, tn), lambda j: (0, j)),
        out_shape=jax.ShapeDtypeStruct((rows, n), F32),
        compiler_params=_cparams(("arbitrary",)),
        name="adaln",
    )(cp, w, b.reshape(1, n))
    return out[:bsz]


def _norm_mod(x, g, sc, sh):
    ms = jnp.mean(x * x, axis=-1, keepdims=True)
    y = x * lax.rsqrt(ms + NORM_EPS)
    y = y * g
    return y * (1.0 + sc) + sh


def _norm_kernel(x_ref, g_ref, sc_ref, sh_ref, o_ref):
    o_ref[...] = _norm_mod(x_ref[...], g_ref[...], sc_ref[...], sh_ref[...]).astype(BF16)


def _norm(x2, g, mod3, seq, tm=512):
    m, d = x2.shape
    per_b = seq // tm
    return pl.pallas_call(
        _norm_kernel,
        grid=(m // tm,),
        in_specs=[pl.BlockSpec((tm, d), lambda i: (i, 0)),
                  pl.BlockSpec((1, d), lambda i: (0, 0)),
                  pl.BlockSpec((None, 1, d), lambda i: ((i // per_b) * 6 + 1, 0, 0)),
                  pl.BlockSpec((None, 1, d), lambda i: ((i // per_b) * 6 + 0, 0, 0))],
        out_specs=pl.BlockSpec((tm, d), lambda i: (i, 0)),
        out_shape=jax.ShapeDtypeStruct((m, d), BF16),
        compiler_params=_cparams(("arbitrary",)),
        name="norm1",
    )(x2, g, mod3, mod3)


def _mm_nt_kernel(h_ref, wt_ref, o_ref, wb_ref):
    @pl.when(pl.program_id(1) == 0)
    def _():
        wb_ref[...] = wt_ref[...].astype(BF16)

    o_ref[...] = _dot_nt(h_ref[...], wb_ref[...])


def _matmul_nt(h, wt, col_of_tile, n_tiles, tn, name, tm=1024):
    m, d = h.shape
    return pl.pallas_call(
        _mm_nt_kernel,
        grid=(n_tiles, m // tm),
        in_specs=[pl.BlockSpec((tm, d), lambda j, i: (i, 0)),
                  pl.BlockSpec((pl.Element(tn), pl.Element(d)),
                               lambda j, i: (pl.multiple_of(col_of_tile(j), SUBLANES), 0))],
        out_specs=pl.BlockSpec((tm, tn), lambda j, i: (i, j)),
        out_shape=jax.ShapeDtypeStruct((m, n_tiles * tn), F32),
        scratch_shapes=[pltpu.VMEM((tn, d), BF16)],
        compiler_params=_cparams(("arbitrary", "arbitrary")),
        name=name,
    )(h, wt)


def _ffn_glu_kernel(h_ref, wg_ref, wu_ref, o_ref, wgb_ref, wub_ref):
    @pl.when(pl.program_id(1) == 0)
    def _():
        wgb_ref[...] = wg_ref[...].astype(BF16)
        wub_ref[...] = wu_ref[...].astype(BF16)

    h = h_ref[...]
    a = _dot(h, wgb_ref[...])
    u = _dot(h, wub_ref[...])
    o_ref[...] = (a * _sigmoid(a) * u).astype(o_ref.dtype)


def _ffn_glu(h, wg, wu, tm=1024, tn=512):
    m, d = h.shape
    n = wg.shape[1]
    return pl.pallas_call(
        _ffn_glu_kernel,
        grid=(n // tn, m // tm),
        in_specs=[pl.BlockSpec((tm, d), lambda j, i: (i, 0)),
                  pl.BlockSpec((d, tn), lambda j, i: (0, j)),
                  pl.BlockSpec((d, tn), lambda j, i: (0, j))],
        out_specs=pl.BlockSpec((tm, tn), lambda j, i: (i, j)),
        out_shape=jax.ShapeDtypeStruct((m, n), BF16),
        scratch_shapes=[pltpu.VMEM((d, tn), BF16)] * 2,
        compiler_params=_cparams(("arbitrary", "arbitrary")),
        name="ffn_glu",
    )(h, wg, wu)


def _out_proj_kernel(a_ref, r_ref, wa_ref, wr_ref, x_ref, gt_ref, g_ref, sc_ref, sh_ref,
                     x1_ref, h2_ref):
    mixed = _dot(a_ref[...], wa_ref[...]) + _dot(r_ref[...], wr_ref[...])
    x1 = x_ref[...] + gt_ref[...] * mixed
    x1_ref[...] = x1
    h2_ref[...] = _norm_mod(x1, g_ref[...], sc_ref[...], sh_ref[...]).astype(BF16)


def _out_proj(att, rwkv, wa, wr, x2, mod3, g2, seq, tm=512):
    m, ka = att.shape
    kr = rwkv.shape[1]
    n = wa.shape[1]
    per_b = seq // tm

    def mod_row(j):
        return pl.BlockSpec((None, 1, n), lambda i: ((i // per_b) * 6 + j, 0, 0))

    row = pl.BlockSpec((tm, n), lambda i: (i, 0))
    return pl.pallas_call(
        _out_proj_kernel,
        grid=(m // tm,),
        in_specs=[pl.BlockSpec((tm, ka), lambda i: (i, 0)),
                  pl.BlockSpec((tm, kr), lambda i: (i, 0)),
                  pl.BlockSpec((ka, n), lambda i: (0, 0)),
                  pl.BlockSpec((kr, n), lambda i: (0, 0)),
                  row, mod_row(2),
                  pl.BlockSpec((1, n), lambda i: (0, 0)), mod_row(4), mod_row(3)],
        out_specs=[row, row],
        out_shape=[jax.ShapeDtypeStruct((m, n), F32), jax.ShapeDtypeStruct((m, n), BF16)],
        compiler_params=_cparams(("arbitrary",)),
        name="out_proj",
    )(att, rwkv, wa, wr, x2, mod3, g2, mod3, mod3)


def _ffn_down_kernel(h_ref, w_ref, x_ref, gt_ref, o_ref, wb_ref):
    @pl.when(pl.program_id(1) == 0)
    def _():
        wb_ref[...] = w_ref[...].astype(BF16)

    o_ref[...] = x_ref[...] + gt_ref[...] * _dot(h_ref[...], wb_ref[...])


def _ffn_down(h, w, x2, mod3, seq, tm=512, tn=512):
    m, kdim = h.shape
    n = w.shape[1]
    per_b = seq // tm
    return pl.pallas_call(
        _ffn_down_kernel,
        grid=(n // tn, m // tm),
        in_specs=[pl.BlockSpec((tm, kdim), lambda j, i: (i, 0)),
                  pl.BlockSpec((kdim, tn), lambda j, i: (0, j)),
                  pl.BlockSpec((tm, tn), lambda j, i: (i, j)),
                  pl.BlockSpec((None, 1, tn), lambda j, i: ((i // per_b) * 6 + 5, 0, j))],
        out_specs=pl.BlockSpec((tm, tn), lambda j, i: (i, j)),
        out_shape=jax.ShapeDtypeStruct((m, n), F32),
        scratch_shapes=[pltpu.VMEM((kdim, tn), BF16)],
        compiler_params=_cparams(("arbitrary", "arbitrary")),
        name="ffn_down",
    )(h, w, x2, mod3)


def _proj_t_kernel(wt_ref, h_ref, pos_ref, g_ref, f_ref, *refs, mode):
    *o_refs, wb_ref = refs
    o_ref = o_refs[0]

    @pl.when(pl.program_id(0) == 0)
    def _():
        wb_ref[...] = wt_ref[...].astype(BF16)

    yt = _dot_nt(wb_ref[...], h_ref[...])
    tm = yt.shape[1]
    if mode == "v":
        pad_row = lax.broadcasted_iota(I32, (V_ROWS - LANES, KEY_TILE), 0)
        ones_rows = jnp.where(pad_row == 0, 1.0, 0.0).astype(BF16)
        for h in range(ATT_HEADS):
            for j in range(tm // KEY_TILE):
                tile = yt[h * LANES:(h + 1) * LANES, j * KEY_TILE:(j + 1) * KEY_TILE]
                o_ref[h, j, 0:LANES, :] = tile.astype(BF16)
                o_ref[h, j, LANES:V_ROWS, :] = ones_rows
        return

    head = ATT_HEAD_DIM if mode in ("q", "k") else IDX_HEAD_DIM
    n_heads = 1 if mode == "ikw" else yt.shape[0] // head
    half = head // ROPE_FRACTION // 2
    ang = f_ref[...] * pos_ref[...].astype(F32)
    cos, sin = jnp.cos(ang), jnp.sin(ang)
    for hd in range(n_heads):
        y = yt[hd * head:(hd + 1) * head, :]
        if mode in ("q", "k"):
            ms = jnp.mean(y * y, axis=0, keepdims=True)
            y = y * lax.rsqrt(ms + NORM_EPS) * g_ref[...]
        x1, x2 = y[0:half], y[half:2 * half]
        y = jnp.concatenate([x1 * cos - x2 * sin, x2 * cos + x1 * sin, y[2 * half:]], axis=0)
        if mode == "q":
            o_ref[hd] = (y * ((ATT_HEAD_DIM ** -0.5) * LOG2E)).astype(BF16)
        elif mode == "iq":
            rows = slice((hd % 2) * head, (hd % 2 + 1) * head)
            o_ref[hd // 2, rows, :] = (y * (IDX_HEAD_DIM ** -0.5)).astype(BF16)
        elif mode == "k":
            o_ref[:, hd * LANES:(hd + 1) * LANES] = y.T.astype(BF16)
        else:
            o_ref[...] = jnp.concatenate([y, yt[head:, :]], axis=0).T.astype(BF16)
            o_refs[1][...] = yt[head:head + IDX_HEADS, :] * (IDX_HEADS ** -0.5)


def _proj_t(h, wt, col0, n, pos_row, g_col, f_col, mode, tm=1024):
    m, d = h.shape
    tm = min(tm, m)
    if mode == "v":
        out_specs = [pl.BlockSpec((ATT_HEADS, tm // KEY_TILE, V_ROWS, KEY_TILE),
                                  lambda i: (0, i, 0, 0))]
        out_shape = [jax.ShapeDtypeStruct((ATT_HEADS, m // KEY_TILE, V_ROWS, KEY_TILE), BF16)]
    elif mode in ("q", "iq"):
        out_specs = [pl.BlockSpec((n // LANES, LANES, tm), lambda i: (0, 0, i))]
        out_shape = [jax.ShapeDtypeStruct((n // LANES, LANES, m), BF16)]
    else:
        out_specs = [pl.BlockSpec((tm, n), lambda i: (i, 0))]
        out_shape = [jax.ShapeDtypeStruct((m, n), BF16)]
        if mode == "ikw":
            out_specs.append(pl.BlockSpec((IDX_HEADS, tm), lambda i: (0, i)))
            out_shape.append(jax.ShapeDtypeStruct((IDX_HEADS, m), F32))
    out = pl.pallas_call(
        functools.partial(_proj_t_kernel, mode=mode),
        grid=(m // tm,),
        in_specs=[pl.BlockSpec((pl.Element(n), pl.Element(d)), lambda i: (col0, 0)),
                  pl.BlockSpec((tm, d), lambda i: (i, 0)),
                  pl.BlockSpec((1, tm), lambda i: (0, i)),
                  pl.BlockSpec(g_col.shape, lambda i: (0, 0)),
                  pl.BlockSpec(f_col.shape, lambda i: (0, 0))],
        out_specs=out_specs,
        out_shape=out_shape,
        scratch_shapes=[pltpu.VMEM((n, d), BF16)],
        compiler_params=_cparams(("arbitrary",)),
        name="in_proj_" + mode,
    )(wt, h, pos_row, g_col, f_col)
    return out if mode == "ikw" else out[0]


NEG_BIG = -1e30


INT_MIN = -2 ** 31
INT_MAX = 2 ** 31 - 1
MAGNITUDE_BITS = 0x7FFFFFFF
KEY_NEG_INF = -2139095041
BOUND_SLACK = 1.02
MAX_STATIC_SHIFT = 60.0
COUNT_ROWS = 32


def _key_to_float(key):
    bits = key ^ ((key >> 31) & jnp.int32(MAGNITUDE_BITS))
    return lax.bitcast_convert_type(bits, F32)


def _dsa_kernel(bound_ref, qt_ref, iqt_ref, iwt_ref, k_ref, vt_ref, ik_ref, o_ref,
                sc_ref, m_ref, acc_ref, last_ref, *, tq, tk, topk, index_bits):
    qi = pl.program_id(1)
    n_kb = (qi * tq + tq - 1) // tk + 1
    key0 = lax.broadcasted_iota(I32, (tk, tq), 0)
    qidx = qi * tq + lax.broadcasted_iota(I32, (tk, tq), 1)
    iw = iwt_ref[...]

    def score_body(kb, carry):
        start = pl.multiple_of(kb * tk, tk)
        ikb = ik_ref[pl.ds(start, tk), 0:IDX_HEAD_DIM]
        s = jnp.zeros((tk, tq), F32)
        for h in range(IDX_HEADS):
            off = (h % 2) * IDX_HEAD_DIM
            d = _dot(ikb, iqt_ref[h // 2, off:off + IDX_HEAD_DIM, :])
            s = s + jnp.maximum(d, 0.0) * iw[h:h + 1, :]
        sc_ref[kb] = jnp.where(kb * tk + key0 <= qidx, s, -jnp.inf)
        return carry

    lax.fori_loop(0, n_kb, score_body, 0)

    def count(pred):
        def cnt_body(kb, acc):
            one = jnp.where(pred(kb), 1.0, 0.0)
            return acc + jnp.sum(one.reshape(tk // COUNT_ROWS, COUNT_ROWS, tq), axis=0)

        acc = lax.fori_loop(0, n_kb, cnt_body, jnp.zeros((COUNT_ROWS, tq), F32))
        return jnp.sum(acc, axis=0, keepdims=True)

    def bit_body(i, cand):
        trial = cand ^ lax.shift_left(jnp.int32(1), 31 - i)
        trial_f = _key_to_float(trial)
        cnt = count(lambda kb: sc_ref[kb] >= trial_f)
        return jnp.where(cnt >= topk, trial, cand)

    cand = lax.fori_loop(0, 32, bit_body, jnp.full((1, tq), INT_MIN, I32))
    tau = _key_to_float(jnp.maximum(cand, jnp.int32(KEY_NEG_INF)))

    last_ref[...] = jnp.full_like(last_ref, INT_MAX)
    n_ge = count(lambda kb: sc_ref[kb] >= tau)

    @pl.when(jnp.max(n_ge) > topk)
    def _():
        need = topk - count(lambda kb: sc_ref[kb] > tau)

        def idx_body(i, last):
            trial = last | lax.shift_left(jnp.int32(1), index_bits - 1 - i)
            below = count(lambda kb: (sc_ref[kb] == tau) & (kb * tk + key0 < trial))
            return jnp.where(below < need, trial, last)

        last_ref[...] = lax.fori_loop(0, index_bits, idx_body, jnp.zeros((1, tq), I32))

    acc_ref[...] = jnp.zeros_like(acc_ref)
    bound = bound_ref[0]

    def logits(kb, h):
        start = pl.multiple_of(kb * tk, tk)
        return _dot(k_ref[pl.ds(start, tk), h * LANES:(h + 1) * LANES], qt_ref[h])

    last = last_ref[...]

    def selected(kb):
        s = sc_ref[kb]
        kidx = kb * tk + key0
        return ((s > tau) | ((s == tau) & (kidx <= last))) & (kidx <= qidx)

    @pl.when(bound <= MAX_STATIC_SHIFT)
    def _():
        def att_body(kb, carry):
            bias = jnp.where(selected(kb), -bound, -jnp.inf)
            for h in range(ATT_HEADS):
                p = jnp.exp2(logits(kb, h) + bias)
                acc_ref[h] += _dot(vt_ref[h, kb], p.astype(BF16))
            return carry

        lax.fori_loop(0, n_kb, att_body, 0)

    @pl.when(bound > MAX_STATIC_SHIFT)
    def _():
        m_ref[...] = jnp.full_like(m_ref, NEG_BIG)

        def att_body(kb, carry):
            bias = jnp.where(selected(kb), 0.0, -jnp.inf)
            for h in range(ATT_HEADS):
                s = logits(kb, h) + bias
                m_prev = m_ref[h]
                m_next = jnp.maximum(m_prev, jnp.max(s, axis=0, keepdims=True))
                p = jnp.exp2(s - m_next)
                alpha = jnp.exp2(m_prev - m_next)
                acc_ref[h] = alpha * acc_ref[h] + _dot(vt_ref[h, kb], p.astype(BF16))
                m_ref[h] = m_next
            return carry

        lax.fori_loop(0, n_kb, att_body, 0)

    for h in range(ATT_HEADS):
        out = acc_ref[h, 0:LANES, :] / acc_ref[h, LANES:LANES + 1, :]
        o_ref[:, h * LANES:(h + 1) * LANES] = out.T.astype(o_ref.dtype)


def _dsa(bound, qt, iqt, iwt, k, vt, ik, bsz, seq, tk, tq=512):
    tq = min(tq, seq)
    topk = min(TOPK_MAX, seq // 4)
    w = ATT_WIDTH
    nq = seq // tq
    nkb = seq // tk
    npair = iqt.shape[0]
    kern = functools.partial(_dsa_kernel, tq=tq, tk=tk, topk=topk,
                             index_bits=max(1, (seq - 1).bit_length()))
    return pl.pallas_call(
        kern,
        grid=(bsz, nq),
        in_specs=[pl.BlockSpec(memory_space=pltpu.SMEM),
                  pl.BlockSpec((ATT_HEADS, LANES, tq), lambda b, i: (0, 0, b * nq + i)),
                  pl.BlockSpec((npair, LANES, tq), lambda b, i: (0, 0, b * nq + i)),
                  pl.BlockSpec((IDX_HEADS, tq), lambda b, i: (0, b * nq + i)),
                  pl.BlockSpec((seq, w), lambda b, i: (b, 0)),
                  pl.BlockSpec((ATT_HEADS, nkb, V_ROWS, tk), lambda b, i: (0, b, 0, 0)),
                  pl.BlockSpec((seq, LANES), lambda b, i: (b, 0))],
        out_specs=pl.BlockSpec((tq, w), lambda b, i: (b * nq + i, 0)),
        out_shape=jax.ShapeDtypeStruct((bsz * seq, w), BF16),
        scratch_shapes=[pltpu.VMEM((nkb, tk, tq), F32),
                        pltpu.VMEM((ATT_HEADS, 1, tq), F32),
                        pltpu.VMEM((ATT_HEADS, V_ROWS, tq), F32),
                        pltpu.VMEM((1, tq), I32)],
        compiler_params=_cparams(("arbitrary", "arbitrary")),
        name="dsa",
    )(bound, qt, iqt, iwt, k, vt, ik)


def _rwkv_prep_kernel(rr_ref, rk_ref, rv_ref, wa_ref, xg_ref,
                      mr_ref, mk_ref, mv_ref, mwa_ref, mg_ref,
                      w0_ref, a0_ref, kk_ref, ka_ref, rkp_ref,
                      wup_ref, aup_ref, gup_ref, e_ref, et_ref,
                      r_o, lw_o, k_o, v_o, kkn_o, bb_o, g_o, bon_o,
                      c_r, c_k, c_v, c_wa, c_g, *, tm):
    first = pl.program_id(1) == 0

    def shift(y_ref, carry_ref, mu_ref):
        y = y_ref[...]

        @pl.when(first)
        def _():
            carry_ref[...] = jnp.zeros_like(carry_ref)

        prev_last = carry_ref[7:8, :]
        rolled = pltpu.roll(y, 1, 0)
        rows = lax.broadcasted_iota(I32, y.shape, 0)
        yprev = jnp.where(rows == 0, prev_last, rolled)
        carry_ref[...] = y[tm - 8:tm, :]
        return y + (yprev - y) * mu_ref[...]

    r = shift(rr_ref, c_r, mr_ref)
    k = shift(rk_ref, c_k, mk_ref)
    v = shift(rv_ref, c_v, mv_ref)
    wa = shift(wa_ref, c_wa, mwa_ref)
    xw = pltpu.roll(wa, TAIL_TILE - XW_LANE, 1)[:, :LORA_PAD]
    xa = pltpu.roll(wa, TAIL_TILE - XA_LANE, 1)[:, :LORA_PAD]
    xg = shift(xg_ref, c_g, mg_ref)

    w_raw = w0_ref[...] + _dot(jnp.tanh(xw).astype(BF16), wup_ref[...])
    z = -w_raw
    softplus = jnp.maximum(z, 0.0) + jnp.log(1.0 + jnp.exp(-jnp.abs(z)))
    lw_o[...] = -jnp.exp(-softplus - 0.5)
    a = _sigmoid(a0_ref[...] + _dot(xa.astype(BF16), aup_ref[...]))
    g_o[...] = _dot(_sigmoid(xg).astype(BF16), gup_ref[...]).astype(BF16)

    def head_sum(x):
        return _dot_hi(_dot_hi(x, e_ref[...]), et_ref[...])

    kk = k * kk_ref[...]
    ss = head_sum(kk * kk)
    kk = kk / jnp.maximum(jnp.sqrt(ss), 1e-12)
    kmod = k * (1.0 + (a - 1.0) * ka_ref[...])
    r_o[...] = r.astype(BF16)
    k_o[...] = kmod.astype(BF16)
    v_o[...] = v.astype(BF16)
    kkn_o[...] = kk.astype(BF16)
    bb_o[...] = (kk * a).astype(BF16)
    bon_o[...] = (head_sum(r * kmod * rkp_ref[...]) * v).astype(BF16)


def _rwkv_prep(proj_r, proj_t, mus, vecs, wup, aup, gup, e, bsz, seq, tm=256):
    w = RWKV_WIDTH
    per_b = seq // tm

    def wide(off):
        return pl.BlockSpec((tm, w), lambda b, i: (b * per_b + i, off // w))

    def tail(off):
        return pl.BlockSpec((tm, TAIL_TILE), lambda b, i: (b * per_b + i, off // TAIL_TILE))

    def const(shape):
        return pl.BlockSpec(shape, lambda b, i: (0, 0))

    out_blk = pl.BlockSpec((tm, w), lambda b, i: (b * per_b + i, 0))
    kern = functools.partial(_rwkv_prep_kernel, tm=tm)
    return pl.pallas_call(
        kern,
        grid=(bsz, per_b),
        in_specs=[wide(OFF_RR), wide(OFF_RK), wide(OFF_RV), tail(OFF_WA), tail(OFF_XG),
                  const((1, w)), const((1, w)), const((1, w)),
                  const((1, TAIL_TILE)), const((1, GATE_LORA)),
                  const((1, w)), const((1, w)), const((1, w)), const((1, w)), const((1, w)),
                  const((LORA_PAD, w)), const((LORA_PAD, w)), const((GATE_LORA, w)),
                  const((w, LANES)), const((LANES, w))],
        out_specs=[out_blk] * 8,
        out_shape=[jax.ShapeDtypeStruct((bsz * seq, w), F32 if i == 1 else BF16) for i in range(8)],
        scratch_shapes=[pltpu.VMEM((8, w), F32)] * 3
        + [pltpu.VMEM((8, TAIL_TILE), F32), pltpu.VMEM((8, GATE_LORA), F32)],
        compiler_params=_cparams(("arbitrary", "arbitrary")),
        name="rwkv_prep",
    )(proj_r, proj_r, proj_r, proj_t, proj_t, *mus, *vecs, wup, aup, gup, e, e.T)


def _rwkv_core_kernel(r_ref, lw_ref, k_ref, v_ref, kk_ref, bb_ref, g_ref, bon_ref, lg_ref, lb_ref,
                      o_ref, z_ref):
    c = CHUNK

    @pl.when(pl.program_id(1) == 0)
    def _():
        z_ref[...] = jnp.zeros_like(z_ref)

    lw = lw_ref[...]
    tri = jnp.where(lax.broadcasted_iota(I32, (c, c), 1) <= lax.broadcasted_iota(I32, (c, c), 0),
                    1.0, 0.0).astype(BF16)
    hi = lw.astype(BF16)
    rem = lw - hi.astype(F32)
    mid = rem.astype(BF16)
    lo = (rem - mid.astype(F32)).astype(BF16)
    cum = _dot(tri, hi) + _dot(tri, mid) + _dot(tri, lo)
    p_in = jnp.exp(cum)
    p_ex = jnp.exp(cum - lw)
    p_inv = jnp.exp(-cum)
    p_end = p_in[c - 1:c, :]
    a_t = -kk_ref[...].astype(F32) * p_ex
    r_t = r_ref[...].astype(F32) * p_in
    b_h = bb_ref[...].astype(F32) * p_inv
    k_h = k_ref[...].astype(F32) * p_inv
    b_e = b_h * p_end
    k_e = k_h * p_end
    v = v_ref[...].astype(F32)

    n2 = 2 * c
    lane = lax.broadcasted_iota(I32, (1, LANES), 1)
    head0 = lane < RWKV_HEAD_DIM
    ri = lax.broadcasted_iota(I32, (n2, n2), 0)
    ci = lax.broadcasted_iota(I32, (n2, n2), 1)
    same = (ri >= c) == (ci >= c)
    strict = same & (ci < ri)
    incl = same & (ci <= ri)
    eye = ri == ci
    own = ((lax.broadcasted_iota(I32, (n2, LANES), 0) >= c)
           == (lax.broadcasted_iota(I32, (n2, LANES), 1) >= RWKV_HEAD_DIM))
    inv_n = 1.0 / RWKV_HEAD_DIM

    def stack(y):
        return jnp.concatenate([jnp.where(head0, y, 0.0), jnp.where(head0, 0.0, y)], axis=0)

    pairs = range(RWKV_WIDTH // LANES)
    sls = [slice(p * LANES, (p + 1) * LANES) for p in pairs]
    a_s = [stack(a_t[:, sl]) for sl in sls]
    r_s = [stack(r_t[:, sl]) for sl in sls]
    v_s = [stack(v[:, sl]).astype(BF16) for sl in sls]
    g1 = [_dot_nt(jnp.concatenate([a_s[p], r_s[p]], axis=0).astype(BF16),
                  jnp.concatenate([stack(b_h[:, sls[p]]), stack(k_h[:, sls[p]])], axis=0).astype(BF16))
          for p in pairs]
    pw = [jnp.where(strict, g[:n2, :n2], 0.0).astype(BF16) for g in g1]
    a_rb = [jnp.where(incl, g[n2:, :n2], 0.0).astype(BF16) for g in g1]
    a_rk = [jnp.where(incl, g[n2:, n2:], 0.0).astype(BF16) for g in g1]
    akv = [_dot(jnp.where(strict, g1[p][:n2, n2:], 0.0).astype(BF16), v_s[p]) for p in pairs]
    xc = [jnp.concatenate([a_s[p], akv[p]], axis=1) for p in pairs]
    steps = int(np.log2(c))
    for i in range(steps):
        if i + 1 < steps:
            res = [_dot(pw[p], jnp.concatenate([pw[p], xc[p].astype(BF16)], axis=1)) for p in pairs]
            xc = [xc[p] + res[p][:, n2:] for p in pairs]
            pw = [res[p][:, :n2].astype(BF16) for p in pairs]
        else:
            xc = [xc[p] + _dot(pw[p], xc[p].astype(BF16)) for p in pairs]
    xcb = [x.astype(BF16) for x in xc]
    r2 = [_dot(a_rb[p], xcb[p]) for p in pairs]
    ov = [r2[p][:, LANES:] + _dot(a_rk[p], v_s[p]) for p in pairs]
    mg = [_dot_tn(stack(b_e[:, sls[p]]).astype(BF16), xcb[p]) for p in pairs]
    kv = [_dot_tn(stack(k_e[:, sls[p]]).astype(BF16), v_s[p]) for p in pairs]
    for p in pairs:
        q_s = r_s[p] + r2[p][:, :LANES]
        mmat = mg[p][:, :LANES] + jnp.where(eye, p_end[:, sls[p]], 0.0)
        z_hi, z_lo = _split2(z_ref[p])
        qm = jnp.concatenate([q_s, mmat], axis=0).astype(BF16)
        res = _dot(qm, z_hi) + _dot(qm, z_lo)
        z_ref[p] = res[n2:] + mg[p][:, LANES:] + kv[p]
        o_s = res[:n2] + ov[p]
        mean = jnp.sum(o_s, axis=1, keepdims=True) * inv_n
        dev = jnp.where(own, o_s - mean, 0.0)
        var = jnp.sum(dev * dev, axis=1, keepdims=True) * inv_n
        y = dev * lax.rsqrt(var + LNX_EPS)
        y = (y[:c] + y[c:]) * lg_ref[:, sls[p]] + lb_ref[:, sls[p]]
        out = (y + bon_ref[:, sls[p]].astype(F32)) * g_ref[:, sls[p]].astype(F32)
        o_ref[:, sls[p]] = out.astype(o_ref.dtype)


def _rwkv_core(r, lw, k, v, kk, bb, g, bon, lg, lb, bsz, seq):
    c = CHUNK
    w = RWKV_WIDTH
    per_b = seq // c
    blk = pl.BlockSpec((c, w), lambda b, i: (b * per_b + i, 0))
    vec = pl.BlockSpec((1, w), lambda b, i: (0, 0))
    return pl.pallas_call(
        _rwkv_core_kernel,
        grid=(bsz, per_b),
        in_specs=[blk] * 8 + [vec, vec],
        out_specs=blk,
        out_shape=jax.ShapeDtypeStruct((bsz * seq, w), BF16),
        scratch_shapes=[pltpu.VMEM((w // LANES, LANES, LANES), F32)],
        compiler_params=_cparams(("arbitrary", "arbitrary")),
        name="rwkv_core",
    )(r, lw, k, v, kk, bb, g, bon, lg, lb)


def _rope_freqs(head_dim):
    half = head_dim // ROPE_FRACTION // 2
    return (ROPE_THETA ** (-jnp.arange(half, dtype=F32) / half)).reshape(half, 1)


def kernel(x, c, positions, w_ada, b_ada, norm1_g, w_in, q_norm_g, k_norm_g, rwkv_mu, rwkv_w0,
           rwkv_w_up, rwkv_a0, rwkv_a_up, rwkv_g_up, rwkv_k_k, rwkv_k_a, rwkv_r_k, rwkv_lnx_g,
           rwkv_lnx_b, w_out, norm2_g, w_ffn_gate, w_ffn_up, w_ffn_down):
    bsz, seq, d = x.shape
    depth = w_ada.shape[0]
    m = bsz * seq
    pos_row = positions.reshape(1, m)
    fa_col = _rope_freqs(ATT_HEAD_DIM)
    fi_col = _rope_freqs(IDX_HEAD_DIM)
    hd = RWKV_HEAD_DIM
    e = (jnp.arange(RWKV_WIDTH)[:, None] // hd == jnp.arange(LANES)[None, :]).astype(BF16)
    x2 = x.reshape(m, d)

    for l in range(depth):
        mod = _adaln(c, w_ada[l], b_ada[l])
        mod3 = mod.reshape(bsz * 6, 1, d)

        h1 = _norm(x2, norm1_g[l].reshape(1, d), mod3, seq)
        w_in_t = w_in[l].T
        proj_r = _matmul_nt(h1, w_in_t, lambda j: ATT_COLS + j * RWKV_WIDTH, 3, RWKV_WIDTH,
                            "in_proj_rkv")
        proj_t = _matmul_nt(h1, w_in_t, lambda j: TAIL_COL0 + j * TAIL_TILE, 2, TAIL_TILE,
                            "in_proj_tail", tm=min(2048, m))

        none = jnp.zeros((SUBLANES, 1), F32)
        aw = ATT_WIDTH
        qt = _proj_t(h1, w_in_t, OFF_Q, aw, pos_row, q_norm_g[l].reshape(-1, 1), fa_col, "q")
        kn = _proj_t(h1, w_in_t, OFF_K, aw, pos_row, k_norm_g[l].reshape(-1, 1), fa_col, "k")
        vt = _proj_t(h1, w_in_t, OFF_V, aw, pos_row, none, none, "v")
        iqt = _proj_t(h1, w_in_t, OFF_IQ, aw, pos_row, none, fi_col, "iq")
        ik, iwt = _proj_t(h1, w_in_t, 4 * aw, LANES, pos_row, none, fi_col, "ikw")
        bound = (ATT_HEAD_DIM ** 0.5 * LOG2E * BOUND_SLACK
                 * jnp.max(jnp.abs(q_norm_g[l])) * jnp.max(jnp.abs(k_norm_g[l])))
        att = _dsa(bound.reshape(1).astype(F32), qt, iqt, iwt, kn, vt, ik, bsz, seq, KEY_TILE)

        mu = rwkv_mu[l]
        w3 = 3 * RWKV_WIDTH

        n_wa = DECAY_LORA + AAA_LORA
        mu_wa = jnp.zeros((1, TAIL_TILE), F32).at[0, XW_LANE:XW_LANE + n_wa].set(mu[w3:w3 + n_wa])
        mus = [mu[0:RWKV_WIDTH].reshape(1, -1), mu[RWKV_WIDTH:2 * RWKV_WIDTH].reshape(1, -1),
               mu[2 * RWKV_WIDTH:w3].reshape(1, -1), mu_wa, mu[w3 + n_wa:].reshape(1, -1)]
        vecs = [rwkv_w0[l].reshape(1, -1), rwkv_a0[l].reshape(1, -1), rwkv_k_k[l].reshape(1, -1),
                rwkv_k_a[l].reshape(1, -1), rwkv_r_k[l].reshape(1, -1)]

        def pad_rows(wm):
            return jnp.zeros((LORA_PAD, wm.shape[1]), F32).at[:wm.shape[0]].set(wm).astype(BF16)

        r, lw, km, vv, kk, bb, g, bon = _rwkv_prep(
            proj_r, proj_t, mus, vecs, pad_rows(rwkv_w_up[l]), pad_rows(rwkv_a_up[l]),
            rwkv_g_up[l].astype(BF16), e, bsz, seq)
        rw = _rwkv_core(r, lw, km, vv, kk, bb, g, bon, rwkv_lnx_g[l].reshape(1, -1),
                        rwkv_lnx_b[l].reshape(1, -1), bsz, seq)

        wo = w_out[l].astype(BF16)
        x2, h2 = _out_proj(att, rw, wo[:ATT_WIDTH], wo[ATT_WIDTH:], x2, mod3,
                           norm2_g[l].reshape(1, d), seq)

        hglu = _ffn_glu(h2, w_ffn_gate[l], w_ffn_up[l])
        x2 = _ffn_down(hglu, w_ffn_down[l], x2, mod3, seq)
    return x2.reshape(bsz, seq, d)
```

```python
import functools

import jax
import jax.numpy as jnp
import numpy as np
from jax import lax
from jax.experimental import pallas as pl
from jax.experimental.pallas import tpu as pltpu

F32 = jnp.float32
BF16 = jnp.bfloat16
I32 = jnp.int32

D_MODEL = 2048
ATT_HEADS = 8
ATT_HEAD_DIM = 128
ATT_WIDTH = ATT_HEADS * ATT_HEAD_DIM
RWKV_WIDTH = D_MODEL - ATT_WIDTH
RWKV_HEAD_DIM = 64
RWKV_HEADS = RWKV_WIDTH // RWKV_HEAD_DIM
IDX_HEADS = 16
IDX_HEAD_DIM = 64
TOPK_MAX = 256
ROPE_THETA = 500000.0
ROPE_FRACTION = 4
DECAY_LORA = 96
AAA_LORA = 96
GATE_LORA = 256
NORM_EPS = 1e-6
LNX_EPS = 64e-5

LANES = 128
SUBLANES = 8
LORA_PAD = 128
ATT_COLS = 4 * ATT_WIDTH + IDX_HEAD_DIM + IDX_HEADS
IN_COLS = ATT_COLS + 3 * RWKV_WIDTH + DECAY_LORA + AAA_LORA + GATE_LORA
OFF_Q, OFF_K, OFF_V, OFF_IQ = 0, 1024, 2048, 3072
OFF_RR, OFF_RK, OFF_RV = 0, 1024, 2048
TAIL_TILE = 256
TAIL_COL0 = IN_COLS - 2 * TAIL_TILE
XW_LANE = IN_COLS - GATE_LORA - AAA_LORA - DECAY_LORA - TAIL_COL0
XA_LANE = XW_LANE + DECAY_LORA
OFF_WA, OFF_XG = 0, 256

LOG2E = 1.4426950408889634
V_ROWS = 144
KEY_TILE = 512
CHUNK = 64
VMEM_LIMIT = 56 * 1024 * 1024


def _cparams(sem):
    return pltpu.CompilerParams(dimension_semantics=sem, vmem_limit_bytes=VMEM_LIMIT)


def _dot(a, b):
    return jnp.dot(a, b, preferred_element_type=F32)


def _dot_nt(a, b):
    return lax.dot_general(a, b, (((1,), (1,)), ((), ())), preferred_element_type=F32)


def _dot_tn(a, b):
    return lax.dot_general(a, b, (((0,), (0,)), ((), ())), preferred_element_type=F32)


def _split2(x):
    hi = x.astype(BF16)
    lo = (x - hi.astype(F32)).astype(BF16)
    return hi, lo


def _dot_hi(x, w):
    hi, lo = _split2(x)
    return _dot(hi, w) + _dot(lo, w)


def _sigmoid(x):
    return 1.0 / (1.0 + jnp.exp(-x))


def _adaln_kernel(c_ref, w_ref, b_ref, o_ref):
    c = c_ref[...]
    ca = c * _sigmoid(c)
    o_ref[...] = _dot(ca.astype(BF16), w_ref[...].astype(BF16)) + b_ref[...]


def _adaln(c, w, b):
    bsz, d = c.shape
    n = w.shape[1]
    rows = 8
    cp = jnp.zeros((rows, d), F32).at[:bsz].set(c)
    tn = 1024
    out = pl.pallas_call(
        _adaln_kernel,
        grid=(n // tn,),
        in_specs=[pl.BlockSpec((rows, d), lambda j: (0, 0)),
                  pl.BlockSpec((d, tn), lambda j: (0, j)),
                  pl.BlockSpec((1, tn), lambda j: (0, j))],
        out_specs=pl.BlockSpec((rows, tn), lambda j: (0, j)),
        out_shape=jax.ShapeDtypeStruct((rows, n), F32),
        compiler_params=_cparams(("arbitrary",)),
        name="adaln",
    )(cp, w, b.reshape(1, n))
    return out[:bsz]


def _norm_mod(x, g, sc, sh):
    ms = jnp.mean(x * x, axis=-1, keepdims=True)
    y = x * lax.rsqrt(ms + NORM_EPS)
    y = y * g
    return y * (1.0 + sc) + sh


def _norm_kernel(x_ref, g_ref, sc_ref, sh_ref, o_ref):
    o_ref[...] = _norm_mod(x_ref[...], g_ref[...], sc_ref[...], sh_ref[...]).astype(BF16)


def _norm(x2, g, mod3, seq, tm=512):
    m, d = x2.shape
    per_b = seq // tm
    return pl.pallas_call(
        _norm_kernel,
        grid=(m // tm,),
        in_specs=[pl.BlockSpec((tm, d), lambda i: (i, 0)),
                  pl.BlockSpec((1, d), lambda i: (0, 0)),
                  pl.BlockSpec((None, 1, d), lambda i: ((i // per_b) * 6 + 1, 0, 0)),
                  pl.BlockSpec((None, 1, d), lambda i: ((i // per_b) * 6 + 0, 0, 0))],
        out_specs=pl.BlockSpec((tm, d), lambda i: (i, 0)),
        out_shape=jax.ShapeDtypeStruct((m, d), BF16),
        compiler_params=_cparams(("arbitrary",)),
        name="norm1",
    )(x2, g, mod3, mod3)


def _mm_nt_kernel(h_ref, wt_ref, o_ref, wb_ref):
    @pl.when(pl.program_id(1) == 0)
    def _():
        wb_ref[...] = wt_ref[...].astype(BF16)

    o_ref[...] = _dot_nt(h_ref[...], wb_ref[...])


def _matmul_nt(h, wt, col_of_tile, n_tiles, tn, name, tm=1024):
    m, d = h.shape
    return pl.pallas_call(
        _mm_nt_kernel,
        grid=(n_tiles, m // tm),
        in_specs=[pl.BlockSpec((tm, d), lambda j, i: (i, 0)),
                  pl.BlockSpec((pl.Element(tn), pl.Element(d)),
                               lambda j, i: (pl.multiple_of(col_of_tile(j), SUBLANES), 0))],
        out_specs=pl.BlockSpec((tm, tn), lambda j, i: (i, j)),
        out_shape=jax.ShapeDtypeStruct((m, n_tiles * tn), F32),
        scratch_shapes=[pltpu.VMEM((tn, d), BF16)],
        compiler_params=_cparams(("arbitrary", "arbitrary")),
        name=name,
    )(h, wt)


def _ffn_glu_kernel(h_ref, wg_ref, wu_ref, o_ref, wgb_ref, wub_ref):
    @pl.when(pl.program_id(1) == 0)
    def _():
        wgb_ref[...] = wg_ref[...].astype(BF16)
        wub_ref[...] = wu_ref[...].astype(BF16)

    h = h_ref[...]
    a = _dot(h, wgb_ref[...])
    u = _dot(h, wub_ref[...])
    o_ref[...] = (a * _sigmoid(a) * u).astype(o_ref.dtype)


def _ffn_glu(h, wg, wu, tm=1024, tn=512):
    m, d = h.shape
    n = wg.shape[1]
    return pl.pallas_call(
        _ffn_glu_kernel,
        grid=(n // tn, m // tm),
        in_specs=[pl.BlockSpec((tm, d), lambda j, i: (i, 0)),
                  pl.BlockSpec((d, tn), lambda j, i: (0, j)),
                  pl.BlockSpec((d, tn), lambda j, i: (0, j))],
        out_specs=pl.BlockSpec((tm, tn), lambda j, i: (i, j)),
        out_shape=jax.ShapeDtypeStruct((m, n), BF16),
        scratch_shapes=[pltpu.VMEM((d, tn), BF16)] * 2,
        compiler_params=_cparams(("arbitrary", "arbitrary")),
        name="ffn_glu",
    )(h, wg, wu)


def _out_proj_kernel(a_ref, r_ref, wa_ref, wr_ref, x_ref, gt_ref, g_ref, sc_ref, sh_ref,
                     x1_ref, h2_ref):
    mixed = _dot(a_ref[...], wa_ref[...]) + _dot(r_ref[...], wr_ref[...])
    x1 = x_ref[...] + gt_ref[...] * mixed
    x1_ref[...] = x1
    h2_ref[...] = _norm_mod(x1, g_ref[...], sc_ref[...], sh_ref[...]).astype(BF16)


def _out_proj(att, rwkv, wa, wr, x2, mod3, g2, seq, tm=512):
    m, ka = att.shape
    kr = rwkv.shape[1]
    n = wa.shape[1]
    per_b = seq // tm

    def mod_row(j):
        return pl.BlockSpec((None, 1, n), lambda i: ((i // per_b) * 6 + j, 0, 0))

    row = pl.BlockSpec((tm, n), lambda i: (i, 0))
    return pl.pallas_call(
        _out_proj_kernel,
        grid=(m // tm,),
        in_specs=[pl.BlockSpec((tm, ka), lambda i: (i, 0)),
                  pl.BlockSpec((tm, kr), lambda i: (i, 0)),
                  pl.BlockSpec((ka, n), lambda i: (0, 0)),
                  pl.BlockSpec((kr, n), lambda i: (0, 0)),
                  row, mod_row(2),
                  pl.BlockSpec((1, n), lambda i: (0, 0)), mod_row(4), mod_row(3)],
        out_specs=[row, row],
        out_shape=[jax.ShapeDtypeStruct((m, n), F32), jax.ShapeDtypeStruct((m, n), BF16)],
        compiler_params=_cparams(("arbitrary",)),
        name="out_proj",
    )(att, rwkv, wa, wr, x2, mod3, g2, mod3, mod3)


def _ffn_down_kernel(h_ref, w_ref, x_ref, gt_ref, o_ref, wb_ref):
    @pl.when(pl.program_id(1) == 0)
    def _():
        wb_ref[...] = w_ref[...].astype(BF16)

    o_ref[...] = x_ref[...] + gt_ref[...] * _dot(h_ref[...], wb_ref[...])


def _ffn_down(h, w, x2, mod3, seq, tm=512, tn=512):
    m, kdim = h.shape
    n = w.shape[1]
    per_b = seq // tm
    return pl.pallas_call(
        _ffn_down_kernel,
        grid=(n // tn, m // tm),
        in_specs=[pl.BlockSpec((tm, kdim), lambda j, i: (i, 0)),
                  pl.BlockSpec((kdim, tn), lambda j, i: (0, j)),
                  pl.BlockSpec((tm, tn), lambda j, i: (i, j)),
                  pl.BlockSpec((None, 1, tn), lambda j, i: ((i // per_b) * 6 + 5, 0, j))],
        out_specs=pl.BlockSpec((tm, tn), lambda j, i: (i, j)),
        out_shape=jax.ShapeDtypeStruct((m, n), F32),
        scratch_shapes=[pltpu.VMEM((kdim, tn), BF16)],
        compiler_params=_cparams(("arbitrary", "arbitrary")),
        name="ffn_down",
    )(h, w, x2, mod3)


def _proj_t_kernel(wt_ref, h_ref, pos_ref, g_ref, f_ref, *refs, mode):
    *o_refs, wb_ref = refs
    o_ref = o_refs[0]

    @pl.when(pl.program_id(0) == 0)
    def _():
        wb_ref[...] = wt_ref[...].astype(BF16)

    yt = _dot_nt(wb_ref[...], h_ref[...])
    tm = yt.shape[1]
    if mode == "v":
        pad_row = lax.broadcasted_iota(I32, (V_ROWS - LANES, KEY_TILE), 0)
        ones_rows = jnp.where(pad_row == 0, 1.0, 0.0).astype(BF16)
        for h in range(ATT_HEADS):
            for j in range(tm // KEY_TILE):
                tile = yt[h * LANES:(h + 1) * LANES, j * KEY_TILE:(j + 1) * KEY_TILE]
                o_ref[h, j, 0:LANES, :] = tile.astype(BF16)
                o_ref[h, j, LANES:V_ROWS, :] = ones_rows
        return

    head = ATT_HEAD_DIM if mode in ("q", "k") else IDX_HEAD_DIM
    n_heads = 1 if mode == "ikw" else yt.shape[0] // head
    half = head // ROPE_FRACTION // 2
    ang = f_ref[...] * pos_ref[...].astype(F32)
    cos, sin = jnp.cos(ang), jnp.sin(ang)
    for hd in range(n_heads):
        y = yt[hd * head:(hd + 1) * head, :]
        if mode in ("q", "k"):
            ms = jnp.mean(y * y, axis=0, keepdims=True)
            y = y * lax.rsqrt(ms + NORM_EPS) * g_ref[...]
        x1, x2 = y[0:half], y[half:2 * half]
        y = jnp.concatenate([x1 * cos - x2 * sin, x2 * cos + x1 * sin, y[2 * half:]], axis=0)
        if mode == "q":
            o_ref[hd] = (y * ((ATT_HEAD_DIM ** -0.5) * LOG2E)).astype(BF16)
        elif mode == "iq":
            rows = slice((hd % 2) * head, (hd % 2 + 1) * head)
            o_ref[hd // 2, rows, :] = (y * (IDX_HEAD_DIM ** -0.5)).astype(BF16)
        elif mode == "k":
            o_ref[:, hd * LANES:(hd + 1) * LANES] = y.T.astype(BF16)
        else:
            o_ref[...] = jnp.concatenate([y, yt[head:, :]], axis=0).T.astype(BF16)
            o_refs[1][...] = yt[head:head + IDX_HEADS, :] * (IDX_HEADS ** -0.5)


def _proj_t(h, wt, col0, n, pos_row, g_col, f_col, mode, tm=1024):
    m, d = h.shape
    tm = min(tm, m)
    if mode == "v":
        out_specs = [pl.BlockSpec((ATT_HEADS, tm // KEY_TILE, V_ROWS, KEY_TILE),
                                  lambda i: (0, i, 0, 0))]
        out_shape = [jax.ShapeDtypeStruct((ATT_HEADS, m // KEY_TILE, V_ROWS, KEY_TILE), BF16)]
    elif mode in ("q", "iq"):
        out_specs = [pl.BlockSpec((n // LANES, LANES, tm), lambda i: (0, 0, i))]
        out_shape = [jax.ShapeDtypeStruct((n // LANES, LANES, m), BF16)]
    else:
        out_specs = [pl.BlockSpec((tm, n), lambda i: (i, 0))]
        out_shape = [jax.ShapeDtypeStruct((m, n), BF16)]
        if mode == "ikw":
            out_specs.append(pl.BlockSpec((IDX_HEADS, tm), lambda i: (0, i)))
            out_shape.append(jax.ShapeDtypeStruct((IDX_HEADS, m), F32))
    out = pl.pallas_call(
        functools.partial(_proj_t_kernel, mode=mode),
        grid=(m // tm,),
        in_specs=[pl.BlockSpec((pl.Element(n), pl.Element(d)), lambda i: (col0, 0)),
                  pl.BlockSpec((tm, d), lambda i: (i, 0)),
                  pl.BlockSpec((1, tm), lambda i: (0, i)),
                  pl.BlockSpec(g_col.shape, lambda i: (0, 0)),
                  pl.BlockSpec(f_col.shape, lambda i: (0, 0))],
        out_specs=out_specs,
        out_shape=out_shape,
        scratch_shapes=[pltpu.VMEM((n, d), BF16)],
        compiler_params=_cparams(("arbitrary",)),
        name="in_proj_" + mode,
    )(wt, h, pos_row, g_col, f_col)
    return out if mode == "ikw" else out[0]


NEG_BIG = -1e30


INT_MIN = -2 ** 31
INT_MAX = 2 ** 31 - 1
MAGNITUDE_BITS = 0x7FFFFFFF
KEY_NEG_INF = -2139095041
BOUND_SLACK = 1.02
MAX_STATIC_SHIFT = 60.0
COUNT_ROWS = 32


def _key_to_float(key):
    bits = key ^ ((key >> 31) & jnp.int32(MAGNITUDE_BITS))
    return lax.bitcast_convert_type(bits, F32)


def _dsa_kernel(bound_ref, qt_ref, iqt_ref, iwt_ref, k_ref, vt_ref, ik_ref, o_ref,
                sc_ref, m_ref, acc_ref, last_ref, *, tq, tk, topk, index_bits):
    qi = pl.program_id(1)
    n_kb = (qi * tq + tq - 1) // tk + 1
    key0 = lax.broadcasted_iota(I32, (tk, tq), 0)
    qidx = qi * tq + lax.broadcasted_iota(I32, (tk, tq), 1)
    iw = iwt_ref[...]

    def score_body(kb, carry):
        start = pl.multiple_of(kb * tk, tk)
        ikb = ik_ref[pl.ds(start, tk), 0:IDX_HEAD_DIM]
        s = jnp.zeros((tk, tq), F32)
        for h in range(IDX_HEADS):
            off = (h % 2) * IDX_HEAD_DIM
            d = _dot(ikb, iqt_ref[h // 2, off:off + IDX_HEAD_DIM, :])
            s = s + jnp.maximum(d, 0.0) * iw[h:h + 1, :]
        sc_ref[kb] = jnp.where(kb * tk + key0 <= qidx, s, -jnp.inf)
        return carry

    lax.fori_loop(0, n_kb, score_body, 0)

    def count(pred):
        def cnt_body(kb, acc):
            hit = pred(kb).reshape(tk // COUNT_ROWS, COUNT_ROWS, tq)
            for r in range(tk // COUNT_ROWS):
                acc = jnp.where(hit[r], acc + 1.0, acc)
            return acc

        acc = lax.fori_loop(0, n_kb, cnt_body, jnp.zeros((COUNT_ROWS, tq), F32))
        return jnp.sum(acc, axis=0, keepdims=True)

    def bit_body(i, cand):
        trial = cand ^ lax.shift_left(jnp.int32(1), 31 - i)
        trial_f = _key_to_float(trial)
        cnt = count(lambda kb: sc_ref[kb] >= trial_f)
        return jnp.where(cnt >= topk, trial, cand)

    cand = lax.fori_loop(0, 32, bit_body, jnp.full((1, tq), INT_MIN, I32))
    tau = _key_to_float(jnp.maximum(cand, jnp.int32(KEY_NEG_INF)))

    last_ref[...] = jnp.full_like(last_ref, INT_MAX)
    n_ge = count(lambda kb: sc_ref[kb] >= tau)

    @pl.when(jnp.max(n_ge) > topk)
    def _():
        need = topk - count(lambda kb: sc_ref[kb] > tau)

        def idx_body(i, last):
            trial = last | lax.shift_left(jnp.int32(1), index_bits - 1 - i)
            below = count(lambda kb: (sc_ref[kb] == tau) & (kb * tk + key0 < trial))
            return jnp.where(below < need, trial, last)

        last_ref[...] = lax.fori_loop(0, index_bits, idx_body, jnp.zeros((1, tq), I32))

    acc_ref[...] = jnp.zeros_like(acc_ref)
    bound = bound_ref[0]

    def logits(kb, h):
        start = pl.multiple_of(kb * tk, tk)
        return _dot(k_ref[pl.ds(start, tk), h * LANES:(h + 1) * LANES], qt_ref[h])

    last = last_ref[...]

    def selected(kb):
        s = sc_ref[kb]
        kidx = kb * tk + key0
        return ((s > tau) | ((s == tau) & (kidx <= last))) & (kidx <= qidx)

    @pl.when(bound <= MAX_STATIC_SHIFT)
    def _():
        def att_body(kb, carry):
            bias = jnp.where(selected(kb), -bound, -jnp.inf)
            for h in range(ATT_HEADS):
                p = jnp.exp2(logits(kb, h) + bias)
                acc_ref[h] += _dot(vt_ref[h, kb], p.astype(BF16))
            return carry

        lax.fori_loop(0, n_kb, att_body, 0)

    @pl.when(bound > MAX_STATIC_SHIFT)
    def _():
        m_ref[...] = jnp.full_like(m_ref, NEG_BIG)

        def att_body(kb, carry):
            bias = jnp.where(selected(kb), 0.0, -jnp.inf)
            for h in range(ATT_HEADS):
                s = logits(kb, h) + bias
                m_prev = m_ref[h]
                m_next = jnp.maximum(m_prev, jnp.max(s, axis=0, keepdims=True))
                p = jnp.exp2(s - m_next)
                alpha = jnp.exp2(m_prev - m_next)
                acc_ref[h] = alpha * acc_ref[h] + _dot(vt_ref[h, kb], p.astype(BF16))
                m_ref[h] = m_next
            return carry

        lax.fori_loop(0, n_kb, att_body, 0)

    for h in range(ATT_HEADS):
        out = acc_ref[h, 0:LANES, :] / acc_ref[h, LANES:LANES + 1, :]
        o_ref[:, h * LANES:(h + 1) * LANES] = out.T.astype(o_ref.dtype)


def _dsa(bound, qt, iqt, iwt, k, vt, ik, bsz, seq, tk, tq=512):
    tq = min(tq, seq)
    topk = min(TOPK_MAX, seq // 4)
    w = ATT_WIDTH
    nq = seq // tq
    nkb = seq // tk
    npair = iqt.shape[0]
    kern = functools.partial(_dsa_kernel, tq=tq, tk=tk, topk=topk,
                             index_bits=max(1, (seq - 1).bit_length()))
    return pl.pallas_call(
        kern,
        grid=(bsz, nq),
        in_specs=[pl.BlockSpec(memory_space=pltpu.SMEM),
                  pl.BlockSpec((ATT_HEADS, LANES, tq), lambda b, i: (0, 0, b * nq + i)),
                  pl.BlockSpec((npair, LANES, tq), lambda b, i: (0, 0, b * nq + i)),
                  pl.BlockSpec((IDX_HEADS, tq), lambda b, i: (0, b * nq + i)),
                  pl.BlockSpec((seq, w), lambda b, i: (b, 0)),
                  pl.BlockSpec((ATT_HEADS, nkb, V_ROWS, tk), lambda b, i: (0, b, 0, 0)),
                  pl.BlockSpec((seq, LANES), lambda b, i: (b, 0))],
        out_specs=pl.BlockSpec((tq, w), lambda b, i: (b * nq + i, 0)),
        out_shape=jax.ShapeDtypeStruct((bsz * seq, w), BF16),
        scratch_shapes=[pltpu.VMEM((nkb, tk, tq), F32),
                        pltpu.VMEM((ATT_HEADS, 1, tq), F32),
                        pltpu.VMEM((ATT_HEADS, V_ROWS, tq), F32),
                        pltpu.VMEM((1, tq), I32)],
        compiler_params=_cparams(("arbitrary", "arbitrary")),
        name="dsa",
    )(bound, qt, iqt, iwt, k, vt, ik)


def _rwkv_prep_kernel(rr_ref, rk_ref, rv_ref, wa_ref, xg_ref,
                      mr_ref, mk_ref, mv_ref, mwa_ref, mg_ref,
                      w0_ref, a0_ref, kk_ref, ka_ref, rkp_ref,
                      wup_ref, aup_ref, gup_ref, e_ref, et_ref,
                      r_o, lw_o, k_o, v_o, kkn_o, bb_o, g_o, bon_o,
                      c_r, c_k, c_v, c_wa, c_g, *, tm):
    first = pl.program_id(1) == 0

    def shift(y_ref, carry_ref, mu_ref):
        y = y_ref[...]

        @pl.when(first)
        def _():
            carry_ref[...] = jnp.zeros_like(carry_ref)

        prev_last = carry_ref[7:8, :]
        rolled = pltpu.roll(y, 1, 0)
        rows = lax.broadcasted_iota(I32, y.shape, 0)
        yprev = jnp.where(rows == 0, prev_last, rolled)
        carry_ref[...] = y[tm - 8:tm, :]
        return y + (yprev - y) * mu_ref[...]

    r = shift(rr_ref, c_r, mr_ref)
    k = shift(rk_ref, c_k, mk_ref)
    v = shift(rv_ref, c_v, mv_ref)
    wa = shift(wa_ref, c_wa, mwa_ref)
    xw = pltpu.roll(wa, TAIL_TILE - XW_LANE, 1)[:, :LORA_PAD]
    xa = pltpu.roll(wa, TAIL_TILE - XA_LANE, 1)[:, :LORA_PAD]
    xg = shift(xg_ref, c_g, mg_ref)

    w_raw = w0_ref[...] + _dot(jnp.tanh(xw).astype(BF16), wup_ref[...])
    z = -w_raw
    softplus = jnp.maximum(z, 0.0) + jnp.log(1.0 + jnp.exp(-jnp.abs(z)))
    lw_o[...] = -jnp.exp(-softplus - 0.5)
    a = _sigmoid(a0_ref[...] + _dot(xa.astype(BF16), aup_ref[...]))
    g_o[...] = _dot(_sigmoid(xg).astype(BF16), gup_ref[...]).astype(BF16)

    def head_sum(x):
        return _dot_hi(_dot_hi(x, e_ref[...]), et_ref[...])

    kk = k * kk_ref[...]
    ss = head_sum(kk * kk)
    kk = kk / jnp.maximum(jnp.sqrt(ss), 1e-12)
    kmod = k * (1.0 + (a - 1.0) * ka_ref[...])
    r_o[...] = r.astype(BF16)
    k_o[...] = kmod.astype(BF16)
    v_o[...] = v.astype(BF16)
    kkn_o[...] = kk.astype(BF16)
    bb_o[...] = (kk * a).astype(BF16)
    bon_o[...] = (head_sum(r * kmod * rkp_ref[...]) * v).astype(BF16)


def _rwkv_prep(proj_r, proj_t, mus, vecs, wup, aup, gup, e, bsz, seq, tm=256):
    w = RWKV_WIDTH
    per_b = seq // tm

    def wide(off):
        return pl.BlockSpec((tm, w), lambda b, i: (b * per_b + i, off // w))

    def tail(off):
        return pl.BlockSpec((tm, TAIL_TILE), lambda b, i: (b * per_b + i, off // TAIL_TILE))

    def const(shape):
        return pl.BlockSpec(shape, lambda b, i: (0, 0))

    out_blk = pl.BlockSpec((tm, w), lambda b, i: (b * per_b + i, 0))
    kern = functools.partial(_rwkv_prep_kernel, tm=tm)
    return pl.pallas_call(
        kern,
        grid=(bsz, per_b),
        in_specs=[wide(OFF_RR), wide(OFF_RK), wide(OFF_RV), tail(OFF_WA), tail(OFF_XG),
                  const((1, w)), const((1, w)), const((1, w)),
                  const((1, TAIL_TILE)), const((1, GATE_LORA)),
                  const((1, w)), const((1, w)), const((1, w)), const((1, w)), const((1, w)),
                  const((LORA_PAD, w)), const((LORA_PAD, w)), const((GATE_LORA, w)),
                  const((w, LANES)), const((LANES, w))],
        out_specs=[out_blk] * 8,
        out_shape=[jax.ShapeDtypeStruct((bsz * seq, w), F32 if i == 1 else BF16) for i in range(8)],
        scratch_shapes=[pltpu.VMEM((8, w), F32)] * 3
        + [pltpu.VMEM((8, TAIL_TILE), F32), pltpu.VMEM((8, GATE_LORA), F32)],
        compiler_params=_cparams(("arbitrary", "arbitrary")),
        name="rwkv_prep",
    )(proj_r, proj_r, proj_r, proj_t, proj_t, *mus, *vecs, wup, aup, gup, e, e.T)


def _rwkv_core_kernel(r_ref, lw_ref, k_ref, v_ref, kk_ref, bb_ref, g_ref, bon_ref, lg_ref, lb_ref,
                      o_ref, z_ref):
    c = CHUNK

    @pl.when(pl.program_id(1) == 0)
    def _():
        z_ref[...] = jnp.zeros_like(z_ref)

    lw = lw_ref[...]
    tri = jnp.where(lax.broadcasted_iota(I32, (c, c), 1) <= lax.broadcasted_iota(I32, (c, c), 0),
                    1.0, 0.0).astype(BF16)
    hi = lw.astype(BF16)
    rem = lw - hi.astype(F32)
    mid = rem.astype(BF16)
    lo = (rem - mid.astype(F32)).astype(BF16)
    cum = _dot(tri, hi) + _dot(tri, mid) + _dot(tri, lo)
    p_in = jnp.exp(cum)
    p_ex = jnp.exp(cum - lw)
    p_inv = jnp.exp(-cum)
    p_end = p_in[c - 1:c, :]
    a_t = -kk_ref[...].astype(F32) * p_ex
    r_t = r_ref[...].astype(F32) * p_in
    b_h = bb_ref[...].astype(F32) * p_inv
    k_h = k_ref[...].astype(F32) * p_inv
    b_e = b_h * p_end
    k_e = k_h * p_end
    v = v_ref[...].astype(F32)

    n2 = 2 * c
    lane = lax.broadcasted_iota(I32, (1, LANES), 1)
    head0 = lane < RWKV_HEAD_DIM
    ri = lax.broadcasted_iota(I32, (n2, n2), 0)
    ci = lax.broadcasted_iota(I32, (n2, n2), 1)
    same = (ri >= c) == (ci >= c)
    strict = same & (ci < ri)
    incl = same & (ci <= ri)
    eye = ri == ci
    own = ((lax.broadcasted_iota(I32, (n2, LANES), 0) >= c)
           == (lax.broadcasted_iota(I32, (n2, LANES), 1) >= RWKV_HEAD_DIM))
    inv_n = 1.0 / RWKV_HEAD_DIM

    def stack(y):
        return jnp.concatenate([jnp.where(head0, y, 0.0), jnp.where(head0, 0.0, y)], axis=0)

    pairs = range(RWKV_WIDTH // LANES)
    sls = [slice(p * LANES, (p + 1) * LANES) for p in pairs]
    a_s = [stack(a_t[:, sl]) for sl in sls]
    r_s = [stack(r_t[:, sl]) for sl in sls]
    v_s = [stack(v[:, sl]).astype(BF16) for sl in sls]
    g1 = [_dot_nt(jnp.concatenate([a_s[p], r_s[p]], axis=0).astype(BF16),
                  jnp.concatenate([stack(b_h[:, sls[p]]), stack(k_h[:, sls[p]])], axis=0).astype(BF16))
          for p in pairs]
    pw = [jnp.where(strict, g[:n2, :n2], 0.0).astype(BF16) for g in g1]
    a_rb = [jnp.where(incl, g[n2:, :n2], 0.0).astype(BF16) for g in g1]
    a_rk = [jnp.where(incl, g[n2:, n2:], 0.0).astype(BF16) for g in g1]
    akv = [_dot(jnp.where(strict, g1[p][:n2, n2:], 0.0).astype(BF16), v_s[p]) for p in pairs]
    xc = [jnp.concatenate([a_s[p], akv[p]], axis=1) for p in pairs]
    steps = int(np.log2(c))
    for i in range(steps):
        if i + 1 < steps:
            res = [_dot(pw[p], jnp.concatenate([pw[p], xc[p].astype(BF16)], axis=1)) for p in pairs]
            xc = [xc[p] + res[p][:, n2:] for p in pairs]
            pw = [res[p][:, :n2].astype(BF16) for p in pairs]
        else:
            xc = [xc[p] + _dot(pw[p], xc[p].astype(BF16)) for p in pairs]
    xcb = [x.astype(BF16) for x in xc]
    r2 = [_dot(a_rb[p], xcb[p]) for p in pairs]
    ov = [r2[p][:, LANES:] + _dot(a_rk[p], v_s[p]) for p in pairs]
    mg = [_dot_tn(stack(b_e[:, sls[p]]).astype(BF16), xcb[p]) for p in pairs]
    kv = [_dot_tn(stack(k_e[:, sls[p]]).astype(BF16), v_s[p]) for p in pairs]
    for p in pairs:
        q_s = r_s[p] + r2[p][:, :LANES]
        mmat = mg[p][:, :LANES] + jnp.where(eye, p_end[:, sls[p]], 0.0)
        qm = jnp.concatenate([q_s, mmat], axis=0).astype(BF16)
        res = _dot(qm, z_ref[p].astype(BF16))
        z_ref[p] = res[n2:] + mg[p][:, LANES:] + kv[p]
        o_s = res[:n2] + ov[p]
        mean = jnp.sum(o_s, axis=1, keepdims=True) * inv_n
        dev = jnp.where(own, o_s - mean, 0.0)
        var = jnp.sum(dev * dev, axis=1, keepdims=True) * inv_n
        y = dev * lax.rsqrt(var + LNX_EPS)
        y = (y[:c] + y[c:]) * lg_ref[:, sls[p]] + lb_ref[:, sls[p]]
        out = (y + bon_ref[:, sls[p]].astype(F32)) * g_ref[:, sls[p]].astype(F32)
        o_ref[:, sls[p]] = out.astype(o_ref.dtype)


def _rwkv_core(r, lw, k, v, kk, bb, g, bon, lg, lb, bsz, seq):
    c = CHUNK
    w = RWKV_WIDTH
    per_b = seq // c
    blk = pl.BlockSpec((c, w), lambda b, i: (b * per_b + i, 0))
    vec = pl.BlockSpec((1, w), lambda b, i: (0, 0))
    return pl.pallas_call(
        _rwkv_core_kernel,
        grid=(bsz, per_b),
        in_specs=[blk] * 8 + [vec, vec],
        out_specs=blk,
        out_shape=jax.ShapeDtypeStruct((bsz * seq, w), BF16),
        scratch_shapes=[pltpu.VMEM((w // LANES, LANES, LANES), F32)],
        compiler_params=_cparams(("arbitrary", "arbitrary")),
        name="rwkv_core",
    )(r, lw, k, v, kk, bb, g, bon, lg, lb)


def _rope_freqs(head_dim):
    half = head_dim // ROPE_FRACTION // 2
    return (ROPE_THETA ** (-jnp.arange(half, dtype=F32) / half)).reshape(half, 1)


def kernel(x, c, positions, w_ada, b_ada, norm1_g, w_in, q_norm_g, k_norm_g, rwkv_mu, rwkv_w0,
           rwkv_w_up, rwkv_a0, rwkv_a_up, rwkv_g_up, rwkv_k_k, rwkv_k_a, rwkv_r_k, rwkv_lnx_g,
           rwkv_lnx_b, w_out, norm2_g, w_ffn_gate, w_ffn_up, w_ffn_down):
    bsz, seq, d = x.shape
    depth = w_ada.shape[0]
    m = bsz * seq
    pos_row = positions.reshape(1, m)
    fa_col = _rope_freqs(ATT_HEAD_DIM)
    fi_col = _rope_freqs(IDX_HEAD_DIM)
    hd = RWKV_HEAD_DIM
    e = (jnp.arange(RWKV_WIDTH)[:, None] // hd == jnp.arange(LANES)[None, :]).astype(BF16)
    x2 = x.reshape(m, d)

    for l in range(depth):
        mod = _adaln(c, w_ada[l], b_ada[l])
        mod3 = mod.reshape(bsz * 6, 1, d)

        h1 = _norm(x2, norm1_g[l].reshape(1, d), mod3, seq)
        w_in_t = w_in[l].T
        proj_r = _matmul_nt(h1, w_in_t, lambda j: ATT_COLS + j * RWKV_WIDTH, 3, RWKV_WIDTH,
                            "in_proj_rkv")
        proj_t = _matmul_nt(h1, w_in_t, lambda j: TAIL_COL0 + j * TAIL_TILE, 2, TAIL_TILE,
                            "in_proj_tail", tm=min(2048, m))

        none = jnp.zeros((SUBLANES, 1), F32)
        aw = ATT_WIDTH
        qt = _proj_t(h1, w_in_t, OFF_Q, aw, pos_row, q_norm_g[l].reshape(-1, 1), fa_col, "q")
        kn = _proj_t(h1, w_in_t, OFF_K, aw, pos_row, k_norm_g[l].reshape(-1, 1), fa_col, "k")
        vt = _proj_t(h1, w_in_t, OFF_V, aw, pos_row, none, none, "v")
        iqt = _proj_t(h1, w_in_t, OFF_IQ, aw, pos_row, none, fi_col, "iq")
        ik, iwt = _proj_t(h1, w_in_t, 4 * aw, LANES, pos_row, none, fi_col, "ikw")
        bound = (ATT_HEAD_DIM ** 0.5 * LOG2E * BOUND_SLACK
                 * jnp.max(jnp.abs(q_norm_g[l])) * jnp.max(jnp.abs(k_norm_g[l])))
        att = _dsa(bound.reshape(1).astype(F32), qt, iqt, iwt, kn, vt, ik, bsz, seq, KEY_TILE)

        mu = rwkv_mu[l]
        w3 = 3 * RWKV_WIDTH

        n_wa = DECAY_LORA + AAA_LORA
        mu_wa = jnp.zeros((1, TAIL_TILE), F32).at[0, XW_LANE:XW_LANE + n_wa].set(mu[w3:w3 + n_wa])
        mus = [mu[0:RWKV_WIDTH].reshape(1, -1), mu[RWKV_WIDTH:2 * RWKV_WIDTH].reshape(1, -1),
               mu[2 * RWKV_WIDTH:w3].reshape(1, -1), mu_wa, mu[w3 + n_wa:].reshape(1, -1)]
        vecs = [rwkv_w0[l].reshape(1, -1), rwkv_a0[l].reshape(1, -1), rwkv_k_k[l].reshape(1, -1),
                rwkv_k_a[l].reshape(1, -1), rwkv_r_k[l].reshape(1, -1)]

        def pad_rows(wm):
            return jnp.zeros((LORA_PAD, wm.shape[1]), F32).at[:wm.shape[0]].set(wm).astype(BF16)

        r, lw, km, vv, kk, bb, g, bon = _rwkv_prep(
            proj_r, proj_t, mus, vecs, pad_rows(rwkv_w_up[l]), pad_rows(rwkv_a_up[l]),
            rwkv_g_up[l].astype(BF16), e, bsz, seq)
        rw = _rwkv_core(r, lw, km, vv, kk, bb, g, bon, rwkv_lnx_g[l].reshape(1, -1),
                        rwkv_lnx_b[l].reshape(1, -1), bsz, seq)

        wo = w_out[l].astype(BF16)
        x2, h2 = _out_proj(att, rw, wo[:ATT_WIDTH], wo[ATT_WIDTH:], x2, mod3,
                           norm2_g[l].reshape(1, d), seq)

        hglu = _ffn_glu(h2, w_ffn_gate[l], w_ffn_up[l])
        x2 = _ffn_down(hglu, w_ffn_down[l], x2, mod3, seq)
    return x2.reshape(bsz, seq, d)
```

```python
import functools

import jax
import jax.numpy as jnp
import numpy as np
from jax import lax
from jax.experimental import pallas as pl
from jax.experimental.pallas import tpu as pltpu

F32 = jnp.float32
BF16 = jnp.bfloat16
I32 = jnp.int32

D_MODEL = 2048
ATT_HEADS = 8
ATT_HEAD_DIM = 128
ATT_WIDTH = ATT_HEADS * ATT_HEAD_DIM
RWKV_WIDTH = D_MODEL - ATT_WIDTH
RWKV_HEAD_DIM = 64
RWKV_HEADS = RWKV_WIDTH // RWKV_HEAD_DIM
IDX_HEADS = 16
IDX_HEAD_DIM = 64
TOPK_MAX = 256
ROPE_THETA = 500000.0
ROPE_FRACTION = 4
DECAY_LORA = 96
AAA_LORA = 96
GATE_LORA = 256
NORM_EPS = 1e-6
LNX_EPS = 64e-5

LANES = 128
SUBLANES = 8
LORA_PAD = 128
ATT_COLS = 4 * ATT_WIDTH + IDX_HEAD_DIM + IDX_HEADS
IN_COLS = ATT_COLS + 3 * RWKV_WIDTH + DECAY_LORA + AAA_LORA + GATE_LORA
OFF_Q, OFF_K, OFF_V, OFF_IQ = 0, 1024, 2048, 3072
OFF_RR, OFF_RK, OFF_RV = 0, 1024, 2048
TAIL_TILE = 256
TAIL_COL0 = IN_COLS - 2 * TAIL_TILE
XW_LANE = IN_COLS - GATE_LORA - AAA_LORA - DECAY_LORA - TAIL_COL0
XA_LANE = XW_LANE + DECAY_LORA
OFF_WA, OFF_XG = 0, 256

LOG2E = 1.4426950408889634
V_ROWS = 144
KEY_TILE = 512
CHUNK = 64
VMEM_LIMIT = 56 * 1024 * 1024


def _cparams(sem):
    return pltpu.CompilerParams(dimension_semantics=sem, vmem_limit_bytes=VMEM_LIMIT)


def _dot(a, b):
    return jnp.dot(a, b, preferred_element_type=F32)


def _dot_nt(a, b):
    return lax.dot_general(a, b, (((1,), (1,)), ((), ())), preferred_element_type=F32)


def _dot_tn(a, b):
    return lax.dot_general(a, b, (((0,), (0,)), ((), ())), preferred_element_type=F32)


def _split2(x):
    hi = x.astype(BF16)
    lo = (x - hi.astype(F32)).astype(BF16)
    return hi, lo


def _dot_hi(x, w):
    hi, lo = _split2(x)
    return _dot(hi, w) + _dot(lo, w)


def _sigmoid(x):
    return 1.0 / (1.0 + jnp.exp(-x))


def _adaln_kernel(c_ref, w_ref, b_ref, o_ref):
    c = c_ref[...]
    ca = c * _sigmoid(c)
    o_ref[...] = _dot(ca.astype(BF16), w_ref[...].astype(BF16)) + b_ref[...]


def _adaln(c, w, b):
    bsz, d = c.shape
    n = w.shape[1]
    rows = 8
    cp = jnp.zeros((rows, d), F32).at[:bsz].set(c)
    tn = 1024
    out = pl.pallas_call(
        _adaln_kernel,
        grid=(n // tn,),
        in_specs=[pl.BlockSpec((rows, d), lambda j: (0, 0)),
                  pl.BlockSpec((d, tn), lambda j: (0, j)),
                  pl.BlockSpec((1, tn), lambda j: (0, j))],
        out_specs=pl.BlockSpec((rows, tn), lambda j: (0, j)),
        out_shape=jax.ShapeDtypeStruct((rows, n), F32),
        compiler_params=_cparams(("arbitrary",)),
        name="adaln",
    )(cp, w, b.reshape(1, n))
    return out[:bsz]


def _norm_mod(x, g, sc, sh):
    ms = jnp.mean(x * x, axis=-1, keepdims=True)
    y = x * lax.rsqrt(ms + NORM_EPS)
    y = y * g
    return y * (1.0 + sc) + sh


def _norm_kernel(x_ref, g_ref, sc_ref, sh_ref, o_ref):
    o_ref[...] = _norm_mod(x_ref[...], g_ref[...], sc_ref[...], sh_ref[...]).astype(BF16)


def _norm(x2, g, mod3, seq, tm=512):
    m, d = x2.shape
    per_b = seq // tm
    return pl.pallas_call(
        _norm_kernel,
        grid=(m // tm,),
        in_specs=[pl.BlockSpec((tm, d), lambda i: (i, 0)),
                  pl.BlockSpec((1, d), lambda i: (0, 0)),
                  pl.BlockSpec((None, 1, d), lambda i: ((i // per_b) * 6 + 1, 0, 0)),
                  pl.BlockSpec((None, 1, d), lambda i: ((i // per_b) * 6 + 0, 0, 0))],
        out_specs=pl.BlockSpec((tm, d), lambda i: (i, 0)),
        out_shape=jax.ShapeDtypeStruct((m, d), BF16),
        compiler_params=_cparams(("arbitrary",)),
        name="norm1",
    )(x2, g, mod3, mod3)


def _mm_nt_kernel(h_ref, wt_ref, o_ref, wb_ref):
    @pl.when(pl.program_id(1) == 0)
    def _():
        wb_ref[...] = wt_ref[...].astype(BF16)

    o_ref[...] = _dot_nt(h_ref[...], wb_ref[...])


def _matmul_nt(h, wt, col_of_tile, n_tiles, tn, name, tm=1024):
    m, d = h.shape
    return pl.pallas_call(
        _mm_nt_kernel,
        grid=(n_tiles, m // tm),
        in_specs=[pl.BlockSpec((tm, d), lambda j, i: (i, 0)),
                  pl.BlockSpec((pl.Element(tn), pl.Element(d)),
                               lambda j, i: (pl.multiple_of(col_of_tile(j), SUBLANES), 0))],
        out_specs=pl.BlockSpec((tm, tn), lambda j, i: (i, j)),
        out_shape=jax.ShapeDtypeStruct((m, n_tiles * tn), F32),
        scratch_shapes=[pltpu.VMEM((tn, d), BF16)],
        compiler_params=_cparams(("arbitrary", "arbitrary")),
        name=name,
    )(h, wt)


def _ffn_glu_kernel(h_ref, wg_ref, wu_ref, o_ref, wgb_ref, wub_ref):
    @pl.when(pl.program_id(1) == 0)
    def _():
        wgb_ref[...] = wg_ref[...].astype(BF16)
        wub_ref[...] = wu_ref[...].astype(BF16)

    h = h_ref[...]
    a = _dot(h, wgb_ref[...])
    u = _dot(h, wub_ref[...])
    o_ref[...] = (a * _sigmoid(a) * u).astype(o_ref.dtype)


def _ffn_glu(h, wg, wu, tm=2048, tn=512):
    m, d = h.shape
    tm = min(tm, m)
    n = wg.shape[1]
    return pl.pallas_call(
        _ffn_glu_kernel,
        grid=(n // tn, m // tm),
        in_specs=[pl.BlockSpec((tm, d), lambda j, i: (i, 0)),
                  pl.BlockSpec((d, tn), lambda j, i: (0, j)),
                  pl.BlockSpec((d, tn), lambda j, i: (0, j))],
        out_specs=pl.BlockSpec((tm, tn), lambda j, i: (i, j)),
        out_shape=jax.ShapeDtypeStruct((m, n), BF16),
        scratch_shapes=[pltpu.VMEM((d, tn), BF16)] * 2,
        compiler_params=_cparams(("arbitrary", "arbitrary")),
        name="ffn_glu",
    )(h, wg, wu)


def _out_proj_kernel(a_ref, r_ref, wa_ref, wr_ref, x_ref, gt_ref, g_ref, sc_ref, sh_ref,
                     x1_ref, h2_ref, wab_ref, wrb_ref):
    @pl.when(pl.program_id(0) == 0)
    def _():
        wab_ref[...] = wa_ref[...].astype(BF16)
        wrb_ref[...] = wr_ref[...].astype(BF16)

    mixed = _dot(a_ref[...], wab_ref[...]) + _dot(r_ref[...], wrb_ref[...])
    x1 = x_ref[...] + gt_ref[...] * mixed
    x1_ref[...] = x1
    h2_ref[...] = _norm_mod(x1, g_ref[...], sc_ref[...], sh_ref[...]).astype(BF16)


def _out_proj(att, rwkv, w_out, x2, mod3, g2, seq, tm=512):
    m, ka = att.shape
    kr = rwkv.shape[1]
    n = w_out.shape[1]
    per_b = seq // tm
    once = pl.Buffered(1)

    def mod_row(j):
        return pl.BlockSpec((None, 1, n), lambda i: ((i // per_b) * 6 + j, 0, 0))

    row = pl.BlockSpec((tm, n), lambda i: (i, 0))
    return pl.pallas_call(
        _out_proj_kernel,
        grid=(m // tm,),
        in_specs=[pl.BlockSpec((tm, ka), lambda i: (i, 0)),
                  pl.BlockSpec((tm, kr), lambda i: (i, 0)),
                  pl.BlockSpec((ka, n), lambda i: (0, 0), pipeline_mode=once),
                  pl.BlockSpec((kr, n), lambda i: (ka // kr, 0), pipeline_mode=once),
                  row, mod_row(2),
                  pl.BlockSpec((1, n), lambda i: (0, 0)), mod_row(4), mod_row(3)],
        out_specs=[row, row],
        out_shape=[jax.ShapeDtypeStruct((m, n), F32), jax.ShapeDtypeStruct((m, n), BF16)],
        scratch_shapes=[pltpu.VMEM((ka, n), BF16), pltpu.VMEM((kr, n), BF16)],
        compiler_params=_cparams(("arbitrary",)),
        name="out_proj",
    )(att, rwkv, w_out, w_out, x2, mod3, g2, mod3, mod3)


def _ffn_down_kernel(h_ref, w_ref, x_ref, gt_ref, o_ref, wb_ref):
    @pl.when(pl.program_id(1) == 0)
    def _():
        wb_ref[...] = w_ref[...].astype(BF16)

    o_ref[...] = x_ref[...] + gt_ref[...] * _dot(h_ref[...], wb_ref[...])


def _ffn_down(h, w, x2, mod3, seq, tm=512, tn=512):
    m, kdim = h.shape
    n = w.shape[1]
    per_b = seq // tm
    return pl.pallas_call(
        _ffn_down_kernel,
        grid=(n // tn, m // tm),
        in_specs=[pl.BlockSpec((tm, kdim), lambda j, i: (i, 0)),
                  pl.BlockSpec((kdim, tn), lambda j, i: (0, j)),
                  pl.BlockSpec((tm, tn), lambda j, i: (i, j)),
                  pl.BlockSpec((None, 1, tn), lambda j, i: ((i // per_b) * 6 + 5, 0, j))],
        out_specs=pl.BlockSpec((tm, tn), lambda j, i: (i, j)),
        out_shape=jax.ShapeDtypeStruct((m, n), F32),
        scratch_shapes=[pltpu.VMEM((kdim, tn), BF16)],
        compiler_params=_cparams(("arbitrary", "arbitrary")),
        name="ffn_down",
    )(h, w, x2, mod3)


def _proj_t_kernel(wt_ref, h_ref, pos_ref, g_ref, f_ref, *refs, mode):
    *o_refs, wb_ref = refs
    o_ref = o_refs[0]

    @pl.when(pl.program_id(0) == 0)
    def _():
        wb_ref[...] = wt_ref[...].astype(BF16)

    yt = _dot_nt(wb_ref[...], h_ref[...])
    tm = yt.shape[1]
    if mode == "v":
        pad_row = lax.broadcasted_iota(I32, (V_ROWS - LANES, KEY_TILE), 0)
        ones_rows = jnp.where(pad_row == 0, 1.0, 0.0).astype(BF16)
        for h in range(ATT_HEADS):
            for j in range(tm // KEY_TILE):
                tile = yt[h * LANES:(h + 1) * LANES, j * KEY_TILE:(j + 1) * KEY_TILE]
                o_ref[h, j, 0:LANES, :] = tile.astype(BF16)
                o_ref[h, j, LANES:V_ROWS, :] = ones_rows
        return

    head = ATT_HEAD_DIM if mode in ("q", "k") else IDX_HEAD_DIM
    half = head // ROPE_FRACTION // 2
    ang = f_ref[...] * pos_ref[...].astype(F32)
    cos, sin = jnp.cos(ang), jnp.sin(ang)

    def rotary(y):
        x1, x2 = y[0:half], y[half:2 * half]
        return jnp.concatenate([x1 * cos - x2 * sin, x2 * cos + x1 * sin, y[2 * half:]], axis=0)

    for hd in range(ATT_WIDTH // head):
        y = yt[hd * head:(hd + 1) * head, :]
        if mode in ("q", "k"):
            ms = jnp.mean(y * y, axis=0, keepdims=True)
            y = y * lax.rsqrt(ms + NORM_EPS) * g_ref[...]
        y = rotary(y)
        if mode == "q":
            o_ref[hd] = (y * ((ATT_HEAD_DIM ** -0.5) * LOG2E)).astype(BF16)
        elif mode == "iq":
            rows = slice((hd % 2) * head, (hd % 2 + 1) * head)
            o_ref[hd // 2, rows, :] = (y * (IDX_HEAD_DIM ** -0.5)).astype(BF16)
        else:
            o_ref[:, hd * LANES:(hd + 1) * LANES] = y.T.astype(BF16)
    if mode == "iq":
        rest = yt[ATT_WIDTH:, :]
        ikt = jnp.concatenate([rotary(rest[0:head]), rest[head:]], axis=0)
        o_refs[1][...] = ikt.T.astype(BF16)
        o_refs[2][...] = rest[head:head + IDX_HEADS, :] * (IDX_HEADS ** -0.5)


def _proj_t(h, wt, col0, n, pos_row, g_col, f_col, mode, tm=1024):
    m, d = h.shape
    tm = min(tm, m)
    if mode == "v":
        out_specs = [pl.BlockSpec((ATT_HEADS, tm // KEY_TILE, V_ROWS, KEY_TILE),
                                  lambda i: (0, i, 0, 0))]
        out_shape = [jax.ShapeDtypeStruct((ATT_HEADS, m // KEY_TILE, V_ROWS, KEY_TILE), BF16)]
    elif mode in ("q", "iq"):
        out_specs = [pl.BlockSpec((ATT_WIDTH // LANES, LANES, tm), lambda i: (0, 0, i))]
        out_shape = [jax.ShapeDtypeStruct((ATT_WIDTH // LANES, LANES, m), BF16)]
        if mode == "iq":
            out_specs += [pl.BlockSpec((tm, LANES), lambda i: (i, 0)),
                          pl.BlockSpec((IDX_HEADS, tm), lambda i: (0, i))]
            out_shape += [jax.ShapeDtypeStruct((m, LANES), BF16),
                          jax.ShapeDtypeStruct((IDX_HEADS, m), F32)]
    else:
        out_specs = [pl.BlockSpec((tm, n), lambda i: (i, 0))]
        out_shape = [jax.ShapeDtypeStruct((m, n), BF16)]
    out = pl.pallas_call(
        functools.partial(_proj_t_kernel, mode=mode),
        grid=(m // tm,),
        in_specs=[pl.BlockSpec((pl.Element(n), pl.Element(d)), lambda i: (col0, 0)),
                  pl.BlockSpec((tm, d), lambda i: (i, 0)),
                  pl.BlockSpec((1, tm), lambda i: (0, i)),
                  pl.BlockSpec(g_col.shape, lambda i: (0, 0)),
                  pl.BlockSpec(f_col.shape, lambda i: (0, 0))],
        out_specs=out_specs,
        out_shape=out_shape,
        scratch_shapes=[pltpu.VMEM((n, d), BF16)],
        compiler_params=_cparams(("arbitrary",)),
        name="in_proj_" + mode,
    )(wt, h, pos_row, g_col, f_col)
    return out if mode == "iq" else out[0]


NEG_BIG = -1e30


INT_MIN = -2 ** 31
INT_MAX = 2 ** 31 - 1
MAGNITUDE_BITS = 0x7FFFFFFF
KEY_NEG_INF = -2139095041
BOUND_SLACK = 1.02
MAX_STATIC_SHIFT = 60.0
COUNT_ROWS = 32


def _key_to_float(key):
    bits = key ^ ((key >> 31) & jnp.int32(MAGNITUDE_BITS))
    return lax.bitcast_convert_type(bits, F32)


def _dsa_kernel(bound_ref, qt_ref, iqt_ref, iwt_ref, k_ref, vt_ref, ik_ref, o_ref,
                sc_ref, m_ref, acc_ref, last_ref, *, tq, tk, topk, index_bits):
    qi = pl.program_id(1)
    n_kb = (qi * tq + tq - 1) // tk + 1
    key0 = lax.broadcasted_iota(I32, (tk, tq), 0)
    qidx = qi * tq + lax.broadcasted_iota(I32, (tk, tq), 1)
    iw = iwt_ref[...]

    def score_body(kb, carry):
        start = pl.multiple_of(kb * tk, tk)
        ikb = ik_ref[pl.ds(start, tk), 0:IDX_HEAD_DIM]
        s = jnp.zeros((tk, tq), F32)
        for h in range(IDX_HEADS):
            off = (h % 2) * IDX_HEAD_DIM
            d = _dot(ikb, iqt_ref[h // 2, off:off + IDX_HEAD_DIM, :])
            s = s + jnp.maximum(d, 0.0) * iw[h:h + 1, :]
        sc_ref[kb] = jnp.where(kb * tk + key0 <= qidx, s, -jnp.inf)
        return carry

    lax.fori_loop(0, n_kb, score_body, 0)

    def count(pred):
        def cnt_body(kb, acc):
            hit = pred(kb).reshape(tk // COUNT_ROWS, COUNT_ROWS, tq)
            for r in range(tk // COUNT_ROWS):
                acc = jnp.where(hit[r], acc + 1.0, acc)
            return acc

        acc = lax.fori_loop(0, n_kb, cnt_body, jnp.zeros((COUNT_ROWS, tq), F32))
        return jnp.sum(acc, axis=0, keepdims=True)

    def bit_body(i, cand):
        trial = cand ^ lax.shift_left(jnp.int32(1), 31 - i)
        trial_f = _key_to_float(trial)
        cnt = count(lambda kb: sc_ref[kb] >= trial_f)
        return jnp.where(cnt >= topk, trial, cand)

    cand = lax.fori_loop(0, 32, bit_body, jnp.full((1, tq), INT_MIN, I32))
    tau = _key_to_float(jnp.maximum(cand, jnp.int32(KEY_NEG_INF)))

    last_ref[...] = jnp.full_like(last_ref, INT_MAX)
    n_ge = count(lambda kb: sc_ref[kb] >= tau)

    @pl.when(jnp.max(n_ge) > topk)
    def _():
        need = topk - count(lambda kb: sc_ref[kb] > tau)

        def idx_body(i, last):
            trial = last | lax.shift_left(jnp.int32(1), index_bits - 1 - i)
            below = count(lambda kb: (sc_ref[kb] == tau) & (kb * tk + key0 < trial))
            return jnp.where(below < need, trial, last)

        last_ref[...] = lax.fori_loop(0, index_bits, idx_body, jnp.zeros((1, tq), I32))

    acc_ref[...] = jnp.zeros_like(acc_ref)
    bound = bound_ref[0]

    def logits(kb, h):
        start = pl.multiple_of(kb * tk, tk)
        return _dot(k_ref[pl.ds(start, tk), h * LANES:(h + 1) * LANES], qt_ref[h])

    last = last_ref[...]

    def selected(kb):
        s = sc_ref[kb]
        kidx = kb * tk + key0
        return ((s > tau) | ((s == tau) & (kidx <= last))) & (kidx <= qidx)

    @pl.when(bound <= MAX_STATIC_SHIFT)
    def _():
        def att_body(kb, carry):
            bias = jnp.where(selected(kb), -bound, -jnp.inf)
            for h in range(ATT_HEADS):
                p = jnp.exp2(logits(kb, h) + bias)
                acc_ref[h] += _dot(vt_ref[h, kb], p.astype(BF16))
            return carry

        lax.fori_loop(0, n_kb, att_body, 0)

    @pl.when(bound > MAX_STATIC_SHIFT)
    def _():
        m_ref[...] = jnp.full_like(m_ref, NEG_BIG)

        def att_body(kb, carry):
            bias = jnp.where(selected(kb), 0.0, -jnp.inf)
            for h in range(ATT_HEADS):
                s = logits(kb, h) + bias
                m_prev = m_ref[h]
                m_next = jnp.maximum(m_prev, jnp.max(s, axis=0, keepdims=True))
                p = jnp.exp2(s - m_next)
                alpha = jnp.exp2(m_prev - m_next)
                acc_ref[h] = alpha * acc_ref[h] + _dot(vt_ref[h, kb], p.astype(BF16))
                m_ref[h] = m_next
            return carry

        lax.fori_loop(0, n_kb, att_body, 0)

    for h in range(ATT_HEADS):
        out = acc_ref[h, 0:LANES, :] / acc_ref[h, LANES:LANES + 1, :]
        o_ref[:, h * LANES:(h + 1) * LANES] = out.T.astype(o_ref.dtype)


def _dsa(bound, qt, iqt, iwt, k, vt, ik, bsz, seq, tk, tq=512):
    tq = min(tq, seq)
    topk = min(TOPK_MAX, seq // 4)
    w = ATT_WIDTH
    nq = seq // tq
    nkb = seq // tk
    npair = iqt.shape[0]
    kern = functools.partial(_dsa_kernel, tq=tq, tk=tk, topk=topk,
                             index_bits=max(1, (seq - 1).bit_length()))
    return pl.pallas_call(
        kern,
        grid=(bsz, nq),
        in_specs=[pl.BlockSpec(memory_space=pltpu.SMEM),
                  pl.BlockSpec((ATT_HEADS, LANES, tq), lambda b, i: (0, 0, b * nq + i)),
                  pl.BlockSpec((npair, LANES, tq), lambda b, i: (0, 0, b * nq + i)),
                  pl.BlockSpec((IDX_HEADS, tq), lambda b, i: (0, b * nq + i)),
                  pl.BlockSpec((seq, w), lambda b, i: (b, 0)),
                  pl.BlockSpec((ATT_HEADS, nkb, V_ROWS, tk), lambda b, i: (0, b, 0, 0)),
                  pl.BlockSpec((seq, LANES), lambda b, i: (b, 0))],
        out_specs=pl.BlockSpec((tq, w), lambda b, i: (b * nq + i, 0)),
        out_shape=jax.ShapeDtypeStruct((bsz * seq, w), BF16),
        scratch_shapes=[pltpu.VMEM((nkb, tk, tq), F32),
                        pltpu.VMEM((ATT_HEADS, 1, tq), F32),
                        pltpu.VMEM((ATT_HEADS, V_ROWS, tq), F32),
                        pltpu.VMEM((1, tq), I32)],
        compiler_params=_cparams(("arbitrary", "arbitrary")),
        name="dsa",
    )(bound, qt, iqt, iwt, k, vt, ik)


def _rwkv_prep_kernel(rr_ref, rk_ref, rv_ref, wa_ref, xg_ref,
                      mr_ref, mk_ref, mv_ref, mwa_ref, mg_ref,
                      w0_ref, a0_ref, kk_ref, ka_ref, rkp_ref,
                      wup_ref, aup_ref, gup_ref, e_ref, et_ref,
                      r_o, lw_o, k_o, v_o, kkn_o, bb_o, g_o, bon_o,
                      c_r, c_k, c_v, c_wa, c_g, *, tm):
    first = pl.program_id(1) == 0

    def shift(y_ref, carry_ref, mu_ref):
        y = y_ref[...]

        @pl.when(first)
        def _():
            carry_ref[...] = jnp.zeros_like(carry_ref)

        prev_last = carry_ref[7:8, :]
        rolled = pltpu.roll(y, 1, 0)
        rows = lax.broadcasted_iota(I32, y.shape, 0)
        yprev = jnp.where(rows == 0, prev_last, rolled)
        carry_ref[...] = y[tm - 8:tm, :]
        return y + (yprev - y) * mu_ref[...]

    r = shift(rr_ref, c_r, mr_ref)
    k = shift(rk_ref, c_k, mk_ref)
    v = shift(rv_ref, c_v, mv_ref)
    wa = shift(wa_ref, c_wa, mwa_ref)
    xw = pltpu.roll(wa, TAIL_TILE - XW_LANE, 1)[:, :LORA_PAD]
    xa = pltpu.roll(wa, TAIL_TILE - XA_LANE, 1)[:, :LORA_PAD]
    xg = shift(xg_ref, c_g, mg_ref)

    w_raw = w0_ref[...] + _dot(jnp.tanh(xw).astype(BF16), wup_ref[...])
    z = -w_raw
    softplus = jnp.maximum(z, 0.0) + jnp.log(1.0 + jnp.exp(-jnp.abs(z)))
    lw_o[...] = -jnp.exp(-softplus - 0.5)
    a = _sigmoid(a0_ref[...] + _dot(xa.astype(BF16), aup_ref[...]))
    g_o[...] = _dot(_sigmoid(xg).astype(BF16), gup_ref[...]).astype(BF16)

    def head_sum(x):
        return _dot_hi(_dot_hi(x, e_ref[...]), et_ref[...])

    kk = k * kk_ref[...]
    ss = head_sum(kk * kk)
    kk = kk / jnp.maximum(jnp.sqrt(ss), 1e-12)
    kmod = k * (1.0 + (a - 1.0) * ka_ref[...])
    r_o[...] = r.astype(BF16)
    k_o[...] = kmod.astype(BF16)
    v_o[...] = v.astype(BF16)
    kkn_o[...] = kk.astype(BF16)
    bb_o[...] = (kk * a).astype(BF16)
    bon_o[...] = (head_sum(r * kmod * rkp_ref[...]) * v).astype(BF16)


def _rwkv_prep(proj_r, proj_t, mus, vecs, wup, aup, gup, e, bsz, seq, tm=256):
    w = RWKV_WIDTH
    per_b = seq // tm

    def wide(off):
        return pl.BlockSpec((tm, w), lambda b, i: (b * per_b + i, off // w))

    def tail(off):
        return pl.BlockSpec((tm, TAIL_TILE), lambda b, i: (b * per_b + i, off // TAIL_TILE))

    def const(shape):
        return pl.BlockSpec(shape, lambda b, i: (0, 0))

    out_blk = pl.BlockSpec((tm, w), lambda b, i: (b * per_b + i, 0))
    kern = functools.partial(_rwkv_prep_kernel, tm=tm)
    return pl.pallas_call(
        kern,
        grid=(bsz, per_b),
        in_specs=[wide(OFF_RR), wide(OFF_RK), wide(OFF_RV), tail(OFF_WA), tail(OFF_XG),
                  const((1, w)), const((1, w)), const((1, w)),
                  const((1, TAIL_TILE)), const((1, GATE_LORA)),
                  const((1, w)), const((1, w)), const((1, w)), const((1, w)), const((1, w)),
                  const((LORA_PAD, w)), const((LORA_PAD, w)), const((GATE_LORA, w)),
                  const((w, LANES)), const((LANES, w))],
        out_specs=[out_blk] * 8,
        out_shape=[jax.ShapeDtypeStruct((bsz * seq, w), F32 if i == 1 else BF16) for i in range(8)],
        scratch_shapes=[pltpu.VMEM((8, w), F32)] * 3
        + [pltpu.VMEM((8, TAIL_TILE), F32), pltpu.VMEM((8, GATE_LORA), F32)],
        compiler_params=_cparams(("arbitrary", "arbitrary")),
        name="rwkv_prep",
    )(proj_r, proj_r, proj_r, proj_t, proj_t, *mus, *vecs, wup, aup, gup, e, e.T)


def _rwkv_core_kernel(r_ref, lw_ref, k_ref, v_ref, kk_ref, bb_ref, g_ref, bon_ref, lg_ref, lb_ref,
                      o_ref, z_ref):
    c = CHUNK

    @pl.when(pl.program_id(1) == 0)
    def _():
        z_ref[...] = jnp.zeros_like(z_ref)

    lw = lw_ref[...]
    tri = jnp.where(lax.broadcasted_iota(I32, (c, c), 1) <= lax.broadcasted_iota(I32, (c, c), 0),
                    1.0, 0.0).astype(BF16)
    hi = lw.astype(BF16)
    rem = lw - hi.astype(F32)
    mid = rem.astype(BF16)
    lo = (rem - mid.astype(F32)).astype(BF16)
    cum = _dot(tri, hi) + _dot(tri, mid) + _dot(tri, lo)
    p_in = jnp.exp(cum)
    p_ex = jnp.exp(cum - lw)
    p_inv = jnp.exp(-cum)
    p_end = p_in[c - 1:c, :]
    a_t = -kk_ref[...].astype(F32) * p_ex
    r_t = r_ref[...].astype(F32) * p_in
    b_h = bb_ref[...].astype(F32) * p_inv
    k_h = k_ref[...].astype(F32) * p_inv
    b_e = b_h * p_end
    k_e = k_h * p_end
    v = v_ref[...].astype(F32)

    n2 = 2 * c
    lane = lax.broadcasted_iota(I32, (1, LANES), 1)
    head0 = lane < RWKV_HEAD_DIM
    ri = lax.broadcasted_iota(I32, (n2, n2), 0)
    ci = lax.broadcasted_iota(I32, (n2, n2), 1)
    same = (ri >= c) == (ci >= c)
    strict = same & (ci < ri)
    incl = same & (ci <= ri)
    eye = ri == ci
    own = ((lax.broadcasted_iota(I32, (n2, LANES), 0) >= c)
           == (lax.broadcasted_iota(I32, (n2, LANES), 1) >= RWKV_HEAD_DIM))
    inv_n = 1.0 / RWKV_HEAD_DIM

    def stack(y):
        return jnp.concatenate([jnp.where(head0, y, 0.0), jnp.where(head0, 0.0, y)], axis=0)

    pairs = range(RWKV_WIDTH // LANES)
    sls = [slice(p * LANES, (p + 1) * LANES) for p in pairs]
    a_s = [stack(a_t[:, sl]) for sl in sls]
    r_s = [stack(r_t[:, sl]) for sl in sls]
    v_s = [stack(v[:, sl]).astype(BF16) for sl in sls]
    g1 = [_dot_nt(jnp.concatenate([a_s[p], r_s[p]], axis=0).astype(BF16),
                  jnp.concatenate([stack(b_h[:, sls[p]]), stack(k_h[:, sls[p]])], axis=0).astype(BF16))
          for p in pairs]
    pw = [jnp.where(strict, g[:n2, :n2], 0.0).astype(BF16) for g in g1]
    a_rb = [jnp.where(incl, g[n2:, :n2], 0.0).astype(BF16) for g in g1]
    a_rk = [jnp.where(incl, g[n2:, n2:], 0.0).astype(BF16) for g in g1]
    akv = [_dot(jnp.where(strict, g1[p][:n2, n2:], 0.0).astype(BF16), v_s[p]) for p in pairs]
    xc = [jnp.concatenate([a_s[p], akv[p]], axis=1) for p in pairs]
    steps = int(np.log2(c))
    for i in range(steps):
        if i + 1 < steps:
            res = [_dot(pw[p], jnp.concatenate([pw[p], xc[p].astype(BF16)], axis=1)) for p in pairs]
            xc = [xc[p] + res[p][:, n2:] for p in pairs]
            pw = [res[p][:, :n2].astype(BF16) for p in pairs]
        else:
            xc = [xc[p] + _dot(pw[p], xc[p].astype(BF16)) for p in pairs]
    xcb = [x.astype(BF16) for x in xc]
    r2 = [_dot(a_rb[p], xcb[p]) for p in pairs]
    ov = [r2[p][:, LANES:] + _dot(a_rk[p], v_s[p]) for p in pairs]
    mg = [_dot_tn(stack(b_e[:, sls[p]]).astype(BF16), xcb[p]) for p in pairs]
    kv = [_dot_tn(stack(k_e[:, sls[p]]).astype(BF16), v_s[p]) for p in pairs]
    for p in pairs:
        q_s = r_s[p] + r2[p][:, :LANES]
        mmat = mg[p][:, :LANES] + jnp.where(eye, p_end[:, sls[p]], 0.0)
        qm = jnp.concatenate([q_s, mmat], axis=0).astype(BF16)
        res = _dot(qm, z_ref[p].astype(BF16))
        z_ref[p] = res[n2:] + mg[p][:, LANES:] + kv[p]
        o_s = res[:n2] + ov[p]
        mean = jnp.sum(o_s, axis=1, keepdims=True) * inv_n
        dev = jnp.where(own, o_s - mean, 0.0)
        var = jnp.sum(dev * dev, axis=1, keepdims=True) * inv_n
        y = dev * lax.rsqrt(var + LNX_EPS)
        y = (y[:c] + y[c:]) * lg_ref[:, sls[p]] + lb_ref[:, sls[p]]
        out = (y + bon_ref[:, sls[p]].astype(F32)) * g_ref[:, sls[p]].astype(F32)
        o_ref[:, sls[p]] = out.astype(o_ref.dtype)


def _rwkv_core(r, lw, k, v, kk, bb, g, bon, lg, lb, bsz, seq):
    c = CHUNK
    w = RWKV_WIDTH
    per_b = seq // c
    blk = pl.BlockSpec((c, w), lambda b, i: (b * per_b + i, 0))
    vec = pl.BlockSpec((1, w), lambda b, i: (0, 0))
    return pl.pallas_call(
        _rwkv_core_kernel,
        grid=(bsz, per_b),
        in_specs=[blk] * 8 + [vec, vec],
        out_specs=blk,
        out_shape=jax.ShapeDtypeStruct((bsz * seq, w), BF16),
        scratch_shapes=[pltpu.VMEM((w // LANES, LANES, LANES), F32)],
        compiler_params=_cparams(("arbitrary", "arbitrary")),
        name="rwkv_core",
    )(r, lw, k, v, kk, bb, g, bon, lg, lb)


def _rope_freqs(head_dim):
    half = head_dim // ROPE_FRACTION // 2
    return (ROPE_THETA ** (-jnp.arange(half, dtype=F32) / half)).reshape(half, 1)


def kernel(x, c, positions, w_ada, b_ada, norm1_g, w_in, q_norm_g, k_norm_g, rwkv_mu, rwkv_w0,
           rwkv_w_up, rwkv_a0, rwkv_a_up, rwkv_g_up, rwkv_k_k, rwkv_k_a, rwkv_r_k, rwkv_lnx_g,
           rwkv_lnx_b, w_out, norm2_g, w_ffn_gate, w_ffn_up, w_ffn_down):
    bsz, seq, d = x.shape
    depth = w_ada.shape[0]
    m = bsz * seq
    pos_row = positions.reshape(1, m)
    fa_col = _rope_freqs(ATT_HEAD_DIM)
    fi_col = _rope_freqs(IDX_HEAD_DIM)
    hd = RWKV_HEAD_DIM
    e = (jnp.arange(RWKV_WIDTH)[:, None] // hd == jnp.arange(LANES)[None, :]).astype(BF16)
    x2 = x.reshape(m, d)

    for l in range(depth):
        mod = _adaln(c, w_ada[l], b_ada[l])
        mod3 = mod.reshape(bsz * 6, 1, d)

        h1 = _norm(x2, norm1_g[l].reshape(1, d), mod3, seq)
        w_in_t = w_in[l].T
        proj_r = _matmul_nt(h1, w_in_t, lambda j: ATT_COLS + j * RWKV_WIDTH, 3, RWKV_WIDTH,
                            "in_proj_rkv")
        proj_t = _matmul_nt(h1, w_in_t, lambda j: TAIL_COL0 + j * TAIL_TILE, 2, TAIL_TILE,
                            "in_proj_tail", tm=min(2048, m))

        none = jnp.zeros((SUBLANES, 1), F32)
        aw = ATT_WIDTH
        qt = _proj_t(h1, w_in_t, OFF_Q, aw, pos_row, q_norm_g[l].reshape(-1, 1), fa_col, "q")
        kn = _proj_t(h1, w_in_t, OFF_K, aw, pos_row, k_norm_g[l].reshape(-1, 1), fa_col, "k")
        vt = _proj_t(h1, w_in_t, OFF_V, aw, pos_row, none, none, "v")
        iqt, ik, iwt = _proj_t(h1, w_in_t, OFF_IQ, aw + LANES, pos_row, none, fi_col, "iq")
        bound = (ATT_HEAD_DIM ** 0.5 * LOG2E * BOUND_SLACK
                 * jnp.max(jnp.abs(q_norm_g[l])) * jnp.max(jnp.abs(k_norm_g[l])))
        att = _dsa(bound.reshape(1).astype(F32), qt, iqt, iwt, kn, vt, ik, bsz, seq, KEY_TILE)

        mu = rwkv_mu[l]
        w3 = 3 * RWKV_WIDTH

        n_wa = DECAY_LORA + AAA_LORA
        mu_wa = jnp.zeros((1, TAIL_TILE), F32).at[0, XW_LANE:XW_LANE + n_wa].set(mu[w3:w3 + n_wa])
        mus = [mu[0:RWKV_WIDTH].reshape(1, -1), mu[RWKV_WIDTH:2 * RWKV_WIDTH].reshape(1, -1),
               mu[2 * RWKV_WIDTH:w3].reshape(1, -1), mu_wa, mu[w3 + n_wa:].reshape(1, -1)]
        vecs = [rwkv_w0[l].reshape(1, -1), rwkv_a0[l].reshape(1, -1), rwkv_k_k[l].reshape(1, -1),
                rwkv_k_a[l].reshape(1, -1), rwkv_r_k[l].reshape(1, -1)]

        def pad_rows(wm):
            return jnp.zeros((LORA_PAD, wm.shape[1]), F32).at[:wm.shape[0]].set(wm).astype(BF16)

        r, lw, km, vv, kk, bb, g, bon = _rwkv_prep(
            proj_r, proj_t, mus, vecs, pad_rows(rwkv_w_up[l]), pad_rows(rwkv_a_up[l]),
            rwkv_g_up[l].astype(BF16), e, bsz, seq)
        rw = _rwkv_core(r, lw, km, vv, kk, bb, g, bon, rwkv_lnx_g[l].reshape(1, -1),
                        rwkv_lnx_b[l].reshape(1, -1), bsz, seq)

        x2, h2 = _out_proj(att, rw, w_out[l], x2, mod3, norm2_g[l].reshape(1, d), seq)

        hglu = _ffn_glu(h2, w_ffn_gate[l], w_ffn_up[l])
        x2 = _ffn_down(hglu, w_ffn_down[l], x2, mod3, seq)
    return x2.reshape(bsz, seq, d)
```

```python
import functools

import jax
import jax.numpy as jnp
import numpy as np
from jax import lax
from jax.experimental import pallas as pl
from jax.experimental.pallas import tpu as pltpu

F32 = jnp.float32
BF16 = jnp.bfloat16
I32 = jnp.int32

D_MODEL = 2048
ATT_HEADS = 8
ATT_HEAD_DIM = 128
ATT_WIDTH = ATT_HEADS * ATT_HEAD_DIM
RWKV_WIDTH = D_MODEL - ATT_WIDTH
RWKV_HEAD_DIM = 64
RWKV_HEADS = RWKV_WIDTH // RWKV_HEAD_DIM
IDX_HEADS = 16
IDX_HEAD_DIM = 64
TOPK_MAX = 256
ROPE_THETA = 500000.0
ROPE_FRACTION = 4
DECAY_LORA = 96
AAA_LORA = 96
GATE_LORA = 256
NORM_EPS = 1e-6
LNX_EPS = 64e-5

LANES = 128
SUBLANES = 8
LORA_PAD = 128
ATT_COLS = 4 * ATT_WIDTH + IDX_HEAD_DIM + IDX_HEADS
IN_COLS = ATT_COLS + 3 * RWKV_WIDTH + DECAY_LORA + AAA_LORA + GATE_LORA
OFF_Q, OFF_K, OFF_V, OFF_IQ = 0, 1024, 2048, 3072
OFF_RR, OFF_RK, OFF_RV = 0, 1024, 2048
TAIL_TILE = 256
TAIL_COL0 = IN_COLS - 2 * TAIL_TILE
XW_LANE = IN_COLS - GATE_LORA - AAA_LORA - DECAY_LORA - TAIL_COL0
XA_LANE = XW_LANE + DECAY_LORA
OFF_WA, OFF_XG = 0, 256

LOG2E = 1.4426950408889634
V_ROWS = 144
KEY_TILE = 512
CHUNKS_PER_STEP = 4
CHUNK = 64
VMEM_LIMIT = 56 * 1024 * 1024


def _cparams(sem):
    return pltpu.CompilerParams(dimension_semantics=sem, vmem_limit_bytes=VMEM_LIMIT)


def _dot(a, b):
    return jnp.dot(a, b, preferred_element_type=F32)


def _dot_nt(a, b):
    return lax.dot_general(a, b, (((1,), (1,)), ((), ())), preferred_element_type=F32)


def _dot_tn(a, b):
    return lax.dot_general(a, b, (((0,), (0,)), ((), ())), preferred_element_type=F32)


def _split2(x):
    hi = x.astype(BF16)
    lo = (x - hi.astype(F32)).astype(BF16)
    return hi, lo


def _dot_hi(x, w):
    hi, lo = _split2(x)
    return _dot(hi, w) + _dot(lo, w)


def _sigmoid(x):
    return 1.0 / (1.0 + jnp.exp(-x))


def _adaln_kernel(c_ref, w_ref, b_ref, o_ref):
    c = c_ref[...]
    ca = c * _sigmoid(c)
    o_ref[...] = _dot(ca.astype(BF16), w_ref[...].astype(BF16)) + b_ref[...]


def _adaln(c, w, b):
    bsz, d = c.shape
    n = w.shape[1]
    rows = 8
    cp = jnp.zeros((rows, d), F32).at[:bsz].set(c)
    tn = 1024
    out = pl.pallas_call(
        _adaln_kernel,
        grid=(n // tn,),
        in_specs=[pl.BlockSpec((rows, d), lambda j: (0, 0)),
                  pl.BlockSpec((d, tn), lambda j: (0, j)),
                  pl.BlockSpec((1, tn), lambda j: (0, j))],
        out_specs=pl.BlockSpec((rows, tn), lambda j: (0, j)),
        out_shape=jax.ShapeDtypeStruct((rows, n), F32),
        compiler_params=_cparams(("arbitrary",)),
        name="adaln",
    )(cp, w, b.reshape(1, n))
    return out[:bsz]


def _norm_mod(x, g, sc, sh):
    ms = jnp.mean(x * x, axis=-1, keepdims=True)
    y = x * lax.rsqrt(ms + NORM_EPS)
    y = y * g
    return y * (1.0 + sc) + sh


def _norm_kernel(x_ref, g_ref, sc_ref, sh_ref, o_ref):
    o_ref[...] = _norm_mod(x_ref[...], g_ref[...], sc_ref[...], sh_ref[...]).astype(BF16)


def _norm(x2, g, mod3, seq, tm=512):
    m, d = x2.shape
    per_b = seq // tm
    return pl.pallas_call(
        _norm_kernel,
        grid=(m // tm,),
        in_specs=[pl.BlockSpec((tm, d), lambda i: (i, 0)),
                  pl.BlockSpec((1, d), lambda i: (0, 0)),
                  pl.BlockSpec((None, 1, d), lambda i: ((i // per_b) * 6 + 1, 0, 0)),
                  pl.BlockSpec((None, 1, d), lambda i: ((i // per_b) * 6 + 0, 0, 0))],
        out_specs=pl.BlockSpec((tm, d), lambda i: (i, 0)),
        out_shape=jax.ShapeDtypeStruct((m, d), BF16),
        compiler_params=_cparams(("arbitrary",)),
        name="norm1",
    )(x2, g, mod3, mod3)


def _mm_nt_kernel(h_ref, wt_ref, o_ref, wb_ref):
    @pl.when(pl.program_id(1) == 0)
    def _():
        wb_ref[...] = wt_ref[...].astype(BF16)

    o_ref[...] = _dot_nt(h_ref[...], wb_ref[...])


def _matmul_nt(h, wt, col_of_tile, n_tiles, tn, name, tm=1024):
    m, d = h.shape
    return pl.pallas_call(
        _mm_nt_kernel,
        grid=(n_tiles, m // tm),
        in_specs=[pl.BlockSpec((tm, d), lambda j, i: (i, 0)),
                  pl.BlockSpec((pl.Element(tn), pl.Element(d)),
                               lambda j, i: (pl.multiple_of(col_of_tile(j), SUBLANES), 0))],
        out_specs=pl.BlockSpec((tm, tn), lambda j, i: (i, j)),
        out_shape=jax.ShapeDtypeStruct((m, n_tiles * tn), F32),
        scratch_shapes=[pltpu.VMEM((tn, d), BF16)],
        compiler_params=_cparams(("arbitrary", "arbitrary")),
        name=name,
    )(h, wt)


def _ffn_glu_kernel(h_ref, wg_ref, wu_ref, o_ref, wgb_ref, wub_ref):
    @pl.when(pl.program_id(1) == 0)
    def _():
        wgb_ref[...] = wg_ref[...].astype(BF16)
        wub_ref[...] = wu_ref[...].astype(BF16)

    h = h_ref[...]
    a = _dot(h, wgb_ref[...])
    u = _dot(h, wub_ref[...])
    o_ref[...] = (a * _sigmoid(a) * u).astype(o_ref.dtype)


def _ffn_glu(h, wg, wu, tm=1024, tn=512):
    m, d = h.shape
    tm = min(tm, m)
    n = wg.shape[1]
    return pl.pallas_call(
        _ffn_glu_kernel,
        grid=(n // tn, m // tm),
        in_specs=[pl.BlockSpec((tm, d), lambda j, i: (i, 0)),
                  pl.BlockSpec((d, tn), lambda j, i: (0, j)),
                  pl.BlockSpec((d, tn), lambda j, i: (0, j))],
        out_specs=pl.BlockSpec((tm, tn), lambda j, i: (i, j)),
        out_shape=jax.ShapeDtypeStruct((m, n), BF16),
        scratch_shapes=[pltpu.VMEM((d, tn), BF16)] * 2,
        compiler_params=_cparams(("arbitrary", "arbitrary")),
        name="ffn_glu",
    )(h, wg, wu)


def _out_proj_kernel(a_ref, r_ref, wa_ref, wr_ref, x_ref, gt_ref, g_ref, sc_ref, sh_ref,
                     x1_ref, h2_ref, wab_ref, wrb_ref):
    @pl.when(pl.program_id(0) == 0)
    def _():
        wab_ref[...] = wa_ref[...].astype(BF16)
        wrb_ref[...] = wr_ref[...].astype(BF16)

    mixed = _dot(a_ref[...], wab_ref[...]) + _dot(r_ref[...], wrb_ref[...])
    x1 = x_ref[...] + gt_ref[...] * mixed
    x1_ref[...] = x1
    h2_ref[...] = _norm_mod(x1, g_ref[...], sc_ref[...], sh_ref[...]).astype(BF16)


def _out_proj(att, rwkv, w_out, x2, mod3, g2, seq, tm=512):
    m, ka = att.shape
    kr = rwkv.shape[1]
    n = w_out.shape[1]
    per_b = seq // tm
    once = pl.Buffered(1)

    def mod_row(j):
        return pl.BlockSpec((None, 1, n), lambda i: ((i // per_b) * 6 + j, 0, 0))

    row = pl.BlockSpec((tm, n), lambda i: (i, 0))
    return pl.pallas_call(
        _out_proj_kernel,
        grid=(m // tm,),
        in_specs=[pl.BlockSpec((tm, ka), lambda i: (i, 0)),
                  pl.BlockSpec((tm, kr), lambda i: (i, 0)),
                  pl.BlockSpec((ka, n), lambda i: (0, 0), pipeline_mode=once),
                  pl.BlockSpec((kr, n), lambda i: (ka // kr, 0), pipeline_mode=once),
                  row, mod_row(2),
                  pl.BlockSpec((1, n), lambda i: (0, 0)), mod_row(4), mod_row(3)],
        out_specs=[row, row],
        out_shape=[jax.ShapeDtypeStruct((m, n), F32), jax.ShapeDtypeStruct((m, n), BF16)],
        scratch_shapes=[pltpu.VMEM((ka, n), BF16), pltpu.VMEM((kr, n), BF16)],
        compiler_params=_cparams(("arbitrary",)),
        name="out_proj",
    )(att, rwkv, w_out, w_out, x2, mod3, g2, mod3, mod3)


def _ffn_down_kernel(h_ref, w_ref, x_ref, gt_ref, o_ref, wb_ref):
    @pl.when(pl.program_id(1) == 0)
    def _():
        wb_ref[...] = w_ref[...].astype(BF16)

    o_ref[...] = x_ref[...] + gt_ref[...] * _dot(h_ref[...], wb_ref[...])


def _ffn_down(h, w, x2, mod3, seq, tm=512, tn=512):
    m, kdim = h.shape
    n = w.shape[1]
    per_b = seq // tm
    return pl.pallas_call(
        _ffn_down_kernel,
        grid=(n // tn, m // tm),
        in_specs=[pl.BlockSpec((tm, kdim), lambda j, i: (i, 0)),
                  pl.BlockSpec((kdim, tn), lambda j, i: (0, j)),
                  pl.BlockSpec((tm, tn), lambda j, i: (i, j)),
                  pl.BlockSpec((None, 1, tn), lambda j, i: ((i // per_b) * 6 + 5, 0, j))],
        out_specs=pl.BlockSpec((tm, tn), lambda j, i: (i, j)),
        out_shape=jax.ShapeDtypeStruct((m, n), F32),
        scratch_shapes=[pltpu.VMEM((kdim, tn), BF16)],
        compiler_params=_cparams(("arbitrary", "arbitrary")),
        name="ffn_down",
    )(h, w, x2, mod3)


def _proj_t_kernel(wt_ref, h_ref, pos_ref, g_ref, f_ref, *refs, mode):
    *o_refs, wb_ref = refs
    o_ref = o_refs[0]

    @pl.when(pl.program_id(0) == 0)
    def _():
        wb_ref[...] = wt_ref[...].astype(BF16)

    yt = _dot_nt(wb_ref[...], h_ref[...])
    tm = yt.shape[1]
    if mode == "v":
        pad_row = lax.broadcasted_iota(I32, (V_ROWS - LANES, KEY_TILE), 0)
        ones_rows = jnp.where(pad_row == 0, 1.0, 0.0).astype(BF16)
        for h in range(ATT_HEADS):
            for j in range(tm // KEY_TILE):
                tile = yt[h * LANES:(h + 1) * LANES, j * KEY_TILE:(j + 1) * KEY_TILE]
                o_ref[h, j, 0:LANES, :] = tile.astype(BF16)
                o_ref[h, j, LANES:V_ROWS, :] = ones_rows
        return

    head = ATT_HEAD_DIM if mode in ("q", "k") else IDX_HEAD_DIM
    half = head // ROPE_FRACTION // 2
    ang = f_ref[...] * pos_ref[...].astype(F32)
    cos, sin = jnp.cos(ang), jnp.sin(ang)

    def rotary(y):
        x1, x2 = y[0:half], y[half:2 * half]
        return jnp.concatenate([x1 * cos - x2 * sin, x2 * cos + x1 * sin, y[2 * half:]], axis=0)

    for hd in range(ATT_WIDTH // head):
        y = yt[hd * head:(hd + 1) * head, :]
        if mode in ("q", "k"):
            ms = jnp.mean(y * y, axis=0, keepdims=True)
            y = y * lax.rsqrt(ms + NORM_EPS) * g_ref[...]
        y = rotary(y)
        if mode == "q":
            o_ref[hd] = (y * ((ATT_HEAD_DIM ** -0.5) * LOG2E)).astype(BF16)
        elif mode == "iq":
            rows = slice((hd % 2) * head, (hd % 2 + 1) * head)
            o_ref[hd // 2, rows, :] = (y * (IDX_HEAD_DIM ** -0.5)).astype(BF16)
        else:
            o_ref[:, hd * LANES:(hd + 1) * LANES] = y.T.astype(BF16)
    if mode == "iq":
        rest = yt[ATT_WIDTH:, :]
        ikt = jnp.concatenate([rotary(rest[0:head]), rest[head:]], axis=0)
        o_refs[1][...] = ikt.T.astype(BF16)
        o_refs[2][...] = rest[head:head + IDX_HEADS, :] * (IDX_HEADS ** -0.5)


def _proj_t(h, wt, col0, n, pos_row, g_col, f_col, mode, tm=1024):
    m, d = h.shape
    tm = min(tm, m)
    if mode == "v":
        out_specs = [pl.BlockSpec((ATT_HEADS, tm // KEY_TILE, V_ROWS, KEY_TILE),
                                  lambda i: (0, i, 0, 0))]
        out_shape = [jax.ShapeDtypeStruct((ATT_HEADS, m // KEY_TILE, V_ROWS, KEY_TILE), BF16)]
    elif mode in ("q", "iq"):
        out_specs = [pl.BlockSpec((ATT_WIDTH // LANES, LANES, tm), lambda i: (0, 0, i))]
        out_shape = [jax.ShapeDtypeStruct((ATT_WIDTH // LANES, LANES, m), BF16)]
        if mode == "iq":
            out_specs += [pl.BlockSpec((tm, LANES), lambda i: (i, 0)),
                          pl.BlockSpec((IDX_HEADS, tm), lambda i: (0, i))]
            out_shape += [jax.ShapeDtypeStruct((m, LANES), BF16),
                          jax.ShapeDtypeStruct((IDX_HEADS, m), F32)]
    else:
        out_specs = [pl.BlockSpec((tm, n), lambda i: (i, 0))]
        out_shape = [jax.ShapeDtypeStruct((m, n), BF16)]
    out = pl.pallas_call(
        functools.partial(_proj_t_kernel, mode=mode),
        grid=(m // tm,),
        in_specs=[pl.BlockSpec((pl.Element(n), pl.Element(d)), lambda i: (col0, 0)),
                  pl.BlockSpec((tm, d), lambda i: (i, 0)),
                  pl.BlockSpec((1, tm), lambda i: (0, i)),
                  pl.BlockSpec(g_col.shape, lambda i: (0, 0)),
                  pl.BlockSpec(f_col.shape, lambda i: (0, 0))],
        out_specs=out_specs,
        out_shape=out_shape,
        scratch_shapes=[pltpu.VMEM((n, d), BF16)],
        compiler_params=_cparams(("arbitrary",)),
        name="in_proj_" + mode,
    )(wt, h, pos_row, g_col, f_col)
    return out if mode == "iq" else out[0]


NEG_BIG = -1e30


INT_MIN = -2 ** 31
INT_MAX = 2 ** 31 - 1
MAGNITUDE_BITS = 0x7FFFFFFF
KEY_NEG_INF = -2139095041
BOUND_SLACK = 1.02
MAX_STATIC_SHIFT = 60.0
COUNT_ROWS = 32


def _key_to_float(key):
    bits = key ^ ((key >> 31) & jnp.int32(MAGNITUDE_BITS))
    return lax.bitcast_convert_type(bits, F32)


def _dsa_kernel(bound_ref, qt_ref, iqt_ref, iwt_ref, k_ref, vt_ref, ik_ref, o_ref,
                sc_ref, m_ref, acc_ref, last_ref, *, tq, tk, topk, index_bits):
    qi = pl.program_id(1)
    n_kb = (qi * tq + tq - 1) // tk + 1
    key0 = lax.broadcasted_iota(I32, (tk, tq), 0)
    qidx = qi * tq + lax.broadcasted_iota(I32, (tk, tq), 1)
    iw = iwt_ref[...]

    def score_body(kb, carry):
        start = pl.multiple_of(kb * tk, tk)
        ikb = ik_ref[pl.ds(start, tk), 0:IDX_HEAD_DIM]
        s = jnp.zeros((tk, tq), F32)
        for h in range(IDX_HEADS):
            off = (h % 2) * IDX_HEAD_DIM
            d = _dot(ikb, iqt_ref[h // 2, off:off + IDX_HEAD_DIM, :])
            s = s + jnp.maximum(d, 0.0) * iw[h:h + 1, :]
        sc_ref[kb] = jnp.where(kb * tk + key0 <= qidx, s, -jnp.inf)
        return carry

    lax.fori_loop(0, n_kb, score_body, 0)

    def count(pred):
        def cnt_body(kb, acc):
            hit = pred(kb).reshape(tk // COUNT_ROWS, COUNT_ROWS, tq)
            for r in range(tk // COUNT_ROWS):
                acc = jnp.where(hit[r], acc + 1.0, acc)
            return acc

        acc = lax.fori_loop(0, n_kb, cnt_body, jnp.zeros((COUNT_ROWS, tq), F32))
        return jnp.sum(acc, axis=0, keepdims=True)

    def bit_body(i, cand):
        trial = cand ^ lax.shift_left(jnp.int32(1), 31 - i)
        trial_f = _key_to_float(trial)
        cnt = count(lambda kb: sc_ref[kb] >= trial_f)
        return jnp.where(cnt >= topk, trial, cand)

    cand = lax.fori_loop(0, 32, bit_body, jnp.full((1, tq), INT_MIN, I32))
    tau = _key_to_float(jnp.maximum(cand, jnp.int32(KEY_NEG_INF)))

    last_ref[...] = jnp.full_like(last_ref, INT_MAX)
    n_ge = count(lambda kb: sc_ref[kb] >= tau)

    @pl.when(jnp.max(n_ge) > topk)
    def _():
        need = topk - count(lambda kb: sc_ref[kb] > tau)

        def idx_body(i, last):
            trial = last | lax.shift_left(jnp.int32(1), index_bits - 1 - i)
            below = count(lambda kb: (sc_ref[kb] == tau) & (kb * tk + key0 < trial))
            return jnp.where(below < need, trial, last)

        last_ref[...] = lax.fori_loop(0, index_bits, idx_body, jnp.zeros((1, tq), I32))

    acc_ref[...] = jnp.zeros_like(acc_ref)
    bound = bound_ref[0]

    def logits(kb, h):
        start = pl.multiple_of(kb * tk, tk)
        return _dot(k_ref[pl.ds(start, tk), h * LANES:(h + 1) * LANES], qt_ref[h])

    last = last_ref[...]

    def selected(kb):
        s = sc_ref[kb]
        kidx = kb * tk + key0
        return ((s > tau) | ((s == tau) & (kidx <= last))) & (kidx <= qidx)

    @pl.when(bound <= MAX_STATIC_SHIFT)
    def _():
        def att_body(kb, carry):
            bias = jnp.where(selected(kb), -bound, -jnp.inf)
            for h in range(ATT_HEADS):
                p = jnp.exp2(logits(kb, h) + bias)
                acc_ref[h] += _dot(vt_ref[h, kb], p.astype(BF16))
            return carry

        lax.fori_loop(0, n_kb, att_body, 0)

    @pl.when(bound > MAX_STATIC_SHIFT)
    def _():
        m_ref[...] = jnp.full_like(m_ref, NEG_BIG)

        def att_body(kb, carry):
            bias = jnp.where(selected(kb), 0.0, -jnp.inf)
            for h in range(ATT_HEADS):
                s = logits(kb, h) + bias
                m_prev = m_ref[h]
                m_next = jnp.maximum(m_prev, jnp.max(s, axis=0, keepdims=True))
                p = jnp.exp2(s - m_next)
                alpha = jnp.exp2(m_prev - m_next)
                acc_ref[h] = alpha * acc_ref[h] + _dot(vt_ref[h, kb], p.astype(BF16))
                m_ref[h] = m_next
            return carry

        lax.fori_loop(0, n_kb, att_body, 0)

    for h in range(ATT_HEADS):
        out = acc_ref[h, 0:LANES, :] / acc_ref[h, LANES:LANES + 1, :]
        o_ref[:, h * LANES:(h + 1) * LANES] = out.T.astype(o_ref.dtype)


def _dsa(bound, qt, iqt, iwt, k, vt, ik, bsz, seq, tk, tq=512):
    tq = min(tq, seq)
    topk = min(TOPK_MAX, seq // 4)
    w = ATT_WIDTH
    nq = seq // tq
    nkb = seq // tk
    npair = iqt.shape[0]
    kern = functools.partial(_dsa_kernel, tq=tq, tk=tk, topk=topk,
                             index_bits=max(1, (seq - 1).bit_length()))
    return pl.pallas_call(
        kern,
        grid=(bsz, nq),
        in_specs=[pl.BlockSpec(memory_space=pltpu.SMEM),
                  pl.BlockSpec((ATT_HEADS, LANES, tq), lambda b, i: (0, 0, b * nq + i)),
                  pl.BlockSpec((npair, LANES, tq), lambda b, i: (0, 0, b * nq + i)),
                  pl.BlockSpec((IDX_HEADS, tq), lambda b, i: (0, b * nq + i)),
                  pl.BlockSpec((seq, w), lambda b, i: (b, 0)),
                  pl.BlockSpec((ATT_HEADS, nkb, V_ROWS, tk), lambda b, i: (0, b, 0, 0)),
                  pl.BlockSpec((seq, LANES), lambda b, i: (b, 0))],
        out_specs=pl.BlockSpec((tq, w), lambda b, i: (b * nq + i, 0)),
        out_shape=jax.ShapeDtypeStruct((bsz * seq, w), BF16),
        scratch_shapes=[pltpu.VMEM((nkb, tk, tq), F32),
                        pltpu.VMEM((ATT_HEADS, 1, tq), F32),
                        pltpu.VMEM((ATT_HEADS, V_ROWS, tq), F32),
                        pltpu.VMEM((1, tq), I32)],
        compiler_params=_cparams(("arbitrary", "arbitrary")),
        name="dsa",
    )(bound, qt, iqt, iwt, k, vt, ik)


def _rwkv_prep_kernel(rr_ref, rk_ref, rv_ref, wa_ref, xg_ref,
                      mr_ref, mk_ref, mv_ref, mwa_ref, mg_ref,
                      w0_ref, a0_ref, kk_ref, ka_ref, rkp_ref,
                      wup_ref, aup_ref, gup_ref, e_ref, et_ref,
                      r_o, lw_o, k_o, v_o, kkn_o, bb_o, g_o, bon_o,
                      c_r, c_k, c_v, c_wa, c_g, *, tm):
    first = pl.program_id(1) == 0

    def shift(y_ref, carry_ref, mu_ref):
        y = y_ref[...]

        @pl.when(first)
        def _():
            carry_ref[...] = jnp.zeros_like(carry_ref)

        prev_last = carry_ref[7:8, :]
        rolled = pltpu.roll(y, 1, 0)
        rows = lax.broadcasted_iota(I32, y.shape, 0)
        yprev = jnp.where(rows == 0, prev_last, rolled)
        carry_ref[...] = y[tm - 8:tm, :]
        return y + (yprev - y) * mu_ref[...]

    r = shift(rr_ref, c_r, mr_ref)
    k = shift(rk_ref, c_k, mk_ref)
    v = shift(rv_ref, c_v, mv_ref)
    wa = shift(wa_ref, c_wa, mwa_ref)
    xw = pltpu.roll(wa, TAIL_TILE - XW_LANE, 1)[:, :LORA_PAD]
    xa = pltpu.roll(wa, TAIL_TILE - XA_LANE, 1)[:, :LORA_PAD]
    xg = shift(xg_ref, c_g, mg_ref)

    w_raw = w0_ref[...] + _dot(jnp.tanh(xw).astype(BF16), wup_ref[...])
    z = -w_raw
    softplus = jnp.maximum(z, 0.0) + jnp.log(1.0 + jnp.exp(-jnp.abs(z)))
    lw_o[...] = -jnp.exp(-softplus - 0.5)
    a = _sigmoid(a0_ref[...] + _dot(xa.astype(BF16), aup_ref[...]))
    g_o[...] = _dot(_sigmoid(xg).astype(BF16), gup_ref[...]).astype(BF16)

    def head_sum(x):
        return _dot_hi(_dot(x.astype(BF16), e_ref[...]), et_ref[...])

    kk = k * kk_ref[...]
    ss = head_sum(kk * kk)
    kk = kk / jnp.maximum(jnp.sqrt(ss), 1e-12)
    kmod = k * (1.0 + (a - 1.0) * ka_ref[...])
    r_o[...] = r.astype(BF16)
    k_o[...] = kmod.astype(BF16)
    v_o[...] = v.astype(BF16)
    kkn_o[...] = kk.astype(BF16)
    bb_o[...] = (kk * a).astype(BF16)
    bon_o[...] = (head_sum(r * kmod * rkp_ref[...]) * v).astype(BF16)


def _rwkv_prep(proj_r, proj_t, mus, vecs, wup, aup, gup, e, bsz, seq, tm=256):
    w = RWKV_WIDTH
    per_b = seq // tm

    def wide(off):
        return pl.BlockSpec((tm, w), lambda b, i: (b * per_b + i, off // w))

    def tail(off):
        return pl.BlockSpec((tm, TAIL_TILE), lambda b, i: (b * per_b + i, off // TAIL_TILE))

    def const(shape):
        return pl.BlockSpec(shape, lambda b, i: (0, 0))

    out_blk = pl.BlockSpec((tm, w), lambda b, i: (b * per_b + i, 0))
    kern = functools.partial(_rwkv_prep_kernel, tm=tm)
    return pl.pallas_call(
        kern,
        grid=(bsz, per_b),
        in_specs=[wide(OFF_RR), wide(OFF_RK), wide(OFF_RV), tail(OFF_WA), tail(OFF_XG),
                  const((1, w)), const((1, w)), const((1, w)),
                  const((1, TAIL_TILE)), const((1, GATE_LORA)),
                  const((1, w)), const((1, w)), const((1, w)), const((1, w)), const((1, w)),
                  const((LORA_PAD, w)), const((LORA_PAD, w)), const((GATE_LORA, w)),
                  const((w, LANES)), const((LANES, w))],
        out_specs=[out_blk] * 8,
        out_shape=[jax.ShapeDtypeStruct((bsz * seq, w), F32 if i == 1 else BF16) for i in range(8)],
        scratch_shapes=[pltpu.VMEM((8, w), F32)] * 3
        + [pltpu.VMEM((8, TAIL_TILE), F32), pltpu.VMEM((8, GATE_LORA), F32)],
        compiler_params=_cparams(("arbitrary", "arbitrary")),
        name="rwkv_prep",
    )(proj_r, proj_r, proj_r, proj_t, proj_t, *mus, *vecs, wup, aup, gup, e, e.T)


def _rwkv_chunk(rows, r_ref, lw_ref, k_ref, v_ref, kk_ref, bb_ref, g_ref, bon_ref, lg_ref, lb_ref,
                o_ref, z_ref):
    c = CHUNK
    lw = lw_ref[rows, :]
    tri = jnp.where(lax.broadcasted_iota(I32, (c, c), 1) <= lax.broadcasted_iota(I32, (c, c), 0),
                    1.0, 0.0).astype(BF16)
    hi = lw.astype(BF16)
    rem = lw - hi.astype(F32)
    mid = rem.astype(BF16)
    lo = (rem - mid.astype(F32)).astype(BF16)
    cum = _dot(tri, hi) + _dot(tri, mid) + _dot(tri, lo)
    p_in = jnp.exp(cum)
    p_ex = jnp.exp(cum - lw)
    p_inv = jnp.exp(-cum)
    p_end = p_in[c - 1:c, :]
    a_t = -kk_ref[rows, :].astype(F32) * p_ex
    r_t = r_ref[rows, :].astype(F32) * p_in
    b_h = bb_ref[rows, :].astype(F32) * p_inv
    k_h = k_ref[rows, :].astype(F32) * p_inv
    b_e = b_h * p_end
    k_e = k_h * p_end
    v = v_ref[rows, :].astype(F32)

    n2 = 2 * c
    lane = lax.broadcasted_iota(I32, (1, LANES), 1)
    head0 = lane < RWKV_HEAD_DIM
    ri = lax.broadcasted_iota(I32, (n2, n2), 0)
    ci = lax.broadcasted_iota(I32, (n2, n2), 1)
    same = (ri >= c) == (ci >= c)
    strict = same & (ci < ri)
    incl = same & (ci <= ri)
    eye = ri == ci
    own = ((lax.broadcasted_iota(I32, (n2, LANES), 0) >= c)
           == (lax.broadcasted_iota(I32, (n2, LANES), 1) >= RWKV_HEAD_DIM))
    inv_n = 1.0 / RWKV_HEAD_DIM

    def stack(y):
        return jnp.concatenate([jnp.where(head0, y, 0.0), jnp.where(head0, 0.0, y)], axis=0)

    pairs = range(RWKV_WIDTH // LANES)
    sls = [slice(p * LANES, (p + 1) * LANES) for p in pairs]
    a_s = [stack(a_t[:, sl]) for sl in sls]
    r_s = [stack(r_t[:, sl]) for sl in sls]
    v_s = [stack(v[:, sl]).astype(BF16) for sl in sls]
    g1 = [_dot_nt(jnp.concatenate([a_s[p], r_s[p]], axis=0).astype(BF16),
                  jnp.concatenate([stack(b_h[:, sls[p]]), stack(k_h[:, sls[p]])], axis=0).astype(BF16))
          for p in pairs]
    pw = [jnp.where(strict, g[:n2, :n2], 0.0).astype(BF16) for g in g1]
    a_rb = [jnp.where(incl, g[n2:, :n2], 0.0).astype(BF16) for g in g1]
    a_rk = [jnp.where(incl, g[n2:, n2:], 0.0).astype(BF16) for g in g1]
    akv = [_dot(jnp.where(strict, g1[p][:n2, n2:], 0.0).astype(BF16), v_s[p]) for p in pairs]
    xc = [jnp.concatenate([a_s[p], akv[p]], axis=1) for p in pairs]
    steps = int(np.log2(c))
    for i in range(steps):
        if i + 1 < steps:
            res = [_dot(pw[p], jnp.concatenate([pw[p], xc[p].astype(BF16)], axis=1)) for p in pairs]
            xc = [xc[p] + res[p][:, n2:] for p in pairs]
            pw = [res[p][:, :n2].astype(BF16) for p in pairs]
        else:
            xc = [xc[p] + _dot(pw[p], xc[p].astype(BF16)) for p in pairs]
    xcb = [x.astype(BF16) for x in xc]
    r2 = [_dot(a_rb[p], xcb[p]) for p in pairs]
    ov = [r2[p][:, LANES:] + _dot(a_rk[p], v_s[p]) for p in pairs]
    mg = [_dot_tn(stack(b_e[:, sls[p]]).astype(BF16), xcb[p]) for p in pairs]
    kv = [_dot_tn(stack(k_e[:, sls[p]]).astype(BF16), v_s[p]) for p in pairs]
    for p in pairs:
        q_s = r_s[p] + r2[p][:, :LANES]
        mmat = mg[p][:, :LANES] + jnp.where(eye, p_end[:, sls[p]], 0.0)
        qm = jnp.concatenate([q_s, mmat], axis=0).astype(BF16)
        res = _dot(qm, z_ref[p].astype(BF16))
        z_ref[p] = res[n2:] + mg[p][:, LANES:] + kv[p]
        o_s = res[:n2] + ov[p]
        mean = jnp.sum(o_s, axis=1, keepdims=True) * inv_n
        dev = jnp.where(own, o_s - mean, 0.0)
        var = jnp.sum(dev * dev, axis=1, keepdims=True) * inv_n
        y = dev * lax.rsqrt(var + LNX_EPS)
        y = (y[:c] + y[c:]) * lg_ref[:, sls[p]] + lb_ref[:, sls[p]]
        out = (y + bon_ref[rows, sls[p]].astype(F32)) * g_ref[rows, sls[p]].astype(F32)
        o_ref[rows, sls[p]] = out.astype(o_ref.dtype)


def _rwkv_core_kernel(*refs):
    z_ref = refs[-1]

    @pl.when(pl.program_id(1) == 0)
    def _():
        z_ref[...] = jnp.zeros_like(z_ref)

    for sub in range(CHUNKS_PER_STEP):
        _rwkv_chunk(slice(sub * CHUNK, (sub + 1) * CHUNK), *refs)


def _rwkv_core(r, lw, k, v, kk, bb, g, bon, lg, lb, bsz, seq):
    c = CHUNK * CHUNKS_PER_STEP
    w = RWKV_WIDTH
    per_b = seq // c
    blk = pl.BlockSpec((c, w), lambda b, i: (b * per_b + i, 0))
    vec = pl.BlockSpec((1, w), lambda b, i: (0, 0))
    return pl.pallas_call(
        _rwkv_core_kernel,
        grid=(bsz, per_b),
        in_specs=[blk] * 8 + [vec, vec],
        out_specs=blk,
        out_shape=jax.ShapeDtypeStruct((bsz * seq, w), BF16),
        scratch_shapes=[pltpu.VMEM((w // LANES, LANES, LANES), F32)],
        compiler_params=_cparams(("arbitrary", "arbitrary")),
        name="rwkv_core",
    )(r, lw, k, v, kk, bb, g, bon, lg, lb)


def _rope_freqs(head_dim):
    half = head_dim // ROPE_FRACTION // 2
    return (ROPE_THETA ** (-jnp.arange(half, dtype=F32) / half)).reshape(half, 1)


def kernel(x, c, positions, w_ada, b_ada, norm1_g, w_in, q_norm_g, k_norm_g, rwkv_mu, rwkv_w0,
           rwkv_w_up, rwkv_a0, rwkv_a_up, rwkv_g_up, rwkv_k_k, rwkv_k_a, rwkv_r_k, rwkv_lnx_g,
           rwkv_lnx_b, w_out, norm2_g, w_ffn_gate, w_ffn_up, w_ffn_down):
    bsz, seq, d = x.shape
    depth = w_ada.shape[0]
    m = bsz * seq
    pos_row = positions.reshape(1, m)
    fa_col = _rope_freqs(ATT_HEAD_DIM)
    fi_col = _rope_freqs(IDX_HEAD_DIM)
    hd = RWKV_HEAD_DIM
    e = (jnp.arange(RWKV_WIDTH)[:, None] // hd == jnp.arange(LANES)[None, :]).astype(BF16)
    x2 = x.reshape(m, d)

    for l in range(depth):
        mod = _adaln(c, w_ada[l], b_ada[l])
        mod3 = mod.reshape(bsz * 6, 1, d)

        h1 = _norm(x2, norm1_g[l].reshape(1, d), mod3, seq)
        w_in_t = w_in[l].T
        proj_r = _matmul_nt(h1, w_in_t, lambda j: ATT_COLS + j * RWKV_WIDTH, 3, RWKV_WIDTH,
                            "in_proj_rkv")
        proj_t = _matmul_nt(h1, w_in_t, lambda j: TAIL_COL0 + j * TAIL_TILE, 2, TAIL_TILE,
                            "in_proj_tail", tm=min(2048, m))

        none = jnp.zeros((SUBLANES, 1), F32)
        aw = ATT_WIDTH
        qt = _proj_t(h1, w_in_t, OFF_Q, aw, pos_row, q_norm_g[l].reshape(-1, 1), fa_col, "q")
        kn = _proj_t(h1, w_in_t, OFF_K, aw, pos_row, k_norm_g[l].reshape(-1, 1), fa_col, "k")
        vt = _proj_t(h1, w_in_t, OFF_V, aw, pos_row, none, none, "v")
        iqt, ik, iwt = _proj_t(h1, w_in_t, OFF_IQ, aw + LANES, pos_row, none, fi_col, "iq")
        bound = (ATT_HEAD_DIM ** 0.5 * LOG2E * BOUND_SLACK
                 * jnp.max(jnp.abs(q_norm_g[l])) * jnp.max(jnp.abs(k_norm_g[l])))
        att = _dsa(bound.reshape(1).astype(F32), qt, iqt, iwt, kn, vt, ik, bsz, seq, KEY_TILE)

        mu = rwkv_mu[l]
        w3 = 3 * RWKV_WIDTH

        n_wa = DECAY_LORA + AAA_LORA
        mu_wa = jnp.zeros((1, TAIL_TILE), F32).at[0, XW_LANE:XW_LANE + n_wa].set(mu[w3:w3 + n_wa])
        mus = [mu[0:RWKV_WIDTH].reshape(1, -1), mu[RWKV_WIDTH:2 * RWKV_WIDTH].reshape(1, -1),
               mu[2 * RWKV_WIDTH:w3].reshape(1, -1), mu_wa, mu[w3 + n_wa:].reshape(1, -1)]
        vecs = [rwkv_w0[l].reshape(1, -1), rwkv_a0[l].reshape(1, -1), rwkv_k_k[l].reshape(1, -1),
                rwkv_k_a[l].reshape(1, -1), rwkv_r_k[l].reshape(1, -1)]

        def pad_rows(wm):
            return jnp.zeros((LORA_PAD, wm.shape[1]), F32).at[:wm.shape[0]].set(wm).astype(BF16)

        r, lw, km, vv, kk, bb, g, bon = _rwkv_prep(
            proj_r, proj_t, mus, vecs, pad_rows(rwkv_w_up[l]), pad_rows(rwkv_a_up[l]),
            rwkv_g_up[l].astype(BF16), e, bsz, seq)
        rw = _rwkv_core(r, lw, km, vv, kk, bb, g, bon, rwkv_lnx_g[l].reshape(1, -1),
                        rwkv_lnx_b[l].reshape(1, -1), bsz, seq)

        x2, h2 = _out_proj(att, rw, w_out[l], x2, mod3, norm2_g[l].reshape(1, d), seq)

        hglu = _ffn_glu(h2, w_ffn_gate[l], w_ffn_up[l])
        x2 = _ffn_down(hglu, w_ffn_down[l], x2, mod3, seq)
    return x2.reshape(bsz, seq, d)
```

```python
import functools

import jax
import jax.numpy as jnp
import numpy as np
from jax import lax
from jax.experimental import pallas as pl
from jax.experimental.pallas import tpu as pltpu

F32 = jnp.float32
BF16 = jnp.bfloat16
I32 = jnp.int32

D_MODEL = 2048
ATT_HEADS = 8
ATT_HEAD_DIM = 128
ATT_WIDTH = ATT_HEADS * ATT_HEAD_DIM
RWKV_WIDTH = D_MODEL - ATT_WIDTH
RWKV_HEAD_DIM = 64
RWKV_HEADS = RWKV_WIDTH // RWKV_HEAD_DIM
IDX_HEADS = 16
IDX_HEAD_DIM = 64
TOPK_MAX = 256
ROPE_THETA = 500000.0
ROPE_FRACTION = 4
DECAY_LORA = 96
AAA_LORA = 96
GATE_LORA = 256
NORM_EPS = 1e-6
LNX_EPS = 64e-5

LANES = 128
SUBLANES = 8
LORA_PAD = 128
ATT_COLS = 4 * ATT_WIDTH + IDX_HEAD_DIM + IDX_HEADS
IN_COLS = ATT_COLS + 3 * RWKV_WIDTH + DECAY_LORA + AAA_LORA + GATE_LORA
OFF_Q, OFF_K, OFF_V, OFF_IQ = 0, 1024, 2048, 3072
OFF_RR, OFF_RK, OFF_RV = 0, 1024, 2048
TAIL_TILE = 256
TAIL_COL0 = IN_COLS - 2 * TAIL_TILE
XW_LANE = IN_COLS - GATE_LORA - AAA_LORA - DECAY_LORA - TAIL_COL0
XA_LANE = XW_LANE + DECAY_LORA
OFF_WA = 3 * RWKV_WIDTH + TAIL_COL0 - (IN_COLS - RWKV_WIDTH)
OFF_XG = OFF_WA + TAIL_TILE

LOG2E = 1.4426950408889634
V_ROWS = 144
KEY_TILE = 512
CHUNKS_PER_STEP = 4
CHUNK = 64
VMEM_LIMIT = 56 * 1024 * 1024


def _cparams(sem):
    return pltpu.CompilerParams(dimension_semantics=sem, vmem_limit_bytes=VMEM_LIMIT)


def _dot(a, b):
    return jnp.dot(a, b, preferred_element_type=F32)


def _dot_nt(a, b):
    return lax.dot_general(a, b, (((1,), (1,)), ((), ())), preferred_element_type=F32)


def _dot_tn(a, b):
    return lax.dot_general(a, b, (((0,), (0,)), ((), ())), preferred_element_type=F32)


def _split2(x):
    hi = x.astype(BF16)
    lo = (x - hi.astype(F32)).astype(BF16)
    return hi, lo


def _dot_hi(x, w):
    hi, lo = _split2(x)
    return _dot(hi, w) + _dot(lo, w)


def _sigmoid(x):
    return 1.0 / (1.0 + jnp.exp(-x))


def _adaln_kernel(c_ref, w_ref, b_ref, o_ref):
    c = c_ref[...]
    ca = c * _sigmoid(c)
    o_ref[...] = _dot(ca.astype(BF16), w_ref[...].astype(BF16)) + b_ref[...]


def _adaln(c, w, b):
    bsz, d = c.shape
    n = w.shape[1]
    rows = 8
    cp = jnp.zeros((rows, d), F32).at[:bsz].set(c)
    tn = 1024
    out = pl.pallas_call(
        _adaln_kernel,
        grid=(n // tn,),
        in_specs=[pl.BlockSpec((rows, d), lambda j: (0, 0)),
                  pl.BlockSpec((d, tn), lambda j: (0, j)),
                  pl.BlockSpec((1, tn), lambda j: (0, j))],
        out_specs=pl.BlockSpec((rows, tn), lambda j: (0, j)),
        out_shape=jax.ShapeDtypeStruct((rows, n), F32),
        compiler_params=_cparams(("arbitrary",)),
        name="adaln",
    )(cp, w, b.reshape(1, n))
    return out[:bsz]


def _norm_mod(x, g, sc, sh):
    ms = jnp.mean(x * x, axis=-1, keepdims=True)
    y = x * lax.rsqrt(ms + NORM_EPS)
    y = y * g
    return y * (1.0 + sc) + sh


def _norm_kernel(x_ref, g_ref, sc_ref, sh_ref, o_ref):
    o_ref[...] = _norm_mod(x_ref[...], g_ref[...], sc_ref[...], sh_ref[...]).astype(BF16)


def _norm(x2, g, mod3, seq, tm=512):
    m, d = x2.shape
    per_b = seq // tm
    return pl.pallas_call(
        _norm_kernel,
        grid=(m // tm,),
        in_specs=[pl.BlockSpec((tm, d), lambda i: (i, 0)),
                  pl.BlockSpec((1, d), lambda i: (0, 0)),
                  pl.BlockSpec((None, 1, d), lambda i: ((i // per_b) * 6 + 1, 0, 0)),
                  pl.BlockSpec((None, 1, d), lambda i: ((i // per_b) * 6 + 0, 0, 0))],
        out_specs=pl.BlockSpec((tm, d), lambda i: (i, 0)),
        out_shape=jax.ShapeDtypeStruct((m, d), BF16),
        compiler_params=_cparams(("arbitrary",)),
        name="norm1",
    )(x2, g, mod3, mod3)


def _mm_nt_kernel(h_ref, wt_ref, o_ref, wb_ref):
    @pl.when(pl.program_id(1) == 0)
    def _():
        wb_ref[...] = wt_ref[...].astype(BF16)

    o_ref[...] = _dot_nt(h_ref[...], wb_ref[...])


def _matmul_nt(h, wt, col_of_tile, n_tiles, tn, name, tm=1024):
    m, d = h.shape
    return pl.pallas_call(
        _mm_nt_kernel,
        grid=(n_tiles, m // tm),
        in_specs=[pl.BlockSpec((tm, d), lambda j, i: (i, 0)),
                  pl.BlockSpec((pl.Element(tn), pl.Element(d)),
                               lambda j, i: (pl.multiple_of(col_of_tile(j), SUBLANES), 0))],
        out_specs=pl.BlockSpec((tm, tn), lambda j, i: (i, j)),
        out_shape=jax.ShapeDtypeStruct((m, n_tiles * tn), F32),
        scratch_shapes=[pltpu.VMEM((tn, d), BF16)],
        compiler_params=_cparams(("arbitrary", "arbitrary")),
        name=name,
    )(h, wt)


def _ffn_glu_kernel(h_ref, wg_ref, wu_ref, o_ref, wgb_ref, wub_ref):
    @pl.when(pl.program_id(1) == 0)
    def _():
        wgb_ref[...] = wg_ref[...].astype(BF16)
        wub_ref[...] = wu_ref[...].astype(BF16)

    h = h_ref[...]
    a = _dot(h, wgb_ref[...])
    u = _dot(h, wub_ref[...])
    o_ref[...] = (a * _sigmoid(a) * u).astype(o_ref.dtype)


def _ffn_glu(h, wg, wu, tm=1024, tn=512):
    m, d = h.shape
    tm = min(tm, m)
    n = wg.shape[1]
    return pl.pallas_call(
        _ffn_glu_kernel,
        grid=(n // tn, m // tm),
        in_specs=[pl.BlockSpec((tm, d), lambda j, i: (i, 0)),
                  pl.BlockSpec((d, tn), lambda j, i: (0, j)),
                  pl.BlockSpec((d, tn), lambda j, i: (0, j))],
        out_specs=pl.BlockSpec((tm, tn), lambda j, i: (i, j)),
        out_shape=jax.ShapeDtypeStruct((m, n), BF16),
        scratch_shapes=[pltpu.VMEM((d, tn), BF16)] * 2,
        compiler_params=_cparams(("arbitrary", "arbitrary")),
        name="ffn_glu",
    )(h, wg, wu)


def _out_proj_kernel(a_ref, r_ref, wa_ref, wr_ref, x_ref, gt_ref, g_ref, sc_ref, sh_ref,
                     x1_ref, h2_ref, wab_ref, wrb_ref):
    @pl.when(pl.program_id(0) == 0)
    def _():
        wab_ref[...] = wa_ref[...].astype(BF16)
        wrb_ref[...] = wr_ref[...].astype(BF16)

    mixed = _dot(a_ref[...], wab_ref[...]) + _dot(r_ref[...], wrb_ref[...])
    x1 = x_ref[...] + gt_ref[...] * mixed
    x1_ref[...] = x1
    h2_ref[...] = _norm_mod(x1, g_ref[...], sc_ref[...], sh_ref[...]).astype(BF16)


def _out_proj(att, rwkv, w_out, x2, mod3, g2, seq, tm=512):
    m, ka = att.shape
    kr = rwkv.shape[1]
    n = w_out.shape[1]
    per_b = seq // tm
    once = pl.Buffered(1)

    def mod_row(j):
        return pl.BlockSpec((None, 1, n), lambda i: ((i // per_b) * 6 + j, 0, 0))

    row = pl.BlockSpec((tm, n), lambda i: (i, 0))
    return pl.pallas_call(
        _out_proj_kernel,
        grid=(m // tm,),
        in_specs=[pl.BlockSpec((tm, ka), lambda i: (i, 0)),
                  pl.BlockSpec((tm, kr), lambda i: (i, 0)),
                  pl.BlockSpec((ka, n), lambda i: (0, 0), pipeline_mode=once),
                  pl.BlockSpec((kr, n), lambda i: (ka // kr, 0), pipeline_mode=once),
                  row, mod_row(2),
                  pl.BlockSpec((1, n), lambda i: (0, 0)), mod_row(4), mod_row(3)],
        out_specs=[row, row],
        out_shape=[jax.ShapeDtypeStruct((m, n), F32), jax.ShapeDtypeStruct((m, n), BF16)],
        scratch_shapes=[pltpu.VMEM((ka, n), BF16), pltpu.VMEM((kr, n), BF16)],
        compiler_params=_cparams(("arbitrary",)),
        name="out_proj",
    )(att, rwkv, w_out, w_out, x2, mod3, g2, mod3, mod3)


def _ffn_down_kernel(h_ref, w_ref, x_ref, gt_ref, o_ref, wb_ref):
    @pl.when(pl.program_id(1) == 0)
    def _():
        wb_ref[...] = w_ref[...].astype(BF16)

    o_ref[...] = x_ref[...] + gt_ref[...] * _dot(h_ref[...], wb_ref[...])


def _ffn_down(h, w, x2, mod3, seq, tm=512, tn=512):
    m, kdim = h.shape
    n = w.shape[1]
    per_b = seq // tm
    return pl.pallas_call(
        _ffn_down_kernel,
        grid=(n // tn, m // tm),
        in_specs=[pl.BlockSpec((tm, kdim), lambda j, i: (i, 0)),
                  pl.BlockSpec((kdim, tn), lambda j, i: (0, j)),
                  pl.BlockSpec((tm, tn), lambda j, i: (i, j)),
                  pl.BlockSpec((None, 1, tn), lambda j, i: ((i // per_b) * 6 + 5, 0, j))],
        out_specs=pl.BlockSpec((tm, tn), lambda j, i: (i, j)),
        out_shape=jax.ShapeDtypeStruct((m, n), F32),
        scratch_shapes=[pltpu.VMEM((kdim, tn), BF16)],
        compiler_params=_cparams(("arbitrary", "arbitrary")),
        name="ffn_down",
    )(h, w, x2, mod3)


def _proj_t_kernel(wt_ref, h_ref, pos_ref, g_ref, f_ref, *refs, mode):
    *o_refs, wb_ref = refs
    o_ref = o_refs[0]

    @pl.when(pl.program_id(0) == 0)
    def _():
        wb_ref[...] = wt_ref[...].astype(BF16)

    yt = _dot_nt(wb_ref[...], h_ref[...])
    tm = yt.shape[1]
    if mode == "v":
        pad_row = lax.broadcasted_iota(I32, (V_ROWS - LANES, KEY_TILE), 0)
        ones_rows = jnp.where(pad_row == 0, 1.0, 0.0).astype(BF16)
        for h in range(ATT_HEADS):
            for j in range(tm // KEY_TILE):
                tile = yt[h * LANES:(h + 1) * LANES, j * KEY_TILE:(j + 1) * KEY_TILE]
                o_ref[h, j, 0:LANES, :] = tile.astype(BF16)
                o_ref[h, j, LANES:V_ROWS, :] = ones_rows
        return

    head = ATT_HEAD_DIM if mode in ("q", "k") else IDX_HEAD_DIM
    half = head // ROPE_FRACTION // 2
    ang = f_ref[...] * pos_ref[...].astype(F32)
    cos, sin = jnp.cos(ang), jnp.sin(ang)

    def rotary(y):
        x1, x2 = y[0:half], y[half:2 * half]
        return jnp.concatenate([x1 * cos - x2 * sin, x2 * cos + x1 * sin, y[2 * half:]], axis=0)

    for hd in range(ATT_WIDTH // head):
        y = yt[hd * head:(hd + 1) * head, :]
        if mode in ("q", "k"):
            ms = jnp.mean(y * y, axis=0, keepdims=True)
            y = y * lax.rsqrt(ms + NORM_EPS) * g_ref[...]
        y = rotary(y)
        if mode == "q":
            o_ref[hd] = (y * ((ATT_HEAD_DIM ** -0.5) * LOG2E)).astype(BF16)
        elif mode == "iq":
            rows = slice((hd % 2) * head, (hd % 2 + 1) * head)
            o_ref[hd // 2, rows, :] = (y * (IDX_HEAD_DIM ** -0.5)).astype(BF16)
        else:
            o_ref[:, hd * LANES:(hd + 1) * LANES] = y.T.astype(BF16)
    if mode == "iq":
        rest = yt[ATT_WIDTH:, :]
        ikt = jnp.concatenate([rotary(rest[0:head]), rest[head:]], axis=0)
        o_refs[1][...] = ikt.T.astype(BF16)
        o_refs[2][...] = rest[head:head + IDX_HEADS, :] * (IDX_HEADS ** -0.5)


def _proj_t(h, wt, col0, n, pos_row, g_col, f_col, mode, tm=1024):
    m, d = h.shape
    tm = min(tm, m)
    if mode == "v":
        out_specs = [pl.BlockSpec((ATT_HEADS, tm // KEY_TILE, V_ROWS, KEY_TILE),
                                  lambda i: (0, i, 0, 0))]
        out_shape = [jax.ShapeDtypeStruct((ATT_HEADS, m // KEY_TILE, V_ROWS, KEY_TILE), BF16)]
    elif mode in ("q", "iq"):
        out_specs = [pl.BlockSpec((ATT_WIDTH // LANES, LANES, tm), lambda i: (0, 0, i))]
        out_shape = [jax.ShapeDtypeStruct((ATT_WIDTH // LANES, LANES, m), BF16)]
        if mode == "iq":
            out_specs += [pl.BlockSpec((tm, LANES), lambda i: (i, 0)),
                          pl.BlockSpec((IDX_HEADS, tm), lambda i: (0, i))]
            out_shape += [jax.ShapeDtypeStruct((m, LANES), BF16),
                          jax.ShapeDtypeStruct((IDX_HEADS, m), F32)]
    else:
        out_specs = [pl.BlockSpec((tm, n), lambda i: (i, 0))]
        out_shape = [jax.ShapeDtypeStruct((m, n), BF16)]
    out = pl.pallas_call(
        functools.partial(_proj_t_kernel, mode=mode),
        grid=(m // tm,),
        in_specs=[pl.BlockSpec((pl.Element(n), pl.Element(d)), lambda i: (col0, 0)),
                  pl.BlockSpec((tm, d), lambda i: (i, 0)),
                  pl.BlockSpec((1, tm), lambda i: (0, i)),
                  pl.BlockSpec(g_col.shape, lambda i: (0, 0)),
                  pl.BlockSpec(f_col.shape, lambda i: (0, 0))],
        out_specs=out_specs,
        out_shape=out_shape,
        scratch_shapes=[pltpu.VMEM((n, d), BF16)],
        compiler_params=_cparams(("arbitrary",)),
        name="in_proj_" + mode,
    )(wt, h, pos_row, g_col, f_col)
    return out if mode == "iq" else out[0]


NEG_BIG = -1e30


INT_MIN = -2 ** 31
INT_MAX = 2 ** 31 - 1
MAGNITUDE_BITS = 0x7FFFFFFF
KEY_NEG_INF = -2139095041
BOUND_SLACK = 1.02
MAX_STATIC_SHIFT = 60.0
COUNT_ROWS = 32


def _key_to_float(key):
    bits = key ^ ((key >> 31) & jnp.int32(MAGNITUDE_BITS))
    return lax.bitcast_convert_type(bits, F32)


def _dsa_kernel(bound_ref, qt_ref, iqt_ref, iwt_ref, k_ref, vt_ref, ik_ref, o_ref,
                sc_ref, m_ref, acc_ref, last_ref, *, tq, tk, topk, index_bits):
    qi = pl.program_id(1)
    n_kb = (qi * tq + tq - 1) // tk + 1
    key0 = lax.broadcasted_iota(I32, (tk, tq), 0)
    qidx = qi * tq + lax.broadcasted_iota(I32, (tk, tq), 1)
    iw = iwt_ref[...]

    def score_body(kb, carry):
        start = pl.multiple_of(kb * tk, tk)
        ikb = ik_ref[pl.ds(start, tk), 0:IDX_HEAD_DIM]
        s = jnp.zeros((tk, tq), F32)
        for h in range(IDX_HEADS):
            off = (h % 2) * IDX_HEAD_DIM
            d = _dot(ikb, iqt_ref[h // 2, off:off + IDX_HEAD_DIM, :])
            s = s + jnp.maximum(d, 0.0) * iw[h:h + 1, :]
        sc_ref[kb] = jnp.where(kb * tk + key0 <= qidx, s, -jnp.inf)
        return carry

    lax.fori_loop(0, n_kb, score_body, 0)

    def count(pred):
        def cnt_body(kb, acc):
            hit = pred(kb).reshape(tk // COUNT_ROWS, COUNT_ROWS, tq)
            for r in range(tk // COUNT_ROWS):
                acc = jnp.where(hit[r], acc + 1.0, acc)
            return acc

        acc = lax.fori_loop(0, n_kb, cnt_body, jnp.zeros((COUNT_ROWS, tq), F32))
        return jnp.sum(acc, axis=0, keepdims=True)

    def bit_body(i, cand):
        trial = cand ^ lax.shift_left(jnp.int32(1), 31 - i)
        trial_f = _key_to_float(trial)
        cnt = count(lambda kb: sc_ref[kb] >= trial_f)
        return jnp.where(cnt >= topk, trial, cand)

    cand = lax.fori_loop(0, 32, bit_body, jnp.full((1, tq), INT_MIN, I32))
    tau = _key_to_float(jnp.maximum(cand, jnp.int32(KEY_NEG_INF)))

    last_ref[...] = jnp.full_like(last_ref, INT_MAX)
    n_ge = count(lambda kb: sc_ref[kb] >= tau)

    @pl.when(jnp.max(n_ge) > topk)
    def _():
        need = topk - count(lambda kb: sc_ref[kb] > tau)

        def idx_body(i, last):
            trial = last | lax.shift_left(jnp.int32(1), index_bits - 1 - i)
            below = count(lambda kb: (sc_ref[kb] == tau) & (kb * tk + key0 < trial))
            return jnp.where(below < need, trial, last)

        last_ref[...] = lax.fori_loop(0, index_bits, idx_body, jnp.zeros((1, tq), I32))

    acc_ref[...] = jnp.zeros_like(acc_ref)
    bound = bound_ref[0]

    def logits(kb, h):
        start = pl.multiple_of(kb * tk, tk)
        return _dot(k_ref[pl.ds(start, tk), h * LANES:(h + 1) * LANES], qt_ref[h])

    last = last_ref[...]

    def selected(kb):
        s = sc_ref[kb]
        kidx = kb * tk + key0
        return ((s > tau) | ((s == tau) & (kidx <= last))) & (kidx <= qidx)

    @pl.when(bound <= MAX_STATIC_SHIFT)
    def _():
        def att_body(kb, carry):
            bias = jnp.where(selected(kb), -bound, -jnp.inf)
            for h in range(ATT_HEADS):
                p = jnp.exp2(logits(kb, h) + bias)
                acc_ref[h] += _dot(vt_ref[h, kb], p.astype(BF16))
            return carry

        lax.fori_loop(0, n_kb, att_body, 0)

    @pl.when(bound > MAX_STATIC_SHIFT)
    def _():
        m_ref[...] = jnp.full_like(m_ref, NEG_BIG)

        def att_body(kb, carry):
            bias = jnp.where(selected(kb), 0.0, -jnp.inf)
            for h in range(ATT_HEADS):
                s = logits(kb, h) + bias
                m_prev = m_ref[h]
                m_next = jnp.maximum(m_prev, jnp.max(s, axis=0, keepdims=True))
                p = jnp.exp2(s - m_next)
                alpha = jnp.exp2(m_prev - m_next)
                acc_ref[h] = alpha * acc_ref[h] + _dot(vt_ref[h, kb], p.astype(BF16))
                m_ref[h] = m_next
            return carry

        lax.fori_loop(0, n_kb, att_body, 0)

    for h in range(ATT_HEADS):
        out = acc_ref[h, 0:LANES, :] / acc_ref[h, LANES:LANES + 1, :]
        o_ref[:, h * LANES:(h + 1) * LANES] = out.T.astype(o_ref.dtype)


def _dsa(bound, qt, iqt, iwt, k, vt, ik, bsz, seq, tk, tq=512):
    tq = min(tq, seq)
    topk = min(TOPK_MAX, seq // 4)
    w = ATT_WIDTH
    nq = seq // tq
    nkb = seq // tk
    npair = iqt.shape[0]
    kern = functools.partial(_dsa_kernel, tq=tq, tk=tk, topk=topk,
                             index_bits=max(1, (seq - 1).bit_length()))
    return pl.pallas_call(
        kern,
        grid=(bsz, nq),
        in_specs=[pl.BlockSpec(memory_space=pltpu.SMEM),
                  pl.BlockSpec((ATT_HEADS, LANES, tq), lambda b, i: (0, 0, b * nq + i)),
                  pl.BlockSpec((npair, LANES, tq), lambda b, i: (0, 0, b * nq + i)),
                  pl.BlockSpec((IDX_HEADS, tq), lambda b, i: (0, b * nq + i)),
                  pl.BlockSpec((seq, w), lambda b, i: (b, 0)),
                  pl.BlockSpec((ATT_HEADS, nkb, V_ROWS, tk), lambda b, i: (0, b, 0, 0)),
                  pl.BlockSpec((seq, LANES), lambda b, i: (b, 0))],
        out_specs=pl.BlockSpec((tq, w), lambda b, i: (b * nq + i, 0)),
        out_shape=jax.ShapeDtypeStruct((bsz * seq, w), BF16),
        scratch_shapes=[pltpu.VMEM((nkb, tk, tq), F32),
                        pltpu.VMEM((ATT_HEADS, 1, tq), F32),
                        pltpu.VMEM((ATT_HEADS, V_ROWS, tq), F32),
                        pltpu.VMEM((1, tq), I32)],
        compiler_params=_cparams(("arbitrary", "arbitrary")),
        name="dsa",
    )(bound, qt, iqt, iwt, k, vt, ik)


def _rwkv_prep_kernel(rr_ref, rk_ref, rv_ref, wa_ref, xg_ref,
                      mr_ref, mk_ref, mv_ref, mwa_ref, mg_ref,
                      w0_ref, a0_ref, kk_ref, ka_ref, rkp_ref,
                      wup_ref, aup_ref, gup_ref, e_ref,
                      r_o, lw_o, k_o, v_o, kkn_o, bb_o, g_o, bon_o,
                      c_r, c_k, c_v, c_wa, c_g, *, tm):
    first = pl.program_id(1) == 0

    for carry_ref in (c_r, c_k, c_v, c_wa, c_g):
        @pl.when(first)
        def _(carry_ref=carry_ref):
            carry_ref[...] = jnp.zeros_like(carry_ref)

    def shift(y_ref, carry_ref, mu_ref, cols=slice(None)):
        y = y_ref[:, cols]
        prev_last = carry_ref[7:8, cols]
        rolled = pltpu.roll(y, 1, 0)
        rows = lax.broadcasted_iota(I32, y.shape, 0)
        yprev = jnp.where(rows == 0, prev_last, rolled)
        carry_ref[:, cols] = y[tm - 8:tm, :]
        return y + (yprev - y) * mu_ref[:, cols]

    wa = shift(wa_ref, c_wa, mwa_ref)
    xw = pltpu.roll(wa, TAIL_TILE - XW_LANE, 1)[:, :LORA_PAD]
    xa = pltpu.roll(wa, TAIL_TILE - XA_LANE, 1)[:, :LORA_PAD]
    xg = shift(xg_ref, c_g, mg_ref)
    w_raw = w0_ref[...] + _dot(jnp.tanh(xw).astype(BF16), wup_ref[...])
    a_pre = a0_ref[...] + _dot(xa.astype(BF16), aup_ref[...])
    g_o[...] = _dot(_sigmoid(xg).astype(BF16), gup_ref[...]).astype(BF16)

    e2 = e_ref[...]
    for p in range(RWKV_WIDTH // LANES):
        sl = slice(p * LANES, (p + 1) * LANES)
        r = shift(rr_ref, c_r, mr_ref, sl)
        k = shift(rk_ref, c_k, mk_ref, sl)
        v = shift(rv_ref, c_v, mv_ref, sl)
        z = -w_raw[:, sl]
        softplus = jnp.maximum(z, 0.0) + jnp.log(1.0 + jnp.exp(-jnp.abs(z)))
        lw_o[:, sl] = -jnp.exp(-softplus - 0.5)
        a = _sigmoid(a_pre[:, sl])
        kk = k * kk_ref[:, sl]
        ss = _dot((kk * kk).astype(BF16), e2)
        kk = kk / jnp.maximum(jnp.sqrt(ss), 1e-12)
        kmod = k * (1.0 + (a - 1.0) * ka_ref[:, sl])
        r_o[:, sl] = r.astype(BF16)
        k_o[:, sl] = kmod.astype(BF16)
        v_o[:, sl] = v.astype(BF16)
        kkn_o[:, sl] = kk.astype(BF16)
        bb_o[:, sl] = (kk * a).astype(BF16)
        rkr = _dot((r * kmod * rkp_ref[:, sl]).astype(BF16), e2)
        bon_o[:, sl] = (rkr * v).astype(BF16)


def _rwkv_prep(proj_r, mus, vecs, wup, aup, gup, e, bsz, seq, tm=256):
    w = RWKV_WIDTH
    per_b = seq // tm

    def wide(off):
        return pl.BlockSpec((tm, w), lambda b, i: (b * per_b + i, off // w))

    def tail(off):
        return pl.BlockSpec((tm, TAIL_TILE), lambda b, i: (b * per_b + i, off // TAIL_TILE))

    def const(shape):
        return pl.BlockSpec(shape, lambda b, i: (0, 0))

    out_blk = pl.BlockSpec((tm, w), lambda b, i: (b * per_b + i, 0))
    kern = functools.partial(_rwkv_prep_kernel, tm=tm)
    return pl.pallas_call(
        kern,
        grid=(bsz, per_b),
        in_specs=[wide(OFF_RR), wide(OFF_RK), wide(OFF_RV), tail(OFF_WA), tail(OFF_XG),
                  const((1, w)), const((1, w)), const((1, w)),
                  const((1, TAIL_TILE)), const((1, GATE_LORA)),
                  const((1, w)), const((1, w)), const((1, w)), const((1, w)), const((1, w)),
                  const((LORA_PAD, w)), const((LORA_PAD, w)), const((GATE_LORA, w)),
                  const((LANES, LANES))],
        out_specs=[out_blk] * 8,
        out_shape=[jax.ShapeDtypeStruct((bsz * seq, w), F32 if i == 1 else BF16) for i in range(8)],
        scratch_shapes=[pltpu.VMEM((8, w), F32)] * 3
        + [pltpu.VMEM((8, TAIL_TILE), F32), pltpu.VMEM((8, GATE_LORA), F32)],
        compiler_params=_cparams(("arbitrary", "arbitrary")),
        name="rwkv_prep",
    )(proj_r, proj_r, proj_r, proj_r, proj_r, *mus, *vecs, wup, aup, gup, e)


def _rwkv_chunk(rows, r_ref, lw_ref, k_ref, v_ref, kk_ref, bb_ref, g_ref, bon_ref, lg_ref, lb_ref,
                o_ref, z_ref):
    c = CHUNK
    lw = lw_ref[rows, :]
    tri = jnp.where(lax.broadcasted_iota(I32, (c, c), 1) <= lax.broadcasted_iota(I32, (c, c), 0),
                    1.0, 0.0).astype(BF16)
    hi = lw.astype(BF16)
    rem = lw - hi.astype(F32)
    mid = rem.astype(BF16)
    lo = (rem - mid.astype(F32)).astype(BF16)
    cum = _dot(tri, hi) + _dot(tri, mid) + _dot(tri, lo)
    p_in = jnp.exp(cum)
    p_ex = jnp.exp(cum - lw)
    p_inv = jnp.exp(-cum)
    p_end = p_in[c - 1:c, :]
    a_t = -kk_ref[rows, :].astype(F32) * p_ex
    r_t = r_ref[rows, :].astype(F32) * p_in
    b_h = bb_ref[rows, :].astype(F32) * p_inv
    k_h = k_ref[rows, :].astype(F32) * p_inv
    b_e = b_h * p_end
    k_e = k_h * p_end
    v = v_ref[rows, :].astype(F32)

    n2 = 2 * c
    lane = lax.broadcasted_iota(I32, (1, LANES), 1)
    head0 = lane < RWKV_HEAD_DIM
    ri = lax.broadcasted_iota(I32, (n2, n2), 0)
    ci = lax.broadcasted_iota(I32, (n2, n2), 1)
    same = (ri >= c) == (ci >= c)
    strict = same & (ci < ri)
    incl = same & (ci <= ri)
    eye = ri == ci
    own = ((lax.broadcasted_iota(I32, (n2, LANES), 0) >= c)
           == (lax.broadcasted_iota(I32, (n2, LANES), 1) >= RWKV_HEAD_DIM))
    inv_n = 1.0 / RWKV_HEAD_DIM

    def stack(y):
        return jnp.concatenate([jnp.where(head0, y, 0.0), jnp.where(head0, 0.0, y)], axis=0)

    pairs = range(RWKV_WIDTH // LANES)
    sls = [slice(p * LANES, (p + 1) * LANES) for p in pairs]
    a_s = [stack(a_t[:, sl]) for sl in sls]
    r_s = [stack(r_t[:, sl]) for sl in sls]
    v_s = [stack(v[:, sl]).astype(BF16) for sl in sls]
    g1 = [_dot_nt(jnp.concatenate([a_s[p], r_s[p]], axis=0).astype(BF16),
                  jnp.concatenate([stack(b_h[:, sls[p]]), stack(k_h[:, sls[p]])], axis=0).astype(BF16))
          for p in pairs]
    pw = [jnp.where(strict, g[:n2, :n2], 0.0).astype(BF16) for g in g1]
    a_rb = [jnp.where(incl, g[n2:, :n2], 0.0).astype(BF16) for g in g1]
    a_rk = [jnp.where(incl, g[n2:, n2:], 0.0).astype(BF16) for g in g1]
    akv = [_dot(jnp.where(strict, g1[p][:n2, n2:], 0.0).astype(BF16), v_s[p]) for p in pairs]
    xc = [jnp.concatenate([a_s[p], akv[p]], axis=1) for p in pairs]
    steps = int(np.log2(c))
    for i in range(steps):
        if i + 1 < steps:
            res = [_dot(pw[p], jnp.concatenate([pw[p], xc[p].astype(BF16)], axis=1)) for p in pairs]
            xc = [xc[p] + res[p][:, n2:] for p in pairs]
            pw = [res[p][:, :n2].astype(BF16) for p in pairs]
        else:
            xc = [xc[p] + _dot(pw[p], xc[p].astype(BF16)) for p in pairs]
    xcb = [x.astype(BF16) for x in xc]
    r2 = [_dot(a_rb[p], xcb[p]) for p in pairs]
    ov = [r2[p][:, LANES:] + _dot(a_rk[p], v_s[p]) for p in pairs]
    mg = [_dot_tn(stack(b_e[:, sls[p]]).astype(BF16), xcb[p]) for p in pairs]
    kv = [_dot_tn(stack(k_e[:, sls[p]]).astype(BF16), v_s[p]) for p in pairs]
    for p in pairs:
        q_s = r_s[p] + r2[p][:, :LANES]
        mmat = mg[p][:, :LANES] + jnp.where(eye, p_end[:, sls[p]], 0.0)
        qm = jnp.concatenate([q_s, mmat], axis=0).astype(BF16)
        res = _dot(qm, z_ref[p].astype(BF16))
        z_ref[p] = res[n2:] + mg[p][:, LANES:] + kv[p]
        o_s = res[:n2] + ov[p]
        mean = jnp.sum(o_s, axis=1, keepdims=True) * inv_n
        dev = jnp.where(own, o_s - mean, 0.0)
        var = jnp.sum(dev * dev, axis=1, keepdims=True) * inv_n
        y = dev * lax.rsqrt(var + LNX_EPS)
        y = (y[:c] + y[c:]) * lg_ref[:, sls[p]] + lb_ref[:, sls[p]]
        out = (y + bon_ref[rows, sls[p]].astype(F32)) * g_ref[rows, sls[p]].astype(F32)
        o_ref[rows, sls[p]] = out.astype(o_ref.dtype)


def _rwkv_core_kernel(*refs):
    z_ref = refs[-1]

    @pl.when(pl.program_id(1) == 0)
    def _():
        z_ref[...] = jnp.zeros_like(z_ref)

    for sub in range(CHUNKS_PER_STEP):
        _rwkv_chunk(slice(sub * CHUNK, (sub + 1) * CHUNK), *refs)


def _rwkv_core(r, lw, k, v, kk, bb, g, bon, lg, lb, bsz, seq):
    c = CHUNK * CHUNKS_PER_STEP
    w = RWKV_WIDTH
    per_b = seq // c
    blk = pl.BlockSpec((c, w), lambda b, i: (b * per_b + i, 0))
    vec = pl.BlockSpec((1, w), lambda b, i: (0, 0))
    return pl.pallas_call(
        _rwkv_core_kernel,
        grid=(bsz, per_b),
        in_specs=[blk] * 8 + [vec, vec],
        out_specs=blk,
        out_shape=jax.ShapeDtypeStruct((bsz * seq, w), BF16),
        scratch_shapes=[pltpu.VMEM((w // LANES, LANES, LANES), F32)],
        compiler_params=_cparams(("arbitrary", "arbitrary")),
        name="rwkv_core",
    )(r, lw, k, v, kk, bb, g, bon, lg, lb)


def _rope_freqs(head_dim):
    half = head_dim // ROPE_FRACTION // 2
    return (ROPE_THETA ** (-jnp.arange(half, dtype=F32) / half)).reshape(half, 1)


def kernel(x, c, positions, w_ada, b_ada, norm1_g, w_in, q_norm_g, k_norm_g, rwkv_mu, rwkv_w0,
           rwkv_w_up, rwkv_a0, rwkv_a_up, rwkv_g_up, rwkv_k_k, rwkv_k_a, rwkv_r_k, rwkv_lnx_g,
           rwkv_lnx_b, w_out, norm2_g, w_ffn_gate, w_ffn_up, w_ffn_down):
    bsz, seq, d = x.shape
    depth = w_ada.shape[0]
    m = bsz * seq
    pos_row = positions.reshape(1, m)
    fa_col = _rope_freqs(ATT_HEAD_DIM)
    fi_col = _rope_freqs(IDX_HEAD_DIM)
    hd = RWKV_HEAD_DIM
    e = (jnp.arange(LANES)[:, None] // hd == jnp.arange(LANES)[None, :] // hd).astype(BF16)
    x2 = x.reshape(m, d)

    for l in range(depth):
        mod = _adaln(c, w_ada[l], b_ada[l])
        mod3 = mod.reshape(bsz * 6, 1, d)

        h1 = _norm(x2, norm1_g[l].reshape(1, d), mod3, seq)
        w_in_t = w_in[l].T
        proj_r = _matmul_nt(
            h1, w_in_t,
            lambda j: jnp.where(j < 3, ATT_COLS + j * RWKV_WIDTH, IN_COLS - RWKV_WIDTH),
            4, RWKV_WIDTH, "in_proj_rkv")

        none = jnp.zeros((SUBLANES, 1), F32)
        aw = ATT_WIDTH
        qt = _proj_t(h1, w_in_t, OFF_Q, aw, pos_row, q_norm_g[l].reshape(-1, 1), fa_col, "q")
        kn = _proj_t(h1, w_in_t, OFF_K, aw, pos_row, k_norm_g[l].reshape(-1, 1), fa_col, "k")
        vt = _proj_t(h1, w_in_t, OFF_V, aw, pos_row, none, none, "v")
        iqt, ik, iwt = _proj_t(h1, w_in_t, OFF_IQ, aw + LANES, pos_row, none, fi_col, "iq")
        bound = (ATT_HEAD_DIM ** 0.5 * LOG2E * BOUND_SLACK
                 * jnp.max(jnp.abs(q_norm_g[l])) * jnp.max(jnp.abs(k_norm_g[l])))
        att = _dsa(bound.reshape(1).astype(F32), qt, iqt, iwt, kn, vt, ik, bsz, seq, KEY_TILE)

        mu = rwkv_mu[l]
        w3 = 3 * RWKV_WIDTH

        n_wa = DECAY_LORA + AAA_LORA
        mu_wa = jnp.zeros((1, TAIL_TILE), F32).at[0, XW_LANE:XW_LANE + n_wa].set(mu[w3:w3 + n_wa])
        mus = [mu[0:RWKV_WIDTH].reshape(1, -1), mu[RWKV_WIDTH:2 * RWKV_WIDTH].reshape(1, -1),
               mu[2 * RWKV_WIDTH:w3].reshape(1, -1), mu_wa, mu[w3 + n_wa:].reshape(1, -1)]
        vecs = [rwkv_w0[l].reshape(1, -1), rwkv_a0[l].reshape(1, -1), rwkv_k_k[l].reshape(1, -1),
                rwkv_k_a[l].reshape(1, -1), rwkv_r_k[l].reshape(1, -1)]

        def pad_rows(wm):
            return jnp.zeros((LORA_PAD, wm.shape[1]), F32).at[:wm.shape[0]].set(wm).astype(BF16)

        r, lw, km, vv, kk, bb, g, bon = _rwkv_prep(
            proj_r, mus, vecs, pad_rows(rwkv_w_up[l]), pad_rows(rwkv_a_up[l]),
            rwkv_g_up[l].astype(BF16), e, bsz, seq)
        rw = _rwkv_core(r, lw, km, vv, kk, bb, g, bon, rwkv_lnx_g[l].reshape(1, -1),
                        rwkv_lnx_b[l].reshape(1, -1), bsz, seq)

        x2, h2 = _out_proj(att, rw, w_out[l], x2, mod3, norm2_g[l].reshape(1, d), seq)

        hglu = _ffn_glu(h2, w_ffn_gate[l], w_ffn_up[l])
        x2 = _ffn_down(hglu, w_ffn_down[l], x2, mod3, seq)
    return x2.reshape(bsz, seq, d)
```

```python
import functools

import jax
import jax.numpy as jnp
import numpy as np
from jax import lax
from jax.experimental import pallas as pl
from jax.experimental.pallas import tpu as pltpu

F32 = jnp.float32
BF16 = jnp.bfloat16
I32 = jnp.int32

D_MODEL = 2048
ATT_HEADS = 8
ATT_HEAD_DIM = 128
ATT_WIDTH = ATT_HEADS * ATT_HEAD_DIM
RWKV_WIDTH = D_MODEL - ATT_WIDTH
RWKV_HEAD_DIM = 64
RWKV_HEADS = RWKV_WIDTH // RWKV_HEAD_DIM
IDX_HEADS = 16
IDX_HEAD_DIM = 64
TOPK_MAX = 256
ROPE_THETA = 500000.0
ROPE_FRACTION = 4
DECAY_LORA = 96
AAA_LORA = 96
GATE_LORA = 256
NORM_EPS = 1e-6
LNX_EPS = 64e-5

LANES = 128
SUBLANES = 8
LORA_PAD = 128
ATT_COLS = 4 * ATT_WIDTH + IDX_HEAD_DIM + IDX_HEADS
IN_COLS = ATT_COLS + 3 * RWKV_WIDTH + DECAY_LORA + AAA_LORA + GATE_LORA
OFF_Q, OFF_K, OFF_V, OFF_IQ = 0, 1024, 2048, 3072
OFF_RR, OFF_RK, OFF_RV = 0, 1024, 2048
TAIL_TILE = 256
TAIL_COL0 = IN_COLS - 2 * TAIL_TILE
XW_LANE = IN_COLS - GATE_LORA - AAA_LORA - DECAY_LORA - TAIL_COL0
XA_LANE = XW_LANE + DECAY_LORA
OFF_WA = 3 * RWKV_WIDTH + TAIL_COL0 - (IN_COLS - RWKV_WIDTH)
OFF_XG = OFF_WA + TAIL_TILE

LOG2E = 1.4426950408889634
V_ROWS = 144
KEY_TILE = 512
CHUNKS_PER_STEP = 4
CHUNK = 64
VMEM_LIMIT = 56 * 1024 * 1024


def _cparams(sem):
    return pltpu.CompilerParams(dimension_semantics=sem, vmem_limit_bytes=VMEM_LIMIT)


def _dot(a, b):
    return jnp.dot(a, b, preferred_element_type=F32)


def _dot_nt(a, b):
    return lax.dot_general(a, b, (((1,), (1,)), ((), ())), preferred_element_type=F32)


def _dot_tn(a, b):
    return lax.dot_general(a, b, (((0,), (0,)), ((), ())), preferred_element_type=F32)


def _split2(x):
    hi = x.astype(BF16)
    lo = (x - hi.astype(F32)).astype(BF16)
    return hi, lo


def _dot_hi(x, w):
    hi, lo = _split2(x)
    return _dot(hi, w) + _dot(lo, w)


def _sigmoid(x):
    return 1.0 / (1.0 + jnp.exp(-x))


def _adaln_kernel(c_ref, w_ref, b_ref, o_ref):
    c = c_ref[...]
    ca = c * _sigmoid(c)
    o_ref[...] = _dot(ca.astype(BF16), w_ref[...].astype(BF16)) + b_ref[...]


def _adaln(c, w, b):
    bsz, d = c.shape
    n = w.shape[1]
    rows = 8
    cp = jnp.zeros((rows, d), F32).at[:bsz].set(c)
    tn = 1024
    out = pl.pallas_call(
        _adaln_kernel,
        grid=(n // tn,),
        in_specs=[pl.BlockSpec((rows, d), lambda j: (0, 0)),
                  pl.BlockSpec((d, tn), lambda j: (0, j)),
                  pl.BlockSpec((1, tn), lambda j: (0, j))],
        out_specs=pl.BlockSpec((rows, tn), lambda j: (0, j)),
        out_shape=jax.ShapeDtypeStruct((rows, n), F32),
        compiler_params=_cparams(("arbitrary",)),
        name="adaln",
    )(cp, w, b.reshape(1, n))
    return out[:bsz]


def _norm_mod(x, g, sc, sh):
    ms = jnp.mean(x * x, axis=-1, keepdims=True)
    y = x * lax.rsqrt(ms + NORM_EPS)
    y = y * g
    return y * (1.0 + sc) + sh


def _norm_kernel(x_ref, g_ref, sc_ref, sh_ref, o_ref):
    o_ref[...] = _norm_mod(x_ref[...], g_ref[...], sc_ref[...], sh_ref[...]).astype(BF16)


def _norm(x2, g, mod3, seq, tm=512):
    m, d = x2.shape
    per_b = seq // tm
    return pl.pallas_call(
        _norm_kernel,
        grid=(m // tm,),
        in_specs=[pl.BlockSpec((tm, d), lambda i: (i, 0)),
                  pl.BlockSpec((1, d), lambda i: (0, 0)),
                  pl.BlockSpec((None, 1, d), lambda i: ((i // per_b) * 6 + 1, 0, 0)),
                  pl.BlockSpec((None, 1, d), lambda i: ((i // per_b) * 6 + 0, 0, 0))],
        out_specs=pl.BlockSpec((tm, d), lambda i: (i, 0)),
        out_shape=jax.ShapeDtypeStruct((m, d), BF16),
        compiler_params=_cparams(("arbitrary",)),
        name="norm1",
    )(x2, g, mod3, mod3)


def _mm_nt_kernel(h_ref, wt_ref, o_ref, wb_ref):
    @pl.when(pl.program_id(1) == 0)
    def _():
        wb_ref[...] = wt_ref[...].astype(BF16)

    o_ref[...] = _dot_nt(h_ref[...], wb_ref[...])


def _matmul_nt(h, wt, col_of_tile, n_tiles, tn, name, tm=1024):
    m, d = h.shape
    return pl.pallas_call(
        _mm_nt_kernel,
        grid=(n_tiles, m // tm),
        in_specs=[pl.BlockSpec((tm, d), lambda j, i: (i, 0)),
                  pl.BlockSpec((pl.Element(tn), pl.Element(d)),
                               lambda j, i: (pl.multiple_of(col_of_tile(j), SUBLANES), 0))],
        out_specs=pl.BlockSpec((tm, tn), lambda j, i: (i, j)),
        out_shape=jax.ShapeDtypeStruct((m, n_tiles * tn), F32),
        scratch_shapes=[pltpu.VMEM((tn, d), BF16)],
        compiler_params=_cparams(("arbitrary", "arbitrary")),
        name=name,
    )(h, wt)


def _ffn_glu_kernel(h_ref, wg_ref, wu_ref, o_ref, wgb_ref, wub_ref):
    @pl.when(pl.program_id(1) == 0)
    def _():
        wgb_ref[...] = wg_ref[...].astype(BF16)
        wub_ref[...] = wu_ref[...].astype(BF16)

    h = h_ref[...]
    a = _dot(h, wgb_ref[...])
    u = _dot(h, wub_ref[...])
    o_ref[...] = (a * _sigmoid(a) * u).astype(o_ref.dtype)


def _ffn_glu(h, wg, wu, tm=1024, tn=512):
    m, d = h.shape
    tm = min(tm, m)
    n = wg.shape[1]
    return pl.pallas_call(
        _ffn_glu_kernel,
        grid=(n // tn, m // tm),
        in_specs=[pl.BlockSpec((tm, d), lambda j, i: (i, 0)),
                  pl.BlockSpec((d, tn), lambda j, i: (0, j)),
                  pl.BlockSpec((d, tn), lambda j, i: (0, j))],
        out_specs=pl.BlockSpec((tm, tn), lambda j, i: (i, j)),
        out_shape=jax.ShapeDtypeStruct((m, n), BF16),
        scratch_shapes=[pltpu.VMEM((d, tn), BF16)] * 2,
        compiler_params=_cparams(("arbitrary", "arbitrary")),
        name="ffn_glu",
    )(h, wg, wu)


def _out_proj_kernel(a_ref, r_ref, wa_ref, wr_ref, x_ref, gt_ref, g_ref, sc_ref, sh_ref,
                     x1_ref, h2_ref, wab_ref, wrb_ref):
    @pl.when(pl.program_id(0) == 0)
    def _():
        wab_ref[...] = wa_ref[...].astype(BF16)
        wrb_ref[...] = wr_ref[...].astype(BF16)

    mixed = _dot(a_ref[...], wab_ref[...]) + _dot(r_ref[...], wrb_ref[...])
    x1 = x_ref[...] + gt_ref[...] * mixed
    x1_ref[...] = x1
    h2_ref[...] = _norm_mod(x1, g_ref[...], sc_ref[...], sh_ref[...]).astype(BF16)


def _out_proj(att, rwkv, w_out, x2, mod3, g2, seq, tm=512):
    m, ka = att.shape
    kr = rwkv.shape[1]
    n = w_out.shape[1]
    per_b = seq // tm
    once = pl.Buffered(1)

    def mod_row(j):
        return pl.BlockSpec((None, 1, n), lambda i: ((i // per_b) * 6 + j, 0, 0))

    row = pl.BlockSpec((tm, n), lambda i: (i, 0))
    return pl.pallas_call(
        _out_proj_kernel,
        grid=(m // tm,),
        in_specs=[pl.BlockSpec((tm, ka), lambda i: (i, 0)),
                  pl.BlockSpec((tm, kr), lambda i: (i, 0)),
                  pl.BlockSpec((ka, n), lambda i: (0, 0), pipeline_mode=once),
                  pl.BlockSpec((kr, n), lambda i: (ka // kr, 0), pipeline_mode=once),
                  row, mod_row(2),
                  pl.BlockSpec((1, n), lambda i: (0, 0)), mod_row(4), mod_row(3)],
        out_specs=[row, row],
        out_shape=[jax.ShapeDtypeStruct((m, n), F32), jax.ShapeDtypeStruct((m, n), BF16)],
        scratch_shapes=[pltpu.VMEM((ka, n), BF16), pltpu.VMEM((kr, n), BF16)],
        compiler_params=_cparams(("arbitrary",)),
        name="out_proj",
    )(att, rwkv, w_out, w_out, x2, mod3, g2, mod3, mod3)


def _ffn_down_kernel(h_ref, w_ref, x_ref, gt_ref, o_ref, wb_ref):
    @pl.when(pl.program_id(1) == 0)
    def _():
        wb_ref[...] = w_ref[...].astype(BF16)

    o_ref[...] = x_ref[...] + gt_ref[...] * _dot(h_ref[...], wb_ref[...])


def _ffn_down(h, w, x2, mod3, seq, tm=256, tn=1024):
    m, kdim = h.shape
    n = w.shape[1]
    per_b = seq // tm
    return pl.pallas_call(
        _ffn_down_kernel,
        grid=(n // tn, m // tm),
        in_specs=[pl.BlockSpec((tm, kdim), lambda j, i: (i, 0)),
                  pl.BlockSpec((kdim, tn), lambda j, i: (0, j), pipeline_mode=pl.Buffered(1)),
                  pl.BlockSpec((tm, tn), lambda j, i: (i, j)),
                  pl.BlockSpec((None, 1, tn), lambda j, i: ((i // per_b) * 6 + 5, 0, j))],
        out_specs=pl.BlockSpec((tm, tn), lambda j, i: (i, j)),
        out_shape=jax.ShapeDtypeStruct((m, n), F32),
        scratch_shapes=[pltpu.VMEM((kdim, tn), BF16)],
        compiler_params=_cparams(("arbitrary", "arbitrary")),
        name="ffn_down",
    )(h, w, x2, mod3)


def _proj_t_kernel(wt_ref, h_ref, pos_ref, g_ref, f_ref, *refs, mode):
    *o_refs, wb_ref = refs
    o_ref = o_refs[0]

    @pl.when(pl.program_id(0) == 0)
    def _():
        wb_ref[...] = wt_ref[...].astype(BF16)

    yt = _dot_nt(wb_ref[...], h_ref[...])
    tm = yt.shape[1]
    if mode == "v":
        pad_row = lax.broadcasted_iota(I32, (V_ROWS - LANES, KEY_TILE), 0)
        ones_rows = jnp.where(pad_row == 0, 1.0, 0.0).astype(BF16)
        for h in range(ATT_HEADS):
            for j in range(tm // KEY_TILE):
                tile = yt[h * LANES:(h + 1) * LANES, j * KEY_TILE:(j + 1) * KEY_TILE]
                o_ref[h, j, 0:LANES, :] = tile.astype(BF16)
                o_ref[h, j, LANES:V_ROWS, :] = ones_rows
        return

    head = ATT_HEAD_DIM if mode in ("q", "k") else IDX_HEAD_DIM
    half = head // ROPE_FRACTION // 2
    ang = f_ref[...] * pos_ref[...].astype(F32)
    cos, sin = jnp.cos(ang), jnp.sin(ang)

    def rotary(y):
        x1, x2 = y[0:half], y[half:2 * half]
        return jnp.concatenate([x1 * cos - x2 * sin, x2 * cos + x1 * sin, y[2 * half:]], axis=0)

    for hd in range(ATT_WIDTH // head):
        y = yt[hd * head:(hd + 1) * head, :]
        if mode in ("q", "k"):
            ms = jnp.mean(y * y, axis=0, keepdims=True)
            y = y * lax.rsqrt(ms + NORM_EPS) * g_ref[...]
        y = rotary(y)
        if mode == "q":
            o_ref[hd] = (y * ((ATT_HEAD_DIM ** -0.5) * LOG2E)).astype(BF16)
        elif mode == "iq":
            rows = slice((hd % 2) * head, (hd % 2 + 1) * head)
            o_ref[hd // 2, rows, :] = (y * (IDX_HEAD_DIM ** -0.5)).astype(BF16)
        else:
            o_ref[:, hd * LANES:(hd + 1) * LANES] = y.T.astype(BF16)
    if mode == "iq":
        rest = yt[ATT_WIDTH:, :]
        ikt = jnp.concatenate([rotary(rest[0:head]), rest[head:]], axis=0)
        o_refs[1][...] = ikt.T.astype(BF16)
        o_refs[2][...] = rest[head:head + IDX_HEADS, :] * (IDX_HEADS ** -0.5)


def _proj_t(h, wt, col0, n, pos_row, g_col, f_col, mode, tm=1024):
    m, d = h.shape
    tm = min(tm, m)
    if mode == "v":
        out_specs = [pl.BlockSpec((ATT_HEADS, tm // KEY_TILE, V_ROWS, KEY_TILE),
                                  lambda i: (0, i, 0, 0))]
        out_shape = [jax.ShapeDtypeStruct((ATT_HEADS, m // KEY_TILE, V_ROWS, KEY_TILE), BF16)]
    elif mode in ("q", "iq"):
        out_specs = [pl.BlockSpec((ATT_WIDTH // LANES, LANES, tm), lambda i: (0, 0, i))]
        out_shape = [jax.ShapeDtypeStruct((ATT_WIDTH // LANES, LANES, m), BF16)]
        if mode == "iq":
            out_specs += [pl.BlockSpec((tm, LANES), lambda i: (i, 0)),
                          pl.BlockSpec((IDX_HEADS, tm), lambda i: (0, i))]
            out_shape += [jax.ShapeDtypeStruct((m, LANES), BF16),
                          jax.ShapeDtypeStruct((IDX_HEADS, m), F32)]
    else:
        out_specs = [pl.BlockSpec((tm, n), lambda i: (i, 0))]
        out_shape = [jax.ShapeDtypeStruct((m, n), BF16)]
    out = pl.pallas_call(
        functools.partial(_proj_t_kernel, mode=mode),
        grid=(m // tm,),
        in_specs=[pl.BlockSpec((pl.Element(n), pl.Element(d)), lambda i: (col0, 0)),
                  pl.BlockSpec((tm, d), lambda i: (i, 0)),
                  pl.BlockSpec((1, tm), lambda i: (0, i)),
                  pl.BlockSpec(g_col.shape, lambda i: (0, 0)),
                  pl.BlockSpec(f_col.shape, lambda i: (0, 0))],
        out_specs=out_specs,
        out_shape=out_shape,
        scratch_shapes=[pltpu.VMEM((n, d), BF16)],
        compiler_params=_cparams(("arbitrary",)),
        name="in_proj_" + mode,
    )(wt, h, pos_row, g_col, f_col)
    return out if mode == "iq" else out[0]


NEG_BIG = -1e30


INT_MIN = -2 ** 31
INT_MAX = 2 ** 31 - 1
MAGNITUDE_BITS = 0x7FFFFFFF
KEY_NEG_INF = -2139095041
BOUND_SLACK = 1.02
MAX_STATIC_SHIFT = 60.0
COUNT_ROWS = 32


def _key_to_float(key):
    bits = key ^ ((key >> 31) & jnp.int32(MAGNITUDE_BITS))
    return lax.bitcast_convert_type(bits, F32)


def _dsa_kernel(bound_ref, qt_ref, iqt_ref, iwt_ref, k_ref, vt_ref, ik_ref, o_ref,
                sc_ref, m_ref, acc_ref, last_ref, *, tq, tk, topk, index_bits):
    qi = pl.program_id(1)
    n_kb = (qi * tq + tq - 1) // tk + 1
    key0 = lax.broadcasted_iota(I32, (tk, tq), 0)
    qidx = qi * tq + lax.broadcasted_iota(I32, (tk, tq), 1)
    iw = iwt_ref[...]

    def score_body(kb, carry):
        start = pl.multiple_of(kb * tk, tk)
        ikb = ik_ref[pl.ds(start, tk), 0:IDX_HEAD_DIM]
        s = jnp.zeros((tk, tq), F32)
        for h in range(IDX_HEADS):
            off = (h % 2) * IDX_HEAD_DIM
            d = _dot(ikb, iqt_ref[h // 2, off:off + IDX_HEAD_DIM, :])
            s = s + jnp.maximum(d, 0.0) * iw[h:h + 1, :]
        sc_ref[kb] = jnp.where(kb * tk + key0 <= qidx, s, -jnp.inf)
        return carry

    lax.fori_loop(0, n_kb, score_body, 0)

    def count(pred):
        def cnt_body(kb, acc):
            hit = pred(kb).reshape(tk // COUNT_ROWS, COUNT_ROWS, tq)
            for r in range(tk // COUNT_ROWS):
                acc = jnp.where(hit[r], acc + 1.0, acc)
            return acc

        acc = lax.fori_loop(0, n_kb, cnt_body, jnp.zeros((COUNT_ROWS, tq), F32))
        return jnp.sum(acc, axis=0, keepdims=True)

    def bit_body(i, cand):
        trial = cand ^ lax.shift_left(jnp.int32(1), 31 - i)
        trial_f = _key_to_float(trial)
        cnt = count(lambda kb: sc_ref[kb] >= trial_f)
        return jnp.where(cnt >= topk, trial, cand)

    cand = lax.fori_loop(0, 32, bit_body, jnp.full((1, tq), INT_MIN, I32))
    tau = _key_to_float(jnp.maximum(cand, jnp.int32(KEY_NEG_INF)))

    last_ref[...] = jnp.full_like(last_ref, INT_MAX)
    n_ge = count(lambda kb: sc_ref[kb] >= tau)

    @pl.when(jnp.max(n_ge) > topk)
    def _():
        need = topk - count(lambda kb: sc_ref[kb] > tau)

        def idx_body(i, last):
            trial = last | lax.shift_left(jnp.int32(1), index_bits - 1 - i)
            below = count(lambda kb: (sc_ref[kb] == tau) & (kb * tk + key0 < trial))
            return jnp.where(below < need, trial, last)

        last_ref[...] = lax.fori_loop(0, index_bits, idx_body, jnp.zeros((1, tq), I32))

    acc_ref[...] = jnp.zeros_like(acc_ref)
    bound = bound_ref[0]

    def logits(kb, h):
        start = pl.multiple_of(kb * tk, tk)
        return _dot(k_ref[pl.ds(start, tk), h * LANES:(h + 1) * LANES], qt_ref[h])

    last = last_ref[...]

    def selected(kb):
        s = sc_ref[kb]
        kidx = kb * tk + key0
        return ((s > tau) | ((s == tau) & (kidx <= last))) & (kidx <= qidx)

    @pl.when(bound <= MAX_STATIC_SHIFT)
    def _():
        def att_body(kb, carry):
            bias = jnp.where(selected(kb), -bound, -jnp.inf)
            for h in range(ATT_HEADS):
                p = jnp.exp2(logits(kb, h) + bias)
                acc_ref[h] += _dot(vt_ref[h, kb], p.astype(BF16))
            return carry

        lax.fori_loop(0, n_kb, att_body, 0)

    @pl.when(bound > MAX_STATIC_SHIFT)
    def _():
        m_ref[...] = jnp.full_like(m_ref, NEG_BIG)

        def att_body(kb, carry):
            bias = jnp.where(selected(kb), 0.0, -jnp.inf)
            for h in range(ATT_HEADS):
                s = logits(kb, h) + bias
                m_prev = m_ref[h]
                m_next = jnp.maximum(m_prev, jnp.max(s, axis=0, keepdims=True))
                p = jnp.exp2(s - m_next)
                alpha = jnp.exp2(m_prev - m_next)
                acc_ref[h] = alpha * acc_ref[h] + _dot(vt_ref[h, kb], p.astype(BF16))
                m_ref[h] = m_next
            return carry

        lax.fori_loop(0, n_kb, att_body, 0)

    for h in range(ATT_HEADS):
        out = acc_ref[h, 0:LANES, :] / acc_ref[h, LANES:LANES + 1, :]
        o_ref[:, h * LANES:(h + 1) * LANES] = out.T.astype(o_ref.dtype)


def _dsa(bound, qt, iqt, iwt, k, vt, ik, bsz, seq, tk, tq=512):
    tq = min(tq, seq)
    topk = min(TOPK_MAX, seq // 4)
    w = ATT_WIDTH
    nq = seq // tq
    nkb = seq // tk
    npair = iqt.shape[0]
    kern = functools.partial(_dsa_kernel, tq=tq, tk=tk, topk=topk,
                             index_bits=max(1, (seq - 1).bit_length()))
    return pl.pallas_call(
        kern,
        grid=(bsz, nq),
        in_specs=[pl.BlockSpec(memory_space=pltpu.SMEM),
                  pl.BlockSpec((ATT_HEADS, LANES, tq), lambda b, i: (0, 0, b * nq + i)),
                  pl.BlockSpec((npair, LANES, tq), lambda b, i: (0, 0, b * nq + i)),
                  pl.BlockSpec((IDX_HEADS, tq), lambda b, i: (0, b * nq + i)),
                  pl.BlockSpec((seq, w), lambda b, i: (b, 0)),
                  pl.BlockSpec((ATT_HEADS, nkb, V_ROWS, tk), lambda b, i: (0, b, 0, 0)),
                  pl.BlockSpec((seq, LANES), lambda b, i: (b, 0))],
        out_specs=pl.BlockSpec((tq, w), lambda b, i: (b * nq + i, 0)),
        out_shape=jax.ShapeDtypeStruct((bsz * seq, w), BF16),
        scratch_shapes=[pltpu.VMEM((nkb, tk, tq), F32),
                        pltpu.VMEM((ATT_HEADS, 1, tq), F32),
                        pltpu.VMEM((ATT_HEADS, V_ROWS, tq), F32),
                        pltpu.VMEM((1, tq), I32)],
        compiler_params=_cparams(("arbitrary", "arbitrary")),
        name="dsa",
    )(bound, qt, iqt, iwt, k, vt, ik)


def _rwkv_prep_kernel(rr_ref, rk_ref, rv_ref, wa_ref, xg_ref,
                      mr_ref, mk_ref, mv_ref, mwa_ref, mg_ref,
                      w0_ref, a0_ref, kk_ref, ka_ref, rkp_ref,
                      wup_ref, aup_ref, gup_ref, e_ref,
                      r_o, lw_o, k_o, v_o, kkn_o, bb_o, g_o, bon_o,
                      c_r, c_k, c_v, c_wa, c_g, *, tm):
    first = pl.program_id(1) == 0

    for carry_ref in (c_r, c_k, c_v, c_wa, c_g):
        @pl.when(first)
        def _(carry_ref=carry_ref):
            carry_ref[...] = jnp.zeros_like(carry_ref)

    def shift(y_ref, carry_ref, mu_ref, cols=slice(None)):
        y = y_ref[:, cols]
        prev_last = carry_ref[7:8, cols]
        rolled = pltpu.roll(y, 1, 0)
        rows = lax.broadcasted_iota(I32, y.shape, 0)
        yprev = jnp.where(rows == 0, prev_last, rolled)
        carry_ref[:, cols] = y[tm - 8:tm, :]
        return y + (yprev - y) * mu_ref[:, cols]

    wa = shift(wa_ref, c_wa, mwa_ref)
    xw = pltpu.roll(wa, TAIL_TILE - XW_LANE, 1)[:, :LORA_PAD]
    xa = pltpu.roll(wa, TAIL_TILE - XA_LANE, 1)[:, :LORA_PAD]
    xg = shift(xg_ref, c_g, mg_ref)
    w_raw = w0_ref[...] + _dot(jnp.tanh(xw).astype(BF16), wup_ref[...])
    a_pre = a0_ref[...] + _dot(xa.astype(BF16), aup_ref[...])
    g_o[...] = _dot(_sigmoid(xg).astype(BF16), gup_ref[...]).astype(BF16)

    e2 = e_ref[...]
    for p in range(RWKV_WIDTH // LANES):
        sl = slice(p * LANES, (p + 1) * LANES)
        r = shift(rr_ref, c_r, mr_ref, sl)
        k = shift(rk_ref, c_k, mk_ref, sl)
        v = shift(rv_ref, c_v, mv_ref, sl)
        z = -w_raw[:, sl]
        softplus = jnp.maximum(z, 0.0) + jnp.log(1.0 + jnp.exp(-jnp.abs(z)))
        lw_o[:, sl] = -jnp.exp(-softplus - 0.5)
        a = _sigmoid(a_pre[:, sl])
        kk = k * kk_ref[:, sl]
        ss = _dot((kk * kk).astype(BF16), e2)
        kk = kk / jnp.maximum(jnp.sqrt(ss), 1e-12)
        kmod = k * (1.0 + (a - 1.0) * ka_ref[:, sl])
        r_o[:, sl] = r.astype(BF16)
        k_o[:, sl] = kmod.astype(BF16)
        v_o[:, sl] = v.astype(BF16)
        kkn_o[:, sl] = kk.astype(BF16)
        bb_o[:, sl] = (kk * a).astype(BF16)
        rkr = _dot((r * kmod * rkp_ref[:, sl]).astype(BF16), e2)
        bon_o[:, sl] = (rkr * v).astype(BF16)


def _rwkv_prep(proj_r, mus, vecs, wup, aup, gup, e, bsz, seq, tm=256):
    w = RWKV_WIDTH
    per_b = seq // tm

    def wide(off):
        return pl.BlockSpec((tm, w), lambda b, i: (b * per_b + i, off // w))

    def tail(off):
        return pl.BlockSpec((tm, TAIL_TILE), lambda b, i: (b * per_b + i, off // TAIL_TILE))

    def const(shape):
        return pl.BlockSpec(shape, lambda b, i: (0, 0))

    out_blk = pl.BlockSpec((tm, w), lambda b, i: (b * per_b + i, 0))
    kern = functools.partial(_rwkv_prep_kernel, tm=tm)
    return pl.pallas_call(
        kern,
        grid=(bsz, per_b),
        in_specs=[wide(OFF_RR), wide(OFF_RK), wide(OFF_RV), tail(OFF_WA), tail(OFF_XG),
                  const((1, w)), const((1, w)), const((1, w)),
                  const((1, TAIL_TILE)), const((1, GATE_LORA)),
                  const((1, w)), const((1, w)), const((1, w)), const((1, w)), const((1, w)),
                  const((LORA_PAD, w)), const((LORA_PAD, w)), const((GATE_LORA, w)),
                  const((LANES, LANES))],
        out_specs=[out_blk] * 8,
        out_shape=[jax.ShapeDtypeStruct((bsz * seq, w), F32 if i == 1 else BF16) for i in range(8)],
        scratch_shapes=[pltpu.VMEM((8, w), F32)] * 3
        + [pltpu.VMEM((8, TAIL_TILE), F32), pltpu.VMEM((8, GATE_LORA), F32)],
        compiler_params=_cparams(("arbitrary", "arbitrary")),
        name="rwkv_prep",
    )(proj_r, proj_r, proj_r, proj_r, proj_r, *mus, *vecs, wup, aup, gup, e)


def _rwkv_chunk(rows, r_ref, lw_ref, k_ref, v_ref, kk_ref, bb_ref, g_ref, bon_ref, lg_ref, lb_ref,
                o_ref, z_ref):
    c = CHUNK
    lw = lw_ref[rows, :]
    tri = jnp.where(lax.broadcasted_iota(I32, (c, c), 1) <= lax.broadcasted_iota(I32, (c, c), 0),
                    1.0, 0.0).astype(BF16)
    hi = lw.astype(BF16)
    rem = lw - hi.astype(F32)
    mid = rem.astype(BF16)
    lo = (rem - mid.astype(F32)).astype(BF16)
    cum = _dot(tri, hi) + _dot(tri, mid) + _dot(tri, lo)
    p_in = jnp.exp(cum)
    p_ex = jnp.exp(cum - lw)
    p_inv = jnp.exp(-cum)
    p_end = p_in[c - 1:c, :]
    a_t = -kk_ref[rows, :].astype(F32) * p_ex
    r_t = r_ref[rows, :].astype(F32) * p_in
    b_h = bb_ref[rows, :].astype(F32) * p_inv
    k_h = k_ref[rows, :].astype(F32) * p_inv
    b_e = b_h * p_end
    k_e = k_h * p_end
    v = v_ref[rows, :].astype(F32)

    n2 = 2 * c
    lane = lax.broadcasted_iota(I32, (1, LANES), 1)
    head0 = lane < RWKV_HEAD_DIM
    ri = lax.broadcasted_iota(I32, (n2, n2), 0)
    ci = lax.broadcasted_iota(I32, (n2, n2), 1)
    same = (ri >= c) == (ci >= c)
    strict = same & (ci < ri)
    incl = same & (ci <= ri)
    eye = ri == ci
    own = ((lax.broadcasted_iota(I32, (n2, LANES), 0) >= c)
           == (lax.broadcasted_iota(I32, (n2, LANES), 1) >= RWKV_HEAD_DIM))
    inv_n = 1.0 / RWKV_HEAD_DIM

    def stack(y):
        return jnp.concatenate([jnp.where(head0, y, 0.0), jnp.where(head0, 0.0, y)], axis=0)

    pairs = range(RWKV_WIDTH // LANES)
    sls = [slice(p * LANES, (p + 1) * LANES) for p in pairs]
    a_s = [stack(a_t[:, sl]) for sl in sls]
    r_s = [stack(r_t[:, sl]) for sl in sls]
    v_s = [stack(v[:, sl]).astype(BF16) for sl in sls]
    g1 = [_dot_nt(jnp.concatenate([a_s[p], r_s[p]], axis=0).astype(BF16),
                  jnp.concatenate([stack(b_h[:, sls[p]]), stack(k_h[:, sls[p]])], axis=0).astype(BF16))
          for p in pairs]
    pw = [jnp.where(strict, g[:n2, :n2], 0.0).astype(BF16) for g in g1]
    a_rb = [jnp.where(incl, g[n2:, :n2], 0.0).astype(BF16) for g in g1]
    a_rk = [jnp.where(incl, g[n2:, n2:], 0.0).astype(BF16) for g in g1]
    akv = [_dot(jnp.where(strict, g1[p][:n2, n2:], 0.0).astype(BF16), v_s[p]) for p in pairs]
    xc = [jnp.concatenate([a_s[p], akv[p]], axis=1) for p in pairs]
    steps = int(np.log2(c))
    for i in range(steps):
        if i + 1 < steps:
            res = [_dot(pw[p], jnp.concatenate([pw[p], xc[p].astype(BF16)], axis=1)) for p in pairs]
            xc = [xc[p] + res[p][:, n2:] for p in pairs]
            pw = [res[p][:, :n2].astype(BF16) for p in pairs]
        else:
            xc = [xc[p] + _dot(pw[p], xc[p].astype(BF16)) for p in pairs]
    xcb = [x.astype(BF16) for x in xc]
    r2 = [_dot(a_rb[p], xcb[p]) for p in pairs]
    ov = [r2[p][:, LANES:] + _dot(a_rk[p], v_s[p]) for p in pairs]
    mg = [_dot_tn(stack(b_e[:, sls[p]]).astype(BF16), xcb[p]) for p in pairs]
    kv = [_dot_tn(stack(k_e[:, sls[p]]).astype(BF16), v_s[p]) for p in pairs]
    for p in pairs:
        q_s = r_s[p] + r2[p][:, :LANES]
        mmat = mg[p][:, :LANES] + jnp.where(eye, p_end[:, sls[p]], 0.0)
        qm = jnp.concatenate([q_s, mmat], axis=0).astype(BF16)
        res = _dot(qm, z_ref[p].astype(BF16))
        z_ref[p] = res[n2:] + mg[p][:, LANES:] + kv[p]
        o_s = res[:n2] + ov[p]
        mean = jnp.sum(o_s, axis=1, keepdims=True) * inv_n
        dev = jnp.where(own, o_s - mean, 0.0)
        var = jnp.sum(dev * dev, axis=1, keepdims=True) * inv_n
        y = dev * lax.rsqrt(var + LNX_EPS)
        y = (y[:c] + y[c:]) * lg_ref[:, sls[p]] + lb_ref[:, sls[p]]
        out = (y + bon_ref[rows, sls[p]].astype(F32)) * g_ref[rows, sls[p]].astype(F32)
        o_ref[rows, sls[p]] = out.astype(o_ref.dtype)


def _rwkv_core_kernel(*refs):
    z_ref = refs[-1]

    @pl.when(pl.program_id(1) == 0)
    def _():
        z_ref[...] = jnp.zeros_like(z_ref)

    for sub in range(CHUNKS_PER_STEP):
        _rwkv_chunk(slice(sub * CHUNK, (sub + 1) * CHUNK), *refs)


def _rwkv_core(r, lw, k, v, kk, bb, g, bon, lg, lb, bsz, seq):
    c = CHUNK * CHUNKS_PER_STEP
    w = RWKV_WIDTH
    per_b = seq // c
    blk = pl.BlockSpec((c, w), lambda b, i: (b * per_b + i, 0))
    vec = pl.BlockSpec((1, w), lambda b, i: (0, 0))
    return pl.pallas_call(
        _rwkv_core_kernel,
        grid=(bsz, per_b),
        in_specs=[blk] * 8 + [vec, vec],
        out_specs=blk,
        out_shape=jax.ShapeDtypeStruct((bsz * seq, w), BF16),
        scratch_shapes=[pltpu.VMEM((w // LANES, LANES, LANES), F32)],
        compiler_params=_cparams(("arbitrary", "arbitrary")),
        name="rwkv_core",
    )(r, lw, k, v, kk, bb, g, bon, lg, lb)


def _rope_freqs(head_dim):
    half = head_dim // ROPE_FRACTION // 2
    return (ROPE_THETA ** (-jnp.arange(half, dtype=F32) / half)).reshape(half, 1)


def kernel(x, c, positions, w_ada, b_ada, norm1_g, w_in, q_norm_g, k_norm_g, rwkv_mu, rwkv_w0,
           rwkv_w_up, rwkv_a0, rwkv_a_up, rwkv_g_up, rwkv_k_k, rwkv_k_a, rwkv_r_k, rwkv_lnx_g,
           rwkv_lnx_b, w_out, norm2_g, w_ffn_gate, w_ffn_up, w_ffn_down):
    bsz, seq, d = x.shape
    depth = w_ada.shape[0]
    m = bsz * seq
    pos_row = positions.reshape(1, m)
    fa_col = _rope_freqs(ATT_HEAD_DIM)
    fi_col = _rope_freqs(IDX_HEAD_DIM)
    hd = RWKV_HEAD_DIM
    e = (jnp.arange(LANES)[:, None] // hd == jnp.arange(LANES)[None, :] // hd).astype(BF16)
    x2 = x.reshape(m, d)

    for l in range(depth):
        mod = _adaln(c, w_ada[l], b_ada[l])
        mod3 = mod.reshape(bsz * 6, 1, d)

        h1 = _norm(x2, norm1_g[l].reshape(1, d), mod3, seq)
        w_in_t = w_in[l].T
        proj_r = _matmul_nt(
            h1, w_in_t,
            lambda j: jnp.where(j < 3, ATT_COLS + j * RWKV_WIDTH, IN_COLS - RWKV_WIDTH),
            4, RWKV_WIDTH, "in_proj_rkv")

        none = jnp.zeros((SUBLANES, 1), F32)
        aw = ATT_WIDTH
        qt = _proj_t(h1, w_in_t, OFF_Q, aw, pos_row, q_norm_g[l].reshape(-1, 1), fa_col, "q")
        kn = _proj_t(h1, w_in_t, OFF_K, aw, pos_row, k_norm_g[l].reshape(-1, 1), fa_col, "k")
        vt = _proj_t(h1, w_in_t, OFF_V, aw, pos_row, none, none, "v")
        iqt, ik, iwt = _proj_t(h1, w_in_t, OFF_IQ, aw + LANES, pos_row, none, fi_col, "iq")
        bound = (ATT_HEAD_DIM ** 0.5 * LOG2E * BOUND_SLACK
                 * jnp.max(jnp.abs(q_norm_g[l])) * jnp.max(jnp.abs(k_norm_g[l])))
        att = _dsa(bound.reshape(1).astype(F32), qt, iqt, iwt, kn, vt, ik, bsz, seq, KEY_TILE)

        mu = rwkv_mu[l]
        w3 = 3 * RWKV_WIDTH

        n_wa = DECAY_LORA + AAA_LORA
        mu_wa = jnp.zeros((1, TAIL_TILE), F32).at[0, XW_LANE:XW_LANE + n_wa].set(mu[w3:w3 + n_wa])
        mus = [mu[0:RWKV_WIDTH].reshape(1, -1), mu[RWKV_WIDTH:2 * RWKV_WIDTH].reshape(1, -1),
               mu[2 * RWKV_WIDTH:w3].reshape(1, -1), mu_wa, mu[w3 + n_wa:].reshape(1, -1)]
        vecs = [rwkv_w0[l].reshape(1, -1), rwkv_a0[l].reshape(1, -1), rwkv_k_k[l].reshape(1, -1),
                rwkv_k_a[l].reshape(1, -1), rwkv_r_k[l].reshape(1, -1)]

        def pad_rows(wm):
            return jnp.zeros((LORA_PAD, wm.shape[1]), F32).at[:wm.shape[0]].set(wm).astype(BF16)

        r, lw, km, vv, kk, bb, g, bon = _rwkv_prep(
            proj_r, mus, vecs, pad_rows(rwkv_w_up[l]), pad_rows(rwkv_a_up[l]),
            rwkv_g_up[l].astype(BF16), e, bsz, seq)
        rw = _rwkv_core(r, lw, km, vv, kk, bb, g, bon, rwkv_lnx_g[l].reshape(1, -1),
                        rwkv_lnx_b[l].reshape(1, -1), bsz, seq)

        x2, h2 = _out_proj(att, rw, w_out[l], x2, mod3, norm2_g[l].reshape(1, d), seq)

        hglu = _ffn_glu(h2, w_ffn_gate[l], w_ffn_up[l])
        x2 = _ffn_down(hglu, w_ffn_down[l], x2, mod3, seq)
    return x2.reshape(bsz, seq, d)
```

```python
import functools

import jax
import jax.numpy as jnp
import numpy as np
from jax import lax
from jax.experimental import pallas as pl
from jax.experimental.pallas import tpu as pltpu

F32 = jnp.float32
BF16 = jnp.bfloat16
I32 = jnp.int32

D_MODEL = 2048
ATT_HEADS = 8
ATT_HEAD_DIM = 128
ATT_WIDTH = ATT_HEADS * ATT_HEAD_DIM
RWKV_WIDTH = D_MODEL - ATT_WIDTH
RWKV_HEAD_DIM = 64
RWKV_HEADS = RWKV_WIDTH // RWKV_HEAD_DIM
IDX_HEADS = 16
IDX_HEAD_DIM = 64
TOPK_MAX = 256
ROPE_THETA = 500000.0
ROPE_FRACTION = 4
DECAY_LORA = 96
AAA_LORA = 96
GATE_LORA = 256
NORM_EPS = 1e-6
LNX_EPS = 64e-5

LANES = 128
SUBLANES = 8
LORA_PAD = 128
ATT_COLS = 4 * ATT_WIDTH + IDX_HEAD_DIM + IDX_HEADS
IN_COLS = ATT_COLS + 3 * RWKV_WIDTH + DECAY_LORA + AAA_LORA + GATE_LORA
OFF_Q, OFF_K, OFF_V, OFF_IQ = 0, 1024, 2048, 3072
OFF_RR, OFF_RK, OFF_RV = 0, 1024, 2048
TAIL_TILE = 256
TAIL_COL0 = IN_COLS - 2 * TAIL_TILE
XW_LANE = IN_COLS - GATE_LORA - AAA_LORA - DECAY_LORA - TAIL_COL0
XA_LANE = XW_LANE + DECAY_LORA
OFF_WA = 3 * RWKV_WIDTH + TAIL_COL0 - (IN_COLS - RWKV_WIDTH)
OFF_XG = OFF_WA + TAIL_TILE

LOG2E = 1.4426950408889634
V_ROWS = 144
KEY_TILE = 512
CHUNKS_PER_STEP = 4
CHUNK = 64
VMEM_LIMIT = 56 * 1024 * 1024


def _cparams(sem):
    return pltpu.CompilerParams(dimension_semantics=sem, vmem_limit_bytes=VMEM_LIMIT)


def _dot(a, b):
    return jnp.dot(a, b, preferred_element_type=F32)


def _dot_nt(a, b):
    return lax.dot_general(a, b, (((1,), (1,)), ((), ())), preferred_element_type=F32)


def _dot_tn(a, b):
    return lax.dot_general(a, b, (((0,), (0,)), ((), ())), preferred_element_type=F32)


def _split2(x):
    hi = x.astype(BF16)
    lo = (x - hi.astype(F32)).astype(BF16)
    return hi, lo


def _dot_hi(x, w):
    hi, lo = _split2(x)
    return _dot(hi, w) + _dot(lo, w)


def _sigmoid(x):
    return 1.0 / (1.0 + jnp.exp(-x))


def _adaln_kernel(c_ref, w_ref, b_ref, o_ref):
    c = c_ref[...]
    ca = c * _sigmoid(c)
    o_ref[...] = _dot(ca.astype(BF16), w_ref[...].astype(BF16)) + b_ref[...]


def _adaln(c, w, b):
    bsz, d = c.shape
    n = w.shape[1]
    rows = 8
    cp = jnp.zeros((rows, d), F32).at[:bsz].set(c)
    tn = 1024
    out = pl.pallas_call(
        _adaln_kernel,
        grid=(n // tn,),
        in_specs=[pl.BlockSpec((rows, d), lambda j: (0, 0)),
                  pl.BlockSpec((d, tn), lambda j: (0, j)),
                  pl.BlockSpec((1, tn), lambda j: (0, j))],
        out_specs=pl.BlockSpec((rows, tn), lambda j: (0, j)),
        out_shape=jax.ShapeDtypeStruct((rows, n), F32),
        compiler_params=_cparams(("arbitrary",)),
        name="adaln",
    )(cp, w, b.reshape(1, n))
    return out[:bsz]


def _norm_mod(x, g, sc, sh):
    ms = jnp.mean(x * x, axis=-1, keepdims=True)
    y = x * lax.rsqrt(ms + NORM_EPS)
    y = y * g
    return y * (1.0 + sc) + sh


def _norm_kernel(x_ref, g_ref, sc_ref, sh_ref, o_ref):
    o_ref[...] = _norm_mod(x_ref[...], g_ref[...], sc_ref[...], sh_ref[...]).astype(BF16)


def _norm(x2, g, mod3, seq, tm=512):
    m, d = x2.shape
    per_b = seq // tm
    return pl.pallas_call(
        _norm_kernel,
        grid=(m // tm,),
        in_specs=[pl.BlockSpec((tm, d), lambda i: (i, 0)),
                  pl.BlockSpec((1, d), lambda i: (0, 0)),
                  pl.BlockSpec((None, 1, d), lambda i: ((i // per_b) * 6 + 1, 0, 0)),
                  pl.BlockSpec((None, 1, d), lambda i: ((i // per_b) * 6 + 0, 0, 0))],
        out_specs=pl.BlockSpec((tm, d), lambda i: (i, 0)),
        out_shape=jax.ShapeDtypeStruct((m, d), BF16),
        compiler_params=_cparams(("arbitrary",)),
        name="norm1",
    )(x2, g, mod3, mod3)


def _mm_nt_kernel(h_ref, wt_ref, o_ref, wb_ref):
    @pl.when(pl.program_id(1) == 0)
    def _():
        wb_ref[...] = wt_ref[...].astype(BF16)

    o_ref[...] = _dot_nt(h_ref[...], wb_ref[...])


def _matmul_nt(h, wt, col_of_tile, n_tiles, tn, name, tm=1024):
    m, d = h.shape
    return pl.pallas_call(
        _mm_nt_kernel,
        grid=(n_tiles, m // tm),
        in_specs=[pl.BlockSpec((tm, d), lambda j, i: (i, 0)),
                  pl.BlockSpec((pl.Element(tn), pl.Element(d)),
                               lambda j, i: (pl.multiple_of(col_of_tile(j), SUBLANES), 0))],
        out_specs=pl.BlockSpec((tm, tn), lambda j, i: (i, j)),
        out_shape=jax.ShapeDtypeStruct((m, n_tiles * tn), F32),
        scratch_shapes=[pltpu.VMEM((tn, d), BF16)],
        compiler_params=_cparams(("arbitrary", "arbitrary")),
        name=name,
    )(h, wt)


def _ffn_glu_kernel(h_ref, wg_ref, wu_ref, o_ref, wgb_ref, wub_ref):
    @pl.when(pl.program_id(1) == 0)
    def _():
        wgb_ref[...] = wg_ref[...].astype(BF16)
        wub_ref[...] = wu_ref[...].astype(BF16)

    h = h_ref[...]
    a = _dot(h, wgb_ref[...])
    u = _dot(h, wub_ref[...])
    o_ref[...] = (a * _sigmoid(a) * u).astype(o_ref.dtype)


def _ffn_glu(h, wg, wu, tm=1024, tn=512):
    m, d = h.shape
    tm = min(tm, m)
    n = wg.shape[1]
    return pl.pallas_call(
        _ffn_glu_kernel,
        grid=(n // tn, m // tm),
        in_specs=[pl.BlockSpec((tm, d), lambda j, i: (i, 0)),
                  pl.BlockSpec((d, tn), lambda j, i: (0, j)),
                  pl.BlockSpec((d, tn), lambda j, i: (0, j))],
        out_specs=pl.BlockSpec((tm, tn), lambda j, i: (i, j)),
        out_shape=jax.ShapeDtypeStruct((m, n), BF16),
        scratch_shapes=[pltpu.VMEM((d, tn), BF16)] * 2,
        compiler_params=_cparams(("arbitrary", "arbitrary")),
        name="ffn_glu",
    )(h, wg, wu)


def _out_proj_kernel(a_ref, r_ref, wa_ref, wr_ref, x_ref, gt_ref, g_ref, sc_ref, sh_ref,
                     x1_ref, h2_ref, wab_ref, wrb_ref):
    @pl.when(pl.program_id(0) == 0)
    def _():
        wab_ref[...] = wa_ref[...].astype(BF16)
        wrb_ref[...] = wr_ref[...].astype(BF16)

    mixed = _dot(a_ref[...], wab_ref[...]) + _dot(r_ref[...], wrb_ref[...])
    x1 = x_ref[...] + gt_ref[...] * mixed
    x1_ref[...] = x1
    h2_ref[...] = _norm_mod(x1, g_ref[...], sc_ref[...], sh_ref[...]).astype(BF16)


def _out_proj(att, rwkv, w_out, x2, mod3, g2, seq, tm=512):
    m, ka = att.shape
    kr = rwkv.shape[1]
    n = w_out.shape[1]
    per_b = seq // tm
    once = pl.Buffered(1)

    def mod_row(j):
        return pl.BlockSpec((None, 1, n), lambda i: ((i // per_b) * 6 + j, 0, 0))

    row = pl.BlockSpec((tm, n), lambda i: (i, 0))
    return pl.pallas_call(
        _out_proj_kernel,
        grid=(m // tm,),
        in_specs=[pl.BlockSpec((tm, ka), lambda i: (i, 0)),
                  pl.BlockSpec((tm, kr), lambda i: (i, 0)),
                  pl.BlockSpec((ka, n), lambda i: (0, 0), pipeline_mode=once),
                  pl.BlockSpec((kr, n), lambda i: (ka // kr, 0), pipeline_mode=once),
                  row, mod_row(2),
                  pl.BlockSpec((1, n), lambda i: (0, 0)), mod_row(4), mod_row(3)],
        out_specs=[row, row],
        out_shape=[jax.ShapeDtypeStruct((m, n), F32), jax.ShapeDtypeStruct((m, n), BF16)],
        scratch_shapes=[pltpu.VMEM((ka, n), BF16), pltpu.VMEM((kr, n), BF16)],
        compiler_params=_cparams(("arbitrary",)),
        name="out_proj",
    )(att, rwkv, w_out, w_out, x2, mod3, g2, mod3, mod3)


def _ffn_down_kernel(h_ref, w_ref, x_ref, gt_ref, o_ref, wb_ref):
    @pl.when(pl.program_id(1) == 0)
    def _():
        wb_ref[...] = w_ref[...].astype(BF16)

    o_ref[...] = x_ref[...] + gt_ref[...] * _dot(h_ref[...], wb_ref[...])


def _ffn_down(h, w, x2, mod3, seq, tm=512, tn=512):
    m, kdim = h.shape
    n = w.shape[1]
    per_b = seq // tm
    return pl.pallas_call(
        _ffn_down_kernel,
        grid=(n // tn, m // tm),
        in_specs=[pl.BlockSpec((tm, kdim), lambda j, i: (i, 0)),
                  pl.BlockSpec((kdim, tn), lambda j, i: (0, j)),
                  pl.BlockSpec((tm, tn), lambda j, i: (i, j)),
                  pl.BlockSpec((None, 1, tn), lambda j, i: ((i // per_b) * 6 + 5, 0, j))],
        out_specs=pl.BlockSpec((tm, tn), lambda j, i: (i, j)),
        out_shape=jax.ShapeDtypeStruct((m, n), F32),
        scratch_shapes=[pltpu.VMEM((kdim, tn), BF16)],
        compiler_params=_cparams(("arbitrary", "arbitrary")),
        name="ffn_down",
    )(h, w, x2, mod3)


def _proj_t_kernel(wt_ref, h_ref, pos_ref, g_ref, f_ref, *refs, mode):
    *o_refs, wb_ref = refs
    o_ref = o_refs[0]

    @pl.when(pl.program_id(0) == 0)
    def _():
        wb_ref[...] = wt_ref[...].astype(BF16)

    yt = _dot_nt(wb_ref[...], h_ref[...])
    tm = yt.shape[1]
    if mode == "v":
        pad_row = lax.broadcasted_iota(I32, (V_ROWS - LANES, KEY_TILE), 0)
        ones_rows = jnp.where(pad_row == 0, 1.0, 0.0).astype(BF16)
        for h in range(ATT_HEADS):
            for j in range(tm // KEY_TILE):
                tile = yt[h * LANES:(h + 1) * LANES, j * KEY_TILE:(j + 1) * KEY_TILE]
                o_ref[h, j, 0:LANES, :] = tile.astype(BF16)
                o_ref[h, j, LANES:V_ROWS, :] = ones_rows
        return

    head = ATT_HEAD_DIM if mode in ("q", "k") else IDX_HEAD_DIM
    half = head // ROPE_FRACTION // 2
    ang = f_ref[...] * pos_ref[...].astype(F32)
    cos, sin = jnp.cos(ang), jnp.sin(ang)

    def rotary(y):
        x1, x2 = y[0:half], y[half:2 * half]
        return jnp.concatenate([x1 * cos - x2 * sin, x2 * cos + x1 * sin, y[2 * half:]], axis=0)

    for hd in range(ATT_WIDTH // head):
        y = yt[hd * head:(hd + 1) * head, :]
        if mode in ("q", "k"):
            ms = jnp.mean(y * y, axis=0, keepdims=True)
            y = y * lax.rsqrt(ms + NORM_EPS) * g_ref[...]
        y = rotary(y)
        if mode == "q":
            o_ref[hd] = (y * ((ATT_HEAD_DIM ** -0.5) * LOG2E)).astype(BF16)
        elif mode == "iq":
            rows = slice((hd % 2) * head, (hd % 2 + 1) * head)
            o_ref[hd // 2, rows, :] = (y * (IDX_HEAD_DIM ** -0.5)).astype(BF16)
        else:
            o_ref[:, hd * LANES:(hd + 1) * LANES] = y.T.astype(BF16)
    if mode == "iq":
        rest = yt[ATT_WIDTH:, :]
        ikt = jnp.concatenate([rotary(rest[0:head]), rest[head:]], axis=0)
        o_refs[1][...] = ikt.T.astype(BF16)
        o_refs[2][...] = rest[head:head + IDX_HEADS, :] * (IDX_HEADS ** -0.5)


def _proj_t(h, wt, col0, n, pos_row, g_col, f_col, mode, tm=1024):
    m, d = h.shape
    tm = min(tm, m)
    if mode == "v":
        out_specs = [pl.BlockSpec((ATT_HEADS, tm // KEY_TILE, V_ROWS, KEY_TILE),
                                  lambda i: (0, i, 0, 0))]
        out_shape = [jax.ShapeDtypeStruct((ATT_HEADS, m // KEY_TILE, V_ROWS, KEY_TILE), BF16)]
    elif mode in ("q", "iq"):
        out_specs = [pl.BlockSpec((ATT_WIDTH // LANES, LANES, tm), lambda i: (0, 0, i))]
        out_shape = [jax.ShapeDtypeStruct((ATT_WIDTH // LANES, LANES, m), BF16)]
        if mode == "iq":
            out_specs += [pl.BlockSpec((tm, LANES), lambda i: (i, 0)),
                          pl.BlockSpec((IDX_HEADS, tm), lambda i: (0, i))]
            out_shape += [jax.ShapeDtypeStruct((m, LANES), BF16),
                          jax.ShapeDtypeStruct((IDX_HEADS, m), F32)]
    else:
        out_specs = [pl.BlockSpec((tm, n), lambda i: (i, 0))]
        out_shape = [jax.ShapeDtypeStruct((m, n), BF16)]
    out = pl.pallas_call(
        functools.partial(_proj_t_kernel, mode=mode),
        grid=(m // tm,),
        in_specs=[pl.BlockSpec((pl.Element(n), pl.Element(d)), lambda i: (col0, 0)),
                  pl.BlockSpec((tm, d), lambda i: (i, 0)),
                  pl.BlockSpec((1, tm), lambda i: (0, i)),
                  pl.BlockSpec(g_col.shape, lambda i: (0, 0)),
                  pl.BlockSpec(f_col.shape, lambda i: (0, 0))],
        out_specs=out_specs,
        out_shape=out_shape,
        scratch_shapes=[pltpu.VMEM((n, d), BF16)],
        compiler_params=_cparams(("arbitrary",)),
        name="in_proj_" + mode,
    )(wt, h, pos_row, g_col, f_col)
    return out if mode == "iq" else out[0]


NEG_BIG = -1e30


INT_MIN = -2 ** 31
INT_MAX = 2 ** 31 - 1
MAGNITUDE_BITS = 0x7FFFFFFF
KEY_NEG_INF = -2139095041
BOUND_SLACK = 1.02
MAX_STATIC_SHIFT = 60.0
COUNT_ROWS = 32


def _key_to_float(key):
    bits = key ^ ((key >> 31) & jnp.int32(MAGNITUDE_BITS))
    return lax.bitcast_convert_type(bits, F32)


def _dsa_kernel(bound_ref, qt_ref, iqt_ref, iwt_ref, k_ref, vt_ref, ik_ref, o_ref,
                sc_ref, m_ref, acc_ref, last_ref, *, tq, tk, topk, index_bits):
    qi = pl.program_id(1)
    n_kb = (qi * tq + tq - 1) // tk + 1
    key0 = lax.broadcasted_iota(I32, (tk, tq), 0)
    qidx = qi * tq + lax.broadcasted_iota(I32, (tk, tq), 1)
    iw = iwt_ref[...]

    def score_body(kb, carry):
        start = pl.multiple_of(kb * tk, tk)
        ikb = ik_ref[pl.ds(start, tk), 0:IDX_HEAD_DIM]
        s = jnp.zeros((tk, tq), F32)
        for h in range(IDX_HEADS):
            off = (h % 2) * IDX_HEAD_DIM
            d = _dot(ikb, iqt_ref[h // 2, off:off + IDX_HEAD_DIM, :])
            s = s + jnp.maximum(d, 0.0) * iw[h:h + 1, :]
        sc_ref[kb] = jnp.where(kb * tk + key0 <= qidx, s, -jnp.inf)
        return carry

    lax.fori_loop(0, n_kb, score_body, 0)

    def count(pred):
        def cnt_body(kb, acc):
            hit = pred(kb).reshape(tk // COUNT_ROWS, COUNT_ROWS, tq)
            for r in range(tk // COUNT_ROWS):
                acc = jnp.where(hit[r], acc + 1.0, acc)
            return acc

        acc = lax.fori_loop(0, n_kb, cnt_body, jnp.zeros((COUNT_ROWS, tq), F32))
        return jnp.sum(acc, axis=0, keepdims=True)

    def bit_body(i, cand):
        trial = cand ^ lax.shift_left(jnp.int32(1), 31 - i)
        trial_f = _key_to_float(trial)
        cnt = count(lambda kb: sc_ref[kb] >= trial_f)
        return jnp.where(cnt >= topk, trial, cand)

    cand = lax.fori_loop(0, 32, bit_body, jnp.full((1, tq), INT_MIN, I32))
    tau = _key_to_float(jnp.maximum(cand, jnp.int32(KEY_NEG_INF)))

    last_ref[...] = jnp.full_like(last_ref, INT_MAX)
    n_ge = count(lambda kb: sc_ref[kb] >= tau)

    @pl.when(jnp.max(n_ge) > topk)
    def _():
        need = topk - count(lambda kb: sc_ref[kb] > tau)

        def idx_body(i, last):
            trial = last | lax.shift_left(jnp.int32(1), index_bits - 1 - i)
            below = count(lambda kb: (sc_ref[kb] == tau) & (kb * tk + key0 < trial))
            return jnp.where(below < need, trial, last)

        last_ref[...] = lax.fori_loop(0, index_bits, idx_body, jnp.zeros((1, tq), I32))

    acc_ref[...] = jnp.zeros_like(acc_ref)
    bound = bound_ref[0]

    def logits(kb, h):
        start = pl.multiple_of(kb * tk, tk)
        return _dot(k_ref[pl.ds(start, tk), h * LANES:(h + 1) * LANES], qt_ref[h])

    last = last_ref[...]

    def selected(kb):
        s = sc_ref[kb]
        kidx = kb * tk + key0
        return ((s > tau) | ((s == tau) & (kidx <= last))) & (kidx <= qidx)

    @pl.when(bound <= MAX_STATIC_SHIFT)
    def _():
        def att_body(kb, carry):
            bias = jnp.where(selected(kb), -bound, -jnp.inf)
            for h in range(ATT_HEADS):
                p = jnp.exp2(logits(kb, h) + bias)
                acc_ref[h] += _dot(vt_ref[h, kb], p.astype(BF16))
            return carry

        lax.fori_loop(0, n_kb, att_body, 0)

    @pl.when(bound > MAX_STATIC_SHIFT)
    def _():
        m_ref[...] = jnp.full_like(m_ref, NEG_BIG)

        def att_body(kb, carry):
            bias = jnp.where(selected(kb), 0.0, -jnp.inf)
            for h in range(ATT_HEADS):
                s = logits(kb, h) + bias
                m_prev = m_ref[h]
                m_next = jnp.maximum(m_prev, jnp.max(s, axis=0, keepdims=True))
                p = jnp.exp2(s - m_next)
                alpha = jnp.exp2(m_prev - m_next)
                acc_ref[h] = alpha * acc_ref[h] + _dot(vt_ref[h, kb], p.astype(BF16))
                m_ref[h] = m_next
            return carry

        lax.fori_loop(0, n_kb, att_body, 0)

    for h in range(ATT_HEADS):
        out = acc_ref[h, 0:LANES, :] / acc_ref[h, LANES:LANES + 1, :]
        o_ref[:, h * LANES:(h + 1) * LANES] = out.T.astype(o_ref.dtype)


def _dsa(bound, qt, iqt, iwt, k, vt, ik, bsz, seq, tk, tq=512):
    tq = min(tq, seq)
    topk = min(TOPK_MAX, seq // 4)
    w = ATT_WIDTH
    nq = seq // tq
    nkb = seq // tk
    npair = iqt.shape[0]
    kern = functools.partial(_dsa_kernel, tq=tq, tk=tk, topk=topk,
                             index_bits=max(1, (seq - 1).bit_length()))
    return pl.pallas_call(
        kern,
        grid=(bsz, nq),
        in_specs=[pl.BlockSpec(memory_space=pltpu.SMEM),
                  pl.BlockSpec((ATT_HEADS, LANES, tq), lambda b, i: (0, 0, b * nq + i)),
                  pl.BlockSpec((npair, LANES, tq), lambda b, i: (0, 0, b * nq + i)),
                  pl.BlockSpec((IDX_HEADS, tq), lambda b, i: (0, b * nq + i)),
                  pl.BlockSpec((seq, w), lambda b, i: (b, 0)),
                  pl.BlockSpec((ATT_HEADS, nkb, V_ROWS, tk), lambda b, i: (0, b, 0, 0)),
                  pl.BlockSpec((seq, LANES), lambda b, i: (b, 0))],
        out_specs=pl.BlockSpec((tq, w), lambda b, i: (b * nq + i, 0)),
        out_shape=jax.ShapeDtypeStruct((bsz * seq, w), BF16),
        scratch_shapes=[pltpu.VMEM((nkb, tk, tq), F32),
                        pltpu.VMEM((ATT_HEADS, 1, tq), F32),
                        pltpu.VMEM((ATT_HEADS, V_ROWS, tq), F32),
                        pltpu.VMEM((1, tq), I32)],
        compiler_params=_cparams(("arbitrary", "arbitrary")),
        name="dsa",
    )(bound, qt, iqt, iwt, k, vt, ik)


def _rwkv_prep_kernel(rr_ref, rk_ref, rv_ref, wa_ref, xg_ref,
                      mr_ref, mk_ref, mv_ref, mwa_ref, mg_ref,
                      w0_ref, a0_ref, kk_ref, ka_ref, rkp_ref,
                      wup_ref, aup_ref, gup_ref, e_ref,
                      r_o, lw_o, k_o, v_o, kkn_o, bb_o, g_o, bon_o,
                      c_r, c_k, c_v, c_wa, c_g, *, tm):
    first = pl.program_id(1) == 0

    for carry_ref in (c_r, c_k, c_v, c_wa, c_g):
        @pl.when(first)
        def _(carry_ref=carry_ref):
            carry_ref[...] = jnp.zeros_like(carry_ref)

    def shift(y_ref, carry_ref, mu_ref, cols=slice(None)):
        y = y_ref[:, cols]
        prev_last = carry_ref[7:8, cols]
        rolled = pltpu.roll(y, 1, 0)
        rows = lax.broadcasted_iota(I32, y.shape, 0)
        yprev = jnp.where(rows == 0, prev_last, rolled)
        carry_ref[:, cols] = y[tm - 8:tm, :]
        return y + (yprev - y) * mu_ref[:, cols]

    wa = shift(wa_ref, c_wa, mwa_ref)
    xw = pltpu.roll(wa, TAIL_TILE - XW_LANE, 1)[:, :LORA_PAD]
    xa = pltpu.roll(wa, TAIL_TILE - XA_LANE, 1)[:, :LORA_PAD]
    xg = shift(xg_ref, c_g, mg_ref)
    w_raw = w0_ref[...] + _dot(jnp.tanh(xw).astype(BF16), wup_ref[...])
    a_pre = a0_ref[...] + _dot(xa.astype(BF16), aup_ref[...])
    g_o[...] = _dot(_sigmoid(xg).astype(BF16), gup_ref[...]).astype(BF16)

    e2 = e_ref[...]
    for p in range(RWKV_WIDTH // LANES):
        sl = slice(p * LANES, (p + 1) * LANES)
        r = shift(rr_ref, c_r, mr_ref, sl)
        k = shift(rk_ref, c_k, mk_ref, sl)
        v = shift(rv_ref, c_v, mv_ref, sl)
        z = -w_raw[:, sl]
        softplus = jnp.maximum(z, 0.0) + jnp.log(1.0 + jnp.exp(-jnp.abs(z)))
        lw_o[:, sl] = -jnp.exp(-softplus - 0.5)
        a = _sigmoid(a_pre[:, sl])
        kk = k * kk_ref[:, sl]
        ss = _dot((kk * kk).astype(BF16), e2)
        kk = kk / jnp.maximum(jnp.sqrt(ss), 1e-12)
        kmod = k * (1.0 + (a - 1.0) * ka_ref[:, sl])
        r_o[:, sl] = r.astype(BF16)
        k_o[:, sl] = kmod.astype(BF16)
        v_o[:, sl] = v.astype(BF16)
        kkn_o[:, sl] = kk.astype(BF16)
        bb_o[:, sl] = (kk * a).astype(BF16)
        rkr = _dot((r * kmod * rkp_ref[:, sl]).astype(BF16), e2)
        bon_o[:, sl] = (rkr * v).astype(BF16)


def _rwkv_prep(proj_r, mus, vecs, wup, aup, gup, e, bsz, seq, tm=256):
    w = RWKV_WIDTH
    per_b = seq // tm

    def wide(off):
        return pl.BlockSpec((tm, w), lambda b, i: (b * per_b + i, off // w))

    def tail(off):
        return pl.BlockSpec((tm, TAIL_TILE), lambda b, i: (b * per_b + i, off // TAIL_TILE))

    def const(shape):
        return pl.BlockSpec(shape, lambda b, i: (0, 0))

    out_blk = pl.BlockSpec((tm, w), lambda b, i: (b * per_b + i, 0))
    kern = functools.partial(_rwkv_prep_kernel, tm=tm)
    return pl.pallas_call(
        kern,
        grid=(bsz, per_b),
        in_specs=[wide(OFF_RR), wide(OFF_RK), wide(OFF_RV), tail(OFF_WA), tail(OFF_XG),
                  const((1, w)), const((1, w)), const((1, w)),
                  const((1, TAIL_TILE)), const((1, GATE_LORA)),
                  const((1, w)), const((1, w)), const((1, w)), const((1, w)), const((1, w)),
                  const((LORA_PAD, w)), const((LORA_PAD, w)), const((GATE_LORA, w)),
                  const((LANES, LANES))],
        out_specs=[out_blk] * 8,
        out_shape=[jax.ShapeDtypeStruct((bsz * seq, w), F32 if i == 1 else BF16) for i in range(8)],
        scratch_shapes=[pltpu.VMEM((8, w), F32)] * 3
        + [pltpu.VMEM((8, TAIL_TILE), F32), pltpu.VMEM((8, GATE_LORA), F32)],
        compiler_params=_cparams(("arbitrary", "arbitrary")),
        name="rwkv_prep",
    )(proj_r, proj_r, proj_r, proj_r, proj_r, *mus, *vecs, wup, aup, gup, e)


def _rwkv_chunk(rows, r_ref, lw_ref, k_ref, v_ref, kk_ref, bb_ref, g_ref, bon_ref, lg_ref, lb_ref,
                o_ref, z_ref):
    c = CHUNK
    lw = lw_ref[rows, :]
    tri = jnp.where(lax.broadcasted_iota(I32, (c, c), 1) <= lax.broadcasted_iota(I32, (c, c), 0),
                    1.0, 0.0).astype(BF16)
    hi = lw.astype(BF16)
    rem = lw - hi.astype(F32)
    mid = rem.astype(BF16)
    lo = (rem - mid.astype(F32)).astype(BF16)
    cum = _dot(tri, hi) + _dot(tri, mid) + _dot(tri, lo)
    p_in = jnp.exp(cum)
    p_ex = jnp.exp(cum - lw)
    p_inv = jnp.exp(-cum)
    p_end = p_in[c - 1:c, :]
    a_t = -kk_ref[rows, :].astype(F32) * p_ex
    r_t = r_ref[rows, :].astype(F32) * p_in
    b_h = bb_ref[rows, :].astype(F32) * p_inv
    k_h = k_ref[rows, :].astype(F32) * p_inv
    b_e = b_h * p_end
    k_e = k_h * p_end
    v = v_ref[rows, :].astype(F32)

    n2 = 2 * c
    lane = lax.broadcasted_iota(I32, (1, LANES), 1)
    head0 = lane < RWKV_HEAD_DIM
    ri = lax.broadcasted_iota(I32, (n2, n2), 0)
    ci = lax.broadcasted_iota(I32, (n2, n2), 1)
    same = (ri >= c) == (ci >= c)
    strict = same & (ci < ri)
    incl = same & (ci <= ri)
    eye = ri == ci
    own = ((lax.broadcasted_iota(I32, (n2, LANES), 0) >= c)
           == (lax.broadcasted_iota(I32, (n2, LANES), 1) >= RWKV_HEAD_DIM))
    inv_n = 1.0 / RWKV_HEAD_DIM

    def stack(y):
        return jnp.concatenate([jnp.where(head0, y, 0.0), jnp.where(head0, 0.0, y)], axis=0)

    pairs = range(RWKV_WIDTH // LANES)
    sls = [slice(p * LANES, (p + 1) * LANES) for p in pairs]
    a_s = [stack(a_t[:, sl]) for sl in sls]
    r_s = [stack(r_t[:, sl]) for sl in sls]
    v_s = [stack(v[:, sl]).astype(BF16) for sl in sls]
    g1 = [_dot_nt(jnp.concatenate([a_s[p], r_s[p]], axis=0).astype(BF16),
                  jnp.concatenate([stack(b_h[:, sls[p]]), stack(k_h[:, sls[p]])], axis=0).astype(BF16))
          for p in pairs]
    pw = [jnp.where(strict, g[:n2, :n2], 0.0).astype(BF16) for g in g1]
    a_rb = [jnp.where(incl, g[n2:, :n2], 0.0).astype(BF16) for g in g1]
    a_rk = [jnp.where(incl, g[n2:, n2:], 0.0).astype(BF16) for g in g1]
    akv = [_dot(jnp.where(strict, g1[p][:n2, n2:], 0.0).astype(BF16), v_s[p]) for p in pairs]
    half_turn = RWKV_HEAD_DIM

    def unpack(x):
        return jnp.where(own, x, 0.0), pltpu.roll(jnp.where(own, 0.0, x), half_turn, 1)

    xc = [a_s[p] + pltpu.roll(akv[p], half_turn, 1) for p in pairs]
    steps = int(np.log2(c))
    for i in range(steps):
        if i + 1 < steps:
            res = [_dot(pw[p], jnp.concatenate([pw[p], xc[p].astype(BF16)], axis=1)) for p in pairs]
            xc = [xc[p] + res[p][:, n2:] for p in pairs]
            pw = [res[p][:, :n2].astype(BF16) for p in pairs]
        else:
            xc = [xc[p] + _dot(pw[p], xc[p].astype(BF16)) for p in pairs]
    xcb = [x.astype(BF16) for x in xc]
    r2 = [unpack(_dot(a_rb[p], xcb[p])) for p in pairs]
    ov = [r2[p][1] + _dot(a_rk[p], v_s[p]) for p in pairs]
    mg = [unpack(_dot_tn(stack(b_e[:, sls[p]]).astype(BF16), xcb[p])) for p in pairs]
    kv = [_dot_tn(stack(k_e[:, sls[p]]).astype(BF16), v_s[p]) for p in pairs]
    for p in pairs:
        q_s = r_s[p] + r2[p][0]
        mmat = mg[p][0] + jnp.where(eye, p_end[:, sls[p]], 0.0)
        qm = jnp.concatenate([q_s, mmat], axis=0).astype(BF16)
        res = _dot(qm, z_ref[p].astype(BF16))
        z_ref[p] = res[n2:] + mg[p][1] + kv[p]
        o_s = res[:n2] + ov[p]
        mean = jnp.sum(o_s, axis=1, keepdims=True) * inv_n
        dev = jnp.where(own, o_s - mean, 0.0)
        var = jnp.sum(dev * dev, axis=1, keepdims=True) * inv_n
        y = dev * lax.rsqrt(var + LNX_EPS)
        y = (y[:c] + y[c:]) * lg_ref[:, sls[p]] + lb_ref[:, sls[p]]
        out = (y + bon_ref[rows, sls[p]].astype(F32)) * g_ref[rows, sls[p]].astype(F32)
        o_ref[rows, sls[p]] = out.astype(o_ref.dtype)


def _rwkv_core_kernel(*refs):
    z_ref = refs[-1]

    @pl.when(pl.program_id(1) == 0)
    def _():
        z_ref[...] = jnp.zeros_like(z_ref)

    for sub in range(CHUNKS_PER_STEP):
        _rwkv_chunk(slice(sub * CHUNK, (sub + 1) * CHUNK), *refs)


def _rwkv_core(r, lw, k, v, kk, bb, g, bon, lg, lb, bsz, seq):
    c = CHUNK * CHUNKS_PER_STEP
    w = RWKV_WIDTH
    per_b = seq // c
    blk = pl.BlockSpec((c, w), lambda b, i: (b * per_b + i, 0))
    vec = pl.BlockSpec((1, w), lambda b, i: (0, 0))
    return pl.pallas_call(
        _rwkv_core_kernel,
        grid=(bsz, per_b),
        in_specs=[blk] * 8 + [vec, vec],
        out_specs=blk,
        out_shape=jax.ShapeDtypeStruct((bsz * seq, w), BF16),
        scratch_shapes=[pltpu.VMEM((w // LANES, LANES, LANES), F32)],
        compiler_params=_cparams(("arbitrary", "arbitrary")),
        name="rwkv_core",
    )(r, lw, k, v, kk, bb, g, bon, lg, lb)


def _rope_freqs(head_dim):
    half = head_dim // ROPE_FRACTION // 2
    return (ROPE_THETA ** (-jnp.arange(half, dtype=F32) / half)).reshape(half, 1)


def kernel(x, c, positions, w_ada, b_ada, norm1_g, w_in, q_norm_g, k_norm_g, rwkv_mu, rwkv_w0,
           rwkv_w_up, rwkv_a0, rwkv_a_up, rwkv_g_up, rwkv_k_k, rwkv_k_a, rwkv_r_k, rwkv_lnx_g,
           rwkv_lnx_b, w_out, norm2_g, w_ffn_gate, w_ffn_up, w_ffn_down):
    bsz, seq, d = x.shape
    depth = w_ada.shape[0]
    m = bsz * seq
    pos_row = positions.reshape(1, m)
    fa_col = _rope_freqs(ATT_HEAD_DIM)
    fi_col = _rope_freqs(IDX_HEAD_DIM)
    hd = RWKV_HEAD_DIM
    e = (jnp.arange(LANES)[:, None] // hd == jnp.arange(LANES)[None, :] // hd).astype(BF16)
    x2 = x.reshape(m, d)

    for l in range(depth):
        mod = _adaln(c, w_ada[l], b_ada[l])
        mod3 = mod.reshape(bsz * 6, 1, d)

        h1 = _norm(x2, norm1_g[l].reshape(1, d), mod3, seq)
        w_in_t = w_in[l].T
        proj_r = _matmul_nt(
            h1, w_in_t,
            lambda j: jnp.where(j < 3, ATT_COLS + j * RWKV_WIDTH, IN_COLS - RWKV_WIDTH),
            4, RWKV_WIDTH, "in_proj_rkv")

        none = jnp.zeros((SUBLANES, 1), F32)
        aw = ATT_WIDTH
        qt = _proj_t(h1, w_in_t, OFF_Q, aw, pos_row, q_norm_g[l].reshape(-1, 1), fa_col, "q")
        kn = _proj_t(h1, w_in_t, OFF_K, aw, pos_row, k_norm_g[l].reshape(-1, 1), fa_col, "k")
        vt = _proj_t(h1, w_in_t, OFF_V, aw, pos_row, none, none, "v")
        iqt, ik, iwt = _proj_t(h1, w_in_t, OFF_IQ, aw + LANES, pos_row, none, fi_col, "iq")
        bound = (ATT_HEAD_DIM ** 0.5 * LOG2E * BOUND_SLACK
                 * jnp.max(jnp.abs(q_norm_g[l])) * jnp.max(jnp.abs(k_norm_g[l])))
        att = _dsa(bound.reshape(1).astype(F32), qt, iqt, iwt, kn, vt, ik, bsz, seq, KEY_TILE)

        mu = rwkv_mu[l]
        w3 = 3 * RWKV_WIDTH

        n_wa = DECAY_LORA + AAA_LORA
        mu_wa = jnp.zeros((1, TAIL_TILE), F32).at[0, XW_LANE:XW_LANE + n_wa].set(mu[w3:w3 + n_wa])
        mus = [mu[0:RWKV_WIDTH].reshape(1, -1), mu[RWKV_WIDTH:2 * RWKV_WIDTH].reshape(1, -1),
               mu[2 * RWKV_WIDTH:w3].reshape(1, -1), mu_wa, mu[w3 + n_wa:].reshape(1, -1)]
        vecs = [rwkv_w0[l].reshape(1, -1), rwkv_a0[l].reshape(1, -1), rwkv_k_k[l].reshape(1, -1),
                rwkv_k_a[l].reshape(1, -1), rwkv_r_k[l].reshape(1, -1)]

        def pad_rows(wm):
            return jnp.zeros((LORA_PAD, wm.shape[1]), F32).at[:wm.shape[0]].set(wm).astype(BF16)

        r, lw, km, vv, kk, bb, g, bon = _rwkv_prep(
            proj_r, mus, vecs, pad_rows(rwkv_w_up[l]), pad_rows(rwkv_a_up[l]),
            rwkv_g_up[l].astype(BF16), e, bsz, seq)
        rw = _rwkv_core(r, lw, km, vv, kk, bb, g, bon, rwkv_lnx_g[l].reshape(1, -1),
                        rwkv_lnx_b[l].reshape(1, -1), bsz, seq)

        x2, h2 = _out_proj(att, rw, w_out[l], x2, mod3, norm2_g[l].reshape(1, d), seq)

        hglu = _ffn_glu(h2, w_ffn_gate[l], w_ffn_up[l])
        x2 = _ffn_down(hglu, w_ffn_down[l], x2, mod3, seq)
    return x2.reshape(bsz, seq, d)
```

```python
import functools

import jax
import jax.numpy as jnp
import numpy as np
from jax import lax
from jax.experimental import pallas as pl
from jax.experimental.pallas import tpu as pltpu

F32 = jnp.float32
BF16 = jnp.bfloat16
I32 = jnp.int32

D_MODEL = 2048
ATT_HEADS = 8
ATT_HEAD_DIM = 128
ATT_WIDTH = ATT_HEADS * ATT_HEAD_DIM
RWKV_WIDTH = D_MODEL - ATT_WIDTH
RWKV_HEAD_DIM = 64
RWKV_HEADS = RWKV_WIDTH // RWKV_HEAD_DIM
IDX_HEADS = 16
IDX_HEAD_DIM = 64
TOPK_MAX = 256
ROPE_THETA = 500000.0
ROPE_FRACTION = 4
DECAY_LORA = 96
AAA_LORA = 96
GATE_LORA = 256
NORM_EPS = 1e-6
LNX_EPS = 64e-5

LANES = 128
SUBLANES = 8
LORA_PAD = 128
ATT_COLS = 4 * ATT_WIDTH + IDX_HEAD_DIM + IDX_HEADS
IN_COLS = ATT_COLS + 3 * RWKV_WIDTH + DECAY_LORA + AAA_LORA + GATE_LORA
OFF_Q, OFF_K, OFF_V, OFF_IQ = 0, 1024, 2048, 3072
OFF_RR, OFF_RK, OFF_RV = 0, 1024, 2048
TAIL_TILE = 256
TAIL_COL0 = IN_COLS - 2 * TAIL_TILE
XW_LANE = IN_COLS - GATE_LORA - AAA_LORA - DECAY_LORA - TAIL_COL0
XA_LANE = XW_LANE + DECAY_LORA
RKV_TILE = 2 * TAIL_TILE
OFF_WA = 3 * RWKV_WIDTH
OFF_XG = OFF_WA + TAIL_TILE

LOG2E = 1.4426950408889634
V_ROWS = 144
KEY_TILE = 512
CHUNKS_PER_STEP = 4
CHUNK = 64
VMEM_LIMIT = 56 * 1024 * 1024


def _cparams(sem):
    return pltpu.CompilerParams(dimension_semantics=sem, vmem_limit_bytes=VMEM_LIMIT)


def _dot(a, b):
    return jnp.dot(a, b, preferred_element_type=F32)


def _dot_nt(a, b):
    return lax.dot_general(a, b, (((1,), (1,)), ((), ())), preferred_element_type=F32)


def _dot_tn(a, b):
    return lax.dot_general(a, b, (((0,), (0,)), ((), ())), preferred_element_type=F32)


def _sigmoid(x):
    return 1.0 / (1.0 + jnp.exp(-x))


def _adaln_kernel(c_ref, w_ref, b_ref, o_ref):
    c = c_ref[...]
    ca = c * _sigmoid(c)
    o_ref[...] = _dot(ca.astype(BF16), w_ref[...].astype(BF16)) + b_ref[...]


def _adaln(c, w, b):
    bsz, d = c.shape
    n = w.shape[1]
    rows = 8
    cp = jnp.zeros((rows, d), F32).at[:bsz].set(c)
    tn = 1024
    out = pl.pallas_call(
        _adaln_kernel,
        grid=(n // tn,),
        in_specs=[pl.BlockSpec((rows, d), lambda j: (0, 0)),
                  pl.BlockSpec((d, tn), lambda j: (0, j)),
                  pl.BlockSpec((1, tn), lambda j: (0, j))],
        out_specs=pl.BlockSpec((rows, tn), lambda j: (0, j)),
        out_shape=jax.ShapeDtypeStruct((rows, n), F32),
        compiler_params=_cparams(("arbitrary",)),
        name="adaln",
    )(cp, w, b.reshape(1, n))
    return out[:bsz]


def _norm_mod(x, g, sc, sh):
    ms = jnp.mean(x * x, axis=-1, keepdims=True)
    y = x * lax.rsqrt(ms + NORM_EPS)
    y = y * g
    return y * (1.0 + sc) + sh


def _norm_kernel(x_ref, g_ref, sc_ref, sh_ref, o_ref):
    o_ref[...] = _norm_mod(x_ref[...], g_ref[...], sc_ref[...], sh_ref[...]).astype(BF16)


def _norm(x2, g, mod3, seq, tm=512):
    m, d = x2.shape
    per_b = seq // tm
    return pl.pallas_call(
        _norm_kernel,
        grid=(m // tm,),
        in_specs=[pl.BlockSpec((tm, d), lambda i: (i, 0)),
                  pl.BlockSpec((1, d), lambda i: (0, 0)),
                  pl.BlockSpec((None, 1, d), lambda i: ((i // per_b) * 6 + 1, 0, 0)),
                  pl.BlockSpec((None, 1, d), lambda i: ((i // per_b) * 6 + 0, 0, 0))],
        out_specs=pl.BlockSpec((tm, d), lambda i: (i, 0)),
        out_shape=jax.ShapeDtypeStruct((m, d), BF16),
        compiler_params=_cparams(("arbitrary",)),
        name="norm1",
    )(x2, g, mod3, mod3)


def _mm_nt_kernel(h_ref, wt_ref, o_ref, wb_ref):
    @pl.when(pl.program_id(1) == 0)
    def _():
        wb_ref[...] = wt_ref[...].astype(BF16)

    o_ref[...] = _dot_nt(h_ref[...], wb_ref[...])


def _matmul_nt(h, wt, col_of_tile, n_tiles, tn, name, tm=1024):
    m, d = h.shape
    return pl.pallas_call(
        _mm_nt_kernel,
        grid=(n_tiles, m // tm),
        in_specs=[pl.BlockSpec((tm, d), lambda j, i: (i, 0)),
                  pl.BlockSpec((pl.Element(tn), pl.Element(d)),
                               lambda j, i: (pl.multiple_of(col_of_tile(j), SUBLANES), 0))],
        out_specs=pl.BlockSpec((tm, tn), lambda j, i: (i, j)),
        out_shape=jax.ShapeDtypeStruct((m, n_tiles * tn), F32),
        scratch_shapes=[pltpu.VMEM((tn, d), BF16)],
        compiler_params=_cparams(("arbitrary", "arbitrary")),
        name=name,
    )(h, wt)


def _ffn_glu_kernel(h_ref, wg_ref, wu_ref, o_ref, wgb_ref, wub_ref):
    @pl.when(pl.program_id(1) == 0)
    def _():
        wgb_ref[...] = wg_ref[...].astype(BF16)
        wub_ref[...] = wu_ref[...].astype(BF16)

    h = h_ref[...]
    a = _dot(h, wgb_ref[...])
    u = _dot(h, wub_ref[...])
    o_ref[...] = (a * _sigmoid(a) * u).astype(o_ref.dtype)


def _ffn_glu(h, wg, wu, tm=1024, tn=512):
    m, d = h.shape
    tm = min(tm, m)
    n = wg.shape[1]
    return pl.pallas_call(
        _ffn_glu_kernel,
        grid=(n // tn, m // tm),
        in_specs=[pl.BlockSpec((tm, d), lambda j, i: (i, 0)),
                  pl.BlockSpec((d, tn), lambda j, i: (0, j)),
                  pl.BlockSpec((d, tn), lambda j, i: (0, j))],
        out_specs=pl.BlockSpec((tm, tn), lambda j, i: (i, j)),
        out_shape=jax.ShapeDtypeStruct((m, n), BF16),
        scratch_shapes=[pltpu.VMEM((d, tn), BF16)] * 2,
        compiler_params=_cparams(("arbitrary", "arbitrary")),
        name="ffn_glu",
    )(h, wg, wu)


def _out_proj_kernel(a_ref, r_ref, wa_ref, wr_ref, x_ref, gt_ref, g_ref, sc_ref, sh_ref,
                     x1_ref, h2_ref, wab_ref, wrb_ref):
    @pl.when(pl.program_id(0) == 0)
    def _():
        wab_ref[...] = wa_ref[...].astype(BF16)
        wrb_ref[...] = wr_ref[...].astype(BF16)

    mixed = _dot(a_ref[...], wab_ref[...]) + _dot(r_ref[...], wrb_ref[...])
    x1 = x_ref[...] + gt_ref[...] * mixed
    x1_ref[...] = x1
    h2_ref[...] = _norm_mod(x1, g_ref[...], sc_ref[...], sh_ref[...]).astype(BF16)


def _out_proj(att, rwkv, w_out, x2, mod3, g2, seq, tm=512):
    m, ka = att.shape
    kr = rwkv.shape[1]
    n = w_out.shape[1]
    per_b = seq // tm
    once = pl.Buffered(1)

    def mod_row(j):
        return pl.BlockSpec((None, 1, n), lambda i: ((i // per_b) * 6 + j, 0, 0))

    row = pl.BlockSpec((tm, n), lambda i: (i, 0))
    return pl.pallas_call(
        _out_proj_kernel,
        grid=(m // tm,),
        in_specs=[pl.BlockSpec((tm, ka), lambda i: (i, 0)),
                  pl.BlockSpec((tm, kr), lambda i: (i, 0)),
                  pl.BlockSpec((ka, n), lambda i: (0, 0), pipeline_mode=once),
                  pl.BlockSpec((kr, n), lambda i: (ka // kr, 0), pipeline_mode=once),
                  row, mod_row(2),
                  pl.BlockSpec((1, n), lambda i: (0, 0)), mod_row(4), mod_row(3)],
        out_specs=[row, row],
        out_shape=[jax.ShapeDtypeStruct((m, n), F32), jax.ShapeDtypeStruct((m, n), BF16)],
        scratch_shapes=[pltpu.VMEM((ka, n), BF16), pltpu.VMEM((kr, n), BF16)],
        compiler_params=_cparams(("arbitrary",)),
        name="out_proj",
    )(att, rwkv, w_out, w_out, x2, mod3, g2, mod3, mod3)


def _ffn_down_kernel(h_ref, w_ref, x_ref, gt_ref, o_ref, wb_ref):
    @pl.when(pl.program_id(1) == 0)
    def _():
        wb_ref[...] = w_ref[...].astype(BF16)

    o_ref[...] = x_ref[...] + gt_ref[...] * _dot(h_ref[...], wb_ref[...])


def _ffn_down(h, w, x2, mod3, seq, tm=512, tn=512):
    m, kdim = h.shape
    n = w.shape[1]
    per_b = seq // tm
    return pl.pallas_call(
        _ffn_down_kernel,
        grid=(n // tn, m // tm),
        in_specs=[pl.BlockSpec((tm, kdim), lambda j, i: (i, 0)),
                  pl.BlockSpec((kdim, tn), lambda j, i: (0, j)),
                  pl.BlockSpec((tm, tn), lambda j, i: (i, j)),
                  pl.BlockSpec((None, 1, tn), lambda j, i: ((i // per_b) * 6 + 5, 0, j))],
        out_specs=pl.BlockSpec((tm, tn), lambda j, i: (i, j)),
        out_shape=jax.ShapeDtypeStruct((m, n), F32),
        scratch_shapes=[pltpu.VMEM((kdim, tn), BF16)],
        compiler_params=_cparams(("arbitrary", "arbitrary")),
        name="ffn_down",
    )(h, w, x2, mod3)


def _proj_t_kernel(wt_ref, h_ref, pos_ref, g_ref, f_ref, *refs, mode):
    *o_refs, wb_ref = refs
    o_ref = o_refs[0]

    @pl.when(pl.program_id(0) == 0)
    def _():
        wb_ref[...] = wt_ref[...].astype(BF16)

    yt = _dot_nt(wb_ref[...], h_ref[...])
    tm = yt.shape[1]
    if mode == "v":
        pad_row = lax.broadcasted_iota(I32, (V_ROWS - LANES, KEY_TILE), 0)
        ones_rows = jnp.where(pad_row == 0, 1.0, 0.0).astype(BF16)
        for h in range(ATT_HEADS):
            for j in range(tm // KEY_TILE):
                tile = yt[h * LANES:(h + 1) * LANES, j * KEY_TILE:(j + 1) * KEY_TILE]
                o_ref[h, j, 0:LANES, :] = tile.astype(BF16)
                o_ref[h, j, LANES:V_ROWS, :] = ones_rows
        return

    head = ATT_HEAD_DIM if mode in ("q", "k") else IDX_HEAD_DIM
    half = head // ROPE_FRACTION // 2
    ang = f_ref[...] * pos_ref[...].astype(F32)
    cos, sin = jnp.cos(ang), jnp.sin(ang)

    def rotary(y):
        x1, x2 = y[0:half], y[half:2 * half]
        return jnp.concatenate([x1 * cos - x2 * sin, x2 * cos + x1 * sin, y[2 * half:]], axis=0)

    for hd in range(ATT_WIDTH // head):
        y = yt[hd * head:(hd + 1) * head, :]
        if mode in ("q", "k"):
            ms = jnp.mean(y * y, axis=0, keepdims=True)
            y = y * lax.rsqrt(ms + NORM_EPS) * g_ref[...]
        y = rotary(y)
        if mode == "q":
            o_ref[hd] = (y * ((ATT_HEAD_DIM ** -0.5) * LOG2E)).astype(BF16)
        elif mode == "iq":
            rows = slice((hd % 2) * head, (hd % 2 + 1) * head)
            o_ref[hd // 2, rows, :] = (y * (IDX_HEAD_DIM ** -0.5)).astype(BF16)
        else:
            o_ref[:, hd * LANES:(hd + 1) * LANES] = y.T.astype(BF16)
    if mode == "iq":
        rest = yt[ATT_WIDTH:, :]
        ikt = jnp.concatenate([rotary(rest[0:head]), rest[head:]], axis=0)
        o_refs[1][...] = ikt.T.astype(BF16)
        o_refs[2][...] = rest[head:head + IDX_HEADS, :] * (IDX_HEADS ** -0.5)


def _proj_t(h, wt, col0, n, pos_row, g_col, f_col, mode, tm=1024):
    m, d = h.shape
    tm = min(tm, m)
    if mode == "v":
        out_specs = [pl.BlockSpec((ATT_HEADS, tm // KEY_TILE, V_ROWS, KEY_TILE),
                                  lambda i: (0, i, 0, 0))]
        out_shape = [jax.ShapeDtypeStruct((ATT_HEADS, m // KEY_TILE, V_ROWS, KEY_TILE), BF16)]
    elif mode in ("q", "iq"):
        out_specs = [pl.BlockSpec((ATT_WIDTH // LANES, LANES, tm), lambda i: (0, 0, i))]
        out_shape = [jax.ShapeDtypeStruct((ATT_WIDTH // LANES, LANES, m), BF16)]
        if mode == "iq":
            out_specs += [pl.BlockSpec((tm, LANES), lambda i: (i, 0)),
                          pl.BlockSpec((IDX_HEADS, tm), lambda i: (0, i))]
            out_shape += [jax.ShapeDtypeStruct((m, LANES), BF16),
                          jax.ShapeDtypeStruct((IDX_HEADS, m), F32)]
    else:
        out_specs = [pl.BlockSpec((tm, n), lambda i: (i, 0))]
        out_shape = [jax.ShapeDtypeStruct((m, n), BF16)]
    out = pl.pallas_call(
        functools.partial(_proj_t_kernel, mode=mode),
        grid=(m // tm,),
        in_specs=[pl.BlockSpec((pl.Element(n), pl.Element(d)), lambda i: (col0, 0)),
                  pl.BlockSpec((tm, d), lambda i: (i, 0)),
                  pl.BlockSpec((1, tm), lambda i: (0, i)),
                  pl.BlockSpec(g_col.shape, lambda i: (0, 0)),
                  pl.BlockSpec(f_col.shape, lambda i: (0, 0))],
        out_specs=out_specs,
        out_shape=out_shape,
        scratch_shapes=[pltpu.VMEM((n, d), BF16)],
        compiler_params=_cparams(("arbitrary",)),
        name="in_proj_" + mode,
    )(wt, h, pos_row, g_col, f_col)
    return out if mode == "iq" else out[0]


NEG_BIG = -1e30


INT_MIN = -2 ** 31
INT_MAX = 2 ** 31 - 1
MAGNITUDE_BITS = 0x7FFFFFFF
KEY_NEG_INF = -2139095041
BOUND_SLACK = 1.02
MAX_STATIC_SHIFT = 60.0
COUNT_ROWS = 32


def _key_to_float(key):
    bits = key ^ ((key >> 31) & jnp.int32(MAGNITUDE_BITS))
    return lax.bitcast_convert_type(bits, F32)


def _dsa_kernel(bound_ref, qt_ref, iqt_ref, iwt_ref, k_ref, vt_ref, ik_ref, o_ref,
                sc_ref, m_ref, acc_ref, last_ref, *, tq, tk, topk, index_bits):
    qi = pl.program_id(1)
    n_kb = (qi * tq + tq - 1) // tk + 1
    key0 = lax.broadcasted_iota(I32, (tk, tq), 0)
    qidx = qi * tq + lax.broadcasted_iota(I32, (tk, tq), 1)
    iw = iwt_ref[...]

    def score_body(kb, carry):
        start = pl.multiple_of(kb * tk, tk)
        ikb = ik_ref[pl.ds(start, tk), 0:IDX_HEAD_DIM]
        s = jnp.zeros((tk, tq), F32)
        for h in range(IDX_HEADS):
            off = (h % 2) * IDX_HEAD_DIM
            d = _dot(ikb, iqt_ref[h // 2, off:off + IDX_HEAD_DIM, :])
            s = s + jnp.maximum(d, 0.0) * iw[h:h + 1, :]
        sc_ref[kb] = jnp.where(kb * tk + key0 <= qidx, s, -jnp.inf)
        return carry

    lax.fori_loop(0, n_kb, score_body, 0)

    def count(pred):
        def cnt_body(kb, acc):
            hit = pred(kb).reshape(tk // COUNT_ROWS, COUNT_ROWS, tq)
            for r in range(tk // COUNT_ROWS):
                acc = jnp.where(hit[r], acc + 1.0, acc)
            return acc

        acc = lax.fori_loop(0, n_kb, cnt_body, jnp.zeros((COUNT_ROWS, tq), F32))
        return jnp.sum(acc, axis=0, keepdims=True)

    def bit_body(i, cand):
        trial = cand ^ lax.shift_left(jnp.int32(1), 31 - i)
        trial_f = _key_to_float(trial)
        cnt = count(lambda kb: sc_ref[kb] >= trial_f)
        return jnp.where(cnt >= topk, trial, cand)

    cand = lax.fori_loop(0, 32, bit_body, jnp.full((1, tq), INT_MIN, I32))
    tau = _key_to_float(jnp.maximum(cand, jnp.int32(KEY_NEG_INF)))

    last_ref[...] = jnp.full_like(last_ref, INT_MAX)
    n_ge = count(lambda kb: sc_ref[kb] >= tau)

    @pl.when(jnp.max(n_ge) > topk)
    def _():
        need = topk - count(lambda kb: sc_ref[kb] > tau)

        def idx_body(i, last):
            trial = last | lax.shift_left(jnp.int32(1), index_bits - 1 - i)
            below = count(lambda kb: (sc_ref[kb] == tau) & (kb * tk + key0 < trial))
            return jnp.where(below < need, trial, last)

        last_ref[...] = lax.fori_loop(0, index_bits, idx_body, jnp.zeros((1, tq), I32))

    acc_ref[...] = jnp.zeros_like(acc_ref)
    bound = bound_ref[0]

    def logits(kb, h):
        start = pl.multiple_of(kb * tk, tk)
        return _dot(k_ref[pl.ds(start, tk), h * LANES:(h + 1) * LANES], qt_ref[h])

    last = last_ref[...]

    def selected(kb):
        s = sc_ref[kb]
        kidx = kb * tk + key0
        return ((s > tau) | ((s == tau) & (kidx <= last))) & (kidx <= qidx)

    @pl.when(bound <= MAX_STATIC_SHIFT)
    def _():
        def att_body(kb, carry):
            bias = jnp.where(selected(kb), -bound, -jnp.inf)
            for h in range(ATT_HEADS):
                p = jnp.exp2(logits(kb, h) + bias)
                acc_ref[h] += _dot(vt_ref[h, kb], p.astype(BF16))
            return carry

        lax.fori_loop(0, n_kb, att_body, 0)

    @pl.when(bound > MAX_STATIC_SHIFT)
    def _():
        m_ref[...] = jnp.full_like(m_ref, NEG_BIG)

        def att_body(kb, carry):
            bias = jnp.where(selected(kb), 0.0, -jnp.inf)
            for h in range(ATT_HEADS):
                s = logits(kb, h) + bias
                m_prev = m_ref[h]
                m_next = jnp.maximum(m_prev, jnp.max(s, axis=0, keepdims=True))
                p = jnp.exp2(s - m_next)
                alpha = jnp.exp2(m_prev - m_next)
                acc_ref[h] = alpha * acc_ref[h] + _dot(vt_ref[h, kb], p.astype(BF16))
                m_ref[h] = m_next
            return carry

        lax.fori_loop(0, n_kb, att_body, 0)

    for h in range(ATT_HEADS):
        out = acc_ref[h, 0:LANES, :] / acc_ref[h, LANES:LANES + 1, :]
        o_ref[:, h * LANES:(h + 1) * LANES] = out.T.astype(o_ref.dtype)


def _dsa(bound, qt, iqt, iwt, k, vt, ik, bsz, seq, tk, tq=512):
    tq = min(tq, seq)
    topk = min(TOPK_MAX, seq // 4)
    w = ATT_WIDTH
    nq = seq // tq
    nkb = seq // tk
    npair = iqt.shape[0]
    kern = functools.partial(_dsa_kernel, tq=tq, tk=tk, topk=topk,
                             index_bits=max(1, (seq - 1).bit_length()))
    return pl.pallas_call(
        kern,
        grid=(bsz, nq),
        in_specs=[pl.BlockSpec(memory_space=pltpu.SMEM),
                  pl.BlockSpec((ATT_HEADS, LANES, tq), lambda b, i: (0, 0, b * nq + i)),
                  pl.BlockSpec((npair, LANES, tq), lambda b, i: (0, 0, b * nq + i)),
                  pl.BlockSpec((IDX_HEADS, tq), lambda b, i: (0, b * nq + i)),
                  pl.BlockSpec((seq, w), lambda b, i: (b, 0)),
                  pl.BlockSpec((ATT_HEADS, nkb, V_ROWS, tk), lambda b, i: (0, b, 0, 0)),
                  pl.BlockSpec((seq, LANES), lambda b, i: (b, 0))],
        out_specs=pl.BlockSpec((tq, w), lambda b, i: (b * nq + i, 0)),
        out_shape=jax.ShapeDtypeStruct((bsz * seq, w), BF16),
        scratch_shapes=[pltpu.VMEM((nkb, tk, tq), F32),
                        pltpu.VMEM((ATT_HEADS, 1, tq), F32),
                        pltpu.VMEM((ATT_HEADS, V_ROWS, tq), F32),
                        pltpu.VMEM((1, tq), I32)],
        compiler_params=_cparams(("arbitrary", "arbitrary")),
        name="dsa",
    )(bound, qt, iqt, iwt, k, vt, ik)


def _rwkv_prep_kernel(rr_ref, rk_ref, rv_ref, wa_ref, xg_ref,
                      mr_ref, mk_ref, mv_ref, mwa_ref, mg_ref,
                      w0_ref, a0_ref, kk_ref, ka_ref, rkp_ref,
                      wup_ref, aup_ref, gup_ref, e_ref,
                      r_o, lw_o, k_o, v_o, kkn_o, bb_o, g_o, bon_o,
                      c_r, c_k, c_v, c_wa, c_g, *, tm):
    first = pl.program_id(1) == 0

    for carry_ref in (c_r, c_k, c_v, c_wa, c_g):
        @pl.when(first)
        def _(carry_ref=carry_ref):
            carry_ref[...] = jnp.zeros_like(carry_ref)

    def shift(y_ref, carry_ref, mu_ref, cols=slice(None)):
        y = y_ref[:, cols]
        prev_last = carry_ref[7:8, cols]
        rolled = pltpu.roll(y, 1, 0)
        rows = lax.broadcasted_iota(I32, y.shape, 0)
        yprev = jnp.where(rows == 0, prev_last, rolled)
        carry_ref[:, cols] = y[tm - 8:tm, :]
        return y + (yprev - y) * mu_ref[:, cols]

    wa = shift(wa_ref, c_wa, mwa_ref)
    xw = pltpu.roll(wa, TAIL_TILE - XW_LANE, 1)[:, :LORA_PAD]
    xa = pltpu.roll(wa, TAIL_TILE - XA_LANE, 1)[:, :LORA_PAD]
    xg = shift(xg_ref, c_g, mg_ref)
    w_raw = w0_ref[...] + _dot(jnp.tanh(xw).astype(BF16), wup_ref[...])
    a_pre = a0_ref[...] + _dot(xa.astype(BF16), aup_ref[...])
    g_o[...] = _dot(_sigmoid(xg).astype(BF16), gup_ref[...]).astype(BF16)

    e2 = e_ref[...]
    for p in range(RWKV_WIDTH // LANES):
        sl = slice(p * LANES, (p + 1) * LANES)
        r = shift(rr_ref, c_r, mr_ref, sl)
        k = shift(rk_ref, c_k, mk_ref, sl)
        v = shift(rv_ref, c_v, mv_ref, sl)
        z = -w_raw[:, sl]
        softplus = jnp.maximum(z, 0.0) + jnp.log(1.0 + jnp.exp(-jnp.abs(z)))
        lw_o[:, sl] = -jnp.exp(-softplus - 0.5)
        a = _sigmoid(a_pre[:, sl])
        kk = k * kk_ref[:, sl]
        ss = _dot((kk * kk).astype(BF16), e2)
        kk = kk / jnp.maximum(jnp.sqrt(ss), 1e-12)
        kmod = k * (1.0 + (a - 1.0) * ka_ref[:, sl])
        r_o[:, sl] = r.astype(BF16)
        k_o[:, sl] = kmod.astype(BF16)
        v_o[:, sl] = v.astype(BF16)
        kkn_o[:, sl] = kk.astype(BF16)
        bb_o[:, sl] = (kk * a).astype(BF16)
        rkr = _dot((r * kmod * rkp_ref[:, sl]).astype(BF16), e2)
        bon_o[:, sl] = (rkr * v).astype(BF16)


def _rwkv_prep(proj_r, mus, vecs, wup, aup, gup, e, bsz, seq, tm=256):
    w = RWKV_WIDTH
    per_b = seq // tm

    def wide(off):
        return pl.BlockSpec((tm, w), lambda b, i: (b * per_b + i, off // w))

    def tail(off):
        return pl.BlockSpec((tm, TAIL_TILE), lambda b, i: (b * per_b + i, off // TAIL_TILE))

    def const(shape):
        return pl.BlockSpec(shape, lambda b, i: (0, 0))

    out_blk = pl.BlockSpec((tm, w), lambda b, i: (b * per_b + i, 0))
    kern = functools.partial(_rwkv_prep_kernel, tm=tm)
    return pl.pallas_call(
        kern,
        grid=(bsz, per_b),
        in_specs=[wide(OFF_RR), wide(OFF_RK), wide(OFF_RV), tail(OFF_WA), tail(OFF_XG),
                  const((1, w)), const((1, w)), const((1, w)),
                  const((1, TAIL_TILE)), const((1, GATE_LORA)),
                  const((1, w)), const((1, w)), const((1, w)), const((1, w)), const((1, w)),
                  const((LORA_PAD, w)), const((LORA_PAD, w)), const((GATE_LORA, w)),
                  const((LANES, LANES))],
        out_specs=[out_blk] * 8,
        out_shape=[jax.ShapeDtypeStruct((bsz * seq, w), F32 if i == 1 else BF16) for i in range(8)],
        scratch_shapes=[pltpu.VMEM((8, w), F32)] * 3
        + [pltpu.VMEM((8, TAIL_TILE), F32), pltpu.VMEM((8, GATE_LORA), F32)],
        compiler_params=_cparams(("arbitrary", "arbitrary")),
        name="rwkv_prep",
    )(proj_r, proj_r, proj_r, proj_r, proj_r, *mus, *vecs, wup, aup, gup, e)


def _rwkv_chunk(rows, r_ref, lw_ref, k_ref, v_ref, kk_ref, bb_ref, g_ref, bon_ref, lg_ref, lb_ref,
                o_ref, z_ref):
    c = CHUNK
    lw = lw_ref[rows, :]
    tri = jnp.where(lax.broadcasted_iota(I32, (c, c), 1) <= lax.broadcasted_iota(I32, (c, c), 0),
                    1.0, 0.0).astype(BF16)
    hi = lw.astype(BF16)
    rem = lw - hi.astype(F32)
    mid = rem.astype(BF16)
    lo = (rem - mid.astype(F32)).astype(BF16)
    cum = _dot(tri, hi) + _dot(tri, mid) + _dot(tri, lo)
    p_in = jnp.exp(cum)
    p_ex = jnp.exp(cum - lw)
    p_inv = jnp.exp(-cum)
    p_end = p_in[c - 1:c, :]
    a_t = -kk_ref[rows, :].astype(F32) * p_ex
    r_t = r_ref[rows, :].astype(F32) * p_in
    b_h = bb_ref[rows, :].astype(F32) * p_inv
    k_h = k_ref[rows, :].astype(F32) * p_inv
    b_e = b_h * p_end
    k_e = k_h * p_end
    v = v_ref[rows, :].astype(F32)

    n2 = 2 * c
    lane = lax.broadcasted_iota(I32, (1, LANES), 1)
    head0 = lane < RWKV_HEAD_DIM
    ri = lax.broadcasted_iota(I32, (n2, n2), 0)
    ci = lax.broadcasted_iota(I32, (n2, n2), 1)
    same = (ri >= c) == (ci >= c)
    strict = same & (ci < ri)
    incl = same & (ci <= ri)
    eye = ri == ci
    own = ((lax.broadcasted_iota(I32, (n2, LANES), 0) >= c)
           == (lax.broadcasted_iota(I32, (n2, LANES), 1) >= RWKV_HEAD_DIM))
    inv_n = 1.0 / RWKV_HEAD_DIM

    def stack(y):
        return jnp.concatenate([jnp.where(head0, y, 0.0), jnp.where(head0, 0.0, y)], axis=0)

    pairs = range(RWKV_WIDTH // LANES)
    sls = [slice(p * LANES, (p + 1) * LANES) for p in pairs]
    a_s = [stack(a_t[:, sl]) for sl in sls]
    r_s = [stack(r_t[:, sl]) for sl in sls]
    v_s = [stack(v[:, sl]).astype(BF16) for sl in sls]
    g1 = [_dot_nt(jnp.concatenate([a_s[p], r_s[p]], axis=0).astype(BF16),
                  jnp.concatenate([stack(b_h[:, sls[p]]), stack(k_h[:, sls[p]])], axis=0).astype(BF16))
          for p in pairs]
    pw = [jnp.where(strict, g[:n2, :n2], 0.0).astype(BF16) for g in g1]
    a_rb = [jnp.where(incl, g[n2:, :n2], 0.0).astype(BF16) for g in g1]
    a_rk = [jnp.where(incl, g[n2:, n2:], 0.0).astype(BF16) for g in g1]
    akv = [_dot(jnp.where(strict, g1[p][:n2, n2:], 0.0).astype(BF16), v_s[p]) for p in pairs]
    half_turn = RWKV_HEAD_DIM

    def unpack(x):
        return jnp.where(own, x, 0.0), pltpu.roll(jnp.where(own, 0.0, x), half_turn, 1)

    xc = [a_s[p] + pltpu.roll(akv[p], half_turn, 1) for p in pairs]
    steps = int(np.log2(c))
    for i in range(steps):
        if i + 1 < steps:
            res = [_dot(pw[p], jnp.concatenate([pw[p], xc[p].astype(BF16)], axis=1)) for p in pairs]
            xc = [xc[p] + res[p][:, n2:] for p in pairs]
            pw = [res[p][:, :n2].astype(BF16) for p in pairs]
        else:
            xc = [xc[p] + _dot(pw[p], xc[p].astype(BF16)) for p in pairs]
    xcb = [x.astype(BF16) for x in xc]
    r2 = [unpack(_dot(a_rb[p], xcb[p])) for p in pairs]
    ov = [r2[p][1] + _dot(a_rk[p], v_s[p]) for p in pairs]
    mg = [unpack(_dot_tn(stack(b_e[:, sls[p]]).astype(BF16), xcb[p])) for p in pairs]
    kv = [_dot_tn(stack(k_e[:, sls[p]]).astype(BF16), v_s[p]) for p in pairs]
    for p in pairs:
        q_s = r_s[p] + r2[p][0]
        mmat = mg[p][0] + jnp.where(eye, p_end[:, sls[p]], 0.0)
        qm = jnp.concatenate([q_s, mmat], axis=0).astype(BF16)
        res = _dot(qm, z_ref[p].astype(BF16))
        z_ref[p] = res[n2:] + mg[p][1] + kv[p]
        o_s = res[:n2] + ov[p]
        mean = jnp.sum(o_s, axis=1, keepdims=True) * inv_n
        dev = jnp.where(own, o_s - mean, 0.0)
        var = jnp.sum(dev * dev, axis=1, keepdims=True) * inv_n
        y = dev * lax.rsqrt(var + LNX_EPS)
        y = (y[:c] + y[c:]) * lg_ref[:, sls[p]] + lb_ref[:, sls[p]]
        out = (y + bon_ref[rows, sls[p]].astype(F32)) * g_ref[rows, sls[p]].astype(F32)
        o_ref[rows, sls[p]] = out.astype(o_ref.dtype)


def _rwkv_core_kernel(*refs):
    z_ref = refs[-1]

    @pl.when(pl.program_id(1) == 0)
    def _():
        z_ref[...] = jnp.zeros_like(z_ref)

    for sub in range(CHUNKS_PER_STEP):
        _rwkv_chunk(slice(sub * CHUNK, (sub + 1) * CHUNK), *refs)


def _rwkv_core(r, lw, k, v, kk, bb, g, bon, lg, lb, bsz, seq):
    c = CHUNK * CHUNKS_PER_STEP
    w = RWKV_WIDTH
    per_b = seq // c
    blk = pl.BlockSpec((c, w), lambda b, i: (b * per_b + i, 0))
    vec = pl.BlockSpec((1, w), lambda b, i: (0, 0))
    return pl.pallas_call(
        _rwkv_core_kernel,
        grid=(bsz, per_b),
        in_specs=[blk] * 8 + [vec, vec],
        out_specs=blk,
        out_shape=jax.ShapeDtypeStruct((bsz * seq, w), BF16),
        scratch_shapes=[pltpu.VMEM((w // LANES, LANES, LANES), F32)],
        compiler_params=_cparams(("arbitrary", "arbitrary")),
        name="rwkv_core",
    )(r, lw, k, v, kk, bb, g, bon, lg, lb)


def _rope_freqs(head_dim):
    half = head_dim // ROPE_FRACTION // 2
    return (ROPE_THETA ** (-jnp.arange(half, dtype=F32) / half)).reshape(half, 1)


def kernel(x, c, positions, w_ada, b_ada, norm1_g, w_in, q_norm_g, k_norm_g, rwkv_mu, rwkv_w0,
           rwkv_w_up, rwkv_a0, rwkv_a_up, rwkv_g_up, rwkv_k_k, rwkv_k_a, rwkv_r_k, rwkv_lnx_g,
           rwkv_lnx_b, w_out, norm2_g, w_ffn_gate, w_ffn_up, w_ffn_down):
    bsz, seq, d = x.shape
    depth = w_ada.shape[0]
    m = bsz * seq
    pos_row = positions.reshape(1, m)
    fa_col = _rope_freqs(ATT_HEAD_DIM)
    fi_col = _rope_freqs(IDX_HEAD_DIM)
    hd = RWKV_HEAD_DIM
    e = (jnp.arange(LANES)[:, None] // hd == jnp.arange(LANES)[None, :] // hd).astype(BF16)
    x2 = x.reshape(m, d)

    for l in range(depth):
        mod = _adaln(c, w_ada[l], b_ada[l])
        mod3 = mod.reshape(bsz * 6, 1, d)

        h1 = _norm(x2, norm1_g[l].reshape(1, d), mod3, seq)
        w_in_t = w_in[l].T
        n_rkv = 3 * RWKV_WIDTH // RKV_TILE
        proj_r = _matmul_nt(
            h1, w_in_t,
            lambda j: jnp.where(j < n_rkv, ATT_COLS + j * RKV_TILE, TAIL_COL0),
            n_rkv + 1, RKV_TILE, "in_proj_rkv")

        none = jnp.zeros((SUBLANES, 1), F32)
        aw = ATT_WIDTH
        qt = _proj_t(h1, w_in_t, OFF_Q, aw, pos_row, q_norm_g[l].reshape(-1, 1), fa_col, "q")
        kn = _proj_t(h1, w_in_t, OFF_K, aw, pos_row, k_norm_g[l].reshape(-1, 1), fa_col, "k")
        vt = _proj_t(h1, w_in_t, OFF_V, aw, pos_row, none, none, "v")
        iqt, ik, iwt = _proj_t(h1, w_in_t, OFF_IQ, aw + LANES, pos_row, none, fi_col, "iq")
        bound = (ATT_HEAD_DIM ** 0.5 * LOG2E * BOUND_SLACK
                 * jnp.max(jnp.abs(q_norm_g[l])) * jnp.max(jnp.abs(k_norm_g[l])))
        att = _dsa(bound.reshape(1).astype(F32), qt, iqt, iwt, kn, vt, ik, bsz, seq, KEY_TILE)

        mu = rwkv_mu[l]
        w3 = 3 * RWKV_WIDTH

        n_wa = DECAY_LORA + AAA_LORA
        mu_wa = jnp.zeros((1, TAIL_TILE), F32).at[0, XW_LANE:XW_LANE + n_wa].set(mu[w3:w3 + n_wa])
        mus = [mu[0:RWKV_WIDTH].reshape(1, -1), mu[RWKV_WIDTH:2 * RWKV_WIDTH].reshape(1, -1),
               mu[2 * RWKV_WIDTH:w3].reshape(1, -1), mu_wa, mu[w3 + n_wa:].reshape(1, -1)]
        vecs = [rwkv_w0[l].reshape(1, -1), rwkv_a0[l].reshape(1, -1), rwkv_k_k[l].reshape(1, -1),
                rwkv_k_a[l].reshape(1, -1), rwkv_r_k[l].reshape(1, -1)]

        def pad_rows(wm):
            return jnp.zeros((LORA_PAD, wm.shape[1]), F32).at[:wm.shape[0]].set(wm).astype(BF16)

        r, lw, km, vv, kk, bb, g, bon = _rwkv_prep(
            proj_r, mus, vecs, pad_rows(rwkv_w_up[l]), pad_rows(rwkv_a_up[l]),
            rwkv_g_up[l].astype(BF16), e, bsz, seq)
        rw = _rwkv_core(r, lw, km, vv, kk, bb, g, bon, rwkv_lnx_g[l].reshape(1, -1),
                        rwkv_lnx_b[l].reshape(1, -1), bsz, seq)

        x2, h2 = _out_proj(att, rw, w_out[l], x2, mod3, norm2_g[l].reshape(1, d), seq)

        hglu = _ffn_glu(h2, w_ffn_gate[l], w_ffn_up[l])
        x2 = _ffn_down(hglu, w_ffn_down[l], x2, mod3, seq)
    return x2.reshape(bsz, seq, d)
```

```python
import functools

import jax
import jax.numpy as jnp
import numpy as np
from jax import lax
from jax.experimental import pallas as pl
from jax.experimental.pallas import tpu as pltpu

F32 = jnp.float32
BF16 = jnp.bfloat16
I32 = jnp.int32

D_MODEL = 2048
ATT_HEADS = 8
ATT_HEAD_DIM = 128
ATT_WIDTH = ATT_HEADS * ATT_HEAD_DIM
RWKV_WIDTH = D_MODEL - ATT_WIDTH
RWKV_HEAD_DIM = 64
RWKV_HEADS = RWKV_WIDTH // RWKV_HEAD_DIM
IDX_HEADS = 16
IDX_HEAD_DIM = 64
TOPK_MAX = 256
ROPE_THETA = 500000.0
ROPE_FRACTION = 4
DECAY_LORA = 96
AAA_LORA = 96
GATE_LORA = 256
NORM_EPS = 1e-6
LNX_EPS = 64e-5

LANES = 128
SUBLANES = 8
LORA_PAD = 128
ATT_COLS = 4 * ATT_WIDTH + IDX_HEAD_DIM + IDX_HEADS
IN_COLS = ATT_COLS + 3 * RWKV_WIDTH + DECAY_LORA + AAA_LORA + GATE_LORA
OFF_Q, OFF_K, OFF_V, OFF_IQ = 0, 1024, 2048, 3072
OFF_RR, OFF_RK, OFF_RV = 0, 1024, 2048
TAIL_TILE = 256
TAIL_COL0 = IN_COLS - 2 * TAIL_TILE
XW_LANE = IN_COLS - GATE_LORA - AAA_LORA - DECAY_LORA - TAIL_COL0
XA_LANE = XW_LANE + DECAY_LORA
OFF_WA, OFF_XG = 0, TAIL_TILE

LOG2E = 1.4426950408889634
V_ROWS = 144
KEY_TILE = 512
CHUNKS_PER_STEP = 4
CHUNK = 64
VMEM_LIMIT = 56 * 1024 * 1024


def _cparams(sem):
    return pltpu.CompilerParams(dimension_semantics=sem, vmem_limit_bytes=VMEM_LIMIT)


def _dot(a, b):
    return jnp.dot(a, b, preferred_element_type=F32)


def _dot_nt(a, b):
    return lax.dot_general(a, b, (((1,), (1,)), ((), ())), preferred_element_type=F32)


def _dot_tn(a, b):
    return lax.dot_general(a, b, (((0,), (0,)), ((), ())), preferred_element_type=F32)


def _sigmoid(x):
    return 1.0 / (1.0 + jnp.exp(-x))


def _adaln_kernel(c_ref, w_ref, b_ref, o_ref):
    c = c_ref[...]
    ca = c * _sigmoid(c)
    o_ref[...] = _dot(ca.astype(BF16), w_ref[...].astype(BF16)) + b_ref[...]


def _adaln(c, w, b):
    bsz, d = c.shape
    n = w.shape[1]
    rows = SUBLANES
    cp = jnp.zeros((rows, d), F32).at[:bsz].set(c)
    tn = 1024
    out = pl.pallas_call(
        _adaln_kernel,
        grid=(n // tn,),
        in_specs=[pl.BlockSpec((rows, d), lambda j: (0, 0)),
                  pl.BlockSpec((d, tn), lambda j: (0, j)),
                  pl.BlockSpec((1, tn), lambda j: (0, j))],
        out_specs=pl.BlockSpec((rows, tn), lambda j: (0, j)),
        out_shape=jax.ShapeDtypeStruct((rows, n), F32),
        compiler_params=_cparams(("arbitrary",)),
        name="adaln",
    )(cp, w, b.reshape(1, n))
    return out[:bsz]


def _norm_mod(x, g, sc, sh):
    ms = jnp.mean(x * x, axis=-1, keepdims=True)
    y = x * lax.rsqrt(ms + NORM_EPS)
    y = y * g
    return y * (1.0 + sc) + sh


def _norm_kernel(x_ref, g_ref, sc_ref, sh_ref, o_ref):
    o_ref[...] = _norm_mod(x_ref[...], g_ref[...], sc_ref[...], sh_ref[...]).astype(BF16)


def _norm(x2, g, mod3, seq, tm=512):
    m, d = x2.shape
    per_b = seq // tm
    return pl.pallas_call(
        _norm_kernel,
        grid=(m // tm,),
        in_specs=[pl.BlockSpec((tm, d), lambda i: (i, 0)),
                  pl.BlockSpec((1, d), lambda i: (0, 0)),
                  pl.BlockSpec((None, 1, d), lambda i: ((i // per_b) * 6 + 1, 0, 0)),
                  pl.BlockSpec((None, 1, d), lambda i: ((i // per_b) * 6 + 0, 0, 0))],
        out_specs=pl.BlockSpec((tm, d), lambda i: (i, 0)),
        out_shape=jax.ShapeDtypeStruct((m, d), BF16),
        compiler_params=_cparams(("arbitrary",)),
        name="norm1",
    )(x2, g, mod3, mod3)


def _mm_nt_kernel(h_ref, wt_ref, o_ref, wb_ref):
    @pl.when(pl.program_id(1) == 0)
    def _():
        wb_ref[...] = wt_ref[...].astype(BF16)

    o_ref[...] = _dot_nt(h_ref[...], wb_ref[...])


def _matmul_nt(h, wt, col_of_tile, n_tiles, tn, name, tm=1024):
    m, d = h.shape

    def w_index(j, i):
        col = col_of_tile(j)
        return (col if isinstance(col, int) else pl.multiple_of(col, SUBLANES)), 0

    return pl.pallas_call(
        _mm_nt_kernel,
        grid=(n_tiles, m // tm),
        in_specs=[pl.BlockSpec((tm, d), lambda j, i: (i, 0)),
                  pl.BlockSpec((pl.Element(tn), pl.Element(d)), w_index)],
        out_specs=pl.BlockSpec((tm, tn), lambda j, i: (i, j)),
        out_shape=jax.ShapeDtypeStruct((m, n_tiles * tn), F32),
        scratch_shapes=[pltpu.VMEM((tn, d), BF16)],
        compiler_params=_cparams(("arbitrary", "arbitrary")),
        name=name,
    )(h, wt)


def _ffn_glu_kernel(h_ref, wg_ref, wu_ref, o_ref, wgb_ref, wub_ref):
    @pl.when(pl.program_id(1) == 0)
    def _():
        wgb_ref[...] = wg_ref[...].astype(BF16)
        wub_ref[...] = wu_ref[...].astype(BF16)

    h = h_ref[...]
    a = _dot(h, wgb_ref[...])
    u = _dot(h, wub_ref[...])
    o_ref[...] = (a * _sigmoid(a) * u).astype(o_ref.dtype)


def _ffn_glu(h, wg, wu, tm=1024, tn=512):
    m, d = h.shape
    tm = min(tm, m)
    n = wg.shape[1]
    return pl.pallas_call(
        _ffn_glu_kernel,
        grid=(n // tn, m // tm),
        in_specs=[pl.BlockSpec((tm, d), lambda j, i: (i, 0)),
                  pl.BlockSpec((d, tn), lambda j, i: (0, j)),
                  pl.BlockSpec((d, tn), lambda j, i: (0, j))],
        out_specs=pl.BlockSpec((tm, tn), lambda j, i: (i, j)),
        out_shape=jax.ShapeDtypeStruct((m, n), BF16),
        scratch_shapes=[pltpu.VMEM((d, tn), BF16)] * 2,
        compiler_params=_cparams(("arbitrary", "arbitrary")),
        name="ffn_glu",
    )(h, wg, wu)


def _out_proj_kernel(a_ref, r_ref, wa_ref, wr_ref, x_ref, gt_ref, g_ref, sc_ref, sh_ref,
                     x1_ref, h2_ref, wab_ref, wrb_ref):
    @pl.when(pl.program_id(0) == 0)
    def _():
        wab_ref[...] = wa_ref[...].astype(BF16)
        wrb_ref[...] = wr_ref[...].astype(BF16)

    mixed = _dot(a_ref[...], wab_ref[...]) + _dot(r_ref[...], wrb_ref[...])
    x1 = x_ref[...] + gt_ref[...] * mixed
    x1_ref[...] = x1
    h2_ref[...] = _norm_mod(x1, g_ref[...], sc_ref[...], sh_ref[...]).astype(BF16)


def _out_proj(att, rwkv, w_out, x2, mod3, g2, seq, tm=512):
    m, ka = att.shape
    kr = rwkv.shape[1]
    n = w_out.shape[1]
    per_b = seq // tm
    once = pl.Buffered(1)

    def mod_row(j):
        return pl.BlockSpec((None, 1, n), lambda i: ((i // per_b) * 6 + j, 0, 0))

    row = pl.BlockSpec((tm, n), lambda i: (i, 0))
    return pl.pallas_call(
        _out_proj_kernel,
        grid=(m // tm,),
        in_specs=[pl.BlockSpec((tm, ka), lambda i: (i, 0)),
                  pl.BlockSpec((tm, kr), lambda i: (i, 0)),
                  pl.BlockSpec((ka, n), lambda i: (0, 0), pipeline_mode=once),
                  pl.BlockSpec((kr, n), lambda i: (ka // kr, 0), pipeline_mode=once),
                  row, mod_row(2),
                  pl.BlockSpec((1, n), lambda i: (0, 0)), mod_row(4), mod_row(3)],
        out_specs=[row, row],
        out_shape=[jax.ShapeDtypeStruct((m, n), F32), jax.ShapeDtypeStruct((m, n), BF16)],
        scratch_shapes=[pltpu.VMEM((ka, n), BF16), pltpu.VMEM((kr, n), BF16)],
        compiler_params=_cparams(("arbitrary",)),
        name="out_proj",
    )(att, rwkv, w_out, w_out, x2, mod3, g2, mod3, mod3)


def _ffn_down_kernel(h_ref, w_ref, x_ref, gt_ref, o_ref, wb_ref):
    @pl.when(pl.program_id(1) == 0)
    def _():
        wb_ref[...] = w_ref[...].astype(BF16)

    o_ref[...] = x_ref[...] + gt_ref[...] * _dot(h_ref[...], wb_ref[...])


def _ffn_down(h, w, x2, mod3, seq, tm=512, tn=512):
    m, kdim = h.shape
    n = w.shape[1]
    per_b = seq // tm
    return pl.pallas_call(
        _ffn_down_kernel,
        grid=(n // tn, m // tm),
        in_specs=[pl.BlockSpec((tm, kdim), lambda j, i: (i, 0)),
                  pl.BlockSpec((kdim, tn), lambda j, i: (0, j)),
                  pl.BlockSpec((tm, tn), lambda j, i: (i, j)),
                  pl.BlockSpec((None, 1, tn), lambda j, i: ((i // per_b) * 6 + 5, 0, j))],
        out_specs=pl.BlockSpec((tm, tn), lambda j, i: (i, j)),
        out_shape=jax.ShapeDtypeStruct((m, n), F32),
        scratch_shapes=[pltpu.VMEM((kdim, tn), BF16)],
        compiler_params=_cparams(("arbitrary", "arbitrary")),
        name="ffn_down",
    )(h, w, x2, mod3)


def _proj_t_kernel(wt_ref, h_ref, pos_ref, g_ref, f_ref, *refs, mode):
    *o_refs, wb_ref = refs
    o_ref = o_refs[0]

    @pl.when(pl.program_id(0) == 0)
    def _():
        wb_ref[...] = wt_ref[...].astype(BF16)

    yt = _dot_nt(wb_ref[...], h_ref[...])
    tm = yt.shape[1]
    if mode == "v":
        pad_row = lax.broadcasted_iota(I32, (V_ROWS - LANES, KEY_TILE), 0)
        ones_rows = jnp.where(pad_row == 0, 1.0, 0.0).astype(BF16)
        for h in range(ATT_HEADS):
            for j in range(tm // KEY_TILE):
                tile = yt[h * LANES:(h + 1) * LANES, j * KEY_TILE:(j + 1) * KEY_TILE]
                o_ref[h, j, 0:LANES, :] = tile.astype(BF16)
                o_ref[h, j, LANES:V_ROWS, :] = ones_rows
        return

    head = ATT_HEAD_DIM if mode in ("q", "k") else IDX_HEAD_DIM
    half = head // ROPE_FRACTION // 2
    ang = f_ref[...] * pos_ref[...].astype(F32)
    cos, sin = jnp.cos(ang), jnp.sin(ang)

    def rotary(y):
        x1, x2 = y[0:half], y[half:2 * half]
        return jnp.concatenate([x1 * cos - x2 * sin, x2 * cos + x1 * sin, y[2 * half:]], axis=0)

    for hd in range(ATT_WIDTH // head):
        y = yt[hd * head:(hd + 1) * head, :]
        if mode in ("q", "k"):
            ms = jnp.mean(y * y, axis=0, keepdims=True)
            y = y * lax.rsqrt(ms + NORM_EPS) * g_ref[...]
        y = rotary(y)
        if mode == "q":
            o_ref[hd] = (y * ((ATT_HEAD_DIM ** -0.5) * LOG2E)).astype(BF16)
        elif mode == "iq":
            rows = slice((hd % 2) * head, (hd % 2 + 1) * head)
            o_ref[hd // 2, rows, :] = (y * (IDX_HEAD_DIM ** -0.5)).astype(BF16)
        else:
            o_ref[:, hd * LANES:(hd + 1) * LANES] = y.T.astype(BF16)
    if mode == "iq":
        rest = yt[ATT_WIDTH:, :]
        ikt = jnp.concatenate([rotary(rest[0:head]), rest[head:]], axis=0)
        o_refs[1][...] = ikt.T.astype(BF16)
        o_refs[2][...] = rest[head:head + IDX_HEADS, :] * (IDX_HEADS ** -0.5)


def _proj_t(h, wt, col0, n, pos_row, g_col, f_col, mode, tm=1024):
    m, d = h.shape
    tm = min(tm, m)
    if mode == "v":
        out_specs = [pl.BlockSpec((ATT_HEADS, tm // KEY_TILE, V_ROWS, KEY_TILE),
                                  lambda i: (0, i, 0, 0))]
        out_shape = [jax.ShapeDtypeStruct((ATT_HEADS, m // KEY_TILE, V_ROWS, KEY_TILE), BF16)]
    elif mode in ("q", "iq"):
        out_specs = [pl.BlockSpec((ATT_WIDTH // LANES, LANES, tm), lambda i: (0, 0, i))]
        out_shape = [jax.ShapeDtypeStruct((ATT_WIDTH // LANES, LANES, m), BF16)]
        if mode == "iq":
            out_specs += [pl.BlockSpec((tm, LANES), lambda i: (i, 0)),
                          pl.BlockSpec((IDX_HEADS, tm), lambda i: (0, i))]
            out_shape += [jax.ShapeDtypeStruct((m, LANES), BF16),
                          jax.ShapeDtypeStruct((IDX_HEADS, m), F32)]
    else:
        out_specs = [pl.BlockSpec((tm, n), lambda i: (i, 0))]
        out_shape = [jax.ShapeDtypeStruct((m, n), BF16)]
    out = pl.pallas_call(
        functools.partial(_proj_t_kernel, mode=mode),
        grid=(m // tm,),
        in_specs=[pl.BlockSpec((pl.Element(n), pl.Element(d)), lambda i: (col0, 0)),
                  pl.BlockSpec((tm, d), lambda i: (i, 0)),
                  pl.BlockSpec((1, tm), lambda i: (0, i)),
                  pl.BlockSpec(g_col.shape, lambda i: (0, 0)),
                  pl.BlockSpec(f_col.shape, lambda i: (0, 0))],
        out_specs=out_specs,
        out_shape=out_shape,
        scratch_shapes=[pltpu.VMEM((n, d), BF16)],
        compiler_params=_cparams(("arbitrary",)),
        name="in_proj_" + mode,
    )(wt, h, pos_row, g_col, f_col)
    return out if mode == "iq" else out[0]


NEG_BIG = -1e30


INT_MIN = -2 ** 31
INT_MAX = 2 ** 31 - 1
MAGNITUDE_BITS = 0x7FFFFFFF
KEY_NEG_INF = -2139095041
BOUND_SLACK = 1.02
MAX_STATIC_SHIFT = 60.0
COUNT_ROWS = 32


def _key_to_float(key):
    bits = key ^ ((key >> 31) & jnp.int32(MAGNITUDE_BITS))
    return lax.bitcast_convert_type(bits, F32)


def _dsa_kernel(bound_ref, qt_ref, iqt_ref, iwt_ref, k_ref, vt_ref, ik_ref, o_ref,
                sc_ref, m_ref, acc_ref, last_ref, *, tq, tk, topk, index_bits):
    qi = pl.program_id(1)
    n_kb = (qi * tq + tq - 1) // tk + 1
    key0 = lax.broadcasted_iota(I32, (tk, tq), 0)
    qidx = qi * tq + lax.broadcasted_iota(I32, (tk, tq), 1)
    iw = iwt_ref[...]

    def score_body(kb, carry):
        start = pl.multiple_of(kb * tk, tk)
        ikb = ik_ref[pl.ds(start, tk), 0:IDX_HEAD_DIM]
        s = jnp.zeros((tk, tq), F32)
        for h in range(IDX_HEADS):
            off = (h % 2) * IDX_HEAD_DIM
            d = _dot(ikb, iqt_ref[h // 2, off:off + IDX_HEAD_DIM, :])
            s = s + jnp.maximum(d, 0.0) * iw[h:h + 1, :]
        sc_ref[kb] = jnp.where(kb * tk + key0 <= qidx, s, -jnp.inf)
        return carry

    lax.fori_loop(0, n_kb, score_body, 0)

    def count(pred):
        def cnt_body(kb, acc):
            hit = pred(kb).reshape(tk // COUNT_ROWS, COUNT_ROWS, tq)
            for r in range(tk // COUNT_ROWS):
                acc = jnp.where(hit[r], acc + 1.0, acc)
            return acc

        acc = lax.fori_loop(0, n_kb, cnt_body, jnp.zeros((COUNT_ROWS, tq), F32))
        return jnp.sum(acc, axis=0, keepdims=True)

    def bit_body(i, cand):
        trial = cand ^ lax.shift_left(jnp.int32(1), 31 - i)
        trial_f = _key_to_float(trial)
        cnt = count(lambda kb: sc_ref[kb] >= trial_f)
        return jnp.where(cnt >= topk, trial, cand)

    cand = lax.fori_loop(0, 32, bit_body, jnp.full((1, tq), INT_MIN, I32))
    tau = _key_to_float(jnp.maximum(cand, jnp.int32(KEY_NEG_INF)))

    last_ref[...] = jnp.full_like(last_ref, INT_MAX)
    n_ge = count(lambda kb: sc_ref[kb] >= tau)

    @pl.when(jnp.max(n_ge) > topk)
    def _():
        need = topk - count(lambda kb: sc_ref[kb] > tau)

        def idx_body(i, last):
            trial = last | lax.shift_left(jnp.int32(1), index_bits - 1 - i)
            below = count(lambda kb: (sc_ref[kb] == tau) & (kb * tk + key0 < trial))
            return jnp.where(below < need, trial, last)

        last_ref[...] = lax.fori_loop(0, index_bits, idx_body, jnp.zeros((1, tq), I32))

    acc_ref[...] = jnp.zeros_like(acc_ref)
    bound = bound_ref[0]

    def logits(kb, h):
        start = pl.multiple_of(kb * tk, tk)
        return _dot(k_ref[pl.ds(start, tk), h * LANES:(h + 1) * LANES], qt_ref[h])

    last = last_ref[...]

    def selected(kb):
        s = sc_ref[kb]
        kidx = kb * tk + key0
        return ((s > tau) | ((s == tau) & (kidx <= last))) & (kidx <= qidx)

    @pl.when(bound <= MAX_STATIC_SHIFT)
    def _():
        def att_body(kb, carry):
            bias = jnp.where(selected(kb), -bound, -jnp.inf)
            for h in range(ATT_HEADS):
                p = jnp.exp2(logits(kb, h) + bias)
                acc_ref[h] += _dot(vt_ref[h, kb], p.astype(BF16))
            return carry

        lax.fori_loop(0, n_kb, att_body, 0)

    @pl.when(bound > MAX_STATIC_SHIFT)
    def _():
        m_ref[...] = jnp.full_like(m_ref, NEG_BIG)

        def att_body(kb, carry):
            bias = jnp.where(selected(kb), 0.0, -jnp.inf)
            for h in range(ATT_HEADS):
                s = logits(kb, h) + bias
                m_prev = m_ref[h]
                m_next = jnp.maximum(m_prev, jnp.max(s, axis=0, keepdims=True))
                p = jnp.exp2(s - m_next)
                alpha = jnp.exp2(m_prev - m_next)
                acc_ref[h] = alpha * acc_ref[h] + _dot(vt_ref[h, kb], p.astype(BF16))
                m_ref[h] = m_next
            return carry

        lax.fori_loop(0, n_kb, att_body, 0)

    for h in range(ATT_HEADS):
        out = acc_ref[h, 0:LANES, :] / acc_ref[h, LANES:LANES + 1, :]
        o_ref[:, h * LANES:(h + 1) * LANES] = out.T.astype(o_ref.dtype)


def _dsa(bound, qt, iqt, iwt, k, vt, ik, bsz, seq, tk, tq=512):
    tq = min(tq, seq)
    topk = min(TOPK_MAX, seq // 4)
    w = ATT_WIDTH
    nq = seq // tq
    nkb = seq // tk
    npair = iqt.shape[0]
    kern = functools.partial(_dsa_kernel, tq=tq, tk=tk, topk=topk,
                             index_bits=max(1, (seq - 1).bit_length()))
    return pl.pallas_call(
        kern,
        grid=(bsz, nq),
        in_specs=[pl.BlockSpec(memory_space=pltpu.SMEM),
                  pl.BlockSpec((ATT_HEADS, LANES, tq), lambda b, i: (0, 0, b * nq + i)),
                  pl.BlockSpec((npair, LANES, tq), lambda b, i: (0, 0, b * nq + i)),
                  pl.BlockSpec((IDX_HEADS, tq), lambda b, i: (0, b * nq + i)),
                  pl.BlockSpec((seq, w), lambda b, i: (b, 0)),
                  pl.BlockSpec((ATT_HEADS, nkb, V_ROWS, tk), lambda b, i: (0, b, 0, 0)),
                  pl.BlockSpec((seq, LANES), lambda b, i: (b, 0))],
        out_specs=pl.BlockSpec((tq, w), lambda b, i: (b * nq + i, 0)),
        out_shape=jax.ShapeDtypeStruct((bsz * seq, w), BF16),
        scratch_shapes=[pltpu.VMEM((nkb, tk, tq), F32),
                        pltpu.VMEM((ATT_HEADS, 1, tq), F32),
                        pltpu.VMEM((ATT_HEADS, V_ROWS, tq), F32),
                        pltpu.VMEM((1, tq), I32)],
        compiler_params=_cparams(("arbitrary", "arbitrary")),
        name="dsa",
    )(bound, qt, iqt, iwt, k, vt, ik)


def _rwkv_prep_kernel(rr_ref, rk_ref, rv_ref, wa_ref, xg_ref,
                      mr_ref, mk_ref, mv_ref, mwa_ref, mg_ref,
                      w0_ref, a0_ref, kk_ref, ka_ref, rkp_ref,
                      wup_ref, aup_ref, gup_ref, e_ref,
                      r_o, lw_o, k_o, v_o, kkn_o, bb_o, g_o, bon_o,
                      c_r, c_k, c_v, c_wa, c_g, *, tm):
    first = pl.program_id(1) == 0

    for carry_ref in (c_r, c_k, c_v, c_wa, c_g):
        @pl.when(first)
        def _(carry_ref=carry_ref):
            carry_ref[...] = jnp.zeros_like(carry_ref)

    def shift(y_ref, carry_ref, mu_ref, cols=slice(None)):
        y = y_ref[:, cols]
        prev_last = carry_ref[7:8, cols]
        rolled = pltpu.roll(y, 1, 0)
        rows = lax.broadcasted_iota(I32, y.shape, 0)
        yprev = jnp.where(rows == 0, prev_last, rolled)
        carry_ref[:, cols] = y[tm - 8:tm, :]
        return y + (yprev - y) * mu_ref[:, cols]

    wa = shift(wa_ref, c_wa, mwa_ref)
    xw = pltpu.roll(wa, TAIL_TILE - XW_LANE, 1)[:, :LORA_PAD]
    xa = pltpu.roll(wa, TAIL_TILE - XA_LANE, 1)[:, :LORA_PAD]
    xg = shift(xg_ref, c_g, mg_ref)
    w_raw = w0_ref[...] + _dot(jnp.tanh(xw).astype(BF16), wup_ref[...])
    a_pre = a0_ref[...] + _dot(xa.astype(BF16), aup_ref[...])
    g_o[...] = _dot(_sigmoid(xg).astype(BF16), gup_ref[...]).astype(BF16)

    e2 = e_ref[...]
    for p in range(RWKV_WIDTH // LANES):
        sl = slice(p * LANES, (p + 1) * LANES)
        r = shift(rr_ref, c_r, mr_ref, sl)
        k = shift(rk_ref, c_k, mk_ref, sl)
        v = shift(rv_ref, c_v, mv_ref, sl)
        z = -w_raw[:, sl]
        softplus = jnp.maximum(z, 0.0) + jnp.log(1.0 + jnp.exp(-jnp.abs(z)))
        lw_o[:, sl] = -jnp.exp(-softplus - 0.5)
        a = _sigmoid(a_pre[:, sl])
        kk = k * kk_ref[:, sl]
        ss = _dot((kk * kk).astype(BF16), e2)
        kk = kk / jnp.maximum(jnp.sqrt(ss), 1e-12)
        kmod = k * (1.0 + (a - 1.0) * ka_ref[:, sl])
        r_o[:, sl] = r.astype(BF16)
        k_o[:, sl] = kmod.astype(BF16)
        v_o[:, sl] = v.astype(BF16)
        kkn_o[:, sl] = kk.astype(BF16)
        bb_o[:, sl] = (kk * a).astype(BF16)
        rkr = _dot((r * kmod * rkp_ref[:, sl]).astype(BF16), e2)
        bon_o[:, sl] = (rkr * v).astype(BF16)


def _rwkv_prep(proj_r, proj_t, mus, vecs, wup, aup, gup, e, bsz, seq, tm=256):
    w = RWKV_WIDTH
    per_b = seq // tm

    def wide(off):
        return pl.BlockSpec((tm, w), lambda b, i: (b * per_b + i, off // w))

    def tail(off):
        return pl.BlockSpec((tm, TAIL_TILE), lambda b, i: (b * per_b + i, off // TAIL_TILE))

    def const(shape):
        return pl.BlockSpec(shape, lambda b, i: (0, 0))

    out_blk = pl.BlockSpec((tm, w), lambda b, i: (b * per_b + i, 0))
    kern = functools.partial(_rwkv_prep_kernel, tm=tm)
    return pl.pallas_call(
        kern,
        grid=(bsz, per_b),
        in_specs=[wide(OFF_RR), wide(OFF_RK), wide(OFF_RV), tail(OFF_WA), tail(OFF_XG),
                  const((1, w)), const((1, w)), const((1, w)),
                  const((1, TAIL_TILE)), const((1, GATE_LORA)),
                  const((1, w)), const((1, w)), const((1, w)), const((1, w)), const((1, w)),
                  const((LORA_PAD, w)), const((LORA_PAD, w)), const((GATE_LORA, w)),
                  const((LANES, LANES))],
        out_specs=[out_blk] * 8,
        out_shape=[jax.ShapeDtypeStruct((bsz * seq, w), F32 if i == 1 else BF16) for i in range(8)],
        scratch_shapes=[pltpu.VMEM((8, w), F32)] * 3
        + [pltpu.VMEM((8, TAIL_TILE), F32), pltpu.VMEM((8, GATE_LORA), F32)],
        compiler_params=_cparams(("arbitrary", "arbitrary")),
        name="rwkv_prep",
    )(proj_r, proj_r, proj_r, proj_t, proj_t, *mus, *vecs, wup, aup, gup, e)


def _rwkv_chunk(rows, r_ref, lw_ref, k_ref, v_ref, kk_ref, bb_ref, g_ref, bon_ref, lg_ref, lb_ref,
                o_ref, z_ref):
    c = CHUNK
    lw = lw_ref[rows, :]
    tri = jnp.where(lax.broadcasted_iota(I32, (c, c), 1) <= lax.broadcasted_iota(I32, (c, c), 0),
                    1.0, 0.0).astype(BF16)
    hi = lw.astype(BF16)
    rem = lw - hi.astype(F32)
    mid = rem.astype(BF16)
    lo = (rem - mid.astype(F32)).astype(BF16)
    cum = _dot(tri, hi) + _dot(tri, mid) + _dot(tri, lo)
    p_in = jnp.exp(cum)
    p_ex = jnp.exp(cum - lw)
    p_inv = jnp.exp(-cum)
    p_end = p_in[c - 1:c, :]
    a_t = -kk_ref[rows, :].astype(F32) * p_ex
    r_t = r_ref[rows, :].astype(F32) * p_in
    b_h = bb_ref[rows, :].astype(F32) * p_inv
    k_h = k_ref[rows, :].astype(F32) * p_inv
    b_e = b_h * p_end
    k_e = k_h * p_end
    v = v_ref[rows, :].astype(F32)

    n2 = 2 * c
    lane = lax.broadcasted_iota(I32, (1, LANES), 1)
    head0 = lane < RWKV_HEAD_DIM
    ri = lax.broadcasted_iota(I32, (n2, n2), 0)
    ci = lax.broadcasted_iota(I32, (n2, n2), 1)
    same = (ri >= c) == (ci >= c)
    strict = same & (ci < ri)
    incl = same & (ci <= ri)
    eye = ri == ci
    own = ((lax.broadcasted_iota(I32, (n2, LANES), 0) >= c)
           == (lax.broadcasted_iota(I32, (n2, LANES), 1) >= RWKV_HEAD_DIM))
    inv_n = 1.0 / RWKV_HEAD_DIM

    def stack(y):
        return jnp.concatenate([jnp.where(head0, y, 0.0), jnp.where(head0, 0.0, y)], axis=0)

    pairs = range(RWKV_WIDTH // LANES)
    sls = [slice(p * LANES, (p + 1) * LANES) for p in pairs]
    a_s = [stack(a_t[:, sl]) for sl in sls]
    r_s = [stack(r_t[:, sl]) for sl in sls]
    v_s = [stack(v[:, sl]).astype(BF16) for sl in sls]
    g1 = [_dot_nt(jnp.concatenate([a_s[p], r_s[p]], axis=0).astype(BF16),
                  jnp.concatenate([stack(b_h[:, sls[p]]), stack(k_h[:, sls[p]])], axis=0).astype(BF16))
          for p in pairs]
    pw = [jnp.where(strict, g[:n2, :n2], 0.0).astype(BF16) for g in g1]
    a_rb = [jnp.where(incl, g[n2:, :n2], 0.0).astype(BF16) for g in g1]
    a_rk = [jnp.where(incl, g[n2:, n2:], 0.0).astype(BF16) for g in g1]
    akv = [_dot(jnp.where(strict, g1[p][:n2, n2:], 0.0).astype(BF16), v_s[p]) for p in pairs]
    half_turn = RWKV_HEAD_DIM

    def unpack(x):
        return jnp.where(own, x, 0.0), pltpu.roll(jnp.where(own, 0.0, x), half_turn, 1)

    xc = [a_s[p] + pltpu.roll(akv[p], half_turn, 1) for p in pairs]
    steps = int(np.log2(c))
    for i in range(steps):
        if i + 1 < steps:
            res = [_dot(pw[p], jnp.concatenate([pw[p], xc[p].astype(BF16)], axis=1)) for p in pairs]
            xc = [xc[p] + res[p][:, n2:] for p in pairs]
            pw = [res[p][:, :n2].astype(BF16) for p in pairs]
        else:
            xc = [xc[p] + _dot(pw[p], xc[p].astype(BF16)) for p in pairs]
    xcb = [x.astype(BF16) for x in xc]
    r2 = [unpack(_dot(a_rb[p], xcb[p])) for p in pairs]
    ov = [r2[p][1] + _dot(a_rk[p], v_s[p]) for p in pairs]
    mg = [unpack(_dot_tn(stack(b_e[:, sls[p]]).astype(BF16), xcb[p])) for p in pairs]
    kv = [_dot_tn(stack(k_e[:, sls[p]]).astype(BF16), v_s[p]) for p in pairs]
    for p in pairs:
        q_s = r_s[p] + r2[p][0]
        mmat = mg[p][0] + jnp.where(eye, p_end[:, sls[p]], 0.0)
        qm = jnp.concatenate([q_s, mmat], axis=0).astype(BF16)
        res = _dot(qm, z_ref[p].astype(BF16))
        z_ref[p] = res[n2:] + mg[p][1] + kv[p]
        o_s = res[:n2] + ov[p]
        mean = jnp.sum(o_s, axis=1, keepdims=True) * inv_n
        dev = jnp.where(own, o_s - mean, 0.0)
        var = jnp.sum(dev * dev, axis=1, keepdims=True) * inv_n
        y = dev * lax.rsqrt(var + LNX_EPS)
        y = (y[:c] + y[c:]) * lg_ref[:, sls[p]] + lb_ref[:, sls[p]]
        out = (y + bon_ref[rows, sls[p]].astype(F32)) * g_ref[rows, sls[p]].astype(F32)
        o_ref[rows, sls[p]] = out.astype(o_ref.dtype)


def _rwkv_core_kernel(*refs):
    z_ref = refs[-1]

    @pl.when(pl.program_id(1) == 0)
    def _():
        z_ref[...] = jnp.zeros_like(z_ref)

    for sub in range(CHUNKS_PER_STEP):
        _rwkv_chunk(slice(sub * CHUNK, (sub + 1) * CHUNK), *refs)


def _rwkv_core(r, lw, k, v, kk, bb, g, bon, lg, lb, bsz, seq):
    c = CHUNK * CHUNKS_PER_STEP
    w = RWKV_WIDTH
    per_b = seq // c
    blk = pl.BlockSpec((c, w), lambda b, i: (b * per_b + i, 0))
    vec = pl.BlockSpec((1, w), lambda b, i: (0, 0))
    return pl.pallas_call(
        _rwkv_core_kernel,
        grid=(bsz, per_b),
        in_specs=[blk] * 8 + [vec, vec],
        out_specs=blk,
        out_shape=jax.ShapeDtypeStruct((bsz * seq, w), BF16),
        scratch_shapes=[pltpu.VMEM((w // LANES, LANES, LANES), F32)],
        compiler_params=_cparams(("arbitrary", "arbitrary")),
        name="rwkv_core",
    )(r, lw, k, v, kk, bb, g, bon, lg, lb)


def _rope_freqs(head_dim):
    half = head_dim // ROPE_FRACTION // 2
    return (ROPE_THETA ** (-jnp.arange(half, dtype=F32) / half)).reshape(half, 1)


def kernel(x, c, positions, w_ada, b_ada, norm1_g, w_in, q_norm_g, k_norm_g, rwkv_mu, rwkv_w0,
           rwkv_w_up, rwkv_a0, rwkv_a_up, rwkv_g_up, rwkv_k_k, rwkv_k_a, rwkv_r_k, rwkv_lnx_g,
           rwkv_lnx_b, w_out, norm2_g, w_ffn_gate, w_ffn_up, w_ffn_down):
    bsz, seq, d = x.shape
    depth = w_ada.shape[0]
    m = bsz * seq
    pos_row = positions.reshape(1, m)
    fa_col = _rope_freqs(ATT_HEAD_DIM)
    fi_col = _rope_freqs(IDX_HEAD_DIM)
    hd = RWKV_HEAD_DIM
    e = (jnp.arange(LANES)[:, None] // hd == jnp.arange(LANES)[None, :] // hd).astype(BF16)
    x2 = x.reshape(m, d)

    for l in range(depth):
        mod = _adaln(c, w_ada[l], b_ada[l])
        mod3 = mod.reshape(bsz * 6, 1, d)

        h1 = _norm(x2, norm1_g[l].reshape(1, d), mod3, seq)
        w_in_t = w_in[l].T
        proj_r = _matmul_nt(h1, w_in_t, lambda j: ATT_COLS + j * RWKV_WIDTH, 3, RWKV_WIDTH,
                            "in_proj_rkv")
        proj_t = _matmul_nt(h1, w_in_t, lambda j: TAIL_COL0, 1, 2 * TAIL_TILE,
                            "in_proj_tail", tm=min(2048, m))

        none = jnp.zeros((SUBLANES, 1), F32)
        aw = ATT_WIDTH
        qt = _proj_t(h1, w_in_t, OFF_Q, aw, pos_row, q_norm_g[l].reshape(-1, 1), fa_col, "q")
        kn = _proj_t(h1, w_in_t, OFF_K, aw, pos_row, k_norm_g[l].reshape(-1, 1), fa_col, "k")
        vt = _proj_t(h1, w_in_t, OFF_V, aw, pos_row, none, none, "v")
        iqt, ik, iwt = _proj_t(h1, w_in_t, OFF_IQ, aw + LANES, pos_row, none, fi_col, "iq")
        bound = (ATT_HEAD_DIM ** 0.5 * LOG2E * BOUND_SLACK
                 * jnp.max(jnp.abs(q_norm_g[l])) * jnp.max(jnp.abs(k_norm_g[l])))
        att = _dsa(bound.reshape(1).astype(F32), qt, iqt, iwt, kn, vt, ik, bsz, seq, KEY_TILE)

        mu = rwkv_mu[l]
        w3 = 3 * RWKV_WIDTH

        n_wa = DECAY_LORA + AAA_LORA
        mu_wa = jnp.zeros((1, TAIL_TILE), F32).at[0, XW_LANE:XW_LANE + n_wa].set(mu[w3:w3 + n_wa])
        mus = [mu[0:RWKV_WIDTH].reshape(1, -1), mu[RWKV_WIDTH:2 * RWKV_WIDTH].reshape(1, -1),
               mu[2 * RWKV_WIDTH:w3].reshape(1, -1), mu_wa, mu[w3 + n_wa:].reshape(1, -1)]
        vecs = [rwkv_w0[l].reshape(1, -1), rwkv_a0[l].reshape(1, -1), rwkv_k_k[l].reshape(1, -1),
                rwkv_k_a[l].reshape(1, -1), rwkv_r_k[l].reshape(1, -1)]

        def pad_rows(wm):
            return jnp.zeros((LORA_PAD, wm.shape[1]), F32).at[:wm.shape[0]].set(wm).astype(BF16)

        r, lw, km, vv, kk, bb, g, bon = _rwkv_prep(
            proj_r, proj_t, mus, vecs, pad_rows(rwkv_w_up[l]), pad_rows(rwkv_a_up[l]),
            rwkv_g_up[l].astype(BF16), e, bsz, seq)
        rw = _rwkv_core(r, lw, km, vv, kk, bb, g, bon, rwkv_lnx_g[l].reshape(1, -1),
                        rwkv_lnx_b[l].reshape(1, -1), bsz, seq)

        x2, h2 = _out_proj(att, rw, w_out[l], x2, mod3, norm2_g[l].reshape(1, d), seq)

        hglu = _ffn_glu(h2, w_ffn_gate[l], w_ffn_up[l])
        x2 = _ffn_down(hglu, w_ffn_down[l], x2, mod3, seq)
    return x2.reshape(bsz, seq, d)
```

```python
import functools

import jax
import jax.numpy as jnp
import numpy as np
from jax import lax
from jax.experimental import pallas as pl
from jax.experimental.pallas import tpu as pltpu

F32 = jnp.float32
BF16 = jnp.bfloat16
I32 = jnp.int32

D_MODEL = 2048
ATT_HEADS = 8
ATT_HEAD_DIM = 128
ATT_WIDTH = ATT_HEADS * ATT_HEAD_DIM
RWKV_WIDTH = D_MODEL - ATT_WIDTH
RWKV_HEAD_DIM = 64
RWKV_HEADS = RWKV_WIDTH // RWKV_HEAD_DIM
IDX_HEADS = 16
IDX_HEAD_DIM = 64
TOPK_MAX = 256
ROPE_THETA = 500000.0
ROPE_FRACTION = 4
DECAY_LORA = 96
AAA_LORA = 96
GATE_LORA = 256
NORM_EPS = 1e-6
LNX_EPS = 64e-5

LANES = 128
SUBLANES = 8
LORA_PAD = 128
ATT_COLS = 4 * ATT_WIDTH + IDX_HEAD_DIM + IDX_HEADS
IN_COLS = ATT_COLS + 3 * RWKV_WIDTH + DECAY_LORA + AAA_LORA + GATE_LORA
OFF_Q, OFF_K, OFF_V, OFF_IQ = 0, 1024, 2048, 3072
OFF_RR, OFF_RK, OFF_RV = 0, 1024, 2048
TAIL_TILE = 256
TAIL_COL0 = IN_COLS - 2 * TAIL_TILE
XW_LANE = IN_COLS - GATE_LORA - AAA_LORA - DECAY_LORA - TAIL_COL0
XA_LANE = XW_LANE + DECAY_LORA
OFF_WA, OFF_XG = 0, TAIL_TILE

LOG2E = 1.4426950408889634
V_ROWS = 144
KEY_TILE = 512
CHUNKS_PER_STEP = 4
CHUNK = 64
VMEM_LIMIT = 56 * 1024 * 1024


def _cparams(sem):
    return pltpu.CompilerParams(dimension_semantics=sem, vmem_limit_bytes=VMEM_LIMIT)


def _dot(a, b):
    return jnp.dot(a, b, preferred_element_type=F32)


def _dot_nt(a, b):
    return lax.dot_general(a, b, (((1,), (1,)), ((), ())), preferred_element_type=F32)


def _dot_tn(a, b):
    return lax.dot_general(a, b, (((0,), (0,)), ((), ())), preferred_element_type=F32)


def _sigmoid(x):
    return 1.0 / (1.0 + jnp.exp(-x))


def _adaln_kernel(c_ref, w_ref, b_ref, o_ref):
    c = c_ref[...]
    ca = c * _sigmoid(c)
    o_ref[...] = _dot(ca.astype(BF16), w_ref[...].astype(BF16)) + b_ref[...]


def _adaln(c, w, b):
    bsz, d = c.shape
    n = w.shape[1]
    rows = SUBLANES
    cp = jnp.zeros((rows, d), F32).at[:bsz].set(c)
    tn = 1024
    out = pl.pallas_call(
        _adaln_kernel,
        grid=(n // tn,),
        in_specs=[pl.BlockSpec((rows, d), lambda j: (0, 0)),
                  pl.BlockSpec((d, tn), lambda j: (0, j)),
                  pl.BlockSpec((1, tn), lambda j: (0, j))],
        out_specs=pl.BlockSpec((rows, tn), lambda j: (0, j)),
        out_shape=jax.ShapeDtypeStruct((rows, n), F32),
        compiler_params=_cparams(("arbitrary",)),
        name="adaln",
    )(cp, w, b.reshape(1, n))
    return out[:bsz]


def _norm_mod(x, g, sc, sh):
    ms = jnp.mean(x * x, axis=-1, keepdims=True)
    y = x * lax.rsqrt(ms + NORM_EPS)
    y = y * g
    return y * (1.0 + sc) + sh


def _norm_kernel(x_ref, g_ref, sc_ref, sh_ref, o_ref):
    o_ref[...] = _norm_mod(x_ref[...], g_ref[...], sc_ref[...], sh_ref[...]).astype(BF16)


def _norm(x2, g, mod3, seq, tm=512):
    m, d = x2.shape
    per_b = seq // tm
    return pl.pallas_call(
        _norm_kernel,
        grid=(m // tm,),
        in_specs=[pl.BlockSpec((tm, d), lambda i: (i, 0)),
                  pl.BlockSpec((1, d), lambda i: (0, 0)),
                  pl.BlockSpec((None, 1, d), lambda i: ((i // per_b) * 6 + 1, 0, 0)),
                  pl.BlockSpec((None, 1, d), lambda i: ((i // per_b) * 6 + 0, 0, 0))],
        out_specs=pl.BlockSpec((tm, d), lambda i: (i, 0)),
        out_shape=jax.ShapeDtypeStruct((m, d), BF16),
        compiler_params=_cparams(("arbitrary",)),
        name="norm1",
    )(x2, g, mod3, mod3)


def _mm_nt_kernel(h_ref, wt_ref, o_ref, wb_ref):
    @pl.when(pl.program_id(1) == 0)
    def _():
        wb_ref[...] = wt_ref[...].astype(BF16)

    o_ref[...] = _dot_nt(h_ref[...], wb_ref[...])


def _matmul_nt(h, wt, col_of_tile, n_tiles, tn, name, tm=1024):
    m, d = h.shape

    def w_index(j, i):
        col = col_of_tile(j)
        return (col if isinstance(col, int) else pl.multiple_of(col, SUBLANES)), 0

    return pl.pallas_call(
        _mm_nt_kernel,
        grid=(n_tiles, m // tm),
        in_specs=[pl.BlockSpec((tm, d), lambda j, i: (i, 0)),
                  pl.BlockSpec((pl.Element(tn), pl.Element(d)), w_index)],
        out_specs=pl.BlockSpec((tm, tn), lambda j, i: (i, j)),
        out_shape=jax.ShapeDtypeStruct((m, n_tiles * tn), F32),
        scratch_shapes=[pltpu.VMEM((tn, d), BF16)],
        compiler_params=_cparams(("arbitrary", "arbitrary")),
        name=name,
    )(h, wt)


def _ffn_glu_kernel(h_ref, wg_ref, wu_ref, o_ref, wgb_ref, wub_ref):
    @pl.when(pl.program_id(1) == 0)
    def _():
        wgb_ref[...] = wg_ref[...].astype(BF16)
        wub_ref[...] = wu_ref[...].astype(BF16)

    h = h_ref[...]
    a = _dot(h, wgb_ref[...])
    u = _dot(h, wub_ref[...])
    o_ref[...] = (a * _sigmoid(a) * u).astype(o_ref.dtype)


def _ffn_glu(h, wg, wu, tm=1024, tn=512):
    m, d = h.shape
    tm = min(tm, m)
    n = wg.shape[1]
    return pl.pallas_call(
        _ffn_glu_kernel,
        grid=(n // tn, m // tm),
        in_specs=[pl.BlockSpec((tm, d), lambda j, i: (i, 0)),
                  pl.BlockSpec((d, tn), lambda j, i: (0, j)),
                  pl.BlockSpec((d, tn), lambda j, i: (0, j))],
        out_specs=pl.BlockSpec((tm, tn), lambda j, i: (i, j)),
        out_shape=jax.ShapeDtypeStruct((m, n), BF16),
        scratch_shapes=[pltpu.VMEM((d, tn), BF16)] * 2,
        compiler_params=_cparams(("arbitrary", "arbitrary")),
        name="ffn_glu",
    )(h, wg, wu)


def _out_proj_kernel(a_ref, r_ref, wa_ref, wr_ref, x_ref, gt_ref, g_ref, sc_ref, sh_ref,
                     x1_ref, h2_ref, wab_ref, wrb_ref):
    @pl.when(pl.program_id(0) == 0)
    def _():
        wab_ref[...] = wa_ref[...].astype(BF16)
        wrb_ref[...] = wr_ref[...].astype(BF16)

    mixed = _dot(a_ref[...], wab_ref[...]) + _dot(r_ref[...], wrb_ref[...])
    x1 = x_ref[...] + gt_ref[...] * mixed
    x1_ref[...] = x1
    h2_ref[...] = _norm_mod(x1, g_ref[...], sc_ref[...], sh_ref[...]).astype(BF16)


def _out_proj(att, rwkv, w_out, x2, mod3, g2, seq, tm=512):
    m, ka = att.shape
    kr = rwkv.shape[1]
    n = w_out.shape[1]
    per_b = seq // tm
    once = pl.Buffered(1)

    def mod_row(j):
        return pl.BlockSpec((None, 1, n), lambda i: ((i // per_b) * 6 + j, 0, 0))

    row = pl.BlockSpec((tm, n), lambda i: (i, 0))
    return pl.pallas_call(
        _out_proj_kernel,
        grid=(m // tm,),
        in_specs=[pl.BlockSpec((tm, ka), lambda i: (i, 0)),
                  pl.BlockSpec((tm, kr), lambda i: (i, 0)),
                  pl.BlockSpec((ka, n), lambda i: (0, 0), pipeline_mode=once),
                  pl.BlockSpec((kr, n), lambda i: (ka // kr, 0), pipeline_mode=once),
                  row, mod_row(2),
                  pl.BlockSpec((1, n), lambda i: (0, 0)), mod_row(4), mod_row(3)],
        out_specs=[row, row],
        out_shape=[jax.ShapeDtypeStruct((m, n), F32), jax.ShapeDtypeStruct((m, n), BF16)],
        scratch_shapes=[pltpu.VMEM((ka, n), BF16), pltpu.VMEM((kr, n), BF16)],
        compiler_params=_cparams(("arbitrary",)),
        name="out_proj",
    )(att, rwkv, w_out, w_out, x2, mod3, g2, mod3, mod3)


def _ffn_down_kernel(h_ref, w_ref, x_ref, gt_ref, o_ref, wb_ref):
    @pl.when(pl.program_id(1) == 0)
    def _():
        wb_ref[...] = w_ref[...].astype(BF16)

    o_ref[...] = x_ref[...] + gt_ref[...] * _dot(h_ref[...], wb_ref[...])


def _ffn_down(h, w, x2, mod3, seq, tm=512, tn=512):
    m, kdim = h.shape
    n = w.shape[1]
    per_b = seq // tm
    return pl.pallas_call(
        _ffn_down_kernel,
        grid=(n // tn, m // tm),
        in_specs=[pl.BlockSpec((tm, kdim), lambda j, i: (i, 0)),
                  pl.BlockSpec((kdim, tn), lambda j, i: (0, j)),
                  pl.BlockSpec((tm, tn), lambda j, i: (i, j)),
                  pl.BlockSpec((None, 1, tn), lambda j, i: ((i // per_b) * 6 + 5, 0, j))],
        out_specs=pl.BlockSpec((tm, tn), lambda j, i: (i, j)),
        out_shape=jax.ShapeDtypeStruct((m, n), F32),
        scratch_shapes=[pltpu.VMEM((kdim, tn), BF16)],
        compiler_params=_cparams(("arbitrary", "arbitrary")),
        name="ffn_down",
    )(h, w, x2, mod3)


def _proj_t_kernel(wt_ref, h_ref, pos_ref, g_ref, f_ref, *refs, mode):
    *o_refs, wb_ref = refs
    o_ref = o_refs[0]

    @pl.when(pl.program_id(0) == 0)
    def _():
        wb_ref[...] = wt_ref[...].astype(BF16)

    yt = _dot_nt(wb_ref[...], h_ref[...])
    tm = yt.shape[1]
    if mode == "v":
        pad_row = lax.broadcasted_iota(I32, (V_ROWS - LANES, KEY_TILE), 0)
        ones_rows = jnp.where(pad_row == 0, 1.0, 0.0).astype(BF16)
        for h in range(ATT_HEADS):
            for j in range(tm // KEY_TILE):
                tile = yt[h * LANES:(h + 1) * LANES, j * KEY_TILE:(j + 1) * KEY_TILE]
                o_ref[h, j, 0:LANES, :] = tile.astype(BF16)
                o_ref[h, j, LANES:V_ROWS, :] = ones_rows
        return

    head = ATT_HEAD_DIM if mode in ("q", "k") else IDX_HEAD_DIM
    half = head // ROPE_FRACTION // 2
    ang = f_ref[...] * pos_ref[...].astype(F32)
    cos, sin = jnp.cos(ang), jnp.sin(ang)

    def rotary(y):
        x1, x2 = y[0:half], y[half:2 * half]
        return jnp.concatenate([x1 * cos - x2 * sin, x2 * cos + x1 * sin, y[2 * half:]], axis=0)

    for hd in range(ATT_WIDTH // head):
        y = yt[hd * head:(hd + 1) * head, :]
        if mode in ("q", "k"):
            ms = jnp.mean(y * y, axis=0, keepdims=True)
            y = y * lax.rsqrt(ms + NORM_EPS) * g_ref[...]
        y = rotary(y)
        if mode == "q":
            o_ref[hd] = (y * ((ATT_HEAD_DIM ** -0.5) * LOG2E)).astype(BF16)
        elif mode == "iq":
            rows = slice((hd % 2) * head, (hd % 2 + 1) * head)
            o_ref[hd // 2, rows, :] = (y * (IDX_HEAD_DIM ** -0.5)).astype(BF16)
        else:
            o_ref[:, hd * LANES:(hd + 1) * LANES] = y.T.astype(BF16)
    if mode == "iq":
        rest = yt[ATT_WIDTH:, :]
        ikt = jnp.concatenate([rotary(rest[0:head]), rest[head:]], axis=0)
        o_refs[1][...] = ikt.T.astype(BF16)
        o_refs[2][...] = rest[head:head + IDX_HEADS, :] * (IDX_HEADS ** -0.5)


def _proj_t(h, wt, col0, n, pos_row, g_col, f_col, mode, tm=1024):
    m, d = h.shape
    tm = min(tm, m)
    if mode == "v":
        out_specs = [pl.BlockSpec((ATT_HEADS, tm // KEY_TILE, V_ROWS, KEY_TILE),
                                  lambda i: (0, i, 0, 0))]
        out_shape = [jax.ShapeDtypeStruct((ATT_HEADS, m // KEY_TILE, V_ROWS, KEY_TILE), BF16)]
    elif mode in ("q", "iq"):
        out_specs = [pl.BlockSpec((ATT_WIDTH // LANES, LANES, tm), lambda i: (0, 0, i))]
        out_shape = [jax.ShapeDtypeStruct((ATT_WIDTH // LANES, LANES, m), BF16)]
        if mode == "iq":
            out_specs += [pl.BlockSpec((tm, LANES), lambda i: (i, 0)),
                          pl.BlockSpec((IDX_HEADS, tm), lambda i: (0, i))]
            out_shape += [jax.ShapeDtypeStruct((m, LANES), BF16),
                          jax.ShapeDtypeStruct((IDX_HEADS, m), F32)]
    else:
        out_specs = [pl.BlockSpec((tm, n), lambda i: (i, 0))]
        out_shape = [jax.ShapeDtypeStruct((m, n), BF16)]
    out = pl.pallas_call(
        functools.partial(_proj_t_kernel, mode=mode),
        grid=(m // tm,),
        in_specs=[pl.BlockSpec((pl.Element(n), pl.Element(d)), lambda i: (col0, 0)),
                  pl.BlockSpec((tm, d), lambda i: (i, 0)),
                  pl.BlockSpec((1, tm), lambda i: (0, i)),
                  pl.BlockSpec(g_col.shape, lambda i: (0, 0)),
                  pl.BlockSpec(f_col.shape, lambda i: (0, 0))],
        out_specs=out_specs,
        out_shape=out_shape,
        scratch_shapes=[pltpu.VMEM((n, d), BF16)],
        compiler_params=_cparams(("arbitrary",)),
        name="in_proj_" + mode,
    )(wt, h, pos_row, g_col, f_col)
    return out if mode == "iq" else out[0]


NEG_BIG = -1e30


INT_MIN = -2 ** 31
INT_MAX = 2 ** 31 - 1
MAGNITUDE_BITS = 0x7FFFFFFF
KEY_NEG_INF = -2139095041
BOUND_SLACK = 1.02
MAX_STATIC_SHIFT = 60.0
COUNT_ROWS = 32


def _key_to_float(key):
    bits = key ^ ((key >> 31) & jnp.int32(MAGNITUDE_BITS))
    return lax.bitcast_convert_type(bits, F32)


def _dsa_kernel(bound_ref, qt_ref, iqt_ref, iwt_ref, k_ref, vt_ref, ik_ref, o_ref,
                sc_ref, m_ref, acc_ref, last_ref, *, tq, tk, topk, index_bits):
    qi = pl.program_id(1)
    n_kb = (qi * tq + tq - 1) // tk + 1
    key0 = lax.broadcasted_iota(I32, (tk, tq), 0)
    qidx = qi * tq + lax.broadcasted_iota(I32, (tk, tq), 1)
    iw = iwt_ref[...]

    def score_body(kb, carry):
        start = pl.multiple_of(kb * tk, tk)
        ikb = ik_ref[pl.ds(start, tk), 0:IDX_HEAD_DIM]
        s = jnp.zeros((tk, tq), F32)
        for h in range(IDX_HEADS):
            off = (h % 2) * IDX_HEAD_DIM
            d = _dot(ikb, iqt_ref[h // 2, off:off + IDX_HEAD_DIM, :])
            s = s + jnp.maximum(d, 0.0) * iw[h:h + 1, :]
        sc_ref[kb] = jnp.where(kb * tk + key0 <= qidx, s, -jnp.inf)
        return carry

    lax.fori_loop(0, n_kb, score_body, 0)

    def count(pred):
        def cnt_body(kb, acc):
            hit = pred(kb).reshape(tk // COUNT_ROWS, COUNT_ROWS, tq)
            for r in range(tk // COUNT_ROWS):
                acc = jnp.where(hit[r], acc + 1.0, acc)
            return acc

        acc = lax.fori_loop(0, n_kb, cnt_body, jnp.zeros((COUNT_ROWS, tq), F32))
        return jnp.sum(acc, axis=0, keepdims=True)

    def bit_body(i, cand):
        trial = cand ^ lax.shift_left(jnp.int32(1), 31 - i)
        trial_f = _key_to_float(trial)
        cnt = count(lambda kb: sc_ref[kb] >= trial_f)
        return jnp.where(cnt >= topk, trial, cand)

    cand = lax.fori_loop(0, 32, bit_body, jnp.full((1, tq), INT_MIN, I32))
    tau = _key_to_float(jnp.maximum(cand, jnp.int32(KEY_NEG_INF)))

    last_ref[...] = jnp.full_like(last_ref, INT_MAX)
    n_ge = count(lambda kb: sc_ref[kb] >= tau)

    @pl.when(jnp.max(n_ge) > topk)
    def _():
        need = topk - count(lambda kb: sc_ref[kb] > tau)

        def idx_body(i, last):
            trial = last | lax.shift_left(jnp.int32(1), index_bits - 1 - i)
            below = count(lambda kb: (sc_ref[kb] == tau) & (kb * tk + key0 < trial))
            return jnp.where(below < need, trial, last)

        last_ref[...] = lax.fori_loop(0, index_bits, idx_body, jnp.zeros((1, tq), I32))

    acc_ref[...] = jnp.zeros_like(acc_ref)
    bound = bound_ref[0]

    def logits(kb, h):
        start = pl.multiple_of(kb * tk, tk)
        return _dot(k_ref[pl.ds(start, tk), h * LANES:(h + 1) * LANES], qt_ref[h])

    last = last_ref[...]

    def selected(kb):
        s = sc_ref[kb]
        kidx = kb * tk + key0
        return ((s > tau) | ((s == tau) & (kidx <= last))) & (kidx <= qidx)

    @pl.when(bound <= MAX_STATIC_SHIFT)
    def _():
        def att_body(kb, carry):
            bias = jnp.where(selected(kb), -bound, -jnp.inf)
            for h in range(ATT_HEADS):
                p = jnp.exp2(logits(kb, h) + bias)
                acc_ref[h] += _dot(vt_ref[h, kb], p.astype(BF16))
            return carry

        lax.fori_loop(0, n_kb, att_body, 0)

    @pl.when(bound > MAX_STATIC_SHIFT)
    def _():
        m_ref[...] = jnp.full_like(m_ref, NEG_BIG)

        def att_body(kb, carry):
            bias = jnp.where(selected(kb), 0.0, -jnp.inf)
            for h in range(ATT_HEADS):
                s = logits(kb, h) + bias
                m_prev = m_ref[h]
                m_next = jnp.maximum(m_prev, jnp.max(s, axis=0, keepdims=True))
                p = jnp.exp2(s - m_next)
                alpha = jnp.exp2(m_prev - m_next)
                acc_ref[h] = alpha * acc_ref[h] + _dot(vt_ref[h, kb], p.astype(BF16))
                m_ref[h] = m_next
            return carry

        lax.fori_loop(0, n_kb, att_body, 0)

    for h in range(ATT_HEADS):
        out = acc_ref[h, 0:LANES, :] / acc_ref[h, LANES:LANES + 1, :]
        o_ref[:, h * LANES:(h + 1) * LANES] = out.T.astype(o_ref.dtype)


def _dsa(bound, qt, iqt, iwt, k, vt, ik, bsz, seq, tk, tq=512):
    tq = min(tq, seq)
    topk = min(TOPK_MAX, seq // 4)
    w = ATT_WIDTH
    nq = seq // tq
    nkb = seq // tk
    npair = iqt.shape[0]
    kern = functools.partial(_dsa_kernel, tq=tq, tk=tk, topk=topk,
                             index_bits=max(1, (seq - 1).bit_length()))
    return pl.pallas_call(
        kern,
        grid=(bsz, nq),
        in_specs=[pl.BlockSpec(memory_space=pltpu.SMEM),
                  pl.BlockSpec((ATT_HEADS, LANES, tq), lambda b, i: (0, 0, b * nq + i)),
                  pl.BlockSpec((npair, LANES, tq), lambda b, i: (0, 0, b * nq + i)),
                  pl.BlockSpec((IDX_HEADS, tq), lambda b, i: (0, b * nq + i)),
                  pl.BlockSpec((seq, w), lambda b, i: (b, 0)),
                  pl.BlockSpec((ATT_HEADS, nkb, V_ROWS, tk), lambda b, i: (0, b, 0, 0)),
                  pl.BlockSpec((seq, LANES), lambda b, i: (b, 0))],
        out_specs=pl.BlockSpec((tq, w), lambda b, i: (b * nq + i, 0)),
        out_shape=jax.ShapeDtypeStruct((bsz * seq, w), BF16),
        scratch_shapes=[pltpu.VMEM((nkb, tk, tq), F32),
                        pltpu.VMEM((ATT_HEADS, 1, tq), F32),
                        pltpu.VMEM((ATT_HEADS, V_ROWS, tq), F32),
                        pltpu.VMEM((1, tq), I32)],
        compiler_params=_cparams(("arbitrary", "arbitrary")),
        name="dsa",
    )(bound, qt, iqt, iwt, k, vt, ik)


def _rwkv_prep_kernel(rr_ref, rk_ref, rv_ref, wa_ref, xg_ref,
                      mr_ref, mk_ref, mv_ref, mwa_ref, mg_ref,
                      w0_ref, a0_ref, kk_ref, ka_ref, rkp_ref,
                      wup_ref, aup_ref, gup_ref, e_ref,
                      r_o, lw_o, k_o, v_o, kkn_o, bb_o, g_o, bon_o,
                      c_r, c_k, c_v, c_wa, c_g, *, tm):
    first = pl.program_id(1) == 0

    for carry_ref in (c_r, c_k, c_v, c_wa, c_g):
        @pl.when(first)
        def _(carry_ref=carry_ref):
            carry_ref[...] = jnp.zeros_like(carry_ref)

    def shift(y_ref, carry_ref, mu_ref, cols=slice(None)):
        y = y_ref[:, cols]
        prev_last = carry_ref[7:8, cols]
        rolled = pltpu.roll(y, 1, 0)
        rows = lax.broadcasted_iota(I32, y.shape, 0)
        yprev = jnp.where(rows == 0, prev_last, rolled)
        carry_ref[:, cols] = y[tm - 8:tm, :]
        return y + (yprev - y) * mu_ref[:, cols]

    wa = shift(wa_ref, c_wa, mwa_ref)
    xw = pltpu.roll(wa, TAIL_TILE - XW_LANE, 1)[:, :LORA_PAD]
    xa = pltpu.roll(wa, TAIL_TILE - XA_LANE, 1)[:, :LORA_PAD]
    xg = shift(xg_ref, c_g, mg_ref)
    w_raw = w0_ref[...] + _dot(jnp.tanh(xw).astype(BF16), wup_ref[...])
    a_pre = a0_ref[...] + _dot(xa.astype(BF16), aup_ref[...])
    g_o[...] = _dot(_sigmoid(xg).astype(BF16), gup_ref[...]).astype(BF16)

    e2 = e_ref[...]
    for p in range(RWKV_WIDTH // LANES):
        sl = slice(p * LANES, (p + 1) * LANES)
        r = shift(rr_ref, c_r, mr_ref, sl)
        k = shift(rk_ref, c_k, mk_ref, sl)
        v = shift(rv_ref, c_v, mv_ref, sl)
        z = -w_raw[:, sl]
        softplus = jnp.maximum(z, 0.0) + jnp.log(1.0 + jnp.exp(-jnp.abs(z)))
        lw_o[:, sl] = -jnp.exp(-softplus - 0.5)
        a = _sigmoid(a_pre[:, sl])
        kk = k * kk_ref[:, sl]
        ss = _dot((kk * kk).astype(BF16), e2)
        kk = kk / jnp.maximum(jnp.sqrt(ss), 1e-12)
        kmod = k * (1.0 + (a - 1.0) * ka_ref[:, sl])
        r_o[:, sl] = r.astype(BF16)
        k_o[:, sl] = kmod.astype(BF16)
        v_o[:, sl] = v.astype(BF16)
        kkn_o[:, sl] = kk.astype(BF16)
        bb_o[:, sl] = (kk * a).astype(BF16)
        rkr = _dot((r * kmod * rkp_ref[:, sl]).astype(BF16), e2)
        bon_o[:, sl] = (rkr * v).astype(BF16)


def _rwkv_prep(proj_r, proj_t, mus, vecs, wup, aup, gup, e, bsz, seq, tm=256):
    w = RWKV_WIDTH
    per_b = seq // tm

    def wide(off):
        return pl.BlockSpec((tm, w), lambda b, i: (b * per_b + i, off // w))

    def tail(off):
        return pl.BlockSpec((tm, TAIL_TILE), lambda b, i: (b * per_b + i, off // TAIL_TILE))

    def const(shape):
        return pl.BlockSpec(shape, lambda b, i: (0, 0))

    out_blk = pl.BlockSpec((tm, w), lambda b, i: (b * per_b + i, 0))
    kern = functools.partial(_rwkv_prep_kernel, tm=tm)
    return pl.pallas_call(
        kern,
        grid=(bsz, per_b),
        in_specs=[wide(OFF_RR), wide(OFF_RK), wide(OFF_RV), tail(OFF_WA), tail(OFF_XG),
                  const((1, w)), const((1, w)), const((1, w)),
                  const((1, TAIL_TILE)), const((1, GATE_LORA)),
                  const((1, w)), const((1, w)), const((1, w)), const((1, w)), const((1, w)),
                  const((LORA_PAD, w)), const((LORA_PAD, w)), const((GATE_LORA, w)),
                  const((LANES, LANES))],
        out_specs=[out_blk] * 8,
        out_shape=[jax.ShapeDtypeStruct((bsz * seq, w), F32 if i == 1 else BF16) for i in range(8)],
        scratch_shapes=[pltpu.VMEM((8, w), F32)] * 3
        + [pltpu.VMEM((8, TAIL_TILE), F32), pltpu.VMEM((8, GATE_LORA), F32)],
        compiler_params=_cparams(("arbitrary", "arbitrary")),
        name="rwkv_prep",
    )(proj_r, proj_r, proj_r, proj_t, proj_t, *mus, *vecs, wup, aup, gup, e)


def _rwkv_chunks(rows_list, r_ref, lw_ref, k_ref, v_ref, kk_ref, bb_ref, g_ref, bon_ref, lg_ref,
                 lb_ref, o_ref, z_ref):
    c = CHUNK
    tri = jnp.where(lax.broadcasted_iota(I32, (c, c), 1) <= lax.broadcasted_iota(I32, (c, c), 0),
                    1.0, 0.0).astype(BF16)
    pre = []
    for rows in rows_list:
        lw = lw_ref[rows, :]
        hi = lw.astype(BF16)
        rem = lw - hi.astype(F32)
        mid = rem.astype(BF16)
        lo = (rem - mid.astype(F32)).astype(BF16)
        cum = _dot(tri, hi) + _dot(tri, mid) + _dot(tri, lo)
        p_in = jnp.exp(cum)
        p_ex = jnp.exp(cum - lw)
        p_inv = jnp.exp(-cum)
        p_end = p_in[c - 1:c, :]
        b_h = bb_ref[rows, :].astype(F32) * p_inv
        k_h = k_ref[rows, :].astype(F32) * p_inv
        pre.append(dict(a_t=-kk_ref[rows, :].astype(F32) * p_ex,
                        r_t=r_ref[rows, :].astype(F32) * p_in,
                        b_h=b_h, k_h=k_h, b_e=b_h * p_end, k_e=k_h * p_end,
                        v=v_ref[rows, :].astype(F32), p_end=p_end))

    n2 = 2 * c
    lane = lax.broadcasted_iota(I32, (1, LANES), 1)
    head0 = lane < RWKV_HEAD_DIM
    ri = lax.broadcasted_iota(I32, (n2, n2), 0)
    ci = lax.broadcasted_iota(I32, (n2, n2), 1)
    same = (ri >= c) == (ci >= c)
    strict = same & (ci < ri)
    incl = same & (ci <= ri)
    eye = ri == ci
    own = ((lax.broadcasted_iota(I32, (n2, LANES), 0) >= c)
           == (lax.broadcasted_iota(I32, (n2, LANES), 1) >= RWKV_HEAD_DIM))
    inv_n = 1.0 / RWKV_HEAD_DIM

    def stack(y):
        return jnp.concatenate([jnp.where(head0, y, 0.0), jnp.where(head0, 0.0, y)], axis=0)

    n_pairs = RWKV_WIDTH // LANES
    chains = [(ci, p) for ci in range(len(rows_list)) for p in range(n_pairs)]
    pairs = range(len(chains))
    sls = [slice(p * LANES, (p + 1) * LANES) for _, p in chains]

    def part(name, n):
        return pre[chains[n][0]][name][:, sls[n]]

    a_s = [stack(part("a_t", n)) for n in pairs]
    r_s = [stack(part("r_t", n)) for n in pairs]
    v_s = [stack(part("v", n)).astype(BF16) for n in pairs]
    g1 = [_dot_nt(jnp.concatenate([a_s[p], r_s[p]], axis=0).astype(BF16),
                  jnp.concatenate([stack(part("b_h", p)), stack(part("k_h", p))],
                                  axis=0).astype(BF16))
          for p in pairs]
    pw = [jnp.where(strict, g[:n2, :n2], 0.0).astype(BF16) for g in g1]
    a_rb = [jnp.where(incl, g[n2:, :n2], 0.0).astype(BF16) for g in g1]
    a_rk = [jnp.where(incl, g[n2:, n2:], 0.0).astype(BF16) for g in g1]
    akv = [_dot(jnp.where(strict, g1[p][:n2, n2:], 0.0).astype(BF16), v_s[p]) for p in pairs]
    half_turn = RWKV_HEAD_DIM

    def unpack(x):
        return jnp.where(own, x, 0.0), pltpu.roll(jnp.where(own, 0.0, x), half_turn, 1)

    xc = [a_s[p] + pltpu.roll(akv[p], half_turn, 1) for p in pairs]
    steps = int(np.log2(c))
    for i in range(steps):
        if i + 1 < steps:
            res = [_dot(pw[p], jnp.concatenate([pw[p], xc[p].astype(BF16)], axis=1)) for p in pairs]
            xc = [xc[p] + res[p][:, n2:] for p in pairs]
            pw = [res[p][:, :n2].astype(BF16) for p in pairs]
        else:
            xc = [xc[p] + _dot(pw[p], xc[p].astype(BF16)) for p in pairs]
    xcb = [x.astype(BF16) for x in xc]
    r2 = [unpack(_dot(a_rb[p], xcb[p])) for p in pairs]
    ov = [r2[p][1] + _dot(a_rk[p], v_s[p]) for p in pairs]
    mg = [unpack(_dot_tn(stack(part("b_e", p)).astype(BF16), xcb[p])) for p in pairs]
    kv = [_dot_tn(stack(part("k_e", p)).astype(BF16), v_s[p]) for p in pairs]
    for p in pairs:
        rows, hp = rows_list[chains[p][0]], chains[p][1]
        q_s = r_s[p] + r2[p][0]
        mmat = mg[p][0] + jnp.where(eye, part("p_end", p), 0.0)
        qm = jnp.concatenate([q_s, mmat], axis=0).astype(BF16)
        res = _dot(qm, z_ref[hp].astype(BF16))
        z_ref[hp] = res[n2:] + mg[p][1] + kv[p]
        o_s = res[:n2] + ov[p]
        mean = jnp.sum(o_s, axis=1, keepdims=True) * inv_n
        dev = jnp.where(own, o_s - mean, 0.0)
        var = jnp.sum(dev * dev, axis=1, keepdims=True) * inv_n
        y = dev * lax.rsqrt(var + LNX_EPS)
        y = (y[:c] + y[c:]) * lg_ref[:, sls[p]] + lb_ref[:, sls[p]]
        out = (y + bon_ref[rows, sls[p]].astype(F32)) * g_ref[rows, sls[p]].astype(F32)
        o_ref[rows, sls[p]] = out.astype(o_ref.dtype)


def _rwkv_core_kernel(*refs):
    z_ref = refs[-1]

    @pl.when(pl.program_id(1) == 0)
    def _():
        z_ref[...] = jnp.zeros_like(z_ref)

    _rwkv_chunks([slice(sub * CHUNK, (sub + 1) * CHUNK) for sub in range(CHUNKS_PER_STEP)], *refs)


def _rwkv_core(r, lw, k, v, kk, bb, g, bon, lg, lb, bsz, seq):
    c = CHUNK * CHUNKS_PER_STEP
    w = RWKV_WIDTH
    per_b = seq // c
    blk = pl.BlockSpec((c, w), lambda b, i: (b * per_b + i, 0))
    vec = pl.BlockSpec((1, w), lambda b, i: (0, 0))
    return pl.pallas_call(
        _rwkv_core_kernel,
        grid=(bsz, per_b),
        in_specs=[blk] * 8 + [vec, vec],
        out_specs=blk,
        out_shape=jax.ShapeDtypeStruct((bsz * seq, w), BF16),
        scratch_shapes=[pltpu.VMEM((w // LANES, LANES, LANES), F32)],
        compiler_params=_cparams(("arbitrary", "arbitrary")),
        name="rwkv_core",
    )(r, lw, k, v, kk, bb, g, bon, lg, lb)


def _rope_freqs(head_dim):
    half = head_dim // ROPE_FRACTION // 2
    return (ROPE_THETA ** (-jnp.arange(half, dtype=F32) / half)).reshape(half, 1)


def kernel(x, c, positions, w_ada, b_ada, norm1_g, w_in, q_norm_g, k_norm_g, rwkv_mu, rwkv_w0,
           rwkv_w_up, rwkv_a0, rwkv_a_up, rwkv_g_up, rwkv_k_k, rwkv_k_a, rwkv_r_k, rwkv_lnx_g,
           rwkv_lnx_b, w_out, norm2_g, w_ffn_gate, w_ffn_up, w_ffn_down):
    bsz, seq, d = x.shape
    depth = w_ada.shape[0]
    m = bsz * seq
    pos_row = positions.reshape(1, m)
    fa_col = _rope_freqs(ATT_HEAD_DIM)
    fi_col = _rope_freqs(IDX_HEAD_DIM)
    hd = RWKV_HEAD_DIM
    e = (jnp.arange(LANES)[:, None] // hd == jnp.arange(LANES)[None, :] // hd).astype(BF16)
    x2 = x.reshape(m, d)

    for l in range(depth):
        mod = _adaln(c, w_ada[l], b_ada[l])
        mod3 = mod.reshape(bsz * 6, 1, d)

        h1 = _norm(x2, norm1_g[l].reshape(1, d), mod3, seq)
        w_in_t = w_in[l].T
        proj_r = _matmul_nt(h1, w_in_t, lambda j: ATT_COLS + j * RWKV_WIDTH, 3, RWKV_WIDTH,
                            "in_proj_rkv")
        proj_t = _matmul_nt(h1, w_in_t, lambda j: TAIL_COL0, 1, 2 * TAIL_TILE,
                            "in_proj_tail", tm=min(2048, m))

        none = jnp.zeros((SUBLANES, 1), F32)
        aw = ATT_WIDTH
        qt = _proj_t(h1, w_in_t, OFF_Q, aw, pos_row, q_norm_g[l].reshape(-1, 1), fa_col, "q")
        kn = _proj_t(h1, w_in_t, OFF_K, aw, pos_row, k_norm_g[l].reshape(-1, 1), fa_col, "k")
        vt = _proj_t(h1, w_in_t, OFF_V, aw, pos_row, none, none, "v")
        iqt, ik, iwt = _proj_t(h1, w_in_t, OFF_IQ, aw + LANES, pos_row, none, fi_col, "iq")
        bound = (ATT_HEAD_DIM ** 0.5 * LOG2E * BOUND_SLACK
                 * jnp.max(jnp.abs(q_norm_g[l])) * jnp.max(jnp.abs(k_norm_g[l])))
        att = _dsa(bound.reshape(1).astype(F32), qt, iqt, iwt, kn, vt, ik, bsz, seq, KEY_TILE)

        mu = rwkv_mu[l]
        w3 = 3 * RWKV_WIDTH

        n_wa = DECAY_LORA + AAA_LORA
        mu_wa = jnp.zeros((1, TAIL_TILE), F32).at[0, XW_LANE:XW_LANE + n_wa].set(mu[w3:w3 + n_wa])
        mus = [mu[0:RWKV_WIDTH].reshape(1, -1), mu[RWKV_WIDTH:2 * RWKV_WIDTH].reshape(1, -1),
               mu[2 * RWKV_WIDTH:w3].reshape(1, -1), mu_wa, mu[w3 + n_wa:].reshape(1, -1)]
        vecs = [rwkv_w0[l].reshape(1, -1), rwkv_a0[l].reshape(1, -1), rwkv_k_k[l].reshape(1, -1),
                rwkv_k_a[l].reshape(1, -1), rwkv_r_k[l].reshape(1, -1)]

        def pad_rows(wm):
            return jnp.zeros((LORA_PAD, wm.shape[1]), F32).at[:wm.shape[0]].set(wm).astype(BF16)

        r, lw, km, vv, kk, bb, g, bon = _rwkv_prep(
            proj_r, proj_t, mus, vecs, pad_rows(rwkv_w_up[l]), pad_rows(rwkv_a_up[l]),
            rwkv_g_up[l].astype(BF16), e, bsz, seq)
        rw = _rwkv_core(r, lw, km, vv, kk, bb, g, bon, rwkv_lnx_g[l].reshape(1, -1),
                        rwkv_lnx_b[l].reshape(1, -1), bsz, seq)

        x2, h2 = _out_proj(att, rw, w_out[l], x2, mod3, norm2_g[l].reshape(1, d), seq)

        hglu = _ffn_glu(h2, w_ffn_gate[l], w_ffn_up[l])
        x2 = _ffn_down(hglu, w_ffn_down[l], x2, mod3, seq)
    return x2.reshape(bsz, seq, d)
```

```python
import functools

import jax
import jax.numpy as jnp
import numpy as np
from jax import lax
from jax.experimental import pallas as pl
from jax.experimental.pallas import tpu as pltpu

F32 = jnp.float32
BF16 = jnp.bfloat16
I32 = jnp.int32

D_MODEL = 2048
ATT_HEADS = 8
ATT_HEAD_DIM = 128
ATT_WIDTH = ATT_HEADS * ATT_HEAD_DIM
RWKV_WIDTH = D_MODEL - ATT_WIDTH
RWKV_HEAD_DIM = 64
RWKV_HEADS = RWKV_WIDTH // RWKV_HEAD_DIM
IDX_HEADS = 16
IDX_HEAD_DIM = 64
TOPK_MAX = 256
ROPE_THETA = 500000.0
ROPE_FRACTION = 4
DECAY_LORA = 96
AAA_LORA = 96
GATE_LORA = 256
NORM_EPS = 1e-6
LNX_EPS = 64e-5

LANES = 128
SUBLANES = 8
LORA_PAD = 128
ATT_COLS = 4 * ATT_WIDTH + IDX_HEAD_DIM + IDX_HEADS
IN_COLS = ATT_COLS + 3 * RWKV_WIDTH + DECAY_LORA + AAA_LORA + GATE_LORA
OFF_Q, OFF_K, OFF_V, OFF_IQ = 0, 1024, 2048, 3072
OFF_RR, OFF_RK, OFF_RV = 0, 1024, 2048
TAIL_TILE = 256
TAIL_COL0 = IN_COLS - 2 * TAIL_TILE
XW_LANE = IN_COLS - GATE_LORA - AAA_LORA - DECAY_LORA - TAIL_COL0
XA_LANE = XW_LANE + DECAY_LORA
OFF_WA, OFF_XG = 0, TAIL_TILE

LOG2E = 1.4426950408889634
V_ROWS = 144
KEY_TILE = 512
CHUNKS_PER_STEP = 4
CHUNK = 64
VMEM_LIMIT = 56 * 1024 * 1024


def _cparams(sem):
    return pltpu.CompilerParams(dimension_semantics=sem, vmem_limit_bytes=VMEM_LIMIT)


def _dot(a, b):
    return jnp.dot(a, b, preferred_element_type=F32)


def _dot_nt(a, b):
    return lax.dot_general(a, b, (((1,), (1,)), ((), ())), preferred_element_type=F32)


def _dot_tn(a, b):
    return lax.dot_general(a, b, (((0,), (0,)), ((), ())), preferred_element_type=F32)


def _sigmoid(x):
    return 1.0 / (1.0 + jnp.exp(-x))


def _adaln_kernel(c_ref, w_ref, b_ref, o_ref):
    c = c_ref[...]
    ca = c * _sigmoid(c)
    o_ref[...] = _dot(ca.astype(BF16), w_ref[...].astype(BF16)) + b_ref[...]


def _adaln(c, w, b):
    bsz, d = c.shape
    n = w.shape[1]
    rows = SUBLANES
    cp = jnp.zeros((rows, d), F32).at[:bsz].set(c)
    tn = 1024
    out = pl.pallas_call(
        _adaln_kernel,
        grid=(n // tn,),
        in_specs=[pl.BlockSpec((rows, d), lambda j: (0, 0)),
                  pl.BlockSpec((d, tn), lambda j: (0, j)),
                  pl.BlockSpec((1, tn), lambda j: (0, j))],
        out_specs=pl.BlockSpec((rows, tn), lambda j: (0, j)),
        out_shape=jax.ShapeDtypeStruct((rows, n), F32),
        compiler_params=_cparams(("arbitrary",)),
        name="adaln",
    )(cp, w, b.reshape(1, n))
    return out[:bsz]


def _norm_mod(x, g, sc, sh):
    ms = jnp.mean(x * x, axis=-1, keepdims=True)
    y = x * lax.rsqrt(ms + NORM_EPS)
    y = y * g
    return y * (1.0 + sc) + sh


def _norm_kernel(x_ref, g_ref, sc_ref, sh_ref, o_ref):
    o_ref[...] = _norm_mod(x_ref[...], g_ref[...], sc_ref[...], sh_ref[...]).astype(BF16)


def _norm(x2, g, mod3, seq, tm=512):
    m, d = x2.shape
    per_b = seq // tm
    return pl.pallas_call(
        _norm_kernel,
        grid=(m // tm,),
        in_specs=[pl.BlockSpec((tm, d), lambda i: (i, 0)),
                  pl.BlockSpec((1, d), lambda i: (0, 0)),
                  pl.BlockSpec((None, 1, d), lambda i: ((i // per_b) * 6 + 1, 0, 0)),
                  pl.BlockSpec((None, 1, d), lambda i: ((i // per_b) * 6 + 0, 0, 0))],
        out_specs=pl.BlockSpec((tm, d), lambda i: (i, 0)),
        out_shape=jax.ShapeDtypeStruct((m, d), BF16),
        compiler_params=_cparams(("arbitrary",)),
        name="norm1",
    )(x2, g, mod3, mod3)


def _mm_nt_kernel(h_ref, wt_ref, o_ref, wb_ref):
    @pl.when(pl.program_id(1) == 0)
    def _():
        wb_ref[...] = wt_ref[...].astype(BF16)

    o_ref[...] = _dot_nt(h_ref[...], wb_ref[...])


def _matmul_nt(h, wt, col_of_tile, n_tiles, tn, name, tm=1024):
    m, d = h.shape

    def w_index(j, i):
        col = col_of_tile(j)
        return (col if isinstance(col, int) else pl.multiple_of(col, SUBLANES)), 0

    return pl.pallas_call(
        _mm_nt_kernel,
        grid=(n_tiles, m // tm),
        in_specs=[pl.BlockSpec((tm, d), lambda j, i: (i, 0)),
                  pl.BlockSpec((pl.Element(tn), pl.Element(d)), w_index)],
        out_specs=pl.BlockSpec((tm, tn), lambda j, i: (i, j)),
        out_shape=jax.ShapeDtypeStruct((m, n_tiles * tn), F32),
        scratch_shapes=[pltpu.VMEM((tn, d), BF16)],
        compiler_params=_cparams(("arbitrary", "arbitrary")),
        name=name,
    )(h, wt)


def _ffn_glu_kernel(h_ref, wg_ref, wu_ref, o_ref, wgb_ref, wub_ref):
    @pl.when(pl.program_id(1) == 0)
    def _():
        wgb_ref[...] = wg_ref[...].astype(BF16)
        wub_ref[...] = wu_ref[...].astype(BF16)

    h = h_ref[...]
    a = _dot(h, wgb_ref[...])
    u = _dot(h, wub_ref[...])
    o_ref[...] = (a * _sigmoid(a) * u).astype(o_ref.dtype)


def _ffn_glu(h, wg, wu, tm=1024, tn=512):
    m, d = h.shape
    tm = min(tm, m)
    n = wg.shape[1]
    return pl.pallas_call(
        _ffn_glu_kernel,
        grid=(n // tn, m // tm),
        in_specs=[pl.BlockSpec((tm, d), lambda j, i: (i, 0)),
                  pl.BlockSpec((d, tn), lambda j, i: (0, j)),
                  pl.BlockSpec((d, tn), lambda j, i: (0, j))],
        out_specs=pl.BlockSpec((tm, tn), lambda j, i: (i, j)),
        out_shape=jax.ShapeDtypeStruct((m, n), BF16),
        scratch_shapes=[pltpu.VMEM((d, tn), BF16)] * 2,
        compiler_params=_cparams(("arbitrary", "arbitrary")),
        name="ffn_glu",
    )(h, wg, wu)


def _out_proj_kernel(a_ref, r_ref, wa_ref, wr_ref, x_ref, gt_ref, g_ref, sc_ref, sh_ref,
                     x1_ref, h2_ref, wab_ref, wrb_ref):
    @pl.when(pl.program_id(0) == 0)
    def _():
        wab_ref[...] = wa_ref[...].astype(BF16)
        wrb_ref[...] = wr_ref[...].astype(BF16)

    mixed = _dot(a_ref[...], wab_ref[...]) + _dot(r_ref[...], wrb_ref[...])
    x1 = x_ref[...] + gt_ref[...] * mixed
    x1_ref[...] = x1
    h2_ref[...] = _norm_mod(x1, g_ref[...], sc_ref[...], sh_ref[...]).astype(BF16)


def _out_proj(att, rwkv, w_out, x2, mod3, g2, seq, tm=512):
    m, ka = att.shape
    kr = rwkv.shape[1]
    n = w_out.shape[1]
    per_b = seq // tm
    once = pl.Buffered(1)

    def mod_row(j):
        return pl.BlockSpec((None, 1, n), lambda i: ((i // per_b) * 6 + j, 0, 0))

    row = pl.BlockSpec((tm, n), lambda i: (i, 0))
    return pl.pallas_call(
        _out_proj_kernel,
        grid=(m // tm,),
        in_specs=[pl.BlockSpec((tm, ka), lambda i: (i, 0)),
                  pl.BlockSpec((tm, kr), lambda i: (i, 0)),
                  pl.BlockSpec((ka, n), lambda i: (0, 0), pipeline_mode=once),
                  pl.BlockSpec((kr, n), lambda i: (ka // kr, 0), pipeline_mode=once),
                  row, mod_row(2),
                  pl.BlockSpec((1, n), lambda i: (0, 0)), mod_row(4), mod_row(3)],
        out_specs=[row, row],
        out_shape=[jax.ShapeDtypeStruct((m, n), F32), jax.ShapeDtypeStruct((m, n), BF16)],
        scratch_shapes=[pltpu.VMEM((ka, n), BF16), pltpu.VMEM((kr, n), BF16)],
        compiler_params=_cparams(("arbitrary",)),
        name="out_proj",
    )(att, rwkv, w_out, w_out, x2, mod3, g2, mod3, mod3)


def _ffn_down_kernel(h_ref, w_ref, x_ref, gt_ref, o_ref, wb_ref):
    @pl.when(pl.program_id(1) == 0)
    def _():
        wb_ref[...] = w_ref[...].astype(BF16)

    o_ref[...] = x_ref[...] + gt_ref[...] * _dot(h_ref[...], wb_ref[...])


def _ffn_down(h, w, x2, mod3, seq, tm=512, tn=512):
    m, kdim = h.shape
    n = w.shape[1]
    per_b = seq // tm
    return pl.pallas_call(
        _ffn_down_kernel,
        grid=(n // tn, m // tm),
        in_specs=[pl.BlockSpec((tm, kdim), lambda j, i: (i, 0)),
                  pl.BlockSpec((kdim, tn), lambda j, i: (0, j)),
                  pl.BlockSpec((tm, tn), lambda j, i: (i, j)),
                  pl.BlockSpec((None, 1, tn), lambda j, i: ((i // per_b) * 6 + 5, 0, j))],
        out_specs=pl.BlockSpec((tm, tn), lambda j, i: (i, j)),
        out_shape=jax.ShapeDtypeStruct((m, n), F32),
        scratch_shapes=[pltpu.VMEM((kdim, tn), BF16)],
        compiler_params=_cparams(("arbitrary", "arbitrary")),
        name="ffn_down",
    )(h, w, x2, mod3)


def _proj_t_kernel(wt_ref, h_ref, pos_ref, g_ref, f_ref, *refs, mode):
    *o_refs, wb_ref = refs
    o_ref = o_refs[0]

    @pl.when(pl.program_id(0) == 0)
    def _():
        wb_ref[...] = wt_ref[...].astype(BF16)

    yt = _dot_nt(wb_ref[...], h_ref[...])
    tm = yt.shape[1]
    if mode == "v":
        pad_row = lax.broadcasted_iota(I32, (V_ROWS - LANES, KEY_TILE), 0)
        ones_rows = jnp.where(pad_row == 0, 1.0, 0.0).astype(BF16)
        for h in range(ATT_HEADS):
            for j in range(tm // KEY_TILE):
                tile = yt[h * LANES:(h + 1) * LANES, j * KEY_TILE:(j + 1) * KEY_TILE]
                o_ref[h, j, 0:LANES, :] = tile.astype(BF16)
                o_ref[h, j, LANES:V_ROWS, :] = ones_rows
        return

    head = ATT_HEAD_DIM if mode in ("q", "k") else IDX_HEAD_DIM
    half = head // ROPE_FRACTION // 2
    ang = f_ref[...] * pos_ref[...].astype(F32)
    cos, sin = jnp.cos(ang), jnp.sin(ang)

    def rotary(y):
        x1, x2 = y[0:half], y[half:2 * half]
        return jnp.concatenate([x1 * cos - x2 * sin, x2 * cos + x1 * sin, y[2 * half:]], axis=0)

    for hd in range(ATT_WIDTH // head):
        y = yt[hd * head:(hd + 1) * head, :]
        if mode in ("q", "k"):
            ms = jnp.mean(y * y, axis=0, keepdims=True)
            y = y * lax.rsqrt(ms + NORM_EPS) * g_ref[...]
        y = rotary(y)
        if mode == "q":
            o_ref[hd] = (y * ((ATT_HEAD_DIM ** -0.5) * LOG2E)).astype(BF16)
        elif mode == "iq":
            rows = slice((hd % 2) * head, (hd % 2 + 1) * head)
            o_ref[hd // 2, rows, :] = (y * (IDX_HEAD_DIM ** -0.5)).astype(BF16)
        else:
            o_ref[:, hd * LANES:(hd + 1) * LANES] = y.T.astype(BF16)
    if mode == "iq":
        rest = yt[ATT_WIDTH:, :]
        ikt = jnp.concatenate([rotary(rest[0:head]), rest[head:]], axis=0)
        o_refs[1][...] = ikt.T.astype(BF16)
        o_refs[2][...] = rest[head:head + IDX_HEADS, :] * (IDX_HEADS ** -0.5)


def _proj_t(h, wt, col0, n, pos_row, g_col, f_col, mode, tm=1024):
    m, d = h.shape
    tm = min(tm, m)
    if mode == "v":
        out_specs = [pl.BlockSpec((ATT_HEADS, tm // KEY_TILE, V_ROWS, KEY_TILE),
                                  lambda i: (0, i, 0, 0))]
        out_shape = [jax.ShapeDtypeStruct((ATT_HEADS, m // KEY_TILE, V_ROWS, KEY_TILE), BF16)]
    elif mode in ("q", "iq"):
        out_specs = [pl.BlockSpec((ATT_WIDTH // LANES, LANES, tm), lambda i: (0, 0, i))]
        out_shape = [jax.ShapeDtypeStruct((ATT_WIDTH // LANES, LANES, m), BF16)]
        if mode == "iq":
            out_specs += [pl.BlockSpec((tm, LANES), lambda i: (i, 0)),
                          pl.BlockSpec((IDX_HEADS, tm), lambda i: (0, i))]
            out_shape += [jax.ShapeDtypeStruct((m, LANES), BF16),
                          jax.ShapeDtypeStruct((IDX_HEADS, m), F32)]
    else:
        out_specs = [pl.BlockSpec((tm, n), lambda i: (i, 0))]
        out_shape = [jax.ShapeDtypeStruct((m, n), BF16)]
    out = pl.pallas_call(
        functools.partial(_proj_t_kernel, mode=mode),
        grid=(m // tm,),
        in_specs=[pl.BlockSpec((pl.Element(n), pl.Element(d)), lambda i: (col0, 0)),
                  pl.BlockSpec((tm, d), lambda i: (i, 0)),
                  pl.BlockSpec((1, tm), lambda i: (0, i)),
                  pl.BlockSpec(g_col.shape, lambda i: (0, 0)),
                  pl.BlockSpec(f_col.shape, lambda i: (0, 0))],
        out_specs=out_specs,
        out_shape=out_shape,
        scratch_shapes=[pltpu.VMEM((n, d), BF16)],
        compiler_params=_cparams(("arbitrary",)),
        name="in_proj_" + mode,
    )(wt, h, pos_row, g_col, f_col)
    return out if mode == "iq" else out[0]


NEG_BIG = -1e30


INT_MIN = -2 ** 31
INT_MAX = 2 ** 31 - 1
MAGNITUDE_BITS = 0x7FFFFFFF
KEY_NEG_INF = -2139095041
BOUND_SLACK = 1.02
MAX_STATIC_SHIFT = 60.0
COUNT_ROWS = 16


def _key_to_float(key):
    bits = key ^ ((key >> 31) & jnp.int32(MAGNITUDE_BITS))
    return lax.bitcast_convert_type(bits, F32)


def _dsa_kernel(bound_ref, qt_ref, iqt_ref, iwt_ref, k_ref, vt_ref, ik_ref, o_ref,
                sc_ref, m_ref, acc_ref, last_ref, *, tq, tk, topk, index_bits):
    qi = pl.program_id(1)
    n_kb = (qi * tq + tq - 1) // tk + 1
    key0 = lax.broadcasted_iota(I32, (tk, tq), 0)
    qidx = qi * tq + lax.broadcasted_iota(I32, (tk, tq), 1)
    iw = iwt_ref[...]

    def score_body(kb, carry):
        start = pl.multiple_of(kb * tk, tk)
        ikb = ik_ref[pl.ds(start, tk), 0:IDX_HEAD_DIM]
        s = jnp.zeros((tk, tq), F32)
        for h in range(IDX_HEADS):
            off = (h % 2) * IDX_HEAD_DIM
            d = _dot(ikb, iqt_ref[h // 2, off:off + IDX_HEAD_DIM, :])
            s = s + jnp.maximum(d, 0.0) * iw[h:h + 1, :]
        sc_ref[kb] = jnp.where(kb * tk + key0 <= qidx, s, -jnp.inf)
        return carry

    lax.fori_loop(0, n_kb, score_body, 0)

    def count(pred):
        def cnt_body(kb, acc):
            hit = pred(kb).reshape(tk // COUNT_ROWS, COUNT_ROWS, tq)
            for r in range(tk // COUNT_ROWS):
                acc = jnp.where(hit[r], acc + 1.0, acc)
            return acc

        acc = lax.fori_loop(0, n_kb, cnt_body, jnp.zeros((COUNT_ROWS, tq), F32))
        return jnp.sum(acc, axis=0, keepdims=True)

    def bit_body(i, cand):
        trial = cand ^ lax.shift_left(jnp.int32(1), 31 - i)
        trial_f = _key_to_float(trial)
        cnt = count(lambda kb: sc_ref[kb] >= trial_f)
        return jnp.where(cnt >= topk, trial, cand)

    cand = lax.fori_loop(0, 32, bit_body, jnp.full((1, tq), INT_MIN, I32))
    tau = _key_to_float(jnp.maximum(cand, jnp.int32(KEY_NEG_INF)))

    last_ref[...] = jnp.full_like(last_ref, INT_MAX)
    n_ge = count(lambda kb: sc_ref[kb] >= tau)

    @pl.when(jnp.max(n_ge) > topk)
    def _():
        need = topk - count(lambda kb: sc_ref[kb] > tau)

        def idx_body(i, last):
            trial = last | lax.shift_left(jnp.int32(1), index_bits - 1 - i)
            below = count(lambda kb: (sc_ref[kb] == tau) & (kb * tk + key0 < trial))
            return jnp.where(below < need, trial, last)

        last_ref[...] = lax.fori_loop(0, index_bits, idx_body, jnp.zeros((1, tq), I32))

    acc_ref[...] = jnp.zeros_like(acc_ref)
    bound = bound_ref[0]

    def logits(kb, h):
        start = pl.multiple_of(kb * tk, tk)
        return _dot(k_ref[pl.ds(start, tk), h * LANES:(h + 1) * LANES], qt_ref[h])

    last = last_ref[...]

    def selected(kb):
        s = sc_ref[kb]
        kidx = kb * tk + key0
        return ((s > tau) | ((s == tau) & (kidx <= last))) & (kidx <= qidx)

    @pl.when(bound <= MAX_STATIC_SHIFT)
    def _():
        def att_body(kb, carry):
            bias = jnp.where(selected(kb), -bound, -jnp.inf)
            for h in range(ATT_HEADS):
                p = jnp.exp2(logits(kb, h) + bias)
                acc_ref[h] += _dot(vt_ref[h, kb], p.astype(BF16))
            return carry

        lax.fori_loop(0, n_kb, att_body, 0)

    @pl.when(bound > MAX_STATIC_SHIFT)
    def _():
        m_ref[...] = jnp.full_like(m_ref, NEG_BIG)

        def att_body(kb, carry):
            bias = jnp.where(selected(kb), 0.0, -jnp.inf)
            for h in range(ATT_HEADS):
                s = logits(kb, h) + bias
                m_prev = m_ref[h]
                m_next = jnp.maximum(m_prev, jnp.max(s, axis=0, keepdims=True))
                p = jnp.exp2(s - m_next)
                alpha = jnp.exp2(m_prev - m_next)
                acc_ref[h] = alpha * acc_ref[h] + _dot(vt_ref[h, kb], p.astype(BF16))
                m_ref[h] = m_next
            return carry

        lax.fori_loop(0, n_kb, att_body, 0)

    for h in range(ATT_HEADS):
        out = acc_ref[h, 0:LANES, :] / acc_ref[h, LANES:LANES + 1, :]
        o_ref[:, h * LANES:(h + 1) * LANES] = out.T.astype(o_ref.dtype)


def _dsa(bound, qt, iqt, iwt, k, vt, ik, bsz, seq, tk, tq=512):
    tq = min(tq, seq)
    topk = min(TOPK_MAX, seq // 4)
    w = ATT_WIDTH
    nq = seq // tq
    nkb = seq // tk
    npair = iqt.shape[0]
    kern = functools.partial(_dsa_kernel, tq=tq, tk=tk, topk=topk,
                             index_bits=max(1, (seq - 1).bit_length()))
    return pl.pallas_call(
        kern,
        grid=(bsz, nq),
        in_specs=[pl.BlockSpec(memory_space=pltpu.SMEM),
                  pl.BlockSpec((ATT_HEADS, LANES, tq), lambda b, i: (0, 0, b * nq + i)),
                  pl.BlockSpec((npair, LANES, tq), lambda b, i: (0, 0, b * nq + i)),
                  pl.BlockSpec((IDX_HEADS, tq), lambda b, i: (0, b * nq + i)),
                  pl.BlockSpec((seq, w), lambda b, i: (b, 0)),
                  pl.BlockSpec((ATT_HEADS, nkb, V_ROWS, tk), lambda b, i: (0, b, 0, 0)),
                  pl.BlockSpec((seq, LANES), lambda b, i: (b, 0))],
        out_specs=pl.BlockSpec((tq, w), lambda b, i: (b * nq + i, 0)),
        out_shape=jax.ShapeDtypeStruct((bsz * seq, w), BF16),
        scratch_shapes=[pltpu.VMEM((nkb, tk, tq), F32),
                        pltpu.VMEM((ATT_HEADS, 1, tq), F32),
                        pltpu.VMEM((ATT_HEADS, V_ROWS, tq), F32),
                        pltpu.VMEM((1, tq), I32)],
        compiler_params=_cparams(("arbitrary", "arbitrary")),
        name="dsa",
    )(bound, qt, iqt, iwt, k, vt, ik)


def _rwkv_prep_kernel(rr_ref, rk_ref, rv_ref, wa_ref, xg_ref,
                      mr_ref, mk_ref, mv_ref, mwa_ref, mg_ref,
                      w0_ref, a0_ref, kk_ref, ka_ref, rkp_ref,
                      wup_ref, aup_ref, gup_ref, e_ref,
                      r_o, lw_o, k_o, v_o, kkn_o, bb_o, g_o, bon_o,
                      c_r, c_k, c_v, c_wa, c_g, *, tm):
    first = pl.program_id(1) == 0

    for carry_ref in (c_r, c_k, c_v, c_wa, c_g):
        @pl.when(first)
        def _(carry_ref=carry_ref):
            carry_ref[...] = jnp.zeros_like(carry_ref)

    def shift(y_ref, carry_ref, mu_ref, cols=slice(None)):
        y = y_ref[:, cols]
        prev_last = carry_ref[7:8, cols]
        rolled = pltpu.roll(y, 1, 0)
        rows = lax.broadcasted_iota(I32, y.shape, 0)
        yprev = jnp.where(rows == 0, prev_last, rolled)
        carry_ref[:, cols] = y[tm - 8:tm, :]
        return y + (yprev - y) * mu_ref[:, cols]

    wa = shift(wa_ref, c_wa, mwa_ref)
    xw = pltpu.roll(wa, TAIL_TILE - XW_LANE, 1)[:, :LORA_PAD]
    xa = pltpu.roll(wa, TAIL_TILE - XA_LANE, 1)[:, :LORA_PAD]
    xg = shift(xg_ref, c_g, mg_ref)
    w_raw = w0_ref[...] + _dot(jnp.tanh(xw).astype(BF16), wup_ref[...])
    a_pre = a0_ref[...] + _dot(xa.astype(BF16), aup_ref[...])
    g_o[...] = _dot(_sigmoid(xg).astype(BF16), gup_ref[...]).astype(BF16)

    e2 = e_ref[...]
    for p in range(RWKV_WIDTH // LANES):
        sl = slice(p * LANES, (p + 1) * LANES)
        r = shift(rr_ref, c_r, mr_ref, sl)
        k = shift(rk_ref, c_k, mk_ref, sl)
        v = shift(rv_ref, c_v, mv_ref, sl)
        z = -w_raw[:, sl]
        softplus = jnp.maximum(z, 0.0) + jnp.log(1.0 + jnp.exp(-jnp.abs(z)))
        lw_o[:, sl] = -jnp.exp(-softplus - 0.5)
        a = _sigmoid(a_pre[:, sl])
        kk = k * kk_ref[:, sl]
        ss = _dot((kk * kk).astype(BF16), e2)
        kk = kk / jnp.maximum(jnp.sqrt(ss), 1e-12)
        kmod = k * (1.0 + (a - 1.0) * ka_ref[:, sl])
        r_o[:, sl] = r.astype(BF16)
        k_o[:, sl] = kmod.astype(BF16)
        v_o[:, sl] = v.astype(BF16)
        kkn_o[:, sl] = kk.astype(BF16)
        bb_o[:, sl] = (kk * a).astype(BF16)
        rkr = _dot((r * kmod * rkp_ref[:, sl]).astype(BF16), e2)
        bon_o[:, sl] = (rkr * v).astype(BF16)


def _rwkv_prep(proj_r, proj_t, mus, vecs, wup, aup, gup, e, bsz, seq, tm=256):
    w = RWKV_WIDTH
    per_b = seq // tm

    def wide(off):
        return pl.BlockSpec((tm, w), lambda b, i: (b * per_b + i, off // w))

    def tail(off):
        return pl.BlockSpec((tm, TAIL_TILE), lambda b, i: (b * per_b + i, off // TAIL_TILE))

    def const(shape):
        return pl.BlockSpec(shape, lambda b, i: (0, 0))

    out_blk = pl.BlockSpec((tm, w), lambda b, i: (b * per_b + i, 0))
    kern = functools.partial(_rwkv_prep_kernel, tm=tm)
    return pl.pallas_call(
        kern,
        grid=(bsz, per_b),
        in_specs=[wide(OFF_RR), wide(OFF_RK), wide(OFF_RV), tail(OFF_WA), tail(OFF_XG),
                  const((1, w)), const((1, w)), const((1, w)),
                  const((1, TAIL_TILE)), const((1, GATE_LORA)),
                  const((1, w)), const((1, w)), const((1, w)), const((1, w)), const((1, w)),
                  const((LORA_PAD, w)), const((LORA_PAD, w)), const((GATE_LORA, w)),
                  const((LANES, LANES))],
        out_specs=[out_blk] * 8,
        out_shape=[jax.ShapeDtypeStruct((bsz * seq, w), F32 if i == 1 else BF16) for i in range(8)],
        scratch_shapes=[pltpu.VMEM((8, w), F32)] * 3
        + [pltpu.VMEM((8, TAIL_TILE), F32), pltpu.VMEM((8, GATE_LORA), F32)],
        compiler_params=_cparams(("arbitrary", "arbitrary")),
        name="rwkv_prep",
    )(proj_r, proj_r, proj_r, proj_t, proj_t, *mus, *vecs, wup, aup, gup, e)


def _rwkv_chunks(rows_list, r_ref, lw_ref, k_ref, v_ref, kk_ref, bb_ref, g_ref, bon_ref, lg_ref,
                 lb_ref, o_ref, z_ref):
    c = CHUNK
    tri = jnp.where(lax.broadcasted_iota(I32, (c, c), 1) <= lax.broadcasted_iota(I32, (c, c), 0),
                    1.0, 0.0).astype(BF16)
    pre = []
    for rows in rows_list:
        lw = lw_ref[rows, :]
        hi = lw.astype(BF16)
        rem = lw - hi.astype(F32)
        mid = rem.astype(BF16)
        lo = (rem - mid.astype(F32)).astype(BF16)
        cum = _dot(tri, hi) + _dot(tri, mid) + _dot(tri, lo)
        p_in = jnp.exp(cum)
        p_ex = jnp.exp(cum - lw)
        p_inv = jnp.exp(-cum)
        p_end = p_in[c - 1:c, :]
        b_h = bb_ref[rows, :].astype(F32) * p_inv
        k_h = k_ref[rows, :].astype(F32) * p_inv
        pre.append(dict(a_t=-kk_ref[rows, :].astype(F32) * p_ex,
                        r_t=r_ref[rows, :].astype(F32) * p_in,
                        b_h=b_h, k_h=k_h, b_e=b_h * p_end, k_e=k_h * p_end,
                        v=v_ref[rows, :].astype(F32), p_end=p_end))

    n2 = 2 * c
    lane = lax.broadcasted_iota(I32, (1, LANES), 1)
    head0 = lane < RWKV_HEAD_DIM
    ri = lax.broadcasted_iota(I32, (n2, n2), 0)
    ci = lax.broadcasted_iota(I32, (n2, n2), 1)
    same = (ri >= c) == (ci >= c)
    strict = same & (ci < ri)
    incl = same & (ci <= ri)
    eye = ri == ci
    own = ((lax.broadcasted_iota(I32, (n2, LANES), 0) >= c)
           == (lax.broadcasted_iota(I32, (n2, LANES), 1) >= RWKV_HEAD_DIM))
    inv_n = 1.0 / RWKV_HEAD_DIM

    def stack(y):
        return jnp.concatenate([jnp.where(head0, y, 0.0), jnp.where(head0, 0.0, y)], axis=0)

    n_pairs = RWKV_WIDTH // LANES
    chains = [(ci, p) for ci in range(len(rows_list)) for p in range(n_pairs)]
    pairs = range(len(chains))
    sls = [slice(p * LANES, (p + 1) * LANES) for _, p in chains]

    def part(name, n):
        return pre[chains[n][0]][name][:, sls[n]]

    a_s = [stack(part("a_t", n)) for n in pairs]
    r_s = [stack(part("r_t", n)) for n in pairs]
    v_s = [stack(part("v", n)).astype(BF16) for n in pairs]
    g1 = [_dot_nt(jnp.concatenate([a_s[p], r_s[p]], axis=0).astype(BF16),
                  jnp.concatenate([stack(part("b_h", p)), stack(part("k_h", p))],
                                  axis=0).astype(BF16))
          for p in pairs]
    pw = [jnp.where(strict, g[:n2, :n2], 0.0).astype(BF16) for g in g1]
    a_rb = [jnp.where(incl, g[n2:, :n2], 0.0).astype(BF16) for g in g1]
    a_rk = [jnp.where(incl, g[n2:, n2:], 0.0).astype(BF16) for g in g1]
    akv = [_dot(jnp.where(strict, g1[p][:n2, n2:], 0.0).astype(BF16), v_s[p]) for p in pairs]
    half_turn = RWKV_HEAD_DIM

    def unpack(x):
        return jnp.where(own, x, 0.0), pltpu.roll(jnp.where(own, 0.0, x), half_turn, 1)

    xc = [a_s[p] + pltpu.roll(akv[p], half_turn, 1) for p in pairs]
    steps = int(np.log2(c))
    for i in range(steps):
        if i + 1 < steps:
            res = [_dot(pw[p], jnp.concatenate([pw[p], xc[p].astype(BF16)], axis=1)) for p in pairs]
            xc = [xc[p] + res[p][:, n2:] for p in pairs]
            pw = [res[p][:, :n2].astype(BF16) for p in pairs]
        else:
            xc = [xc[p] + _dot(pw[p], xc[p].astype(BF16)) for p in pairs]
    xcb = [x.astype(BF16) for x in xc]
    r2 = [unpack(_dot(a_rb[p], xcb[p])) for p in pairs]
    ov = [r2[p][1] + _dot(a_rk[p], v_s[p]) for p in pairs]
    mg = [unpack(_dot_tn(stack(part("b_e", p)).astype(BF16), xcb[p])) for p in pairs]
    kv = [_dot_tn(stack(part("k_e", p)).astype(BF16), v_s[p]) for p in pairs]
    for p in pairs:
        rows, hp = rows_list[chains[p][0]], chains[p][1]
        q_s = r_s[p] + r2[p][0]
        mmat = mg[p][0] + jnp.where(eye, part("p_end", p), 0.0)
        qm = jnp.concatenate([q_s, mmat], axis=0).astype(BF16)
        res = _dot(qm, z_ref[hp].astype(BF16))
        z_ref[hp] = res[n2:] + mg[p][1] + kv[p]
        o_s = res[:n2] + ov[p]
        mean = jnp.sum(o_s, axis=1, keepdims=True) * inv_n
        dev = jnp.where(own, o_s - mean, 0.0)
        var = jnp.sum(dev * dev, axis=1, keepdims=True) * inv_n
        y = dev * lax.rsqrt(var + LNX_EPS)
        y = (y[:c] + y[c:]) * lg_ref[:, sls[p]] + lb_ref[:, sls[p]]
        out = (y + bon_ref[rows, sls[p]].astype(F32)) * g_ref[rows, sls[p]].astype(F32)
        o_ref[rows, sls[p]] = out.astype(o_ref.dtype)


def _rwkv_core_kernel(*refs):
    z_ref = refs[-1]

    @pl.when(pl.program_id(1) == 0)
    def _():
        z_ref[...] = jnp.zeros_like(z_ref)

    _rwkv_chunks([slice(sub * CHUNK, (sub + 1) * CHUNK) for sub in range(CHUNKS_PER_STEP)], *refs)


def _rwkv_core(r, lw, k, v, kk, bb, g, bon, lg, lb, bsz, seq):
    c = CHUNK * CHUNKS_PER_STEP
    w = RWKV_WIDTH
    per_b = seq // c
    blk = pl.BlockSpec((c, w), lambda b, i: (b * per_b + i, 0))
    vec = pl.BlockSpec((1, w), lambda b, i: (0, 0))
    return pl.pallas_call(
        _rwkv_core_kernel,
        grid=(bsz, per_b),
        in_specs=[blk] * 8 + [vec, vec],
        out_specs=blk,
        out_shape=jax.ShapeDtypeStruct((bsz * seq, w), BF16),
        scratch_shapes=[pltpu.VMEM((w // LANES, LANES, LANES), F32)],
        compiler_params=_cparams(("arbitrary", "arbitrary")),
        name="rwkv_core",
    )(r, lw, k, v, kk, bb, g, bon, lg, lb)


def _rope_freqs(head_dim):
    half = head_dim // ROPE_FRACTION // 2
    return (ROPE_THETA ** (-jnp.arange(half, dtype=F32) / half)).reshape(half, 1)


def kernel(x, c, positions, w_ada, b_ada, norm1_g, w_in, q_norm_g, k_norm_g, rwkv_mu, rwkv_w0,
           rwkv_w_up, rwkv_a0, rwkv_a_up, rwkv_g_up, rwkv_k_k, rwkv_k_a, rwkv_r_k, rwkv_lnx_g,
           rwkv_lnx_b, w_out, norm2_g, w_ffn_gate, w_ffn_up, w_ffn_down):
    bsz, seq, d = x.shape
    depth = w_ada.shape[0]
    m = bsz * seq
    pos_row = positions.reshape(1, m)
    fa_col = _rope_freqs(ATT_HEAD_DIM)
    fi_col = _rope_freqs(IDX_HEAD_DIM)
    hd = RWKV_HEAD_DIM
    e = (jnp.arange(LANES)[:, None] // hd == jnp.arange(LANES)[None, :] // hd).astype(BF16)
    x2 = x.reshape(m, d)

    for l in range(depth):
        mod = _adaln(c, w_ada[l], b_ada[l])
        mod3 = mod.reshape(bsz * 6, 1, d)

        h1 = _norm(x2, norm1_g[l].reshape(1, d), mod3, seq)
        w_in_t = w_in[l].T
        proj_r = _matmul_nt(h1, w_in_t, lambda j: ATT_COLS + j * RWKV_WIDTH, 3, RWKV_WIDTH,
                            "in_proj_rkv")
        proj_t = _matmul_nt(h1, w_in_t, lambda j: TAIL_COL0, 1, 2 * TAIL_TILE,
                            "in_proj_tail", tm=min(2048, m))

        none = jnp.zeros((SUBLANES, 1), F32)
        aw = ATT_WIDTH
        qt = _proj_t(h1, w_in_t, OFF_Q, aw, pos_row, q_norm_g[l].reshape(-1, 1), fa_col, "q")
        kn = _proj_t(h1, w_in_t, OFF_K, aw, pos_row, k_norm_g[l].reshape(-1, 1), fa_col, "k")
        vt = _proj_t(h1, w_in_t, OFF_V, aw, pos_row, none, none, "v")
        iqt, ik, iwt = _proj_t(h1, w_in_t, OFF_IQ, aw + LANES, pos_row, none, fi_col, "iq")
        bound = (ATT_HEAD_DIM ** 0.5 * LOG2E * BOUND_SLACK
                 * jnp.max(jnp.abs(q_norm_g[l])) * jnp.max(jnp.abs(k_norm_g[l])))
        att = _dsa(bound.reshape(1).astype(F32), qt, iqt, iwt, kn, vt, ik, bsz, seq, KEY_TILE)

        mu = rwkv_mu[l]
        w3 = 3 * RWKV_WIDTH

        n_wa = DECAY_LORA + AAA_LORA
        mu_wa = jnp.zeros((1, TAIL_TILE), F32).at[0, XW_LANE:XW_LANE + n_wa].set(mu[w3:w3 + n_wa])
        mus = [mu[0:RWKV_WIDTH].reshape(1, -1), mu[RWKV_WIDTH:2 * RWKV_WIDTH].reshape(1, -1),
               mu[2 * RWKV_WIDTH:w3].reshape(1, -1), mu_wa, mu[w3 + n_wa:].reshape(1, -1)]
        vecs = [rwkv_w0[l].reshape(1, -1), rwkv_a0[l].reshape(1, -1), rwkv_k_k[l].reshape(1, -1),
                rwkv_k_a[l].reshape(1, -1), rwkv_r_k[l].reshape(1, -1)]

        def pad_rows(wm):
            return jnp.zeros((LORA_PAD, wm.shape[1]), F32).at[:wm.shape[0]].set(wm).astype(BF16)

        r, lw, km, vv, kk, bb, g, bon = _rwkv_prep(
            proj_r, proj_t, mus, vecs, pad_rows(rwkv_w_up[l]), pad_rows(rwkv_a_up[l]),
            rwkv_g_up[l].astype(BF16), e, bsz, seq)
        rw = _rwkv_core(r, lw, km, vv, kk, bb, g, bon, rwkv_lnx_g[l].reshape(1, -1),
                        rwkv_lnx_b[l].reshape(1, -1), bsz, seq)

        x2, h2 = _out_proj(att, rw, w_out[l], x2, mod3, norm2_g[l].reshape(1, d), seq)

        hglu = _ffn_glu(h2, w_ffn_gate[l], w_ffn_up[l])
        x2 = _ffn_down(hglu, w_ffn_down[l], x2, mod3, seq)
    return x2.reshape(bsz, seq, d)
```

```python
import functools

import jax
import jax.numpy as jnp
import numpy as np
from jax import lax
from jax.experimental import pallas as pl
from jax.experimental.pallas import tpu as pltpu

F32 = jnp.float32
BF16 = jnp.bfloat16
I32 = jnp.int32

D_MODEL = 2048
ATT_HEADS = 8
ATT_HEAD_DIM = 128
ATT_WIDTH = ATT_HEADS * ATT_HEAD_DIM
RWKV_WIDTH = D_MODEL - ATT_WIDTH
RWKV_HEAD_DIM = 64
RWKV_HEADS = RWKV_WIDTH // RWKV_HEAD_DIM
IDX_HEADS = 16
IDX_HEAD_DIM = 64
TOPK_MAX = 256
ROPE_THETA = 500000.0
ROPE_FRACTION = 4
DECAY_LORA = 96
AAA_LORA = 96
GATE_LORA = 256
NORM_EPS = 1e-6
LNX_EPS = 64e-5

LANES = 128
SUBLANES = 8
LORA_PAD = 128
ATT_COLS = 4 * ATT_WIDTH + IDX_HEAD_DIM + IDX_HEADS
IN_COLS = ATT_COLS + 3 * RWKV_WIDTH + DECAY_LORA + AAA_LORA + GATE_LORA
OFF_Q, OFF_K, OFF_V, OFF_IQ = 0, 1024, 2048, 3072
OFF_RR, OFF_RK, OFF_RV = 0, 1024, 2048
TAIL_TILE = 256
TAIL_COL0 = IN_COLS - 2 * TAIL_TILE
XW_LANE = IN_COLS - GATE_LORA - AAA_LORA - DECAY_LORA - TAIL_COL0
XA_LANE = XW_LANE + DECAY_LORA
OFF_WA, OFF_XG = 0, TAIL_TILE

LOG2E = 1.4426950408889634
V_ROWS = 144
KEY_TILE = 512
CHUNKS_PER_STEP = 4
CHUNK = 64
VMEM_LIMIT = 56 * 1024 * 1024


def _cparams(sem):
    return pltpu.CompilerParams(dimension_semantics=sem, vmem_limit_bytes=VMEM_LIMIT)


def _dot(a, b):
    return jnp.dot(a, b, preferred_element_type=F32)


def _dot_nt(a, b):
    return lax.dot_general(a, b, (((1,), (1,)), ((), ())), preferred_element_type=F32)


def _dot_tn(a, b):
    return lax.dot_general(a, b, (((0,), (0,)), ((), ())), preferred_element_type=F32)


def _sigmoid(x):
    return 1.0 / (1.0 + jnp.exp(-x))


def _adaln_kernel(c_ref, w_ref, b_ref, o_ref):
    c = c_ref[...]
    ca = c * _sigmoid(c)
    o_ref[...] = _dot(ca.astype(BF16), w_ref[...].astype(BF16)) + b_ref[...]


def _adaln(c, w, b):
    bsz, d = c.shape
    n = w.shape[1]
    rows = SUBLANES
    cp = jnp.zeros((rows, d), F32).at[:bsz].set(c)
    tn = 1024
    out = pl.pallas_call(
        _adaln_kernel,
        grid=(n // tn,),
        in_specs=[pl.BlockSpec((rows, d), lambda j: (0, 0)),
                  pl.BlockSpec((d, tn), lambda j: (0, j)),
                  pl.BlockSpec((1, tn), lambda j: (0, j))],
        out_specs=pl.BlockSpec((rows, tn), lambda j: (0, j)),
        out_shape=jax.ShapeDtypeStruct((rows, n), F32),
        compiler_params=_cparams(("arbitrary",)),
        name="adaln",
    )(cp, w, b.reshape(1, n))
    return out[:bsz]


def _norm_mod(x, g, sc, sh):
    ms = jnp.mean(x * x, axis=-1, keepdims=True)
    y = x * lax.rsqrt(ms + NORM_EPS)
    y = y * g
    return y * (1.0 + sc) + sh


def _norm_kernel(x_ref, g_ref, sc_ref, sh_ref, o_ref):
    o_ref[...] = _norm_mod(x_ref[...], g_ref[...], sc_ref[...], sh_ref[...]).astype(BF16)


def _norm(x2, g, mod3, seq, tm=512):
    m, d = x2.shape
    per_b = seq // tm
    return pl.pallas_call(
        _norm_kernel,
        grid=(m // tm,),
        in_specs=[pl.BlockSpec((tm, d), lambda i: (i, 0)),
                  pl.BlockSpec((1, d), lambda i: (0, 0)),
                  pl.BlockSpec((None, 1, d), lambda i: ((i // per_b) * 6 + 1, 0, 0)),
                  pl.BlockSpec((None, 1, d), lambda i: ((i // per_b) * 6 + 0, 0, 0))],
        out_specs=pl.BlockSpec((tm, d), lambda i: (i, 0)),
        out_shape=jax.ShapeDtypeStruct((m, d), BF16),
        compiler_params=_cparams(("arbitrary",)),
        name="norm1",
    )(x2, g, mod3, mod3)


def _mm_nt_kernel(h_ref, wt_ref, o_ref, wb_ref):
    @pl.when(pl.program_id(1) == 0)
    def _():
        wb_ref[...] = wt_ref[...].astype(BF16)

    o_ref[...] = _dot_nt(h_ref[...], wb_ref[...])


def _matmul_nt(h, wt, col_of_tile, n_tiles, tn, name, tm=1024):
    m, d = h.shape

    def w_index(j, i):
        col = col_of_tile(j)
        return (col if isinstance(col, int) else pl.multiple_of(col, SUBLANES)), 0

    return pl.pallas_call(
        _mm_nt_kernel,
        grid=(n_tiles, m // tm),
        in_specs=[pl.BlockSpec((tm, d), lambda j, i: (i, 0)),
                  pl.BlockSpec((pl.Element(tn), pl.Element(d)), w_index)],
        out_specs=pl.BlockSpec((tm, tn), lambda j, i: (i, j)),
        out_shape=jax.ShapeDtypeStruct((m, n_tiles * tn), F32),
        scratch_shapes=[pltpu.VMEM((tn, d), BF16)],
        compiler_params=_cparams(("arbitrary", "arbitrary")),
        name=name,
    )(h, wt)


def _ffn_glu_kernel(h_ref, wg_ref, wu_ref, o_ref, wgb_ref, wub_ref):
    @pl.when(pl.program_id(1) == 0)
    def _():
        wgb_ref[...] = wg_ref[...].astype(BF16)
        wub_ref[...] = wu_ref[...].astype(BF16)

    h = h_ref[...]
    a = _dot(h, wgb_ref[...])
    u = _dot(h, wub_ref[...])
    o_ref[...] = (a * _sigmoid(a) * u).astype(o_ref.dtype)


def _ffn_glu(h, wg, wu, tm=1024, tn=512):
    m, d = h.shape
    tm = min(tm, m)
    n = wg.shape[1]
    return pl.pallas_call(
        _ffn_glu_kernel,
        grid=(n // tn, m // tm),
        in_specs=[pl.BlockSpec((tm, d), lambda j, i: (i, 0)),
                  pl.BlockSpec((d, tn), lambda j, i: (0, j)),
                  pl.BlockSpec((d, tn), lambda j, i: (0, j))],
        out_specs=pl.BlockSpec((tm, tn), lambda j, i: (i, j)),
        out_shape=jax.ShapeDtypeStruct((m, n), BF16),
        scratch_shapes=[pltpu.VMEM((d, tn), BF16)] * 2,
        compiler_params=_cparams(("arbitrary", "arbitrary")),
        name="ffn_glu",
    )(h, wg, wu)


def _out_proj_kernel(a_ref, r_ref, wa_ref, wr_ref, x_ref, gt_ref, g_ref, sc_ref, sh_ref,
                     x1_ref, h2_ref, wab_ref, wrb_ref):
    @pl.when(pl.program_id(0) == 0)
    def _():
        wab_ref[...] = wa_ref[...].astype(BF16)
        wrb_ref[...] = wr_ref[...].astype(BF16)

    mixed = _dot(a_ref[...], wab_ref[...]) + _dot(r_ref[...], wrb_ref[...])
    x1 = x_ref[...] + gt_ref[...] * mixed
    x1_ref[...] = x1
    h2_ref[...] = _norm_mod(x1, g_ref[...], sc_ref[...], sh_ref[...]).astype(BF16)


def _out_proj(att, rwkv, w_out, x2, mod3, g2, seq, tm=512):
    m, ka = att.shape
    kr = rwkv.shape[1]
    n = w_out.shape[1]
    per_b = seq // tm
    once = pl.Buffered(1)

    def mod_row(j):
        return pl.BlockSpec((None, 1, n), lambda i: ((i // per_b) * 6 + j, 0, 0))

    row = pl.BlockSpec((tm, n), lambda i: (i, 0))
    return pl.pallas_call(
        _out_proj_kernel,
        grid=(m // tm,),
        in_specs=[pl.BlockSpec((tm, ka), lambda i: (i, 0)),
                  pl.BlockSpec((tm, kr), lambda i: (i, 0)),
                  pl.BlockSpec((ka, n), lambda i: (0, 0), pipeline_mode=once),
                  pl.BlockSpec((kr, n), lambda i: (ka // kr, 0), pipeline_mode=once),
                  row, mod_row(2),
                  pl.BlockSpec((1, n), lambda i: (0, 0)), mod_row(4), mod_row(3)],
        out_specs=[row, row],
        out_shape=[jax.ShapeDtypeStruct((m, n), F32), jax.ShapeDtypeStruct((m, n), BF16)],
        scratch_shapes=[pltpu.VMEM((ka, n), BF16), pltpu.VMEM((kr, n), BF16)],
        compiler_params=_cparams(("arbitrary",)),
        name="out_proj",
    )(att, rwkv, w_out, w_out, x2, mod3, g2, mod3, mod3)


def _ffn_down_kernel(h_ref, w_ref, x_ref, gt_ref, o_ref, wb_ref):
    @pl.when(pl.program_id(1) == 0)
    def _():
        wb_ref[...] = w_ref[...].astype(BF16)

    o_ref[...] = x_ref[...] + gt_ref[...] * _dot(h_ref[...], wb_ref[...])


def _ffn_down(h, w, x2, mod3, seq, tm=512, tn=512):
    m, kdim = h.shape
    n = w.shape[1]
    per_b = seq // tm
    return pl.pallas_call(
        _ffn_down_kernel,
        grid=(n // tn, m // tm),
        in_specs=[pl.BlockSpec((tm, kdim), lambda j, i: (i, 0)),
                  pl.BlockSpec((kdim, tn), lambda j, i: (0, j)),
                  pl.BlockSpec((tm, tn), lambda j, i: (i, j)),
                  pl.BlockSpec((None, 1, tn), lambda j, i: ((i // per_b) * 6 + 5, 0, j))],
        out_specs=pl.BlockSpec((tm, tn), lambda j, i: (i, j)),
        out_shape=jax.ShapeDtypeStruct((m, n), F32),
        scratch_shapes=[pltpu.VMEM((kdim, tn), BF16)],
        compiler_params=_cparams(("arbitrary", "arbitrary")),
        name="ffn_down",
    )(h, w, x2, mod3)


def _proj_t_kernel(wt_ref, h_ref, pos_ref, g_ref, f_ref, *refs, mode):
    *o_refs, wb_ref = refs
    o_ref = o_refs[0]

    @pl.when(pl.program_id(0) == 0)
    def _():
        wb_ref[...] = wt_ref[...].astype(BF16)

    yt = _dot_nt(wb_ref[...], h_ref[...])
    tm = yt.shape[1]
    if mode == "v":
        pad_row = lax.broadcasted_iota(I32, (V_ROWS - LANES, KEY_TILE), 0)
        ones_rows = jnp.where(pad_row == 0, 1.0, 0.0).astype(BF16)
        for h in range(ATT_HEADS):
            for j in range(tm // KEY_TILE):
                tile = yt[h * LANES:(h + 1) * LANES, j * KEY_TILE:(j + 1) * KEY_TILE]
                o_ref[h, j, 0:LANES, :] = tile.astype(BF16)
                o_ref[h, j, LANES:V_ROWS, :] = ones_rows
        return

    head = ATT_HEAD_DIM if mode in ("q", "k") else IDX_HEAD_DIM
    half = head // ROPE_FRACTION // 2
    ang = f_ref[...] * pos_ref[...].astype(F32)
    cos, sin = jnp.cos(ang), jnp.sin(ang)

    def rotary(y):
        x1, x2 = y[0:half], y[half:2 * half]
        return jnp.concatenate([x1 * cos - x2 * sin, x2 * cos + x1 * sin, y[2 * half:]], axis=0)

    for hd in range(ATT_WIDTH // head):
        y = yt[hd * head:(hd + 1) * head, :]
        if mode in ("q", "k"):
            ms = jnp.mean(y * y, axis=0, keepdims=True)
            y = y * lax.rsqrt(ms + NORM_EPS) * g_ref[...]
        y = rotary(y)
        if mode == "q":
            o_ref[hd] = (y * ((ATT_HEAD_DIM ** -0.5) * LOG2E)).astype(BF16)
        elif mode == "iq":
            rows = slice((hd % 2) * head, (hd % 2 + 1) * head)
            o_ref[hd // 2, rows, :] = (y * (IDX_HEAD_DIM ** -0.5)).astype(BF16)
        else:
            o_ref[:, hd * LANES:(hd + 1) * LANES] = y.T.astype(BF16)
    if mode == "iq":
        rest = yt[ATT_WIDTH:, :]
        ikt = jnp.concatenate([rotary(rest[0:head]), rest[head:]], axis=0)
        o_refs[1][...] = ikt.T.astype(BF16)
        o_refs[2][...] = rest[head:head + IDX_HEADS, :] * (IDX_HEADS ** -0.5)


def _proj_t(h, wt, col0, n, pos_row, g_col, f_col, mode, tm=1024):
    m, d = h.shape
    tm = min(tm, m)
    if mode == "v":
        out_specs = [pl.BlockSpec((ATT_HEADS, tm // KEY_TILE, V_ROWS, KEY_TILE),
                                  lambda i: (0, i, 0, 0))]
        out_shape = [jax.ShapeDtypeStruct((ATT_HEADS, m // KEY_TILE, V_ROWS, KEY_TILE), BF16)]
    elif mode in ("q", "iq"):
        out_specs = [pl.BlockSpec((ATT_WIDTH // LANES, LANES, tm), lambda i: (0, 0, i))]
        out_shape = [jax.ShapeDtypeStruct((ATT_WIDTH // LANES, LANES, m), BF16)]
        if mode == "iq":
            out_specs += [pl.BlockSpec((tm, LANES), lambda i: (i, 0)),
                          pl.BlockSpec((IDX_HEADS, tm), lambda i: (0, i))]
            out_shape += [jax.ShapeDtypeStruct((m, LANES), BF16),
                          jax.ShapeDtypeStruct((IDX_HEADS, m), F32)]
    else:
        out_specs = [pl.BlockSpec((tm, n), lambda i: (i, 0))]
        out_shape = [jax.ShapeDtypeStruct((m, n), BF16)]
    out = pl.pallas_call(
        functools.partial(_proj_t_kernel, mode=mode),
        grid=(m // tm,),
        in_specs=[pl.BlockSpec((pl.Element(n), pl.Element(d)), lambda i: (col0, 0)),
                  pl.BlockSpec((tm, d), lambda i: (i, 0)),
                  pl.BlockSpec((1, tm), lambda i: (0, i)),
                  pl.BlockSpec(g_col.shape, lambda i: (0, 0)),
                  pl.BlockSpec(f_col.shape, lambda i: (0, 0))],
        out_specs=out_specs,
        out_shape=out_shape,
        scratch_shapes=[pltpu.VMEM((n, d), BF16)],
        compiler_params=_cparams(("arbitrary",)),
        name="in_proj_" + mode,
    )(wt, h, pos_row, g_col, f_col)
    return out if mode == "iq" else out[0]


NEG_BIG = -1e30


INT_MIN = -2 ** 31
INT_MAX = 2 ** 31 - 1
MAGNITUDE_BITS = 0x7FFFFFFF
KEY_NEG_INF = -2139095041
BOUND_SLACK = 1.02
MAX_STATIC_SHIFT = 60.0
FIRST_TRIALS = (0.0, 2.0, float(np.array(0xBFFFFFFF, np.uint32).view(np.float32)))
COUNT_ROWS = 16


def _key_to_float(key):
    bits = key ^ ((key >> 31) & jnp.int32(MAGNITUDE_BITS))
    return lax.bitcast_convert_type(bits, F32)


def _dsa_kernel(bound_ref, qt_ref, iqt_ref, iwt_ref, k_ref, vt_ref, ik_ref, o_ref,
                sc_ref, m_ref, acc_ref, last_ref, *, tq, tk, topk, index_bits):
    qi = pl.program_id(1)
    n_kb = (qi * tq + tq - 1) // tk + 1
    key0 = lax.broadcasted_iota(I32, (tk, tq), 0)
    qidx = qi * tq + lax.broadcasted_iota(I32, (tk, tq), 1)
    iw = iwt_ref[...]

    def score_body(kb, carry):
        start = pl.multiple_of(kb * tk, tk)
        ikb = ik_ref[pl.ds(start, tk), 0:IDX_HEAD_DIM]
        s = jnp.zeros((tk, tq), F32)
        for h in range(IDX_HEADS):
            off = (h % 2) * IDX_HEAD_DIM
            d = _dot(ikb, iqt_ref[h // 2, off:off + IDX_HEAD_DIM, :])
            s = s + jnp.maximum(d, 0.0) * iw[h:h + 1, :]
        s = jnp.where(kb * tk + key0 <= qidx, s, -jnp.inf)
        sc_ref[kb] = s
        return tuple(add_hits(acc, s >= t) for acc, t in zip(carry, FIRST_TRIALS))

    def add_hits(acc, hit):
        hit = hit.reshape(tk // COUNT_ROWS, COUNT_ROWS, tq)
        for r in range(tk // COUNT_ROWS):
            acc = jnp.where(hit[r], acc + 1.0, acc)
        return acc

    zero_cnt = jnp.zeros((COUNT_ROWS, tq), F32)
    early = lax.fori_loop(0, n_kb, score_body, (zero_cnt,) * len(FIRST_TRIALS))
    n_zero, n_pos, n_neg = [jnp.sum(acc, axis=0, keepdims=True) for acc in early]

    def count(pred):
        acc = lax.fori_loop(0, n_kb, lambda kb, acc: add_hits(acc, pred(kb)), zero_cnt)
        return jnp.sum(acc, axis=0, keepdims=True)

    def bit_body(i, cand):
        trial = cand ^ lax.shift_left(jnp.int32(1), 31 - i)
        trial_f = _key_to_float(trial)
        cnt = count(lambda kb: sc_ref[kb] >= trial_f)
        return jnp.where(cnt >= topk, trial, cand)

    cand = jnp.where(n_zero >= topk, 0, INT_MIN).astype(I32)
    trial = cand ^ jnp.int32(1 << 30)
    cnt = jnp.where(cand == 0, n_pos, n_neg)
    cand = jnp.where(cnt >= topk, trial, cand)
    cand = lax.fori_loop(2, 32, bit_body, cand)
    tau = _key_to_float(jnp.maximum(cand, jnp.int32(KEY_NEG_INF)))

    last_ref[...] = jnp.full_like(last_ref, INT_MAX)
    n_ge = count(lambda kb: sc_ref[kb] >= tau)

    @pl.when(jnp.max(n_ge) > topk)
    def _():
        need = topk - count(lambda kb: sc_ref[kb] > tau)

        def idx_body(i, last):
            trial = last | lax.shift_left(jnp.int32(1), index_bits - 1 - i)
            below = count(lambda kb: (sc_ref[kb] == tau) & (kb * tk + key0 < trial))
            return jnp.where(below < need, trial, last)

        last_ref[...] = lax.fori_loop(0, index_bits, idx_body, jnp.zeros((1, tq), I32))

    acc_ref[...] = jnp.zeros_like(acc_ref)
    bound = bound_ref[0]

    def logits(kb, h):
        start = pl.multiple_of(kb * tk, tk)
        return _dot(k_ref[pl.ds(start, tk), h * LANES:(h + 1) * LANES], qt_ref[h])

    last = last_ref[...]

    def selected(kb):
        s = sc_ref[kb]
        kidx = kb * tk + key0
        return ((s > tau) | ((s == tau) & (kidx <= last))) & (kidx <= qidx)

    @pl.when(bound <= MAX_STATIC_SHIFT)
    def _():
        def att_body(kb, carry):
            bias = jnp.where(selected(kb), -bound, -jnp.inf)
            for h in range(ATT_HEADS):
                p = jnp.exp2(logits(kb, h) + bias)
                acc_ref[h] += _dot(vt_ref[h, kb], p.astype(BF16))
            return carry

        lax.fori_loop(0, n_kb, att_body, 0)

    @pl.when(bound > MAX_STATIC_SHIFT)
    def _():
        m_ref[...] = jnp.full_like(m_ref, NEG_BIG)

        def att_body(kb, carry):
            bias = jnp.where(selected(kb), 0.0, -jnp.inf)
            for h in range(ATT_HEADS):
                s = logits(kb, h) + bias
                m_prev = m_ref[h]
                m_next = jnp.maximum(m_prev, jnp.max(s, axis=0, keepdims=True))
                p = jnp.exp2(s - m_next)
                alpha = jnp.exp2(m_prev - m_next)
                acc_ref[h] = alpha * acc_ref[h] + _dot(vt_ref[h, kb], p.astype(BF16))
                m_ref[h] = m_next
            return carry

        lax.fori_loop(0, n_kb, att_body, 0)

    for h in range(ATT_HEADS):
        out = acc_ref[h, 0:LANES, :] / acc_ref[h, LANES:LANES + 1, :]
        o_ref[:, h * LANES:(h + 1) * LANES] = out.T.astype(o_ref.dtype)


def _dsa(bound, qt, iqt, iwt, k, vt, ik, bsz, seq, tk, tq=512):
    tq = min(tq, seq)
    topk = min(TOPK_MAX, seq // 4)
    w = ATT_WIDTH
    nq = seq // tq
    nkb = seq // tk
    npair = iqt.shape[0]
    kern = functools.partial(_dsa_kernel, tq=tq, tk=tk, topk=topk,
                             index_bits=max(1, (seq - 1).bit_length()))
    return pl.pallas_call(
        kern,
        grid=(bsz, nq),
        in_specs=[pl.BlockSpec(memory_space=pltpu.SMEM),
                  pl.BlockSpec((ATT_HEADS, LANES, tq), lambda b, i: (0, 0, b * nq + i)),
                  pl.BlockSpec((npair, LANES, tq), lambda b, i: (0, 0, b * nq + i)),
                  pl.BlockSpec((IDX_HEADS, tq), lambda b, i: (0, b * nq + i)),
                  pl.BlockSpec((seq, w), lambda b, i: (b, 0)),
                  pl.BlockSpec((ATT_HEADS, nkb, V_ROWS, tk), lambda b, i: (0, b, 0, 0)),
                  pl.BlockSpec((seq, LANES), lambda b, i: (b, 0))],
        out_specs=pl.BlockSpec((tq, w), lambda b, i: (b * nq + i, 0)),
        out_shape=jax.ShapeDtypeStruct((bsz * seq, w), BF16),
        scratch_shapes=[pltpu.VMEM((nkb, tk, tq), F32),
                        pltpu.VMEM((ATT_HEADS, 1, tq), F32),
                        pltpu.VMEM((ATT_HEADS, V_ROWS, tq), F32),
                        pltpu.VMEM((1, tq), I32)],
        compiler_params=_cparams(("arbitrary", "arbitrary")),
        name="dsa",
    )(bound, qt, iqt, iwt, k, vt, ik)


def _rwkv_prep_kernel(rr_ref, rk_ref, rv_ref, wa_ref, xg_ref,
                      mr_ref, mk_ref, mv_ref, mwa_ref, mg_ref,
                      w0_ref, a0_ref, kk_ref, ka_ref, rkp_ref,
                      wup_ref, aup_ref, gup_ref, e_ref,
                      r_o, lw_o, k_o, v_o, kkn_o, bb_o, g_o, bon_o,
                      c_r, c_k, c_v, c_wa, c_g, *, tm):
    first = pl.program_id(1) == 0

    for carry_ref in (c_r, c_k, c_v, c_wa, c_g):
        @pl.when(first)
        def _(carry_ref=carry_ref):
            carry_ref[...] = jnp.zeros_like(carry_ref)

    def shift(y_ref, carry_ref, mu_ref, cols=slice(None)):
        y = y_ref[:, cols]
        prev_last = carry_ref[7:8, cols]
        rolled = pltpu.roll(y, 1, 0)
        rows = lax.broadcasted_iota(I32, y.shape, 0)
        yprev = jnp.where(rows == 0, prev_last, rolled)
        carry_ref[:, cols] = y[tm - 8:tm, :]
        return y + (yprev - y) * mu_ref[:, cols]

    wa = shift(wa_ref, c_wa, mwa_ref)
    xw = pltpu.roll(wa, TAIL_TILE - XW_LANE, 1)[:, :LORA_PAD]
    xa = pltpu.roll(wa, TAIL_TILE - XA_LANE, 1)[:, :LORA_PAD]
    xg = shift(xg_ref, c_g, mg_ref)
    w_raw = w0_ref[...] + _dot(jnp.tanh(xw).astype(BF16), wup_ref[...])
    a_pre = a0_ref[...] + _dot(xa.astype(BF16), aup_ref[...])
    g_o[...] = _dot(_sigmoid(xg).astype(BF16), gup_ref[...]).astype(BF16)

    e2 = e_ref[...]
    for p in range(RWKV_WIDTH // LANES):
        sl = slice(p * LANES, (p + 1) * LANES)
        r = shift(rr_ref, c_r, mr_ref, sl)
        k = shift(rk_ref, c_k, mk_ref, sl)
        v = shift(rv_ref, c_v, mv_ref, sl)
        z = -w_raw[:, sl]
        softplus = jnp.maximum(z, 0.0) + jnp.log(1.0 + jnp.exp(-jnp.abs(z)))
        lw_o[:, sl] = -jnp.exp(-softplus - 0.5)
        a = _sigmoid(a_pre[:, sl])
        kk = k * kk_ref[:, sl]
        ss = _dot((kk * kk).astype(BF16), e2)
        kk = kk / jnp.maximum(jnp.sqrt(ss), 1e-12)
        kmod = k * (1.0 + (a - 1.0) * ka_ref[:, sl])
        r_o[:, sl] = r.astype(BF16)
        k_o[:, sl] = kmod.astype(BF16)
        v_o[:, sl] = v.astype(BF16)
        kkn_o[:, sl] = kk.astype(BF16)
        bb_o[:, sl] = (kk * a).astype(BF16)
        rkr = _dot((r * kmod * rkp_ref[:, sl]).astype(BF16), e2)
        bon_o[:, sl] = (rkr * v).astype(BF16)


def _rwkv_prep(proj_r, proj_t, mus, vecs, wup, aup, gup, e, bsz, seq, tm=256):
    w = RWKV_WIDTH
    per_b = seq // tm

    def wide(off):
        return pl.BlockSpec((tm, w), lambda b, i: (b * per_b + i, off // w))

    def tail(off):
        return pl.BlockSpec((tm, TAIL_TILE), lambda b, i: (b * per_b + i, off // TAIL_TILE))

    def const(shape):
        return pl.BlockSpec(shape, lambda b, i: (0, 0))

    out_blk = pl.BlockSpec((tm, w), lambda b, i: (b * per_b + i, 0))
    kern = functools.partial(_rwkv_prep_kernel, tm=tm)
    return pl.pallas_call(
        kern,
        grid=(bsz, per_b),
        in_specs=[wide(OFF_RR), wide(OFF_RK), wide(OFF_RV), tail(OFF_WA), tail(OFF_XG),
                  const((1, w)), const((1, w)), const((1, w)),
                  const((1, TAIL_TILE)), const((1, GATE_LORA)),
                  const((1, w)), const((1, w)), const((1, w)), const((1, w)), const((1, w)),
                  const((LORA_PAD, w)), const((LORA_PAD, w)), const((GATE_LORA, w)),
                  const((LANES, LANES))],
        out_specs=[out_blk] * 8,
        out_shape=[jax.ShapeDtypeStruct((bsz * seq, w), F32 if i == 1 else BF16) for i in range(8)],
        scratch_shapes=[pltpu.VMEM((8, w), F32)] * 3
        + [pltpu.VMEM((8, TAIL_TILE), F32), pltpu.VMEM((8, GATE_LORA), F32)],
        compiler_params=_cparams(("arbitrary", "arbitrary")),
        name="rwkv_prep",
    )(proj_r, proj_r, proj_r, proj_t, proj_t, *mus, *vecs, wup, aup, gup, e)


def _rwkv_chunks(rows_list, r_ref, lw_ref, k_ref, v_ref, kk_ref, bb_ref, g_ref, bon_ref, lg_ref,
                 lb_ref, o_ref, z_ref):
    c = CHUNK
    tri = jnp.where(lax.broadcasted_iota(I32, (c, c), 1) <= lax.broadcasted_iota(I32, (c, c), 0),
                    1.0, 0.0).astype(BF16)
    pre = []
    for rows in rows_list:
        lw = lw_ref[rows, :]
        hi = lw.astype(BF16)
        rem = lw - hi.astype(F32)
        mid = rem.astype(BF16)
        lo = (rem - mid.astype(F32)).astype(BF16)
        cum = _dot(tri, hi) + _dot(tri, mid) + _dot(tri, lo)
        p_in = jnp.exp(cum)
        p_ex = jnp.exp(cum - lw)
        p_inv = jnp.exp(-cum)
        p_end = p_in[c - 1:c, :]
        b_h = bb_ref[rows, :].astype(F32) * p_inv
        k_h = k_ref[rows, :].astype(F32) * p_inv
        pre.append(dict(a_t=-kk_ref[rows, :].astype(F32) * p_ex,
                        r_t=r_ref[rows, :].astype(F32) * p_in,
                        b_h=b_h, k_h=k_h, b_e=b_h * p_end, k_e=k_h * p_end,
                        v=v_ref[rows, :].astype(F32), p_end=p_end))

    n2 = 2 * c
    lane = lax.broadcasted_iota(I32, (1, LANES), 1)
    head0 = lane < RWKV_HEAD_DIM
    ri = lax.broadcasted_iota(I32, (n2, n2), 0)
    ci = lax.broadcasted_iota(I32, (n2, n2), 1)
    same = (ri >= c) == (ci >= c)
    strict = same & (ci < ri)
    incl = same & (ci <= ri)
    eye = ri == ci
    own = ((lax.broadcasted_iota(I32, (n2, LANES), 0) >= c)
           == (lax.broadcasted_iota(I32, (n2, LANES), 1) >= RWKV_HEAD_DIM))
    inv_n = 1.0 / RWKV_HEAD_DIM

    def stack(y):
        return jnp.concatenate([jnp.where(head0, y, 0.0), jnp.where(head0, 0.0, y)], axis=0)

    n_pairs = RWKV_WIDTH // LANES
    chains = [(ci, p) for ci in range(len(rows_list)) for p in range(n_pairs)]
    pairs = range(len(chains))
    sls = [slice(p * LANES, (p + 1) * LANES) for _, p in chains]

    def part(name, n):
        return pre[chains[n][0]][name][:, sls[n]]

    a_s = [stack(part("a_t", n)) for n in pairs]
    r_s = [stack(part("r_t", n)) for n in pairs]
    v_s = [stack(part("v", n)).astype(BF16) for n in pairs]
    g1 = [_dot_nt(jnp.concatenate([a_s[p], r_s[p]], axis=0).astype(BF16),
                  jnp.concatenate([stack(part("b_h", p)), stack(part("k_h", p))],
                                  axis=0).astype(BF16))
          for p in pairs]
    pw = [jnp.where(strict, g[:n2, :n2], 0.0).astype(BF16) for g in g1]
    a_rb = [jnp.where(incl, g[n2:, :n2], 0.0).astype(BF16) for g in g1]
    a_rk = [jnp.where(incl, g[n2:, n2:], 0.0).astype(BF16) for g in g1]
    akv = [_dot(jnp.where(strict, g1[p][:n2, n2:], 0.0).astype(BF16), v_s[p]) for p in pairs]
    half_turn = RWKV_HEAD_DIM

    def unpack(x):
        return jnp.where(own, x, 0.0), pltpu.roll(jnp.where(own, 0.0, x), half_turn, 1)

    xc = [a_s[p] + pltpu.roll(akv[p], half_turn, 1) for p in pairs]
    steps = int(np.log2(c))
    for i in range(steps):
        if i + 1 < steps:
            res = [_dot(pw[p], jnp.concatenate([pw[p], xc[p].astype(BF16)], axis=1)) for p in pairs]
            xc = [xc[p] + res[p][:, n2:] for p in pairs]
            pw = [res[p][:, :n2].astype(BF16) for p in pairs]
        else:
            xc = [xc[p] + _dot(pw[p], xc[p].astype(BF16)) for p in pairs]
    xcb = [x.astype(BF16) for x in xc]
    r2 = [unpack(_dot(a_rb[p], xcb[p])) for p in pairs]
    ov = [r2[p][1] + _dot(a_rk[p], v_s[p]) for p in pairs]
    mg = [unpack(_dot_tn(stack(part("b_e", p)).astype(BF16), xcb[p])) for p in pairs]
    kv = [_dot_tn(stack(part("k_e", p)).astype(BF16), v_s[p]) for p in pairs]
    for p in pairs:
        rows, hp = rows_list[chains[p][0]], chains[p][1]
        q_s = r_s[p] + r2[p][0]
        mmat = mg[p][0] + jnp.where(eye, part("p_end", p), 0.0)
        qm = jnp.concatenate([q_s, mmat], axis=0).astype(BF16)
        res = _dot(qm, z_ref[hp].astype(BF16))
        z_ref[hp] = res[n2:] + mg[p][1] + kv[p]
        o_s = res[:n2] + ov[p]
        mean = jnp.sum(o_s, axis=1, keepdims=True) * inv_n
        dev = jnp.where(own, o_s - mean, 0.0)
        var = jnp.sum(dev * dev, axis=1, keepdims=True) * inv_n
        y = dev * lax.rsqrt(var + LNX_EPS)
        y = (y[:c] + y[c:]) * lg_ref[:, sls[p]] + lb_ref[:, sls[p]]
        out = (y + bon_ref[rows, sls[p]].astype(F32)) * g_ref[rows, sls[p]].astype(F32)
        o_ref[rows, sls[p]] = out.astype(o_ref.dtype)


def _rwkv_core_kernel(*refs):
    z_ref = refs[-1]

    @pl.when(pl.program_id(1) == 0)
    def _():
        z_ref[...] = jnp.zeros_like(z_ref)

    _rwkv_chunks([slice(sub * CHUNK, (sub + 1) * CHUNK) for sub in range(CHUNKS_PER_STEP)], *refs)


def _rwkv_core(r, lw, k, v, kk, bb, g, bon, lg, lb, bsz, seq):
    c = CHUNK * CHUNKS_PER_STEP
    w = RWKV_WIDTH
    per_b = seq // c
    blk = pl.BlockSpec((c, w), lambda b, i: (b * per_b + i, 0))
    vec = pl.BlockSpec((1, w), lambda b, i: (0, 0))
    return pl.pallas_call(
        _rwkv_core_kernel,
        grid=(bsz, per_b),
        in_specs=[blk] * 8 + [vec, vec],
        out_specs=blk,
        out_shape=jax.ShapeDtypeStruct((bsz * seq, w), BF16),
        scratch_shapes=[pltpu.VMEM((w // LANES, LANES, LANES), F32)],
        compiler_params=_cparams(("arbitrary", "arbitrary")),
        name="rwkv_core",
    )(r, lw, k, v, kk, bb, g, bon, lg, lb)


def _rope_freqs(head_dim):
    half = head_dim // ROPE_FRACTION // 2
    return (ROPE_THETA ** (-jnp.arange(half, dtype=F32) / half)).reshape(half, 1)


def kernel(x, c, positions, w_ada, b_ada, norm1_g, w_in, q_norm_g, k_norm_g, rwkv_mu, rwkv_w0,
           rwkv_w_up, rwkv_a0, rwkv_a_up, rwkv_g_up, rwkv_k_k, rwkv_k_a, rwkv_r_k, rwkv_lnx_g,
           rwkv_lnx_b, w_out, norm2_g, w_ffn_gate, w_ffn_up, w_ffn_down):
    bsz, seq, d = x.shape
    depth = w_ada.shape[0]
    m = bsz * seq
    pos_row = positions.reshape(1, m)
    fa_col = _rope_freqs(ATT_HEAD_DIM)
    fi_col = _rope_freqs(IDX_HEAD_DIM)
    hd = RWKV_HEAD_DIM
    e = (jnp.arange(LANES)[:, None] // hd == jnp.arange(LANES)[None, :] // hd).astype(BF16)
    x2 = x.reshape(m, d)

    for l in range(depth):
        mod = _adaln(c, w_ada[l], b_ada[l])
        mod3 = mod.reshape(bsz * 6, 1, d)

        h1 = _norm(x2, norm1_g[l].reshape(1, d), mod3, seq)
        w_in_t = w_in[l].T
        proj_r = _matmul_nt(h1, w_in_t, lambda j: ATT_COLS + j * RWKV_WIDTH, 3, RWKV_WIDTH,
                            "in_proj_rkv")
        proj_t = _matmul_nt(h1, w_in_t, lambda j: TAIL_COL0, 1, 2 * TAIL_TILE,
                            "in_proj_tail", tm=min(2048, m))

        none = jnp.zeros((SUBLANES, 1), F32)
        aw = ATT_WIDTH
        qt = _proj_t(h1, w_in_t, OFF_Q, aw, pos_row, q_norm_g[l].reshape(-1, 1), fa_col, "q")
        kn = _proj_t(h1, w_in_t, OFF_K, aw, pos_row, k_norm_g[l].reshape(-1, 1), fa_col, "k")
        vt = _proj_t(h1, w_in_t, OFF_V, aw, pos_row, none, none, "v")
        iqt, ik, iwt = _proj_t(h1, w_in_t, OFF_IQ, aw + LANES, pos_row, none, fi_col, "iq")
        bound = (ATT_HEAD_DIM ** 0.5 * LOG2E * BOUND_SLACK
                 * jnp.max(jnp.abs(q_norm_g[l])) * jnp.max(jnp.abs(k_norm_g[l])))
        att = _dsa(bound.reshape(1).astype(F32), qt, iqt, iwt, kn, vt, ik, bsz, seq, KEY_TILE)

        mu = rwkv_mu[l]
        w3 = 3 * RWKV_WIDTH

        n_wa = DECAY_LORA + AAA_LORA
        mu_wa = jnp.zeros((1, TAIL_TILE), F32).at[0, XW_LANE:XW_LANE + n_wa].set(mu[w3:w3 + n_wa])
        mus = [mu[0:RWKV_WIDTH].reshape(1, -1), mu[RWKV_WIDTH:2 * RWKV_WIDTH].reshape(1, -1),
               mu[2 * RWKV_WIDTH:w3].reshape(1, -1), mu_wa, mu[w3 + n_wa:].reshape(1, -1)]
        vecs = [rwkv_w0[l].reshape(1, -1), rwkv_a0[l].reshape(1, -1), rwkv_k_k[l].reshape(1, -1),
                rwkv_k_a[l].reshape(1, -1), rwkv_r_k[l].reshape(1, -1)]

        def pad_rows(wm):
            return jnp.zeros((LORA_PAD, wm.shape[1]), F32).at[:wm.shape[0]].set(wm).astype(BF16)

        r, lw, km, vv, kk, bb, g, bon = _rwkv_prep(
            proj_r, proj_t, mus, vecs, pad_rows(rwkv_w_up[l]), pad_rows(rwkv_a_up[l]),
            rwkv_g_up[l].astype(BF16), e, bsz, seq)
        rw = _rwkv_core(r, lw, km, vv, kk, bb, g, bon, rwkv_lnx_g[l].reshape(1, -1),
                        rwkv_lnx_b[l].reshape(1, -1), bsz, seq)

        x2, h2 = _out_proj(att, rw, w_out[l], x2, mod3, norm2_g[l].reshape(1, d), seq)

        hglu = _ffn_glu(h2, w_ffn_gate[l], w_ffn_up[l])
        x2 = _ffn_down(hglu, w_ffn_down[l], x2, mod3, seq)
    return x2.reshape(bsz, seq, d)
```
